```python
import numpy as np
import jax
import jax.numpy as jnp
from jax import lax

D_MODEL = 1024
BATCH = 16
SEQ = 2048
DEPTH = 2

Q_BLOCK = 128
NEG_INF = -1e30
MLA_HEADS = 8
MLA_Q_LORA = 256
MLA_KV_LORA = 128
MLA_NOPE_DIM = 64
MLA_ROPE_DIM = 32
MLA_V_DIM = 64
MLA_QK_DIM = MLA_NOPE_DIM + MLA_ROPE_DIM
ROPE_THETA = 10000.0
CONV_CHANNELS = 512
CONV_WIDTH = 31
MIX0_IN = MLA_Q_LORA + MLA_KV_LORA + MLA_ROPE_DIM + 2 * CONV_CHANNELS
MIX0_OUT = MLA_HEADS * MLA_V_DIM + CONV_CHANNELS
NSA_HEADS = 16
NSA_KV_GROUPS = 2
NSA_HEADS_PER_GROUP = NSA_HEADS // NSA_KV_GROUPS
NSA_HEAD_DIM = 64
NSA_KV_WIDTH = NSA_KV_GROUPS * NSA_HEAD_DIM
NSA_IN = NSA_HEADS * NSA_HEAD_DIM + 6 * NSA_KV_WIDTH + 3 * NSA_HEADS
CMP_BLOCK = 32
CMP_STRIDE = 16
CMP_HIDDEN = 128
SLC_BLOCK = 64
SLC_TOP_N = 8
WINDOW = 256
FORCE_SCORE = 1e4
MOE_GROUPS = 4
MOE_EXPERTS_PER_GROUP = 8
MOE_EXPERTS = MOE_GROUPS * MOE_EXPERTS_PER_GROUP
MOE_TOP_K = 2
MOE_HIDDEN = 256
MOE_ROW_BLOCK = 128

kernel_name = "hybrid_mla_conformer_nsa_hmoe"


def rms_norm(x, g, eps=1e-6):
    xf = x.astype(jnp.float32)
    y = xf * lax.rsqrt(jnp.mean(xf * xf, axis=-1, keepdims=True) + eps)
    return (y * g.astype(jnp.float32)).astype(x.dtype)


def layer_norm(x, g, b, eps=1e-5):
    xf = x.astype(jnp.float32)
    mu = jnp.mean(xf, axis=-1, keepdims=True)
    xc = xf - mu
    var = jnp.mean(xc * xc, axis=-1, keepdims=True)
    y = xc * lax.rsqrt(var + eps) * g.astype(jnp.float32) + b.astype(jnp.float32)
    return y.astype(x.dtype)


def alibi_slopes(n_heads):
    return jnp.asarray(2.0 ** (-8.0 * np.arange(1, n_heads + 1) / n_heads), jnp.float32)


def apply_rope(x, pos):
    half = x.shape[-1] // 2
    inv_freq = ROPE_THETA ** (-jnp.arange(half, dtype=jnp.float32) / half)
    ang = pos.astype(jnp.float32)[:, :, None, None] * inv_freq
    cos = jnp.cos(ang).astype(x.dtype)
    sin = jnp.sin(ang).astype(x.dtype)
    x1, x2 = x[..., :half], x[..., half:]
    return jnp.concatenate([x1 * cos - x2 * sin, x1 * sin + x2 * cos], axis=-1)


def causal_block_attention(q, k, v, scale):
    S = q.shape[2]
    outs = []
    for i in range(S // Q_BLOCK):
        end = (i + 1) * Q_BLOCK
        s = jnp.einsum('bhqd,bhkd->bhqk', q[:, :, i * Q_BLOCK:end], k[:, :, :end]).astype(jnp.float32) * scale
        mask = (i * Q_BLOCK + jnp.arange(Q_BLOCK))[:, None] >= jnp.arange(end)[None, :]
        p = jax.nn.softmax(jnp.where(mask, s, NEG_INF), axis=-1).astype(v.dtype)
        outs.append(jnp.einsum('bhqk,bhkd->bhqd', p, v[:, :, :end]))
    return jnp.concatenate(outs, axis=2)


def causal_depthwise_conv(u, w, b):
    C = u.shape[-1]
    y = lax.conv_general_dilated(u, w[:, None, :].astype(u.dtype), window_strides=(1,),
                                 padding=((CONV_WIDTH - 1, 0),),
                                 dimension_numbers=('NWC', 'WIO', 'NWC'), feature_group_count=C)
    return y + b


def mla_conv_mixer(h, pos, w_in, q_a_norm, w_uq, kv_a_norm, w_ukv, q_norm, k_norm,
                   dw_w, dw_b, ln_g, ln_b, w_out):
    B, S, _ = h.shape
    proj = h @ w_in
    c_q, c_kv, k_rope, conv_in = jnp.split(
        proj, [MLA_Q_LORA, MLA_Q_LORA + MLA_KV_LORA, MLA_Q_LORA + MLA_KV_LORA + MLA_ROPE_DIM], axis=-1)
    q = (rms_norm(c_q, q_a_norm) @ w_uq).reshape(B, S, MLA_HEADS, MLA_QK_DIM)
    kv = (rms_norm(c_kv, kv_a_norm) @ w_ukv).reshape(B, S, MLA_HEADS, MLA_NOPE_DIM + MLA_V_DIM)
    k_nope, v = kv[..., :MLA_NOPE_DIM], kv[..., MLA_NOPE_DIM:]
    k = jnp.concatenate(
        [k_nope, jnp.broadcast_to(k_rope[:, :, None, :], (B, S, MLA_HEADS, MLA_ROPE_DIM))], axis=-1)
    q = rms_norm(q, q_norm)
    k = rms_norm(k, k_norm)
    q = jnp.concatenate([q[..., :MLA_NOPE_DIM], apply_rope(q[..., MLA_NOPE_DIM:], pos)], axis=-1)
    k = jnp.concatenate([k[..., :MLA_NOPE_DIM], apply_rope(k[..., MLA_NOPE_DIM:], pos)], axis=-1)
    o_mla = causal_block_attention(q.transpose(0, 2, 1, 3), k.transpose(0, 2, 1, 3),
                                   v.transpose(0, 2, 1, 3), MLA_QK_DIM ** -0.5)
    o_mla = o_mla.transpose(0, 2, 1, 3).reshape(B, S, MLA_HEADS * MLA_V_DIM)
    a, g = jnp.split(conv_in, 2, axis=-1)
    u = causal_depthwise_conv(a * jax.nn.sigmoid(g), dw_w, dw_b)
    u = jax.nn.silu(layer_norm(u, ln_g, ln_b))
    return jnp.concatenate([o_mla, u], axis=-1) @ w_out


def compress_blocks(kv, pe, w1, w2):
    B, G, S, DK = kv.shape
    n_cmp = (S - CMP_BLOCK) // CMP_STRIDE + 1
    idx = CMP_STRIDE * np.arange(n_cmp)[:, None] + np.arange(CMP_BLOCK)[None, :]
    blocks = (kv[:, :, idx] + pe).reshape(B, G, n_cmp, CMP_BLOCK * DK)
    return jax.nn.gelu(blocks @ w1) @ w2


def compressed_attention(q, k_cmp, v_cmp, pos, slopes):
    S = q.shape[3]
    n_cmp = k_cmp.shape[2]
    end_idx = np.arange(n_cmp) * CMP_STRIDE + CMP_BLOCK - 1
    s = jnp.einsum('bghtd,bgcd->bghtc', q, k_cmp).astype(jnp.float32) * NSA_HEAD_DIM ** -0.5
    dist = (pos[:, :, None] - pos[:, end_idx][:, None, :]).astype(jnp.float32)
    valid = np.arange(S)[:, None] >= end_idx[None, :]
    s = s - slopes[None, :, :, None, None] * dist[:, None, None]
    s = jnp.where(valid, s, NEG_INF)
    m = jnp.max(s, axis=-1, keepdims=True)
    p = jnp.where(valid, jnp.exp(s - m), 0.0)
    p = p / jnp.maximum(jnp.sum(p, axis=-1, keepdims=True), 1e-20)
    o = jnp.einsum('bghtc,bgcd->bghtd', p.astype(v_cmp.dtype), v_cmp)
    return o, p


def select_blocks(p_cmp, S):
    n_cmp = p_cmp.shape[-1]
    n_sel = S // SLC_BLOCK
    cmp_start = np.arange(n_cmp) * CMP_STRIDE
    slc_start = np.arange(n_sel) * SLC_BLOCK
    overlap = (cmp_start[:, None] < slc_start[None, :] + SLC_BLOCK) & (cmp_start[:, None] + CMP_BLOCK > slc_start[None, :])
    imp = jnp.einsum('bghtc,cj->bgtj', p_cmp, jnp.asarray(overlap, jnp.float32))
    cur = np.arange(S)[:, None] // SLC_BLOCK
    j = np.arange(n_sel)[None, :]
    forced = (j == 0) | (j == cur) | (j == cur - 1)
    imp = jnp.where(forced, FORCE_SCORE, jnp.where(j <= cur, imp, -1.0))
    _, sel = lax.top_k(imp, min(SLC_TOP_N, n_sel))
    return sel


def selected_attention(q, k, v, pos, sel, slopes):
    B, G, HPG, S, DK = q.shape
    n_sel = S // SLC_BLOCK
    nq = S // Q_BLOCK
    n_top = sel.shape[-1]
    kb = k.reshape(B, G, n_sel, SLC_BLOCK, DK)
    vb = v.reshape(B, G, n_sel, SLC_BLOCK, DK)
    pos_blocks = pos.reshape(B, n_sel, SLC_BLOCK)
    q_c = q.reshape(B, G, HPG, nq, Q_BLOCK, DK).transpose(3, 0, 1, 2, 4, 5)
    sel_c = sel.reshape(B, G, nq, Q_BLOCK, n_top).transpose(2, 0, 1, 3, 4)
    pos_c = pos.reshape(B, nq, Q_BLOCK).transpose(1, 0, 2)
    starts = jnp.arange(nq, dtype=jnp.int32) * Q_BLOCK
    gather_kv = jax.vmap(jax.vmap(lambda blk, idx: blk[idx]))
    gather_pos = jax.vmap(lambda pb, idx: pb[idx])

    def one_block(args):
        qc, sc, pq, t0 = args
        kg = gather_kv(kb, sc)
        vg = gather_kv(vb, sc)
        pk = gather_pos(pos_blocks, sc)
        s = jnp.einsum('bghqd,bgqnld->bghqnl', qc, kg).astype(jnp.float32) * DK ** -0.5
        t_idx = t0 + jnp.arange(Q_BLOCK)
        k_idx = sc[..., None] * SLC_BLOCK + jnp.arange(SLC_BLOCK)
        valid = k_idx <= t_idx[None, None, :, None, None]
        dist = (pq[:, None, :, None, None] - pk).astype(jnp.float32)
        s = s - slopes[None, :, :, None, None, None] * dist[:, :, None]
        s = jnp.where(valid[:, :, None], s, NEG_INF).reshape(B, G, HPG, Q_BLOCK, n_top * SLC_BLOCK)
        p = jax.nn.softmax(s, axis=-1).reshape(B, G, HPG, Q_BLOCK, n_top, SLC_BLOCK).astype(v.dtype)
        return jnp.einsum('bghqnl,bgqnld->bghqd', p, vg)

    o = lax.map(one_block, (q_c, sel_c, pos_c, starts))
    return o.transpose(1, 2, 3, 0, 4, 5).reshape(B, G, HPG, S, DK)


def window_attention(q, k, v, pos, slopes):
    B, G, HPG, S, DK = q.shape
    nq = S // Q_BLOCK
    span = WINDOW + Q_BLOCK
    k_pad = jnp.pad(k, ((0, 0), (0, 0), (WINDOW, 0), (0, 0)))
    v_pad = jnp.pad(v, ((0, 0), (0, 0), (WINDOW, 0), (0, 0)))
    pos_pad = jnp.pad(pos, ((0, 0), (WINDOW, 0)))
    q_c = q.reshape(B, G, HPG, nq, Q_BLOCK, DK).transpose(3, 0, 1, 2, 4, 5)
    pos_c = pos.reshape(B, nq, Q_BLOCK).transpose(1, 0, 2)
    starts = jnp.arange(nq, dtype=jnp.int32) * Q_BLOCK

    def one_block(args):
        qc, pq, t0 = args
        kb = lax.dynamic_slice_in_dim(k_pad, t0, span, axis=2)
        vb = lax.dynamic_slice_in_dim(v_pad, t0, span, axis=2)
        pk = lax.dynamic_slice_in_dim(pos_pad, t0, span, axis=1)
        s = jnp.einsum('bghqd,bgkd->bghqk', qc, kb).astype(jnp.float32) * DK ** -0.5
        t_idx = t0 + jnp.arange(Q_BLOCK)
        k_idx = t0 - WINDOW + jnp.arange(span)
        rel = t_idx[:, None] - k_idx[None, :]
        valid = (k_idx[None, :] >= 0) & (rel >= 0) & (rel < WINDOW)
        dist = (pq[:, :, None] - pk[:, None, :]).astype(jnp.float32)
        s = s - slopes[None, :, :, None, None] * dist[:, None, None]
        p = jax.nn.softmax(jnp.where(valid, s, NEG_INF), axis=-1).astype(v.dtype)
        return jnp.einsum('bghqk,bgkd->bghqd', p, vb)

    o = lax.map(one_block, (q_c, pos_c, starts))
    return o.transpose(1, 2, 3, 0, 4, 5).reshape(B, G, HPG, S, DK)


def nsa_mixer(h, pos, slopes, w_in, q_norm, k_norm_cmp, k_norm_slc, k_norm_win,
              cmp_pe_k, cmp_w1_k, cmp_w2_k, cmp_pe_v, cmp_w1_v, cmp_w2_v, w_out):
    B, S, _ = h.shape
    G, HPG, DK = NSA_KV_GROUPS, NSA_HEADS_PER_GROUP, NSA_HEAD_DIM
    proj = h @ w_in
    split_pts = np.cumsum([NSA_HEADS * DK] + [NSA_KV_WIDTH] * 6).tolist()
    q, kc, vc, ks, vs, kw, vw, gates = jnp.split(proj, split_pts, axis=-1)
    q = rms_norm(q.reshape(B, S, G, HPG, DK), q_norm).transpose(0, 2, 3, 1, 4)
    to_kv = lambda t: t.reshape(B, S, G, DK).transpose(0, 2, 1, 3)
    k_cmp = rms_norm(compress_blocks(to_kv(kc), cmp_pe_k, cmp_w1_k, cmp_w2_k), k_norm_cmp)
    v_cmp = compress_blocks(to_kv(vc), cmp_pe_v, cmp_w1_v, cmp_w2_v)
    o_cmp, p_cmp = compressed_attention(q, k_cmp, v_cmp, pos, slopes)
    sel = select_blocks(p_cmp, S)
    o_slc = selected_attention(q, rms_norm(to_kv(ks), k_norm_slc), to_kv(vs), pos, sel, slopes)
    o_win = window_attention(q, rms_norm(to_kv(kw), k_norm_win), to_kv(vw), pos, slopes)
    g = jax.nn.sigmoid(gates).reshape(B, S, 3, G, HPG).transpose(2, 0, 3, 4, 1)[..., None]
    o = g[0] * o_cmp + g[1] * o_slc + g[2] * o_win
    return o.transpose(0, 3, 1, 2, 4).reshape(B, S, NSA_HEADS * DK) @ w_out


def hierarchical_moe(h, router_group, router_group_b, router_expert, router_expert_b, w_gate, w_up, w_down):
    B, S, D = h.shape
    T = B * S
    xt = h.reshape(T, D)
    g_logits = (xt @ router_group).astype(jnp.float32) + router_group_b.astype(jnp.float32)
    _, grp = lax.top_k(g_logits, 1)
    g_w = jnp.take_along_axis(jax.nn.softmax(g_logits, axis=-1), grp, axis=-1)
    e_logits = ((xt @ router_expert).astype(jnp.float32) + router_expert_b.astype(jnp.float32)
                ).reshape(T, MOE_GROUPS, MOE_EXPERTS_PER_GROUP)
    e_in = jnp.take_along_axis(e_logits, grp[:, :, None], axis=1)[:, 0]
    top_val, top_idx = lax.top_k(e_in, MOE_TOP_K)
    gates = jax.nn.softmax(top_val, axis=-1) * g_w
    expert = grp * MOE_EXPERTS_PER_GROUP + top_idx
    A = T * MOE_TOP_K
    e_flat = expert.reshape(A)
    tok_flat = jnp.repeat(jnp.arange(T, dtype=jnp.int32), MOE_TOP_K)
    w_flat = gates.reshape(A).astype(h.dtype)
    order = jnp.argsort(e_flat)
    e_sorted = e_flat[order]
    counts = jnp.bincount(e_flat, length=MOE_EXPERTS)
    starts = jnp.cumsum(counts) - counts
    padded = (counts + MOE_ROW_BLOCK - 1) // MOE_ROW_BLOCK * MOE_ROW_BLOCK
    padded_end = jnp.cumsum(padded)
    dest = (padded_end - padded)[e_sorted] + (jnp.arange(A) - starts[e_sorted])
    R = A + MOE_EXPERTS * MOE_ROW_BLOCK
    n_blk = R // MOE_ROW_BLOCK
    row_tok = jnp.full((R,), T, jnp.int32).at[dest].set(tok_flat[order])
    row_w = jnp.zeros((R,), h.dtype).at[dest].set(w_flat[order])
    blk_expert = jnp.clip(jnp.searchsorted(padded_end, jnp.arange(n_blk) * MOE_ROW_BLOCK, side='right'),
                          0, MOE_EXPERTS - 1)
    x_pad = jnp.concatenate([xt, jnp.zeros((1, D), xt.dtype)], axis=0)

    def expert_block(args):
        toks, wts, e = args
        rows = x_pad[toks]
        y = (jax.nn.silu(rows @ w_gate[e]) * (rows @ w_up[e])) @ w_down[e]
        return y * wts[:, None]

    ys = lax.map(expert_block, (row_tok.reshape(n_blk, MOE_ROW_BLOCK), row_w.reshape(n_blk, MOE_ROW_BLOCK), blk_expert))
    out = jax.ops.segment_sum(ys.reshape(R, D), row_tok, num_segments=T + 1)[:T]
    return out.reshape(B, S, D).astype(h.dtype)


def setup_inputs(seed: int = 0) -> dict:
    key = jax.random.key(seed)
    keys = iter(jax.random.split(key, 64))
    n_even = (DEPTH + 1) // 2
    n_odd = DEPTH // 2
    D = D_MODEL

    def dense(shape, fan_in):
        return jax.random.normal(next(keys), shape, jnp.float32) * fan_in ** -0.5

    def gain(shape):
        return 1.0 + 0.05 * jax.random.normal(next(keys), shape, jnp.float32)

    def small(shape, scale=0.02):
        return scale * jax.random.normal(next(keys), shape, jnp.float32)

    x = jax.random.normal(next(keys), (BATCH, SEQ, D), jnp.float32)
    positions = (jnp.cumsum(jax.random.randint(next(keys), (BATCH, SEQ), 1, 3), axis=1) - 1).astype(jnp.int32)
    L, DK = CMP_BLOCK, NSA_HEAD_DIM
    return {
        "x": x,
        "positions": positions,
        "norm_mix": gain((DEPTH, D)),
        "norm_ffn": gain((DEPTH, D)),
        "mix0_w_in": dense((n_even, D, MIX0_IN), D),
        "mla_q_a_norm": gain((n_even, MLA_Q_LORA)),
        "mla_w_uq": dense((n_even, MLA_Q_LORA, MLA_HEADS * MLA_QK_DIM), MLA_Q_LORA),
        "mla_kv_a_norm": gain((n_even, MLA_KV_LORA)),
        "mla_w_ukv": dense((n_even, MLA_KV_LORA, MLA_HEADS * (MLA_NOPE_DIM + MLA_V_DIM)), MLA_KV_LORA),
        "mla_q_norm": gain((n_even, MLA_QK_DIM)),
        "mla_k_norm": gain((n_even, MLA_QK_DIM)),
        "conv_dw_w": dense((n_even, CONV_WIDTH, CONV_CHANNELS), CONV_WIDTH),
        "conv_dw_b": small((n_even, CONV_CHANNELS)),
        "conv_ln_g": gain((n_even, CONV_CHANNELS)),
        "conv_ln_b": small((n_even, CONV_CHANNELS)),
        "mix0_w_out": dense((n_even, MIX0_OUT, D), MIX0_OUT),
        "nsa_w_in": dense((n_odd, D, NSA_IN), D),
        "nsa_q_norm": gain((n_odd, DK)),
        "nsa_k_norm_cmp": gain((n_odd, DK)),
        "nsa_k_norm_slc": gain((n_odd, DK)),
        "nsa_k_norm_win": gain((n_odd, DK)),
        "nsa_cmp_pe_k": small((n_odd, L, DK), 0.1),
        "nsa_cmp_w1_k": dense((n_odd, L * DK, CMP_HIDDEN), L * DK),
        "nsa_cmp_w2_k": dense((n_odd, CMP_HIDDEN, DK), CMP_HIDDEN),
        "nsa_cmp_pe_v": small((n_odd, L, DK), 0.1),
        "nsa_cmp_w1_v": dense((n_odd, L * DK, CMP_HIDDEN), L * DK),
        "nsa_cmp_w2_v": dense((n_odd, CMP_HIDDEN, DK), CMP_HIDDEN),
        "nsa_w_out": dense((n_odd, NSA_HEADS * DK, D), NSA_HEADS * DK),
        "moe_router_group": dense((DEPTH, D, MOE_GROUPS), D),
        "moe_router_group_b": small((DEPTH, MOE_GROUPS), 0.01),
        "moe_router_expert": dense((DEPTH, D, MOE_EXPERTS), D),
        "moe_router_expert_b": small((DEPTH, MOE_EXPERTS), 0.01),
        "moe_w_gate": dense((DEPTH, MOE_EXPERTS, D, MOE_HIDDEN), D),
        "moe_w_up": dense((DEPTH, MOE_EXPERTS, D, MOE_HIDDEN), D),
        "moe_w_down": dense((DEPTH, MOE_EXPERTS, MOE_HIDDEN, D), MOE_HIDDEN),
    }


def reference(x, positions, norm_mix, norm_ffn, mix0_w_in, mla_q_a_norm, mla_w_uq, mla_kv_a_norm,
              mla_w_ukv, mla_q_norm, mla_k_norm, conv_dw_w, conv_dw_b, conv_ln_g, conv_ln_b, mix0_w_out,
              nsa_w_in, nsa_q_norm, nsa_k_norm_cmp, nsa_k_norm_slc, nsa_k_norm_win,
              nsa_cmp_pe_k, nsa_cmp_w1_k, nsa_cmp_w2_k, nsa_cmp_pe_v, nsa_cmp_w1_v, nsa_cmp_w2_v, nsa_w_out,
              moe_router_group, moe_router_group_b, moe_router_expert, moe_router_expert_b,
              moe_w_gate, moe_w_up, moe_w_down):
    slopes = alibi_slopes(NSA_HEADS).reshape(NSA_KV_GROUPS, NSA_HEADS_PER_GROUP)
    for layer in range(DEPTH):
        i = layer // 2
        h = rms_norm(x, norm_mix[layer])
        if layer % 2 == 0:
            mix = mla_conv_mixer(h, positions, mix0_w_in[i], mla_q_a_norm[i], mla_w_uq[i], mla_kv_a_norm[i],
                                 mla_w_ukv[i], mla_q_norm[i], mla_k_norm[i], conv_dw_w[i], conv_dw_b[i],
                                 conv_ln_g[i], conv_ln_b[i], mix0_w_out[i])
        else:
            mix = nsa_mixer(h, positions, slopes, nsa_w_in[i], nsa_q_norm[i], nsa_k_norm_cmp[i],
                            nsa_k_norm_slc[i], nsa_k_norm_win[i], nsa_cmp_pe_k[i], nsa_cmp_w1_k[i],
                            nsa_cmp_w2_k[i], nsa_cmp_pe_v[i], nsa_cmp_w1_v[i], nsa_cmp_w2_v[i], nsa_w_out[i])
        x = x + mix
        x = x + hierarchical_moe(rms_norm(x, norm_ffn[layer]), moe_router_group[layer], moe_router_group_b[layer],
                                 moe_router_expert[layer], moe_router_expert_b[layer],
                                 moe_w_gate[layer], moe_w_up[layer], moe_w_down[layer])
    return x
```

```python
import functools

import numpy as np
import jax
import jax.numpy as jnp
from jax import lax
from jax.experimental import pallas as pl
from jax.experimental.pallas import tpu as pltpu

F32 = jnp.float32
BF16 = jnp.bfloat16
I32 = jnp.int32

LANES = 128
NEG_INF = -1e30
Q_BLOCK = 128
MLA_HEADS = 8
MLA_Q_LORA = 256
MLA_KV_LORA = 128
MLA_NOPE = 64
MLA_ROPE = 32
MLA_V = 64
MLA_QK = MLA_NOPE + MLA_ROPE
ROPE_THETA = 10000.0
CONV_CH = 512
CONV_W = 31
NSA_HEADS = 16
NSA_G = 2
NSA_HPG = NSA_HEADS // NSA_G
NSA_DK = 64
NSA_KVW = NSA_G * NSA_DK
CMP_BLOCK = 32
CMP_STRIDE = 16
CMP_HIDDEN = 128
SLC_BLOCK = 64
SLC_TOP_N = 8
WINDOW = 256
FORCE_SCORE = 1e4
MOE_GROUPS = 4
MOE_EPG = 8
MOE_EXPERTS = MOE_GROUPS * MOE_EPG
MOE_HIDDEN = 256
MOE_ROW_BLOCK = 256

VMEM_LIMIT = 56 * 1024 * 1024


def _params(*sem):
    return pltpu.CompilerParams(dimension_semantics=sem, vmem_limit_bytes=VMEM_LIMIT)


def _full(shape):
    n = len(shape)
    return pl.BlockSpec(shape, lambda *_: (0,) * n)


def _rms(x, eps=1e-6):
    return x * lax.rsqrt(jnp.mean(x * x, axis=-1, keepdims=True) + eps)


def _dot(a, b):
    return jnp.dot(a, b, preferred_element_type=F32)


def _dot_nt(a, b, **kw):
    return lax.dot_general(a, b, (((1,), (1,)), ((), ())), preferred_element_type=F32, **kw)


def _inproj0_kernel(x_ref, pos_ref, gmix_ref, win_ref, qan_ref, wuq_ref, kvan_ref, wuk_ref, wuv_ref,
                    qg_ref, kg_ref, invf_ref, sgn_ref, q_out, k_out, v_out, u_out):
    h = _rms(x_ref[...]) * gmix_ref[...]
    proj = _dot(h.astype(BF16), win_ref[...])
    c_q = proj[:, :MLA_Q_LORA]
    c_kv = proj[:, MLA_Q_LORA:MLA_Q_LORA + MLA_KV_LORA]
    k_rope = proj[:, 384:512]
    a = proj[:, 512:512 + CONV_CH]
    g = proj[:, 512 + CONV_CH:]
    u_out[...] = a * jax.nn.sigmoid(g)
    cqn = (_rms(c_q) * qan_ref[...]).astype(BF16)
    ckvn = (_rms(c_kv) * kvan_ref[...]).astype(BF16)
    q = _dot(cqn, wuq_ref[...])
    kn = _dot(ckvn, wuk_ref[...])
    v_out[...] = _dot(ckvn, wuv_ref[...]).astype(BF16)
    ang = pos_ref[...].astype(F32) * invf_ref[...]
    cos = jnp.cos(ang)
    sin = jnp.sin(ang) * sgn_ref[...]
    lane = lax.broadcasted_iota(I32, (1, LANES), 1)
    first_half = (lane >= MLA_NOPE) & (lane < MLA_NOPE + MLA_ROPE // 2)
    scale = MLA_QK ** -0.5

    def norm_rope(t, gain):
        t = t * lax.rsqrt(jnp.sum(t * t, axis=-1, keepdims=True) * (1.0 / MLA_QK) + 1e-6) * gain
        partner = jnp.where(first_half, pltpu.roll(t, LANES - MLA_ROPE // 2, 1), pltpu.roll(t, MLA_ROPE // 2, 1))
        return t * cos + partner * sin

    for hd in range(MLA_HEADS):
        sl = slice(hd * LANES, (hd + 1) * LANES)
        q_out[:, sl] = (norm_rope(q[:, sl], qg_ref[...]) * scale).astype(BF16)
        k_out[:, sl] = norm_rope(kn[:, sl] + k_rope, kg_ref[...]).astype(BF16)


def _head_slots(w, n_heads, width, offset=0):
    k = w.shape[0]
    w = w.reshape(k, n_heads, width)
    w = jnp.pad(w, ((0, 0), (0, 0), (offset, LANES - width - offset)))
    return w.reshape(k, n_heads * LANES)


def _inproj0(x2d, pos_col, gmix, w_in, q_a_norm, w_uq, kv_a_norm, w_ukv, q_norm, k_norm, tm=256):
    T, D = x2d.shape
    H = MLA_HEADS
    w_krope = jnp.pad(w_in[:, 384:416], ((0, 0), (MLA_NOPE, LANES - MLA_NOPE - MLA_ROPE)))
    w_in_p = jnp.concatenate([w_in[:, :384], w_krope, w_in[:, 416:]], axis=1).astype(BF16)
    w_uq_p = _head_slots(w_uq, H, MLA_QK).astype(BF16)
    w_ukv3 = w_ukv.reshape(MLA_KV_LORA, H, MLA_NOPE + MLA_V)
    w_uk_p = _head_slots(w_ukv3[:, :, :MLA_NOPE].reshape(MLA_KV_LORA, H * MLA_NOPE), H, MLA_NOPE).astype(BF16)
    w_uv = w_ukv3[:, :, MLA_NOPE:].reshape(MLA_KV_LORA, H * MLA_V).astype(BF16)
    pad = LANES - MLA_QK
    qg = jnp.pad(q_norm, (0, pad)).reshape(1, LANES)
    kg = jnp.pad(k_norm, (0, pad)).reshape(1, LANES)
    half = MLA_ROPE // 2
    inv_freq = ROPE_THETA ** (-jnp.arange(half, dtype=F32) / half)
    invf = jnp.zeros((LANES,), F32).at[MLA_NOPE:MLA_NOPE + half].set(inv_freq).at[MLA_NOPE + half:MLA_QK].set(inv_freq)
    sgn = jnp.zeros((LANES,), F32).at[MLA_NOPE:MLA_NOPE + half].set(-1.0).at[MLA_NOPE + half:MLA_QK].set(1.0)
    row = lambda n: pl.BlockSpec((tm, n), lambda i: (i, 0))
    n_in = w_in_p.shape[1]
    return pl.pallas_call(
        _inproj0_kernel,
        grid=(T // tm,),
        in_specs=[row(D), row(1), _full((1, D)), _full((D, n_in)), _full((1, MLA_Q_LORA)),
                  _full((MLA_Q_LORA, H * LANES)), _full((1, MLA_KV_LORA)), _full((MLA_KV_LORA, H * LANES)),
                  _full((MLA_KV_LORA, H * MLA_V)), _full((1, LANES)), _full((1, LANES)), _full((1, LANES)),
                  _full((1, LANES))],
        out_specs=[row(H * LANES), row(H * LANES), row(H * MLA_V), row(CONV_CH)],
        out_shape=[jax.ShapeDtypeStruct((T, H * LANES), BF16), jax.ShapeDtypeStruct((T, H * LANES), BF16),
                   jax.ShapeDtypeStruct((T, H * MLA_V), BF16), jax.ShapeDtypeStruct((T, CONV_CH), F32)],
        compiler_params=_params("parallel"),
        name="inproj0",
    )(x2d, pos_col, gmix.reshape(1, D), w_in_p, q_a_norm.reshape(1, -1), w_uq_p, kv_a_norm.reshape(1, -1),
      w_uk_p, w_uv, qg, kg, invf.reshape(1, LANES), sgn.reshape(1, LANES))


def _mla_attn_kernel(q_ref, k_ref, v_ref, o_ref, *, blk):
    S = q_ref.shape[1]
    lane = lax.broadcasted_iota(I32, (1, LANES), 1)
    row = lax.broadcasted_iota(I32, (blk, 1), 0)
    col = lax.broadcasted_iota(I32, (1, blk), 1)

    def q_block(qi, carry):
        q0 = pl.multiple_of(qi * blk, blk)
        outs = []
        for hh in range(2):
            hs = slice(hh * LANES, (hh + 1) * LANES)
            q = q_ref[0, pl.ds(q0, blk), hs]

            def kv_step(j, state, masked):
                m, l, acc = state
                k0 = pl.multiple_of(j * blk, blk)
                s = _dot_nt(q, k_ref[0, pl.ds(k0, blk), hs])
                if masked:
                    s = jnp.where(row >= col, s, NEG_INF)
                m_new = jnp.maximum(m, jnp.max(s, axis=-1, keepdims=True))
                alpha = jnp.exp(m - m_new)
                p = jnp.exp(s - m_new)
                l = alpha * l + jnp.sum(p, axis=-1, keepdims=True)
                acc = alpha * acc + _dot(p.astype(BF16), v_ref[0, pl.ds(k0, blk), :])
                return m_new, l, acc

            init = (jnp.full((blk, 1), NEG_INF, F32), jnp.zeros((blk, 1), F32), jnp.zeros((blk, LANES), F32))
            state = lax.fori_loop(0, qi, functools.partial(kv_step, masked=False), init)
            m, l, acc = kv_step(qi, state, True)
            outs.append(acc / l)
        o_ref[0, pl.ds(q0, blk), :] = jnp.where(lane < MLA_V, outs[0], outs[1]).astype(BF16)
        return carry

    lax.fori_loop(0, S // blk, q_block, 0)


def _mla_attn(q, k, v, blk=256):
    B, S, _ = q.shape
    return pl.pallas_call(
        functools.partial(_mla_attn_kernel, blk=blk),
        grid=(B, MLA_HEADS // 2),
        in_specs=[pl.BlockSpec((1, S, 2 * LANES), lambda b, h: (b, 0, h)),
                  pl.BlockSpec((1, S, 2 * LANES), lambda b, h: (b, 0, h)),
                  pl.BlockSpec((1, S, 2 * MLA_V), lambda b, h: (b, 0, h))],
        out_specs=pl.BlockSpec((1, S, 2 * MLA_V), lambda b, h: (b, 0, h)),
        out_shape=jax.ShapeDtypeStruct((B, S, MLA_HEADS * MLA_V), BF16),
        compiler_params=_params("parallel", "parallel"),
        name="mla_attn",
    )(q, k, v)


CONV_TILE = 64
CONV_PAD = 32


def _conv_out0_kernel(u_ref, o_ref, x_ref, dww_ref, dwb_ref, lng_ref, lnb_ref, wo_ref, out_ref, upad, act):
    S = u_ref.shape[1]
    upad[0:CONV_PAD, :] = jnp.zeros((CONV_PAD, CONV_CH), F32)
    upad[CONV_PAD:, :] = u_ref[0]
    lead = CONV_PAD - (CONV_W - 1)

    def tile(i, carry):
        t0 = pl.multiple_of(i * CONV_TILE, CONV_TILE)
        win = upad[pl.ds(t0, CONV_TILE + CONV_PAD), :]
        acc = jnp.zeros((CONV_TILE, CONV_CH), F32) + dwb_ref[...]
        for j in range(CONV_W):
            acc = acc + win[lead + j:lead + j + CONV_TILE, :] * dww_ref[j:j + 1, :]
        mu = jnp.mean(acc, axis=-1, keepdims=True)
        xc = acc - mu
        var = jnp.mean(xc * xc, axis=-1, keepdims=True)
        y = xc * lax.rsqrt(var + 1e-5) * lng_ref[...] + lnb_ref[...]
        act[pl.ds(t0, CONV_TILE), :] = (y * jax.nn.sigmoid(y)).astype(BF16)
        return carry

    lax.fori_loop(0, S // CONV_TILE, tile, 0)
    n_o = o_ref.shape[2]
    rows = 512
    for r in range(S // rows):
        rs = slice(r * rows, (r + 1) * rows)
        y = _dot(o_ref[0, rs, :], wo_ref[:n_o, :]) + _dot(act[rs, :], wo_ref[n_o:, :])
        out_ref[0, rs, :] = x_ref[0, rs, :] + y


def _conv_out0(u, o_mla, x, dw_w, dw_b, ln_g, ln_b, w_out):
    B, S, D = x.shape
    n_o = o_mla.shape[2]
    bspec = lambda n: pl.BlockSpec((1, S, n), lambda b: (b, 0, 0))
    return pl.pallas_call(
        _conv_out0_kernel,
        grid=(B,),
        in_specs=[bspec(CONV_CH), bspec(n_o), bspec(D), _full((CONV_W, CONV_CH)), _full((1, CONV_CH)),
                  _full((1, CONV_CH)), _full((1, CONV_CH)), _full((n_o + CONV_CH, D))],
        out_specs=bspec(D),
        out_shape=jax.ShapeDtypeStruct((B, S, D), F32),
        scratch_shapes=[pltpu.VMEM((S + CONV_PAD, CONV_CH), F32), pltpu.VMEM((S, CONV_CH), BF16)],
        compiler_params=_params("parallel"),
        name="conv_out0",
    )(u, o_mla, x, dw_w, dw_b.reshape(1, -1), ln_g.reshape(1, -1), ln_b.reshape(1, -1), w_out.astype(BF16))


def _mixer0(x, positions, gmix, w_in, q_a_norm, w_uq, kv_a_norm, w_ukv, q_norm, k_norm,
            dw_w, dw_b, ln_g, ln_b, w_out):
    B, S, D = x.shape
    T = B * S
    q, k, v, u = _inproj0(x.reshape(T, D), positions.reshape(T, 1), gmix, w_in, q_a_norm, w_uq, kv_a_norm, w_ukv,
                          q_norm, k_norm)
    o = _mla_attn(q.reshape(B, S, -1), k.reshape(B, S, -1), v.reshape(B, S, -1))
    return _conv_out0(u.reshape(B, S, -1), o, x, dw_w, dw_b, ln_g, ln_b, w_out)


ROUTE_TILE = 512
ROUTE_ROWS = 40
SLABS = 8


def _route_kernel(x_ref, g_ref, wr_ref, br_ref, tri_ref, hn_ref, oi_ref, of_ref, cnt_ref, carry):
    @pl.when(pl.program_id(0) == 0)
    def _():
        carry[...] = jnp.zeros_like(carry)

    hn = _rms(x_ref[...]) * g_ref[...]
    for j in range(SLABS):
        hn_ref[:, j, :] = hn[:, j * LANES:(j + 1) * LANES]
    tm = hn.shape[0]
    logits = _dot_nt(wr_ref[...], hn, precision=lax.Precision.HIGHEST) + br_ref[...]
    gl = logits[MOE_EXPERTS:MOE_EXPERTS + MOE_GROUPS]
    rid_g = lax.broadcasted_iota(I32, (MOE_GROUPS, tm), 0)
    gmax = jnp.max(gl, axis=0, keepdims=True)
    grp = jnp.min(jnp.where(gl == gmax, rid_g, MOE_GROUPS), axis=0, keepdims=True)
    g_w = 1.0 / jnp.sum(jnp.exp(gl - gmax), axis=0, keepdims=True)
    e_in = jnp.zeros((MOE_EPG, tm), F32)
    for gi in range(MOE_GROUPS):
        e_in = jnp.where(grp == gi, logits[gi * MOE_EPG:(gi + 1) * MOE_EPG], e_in)
    rid_e = lax.broadcasted_iota(I32, (MOE_EPG, tm), 0)
    v1 = jnp.max(e_in, axis=0, keepdims=True)
    i1 = jnp.min(jnp.where(e_in == v1, rid_e, MOE_EPG), axis=0, keepdims=True)
    rest = jnp.where(rid_e == i1, -jnp.inf, e_in)
    v2 = jnp.max(rest, axis=0, keepdims=True)
    i2 = jnp.min(jnp.where(rest == v2, rid_e, MOE_EPG), axis=0, keepdims=True)
    ex = jnp.exp(v2 - v1)
    den = 1.0 + ex
    e1 = grp * MOE_EPG + i1
    e2 = grp * MOE_EPG + i2
    rid = lax.broadcasted_iota(I32, (MOE_EXPERTS, tm), 0)
    hit1 = rid == e1
    hit2 = rid == e2
    member = jnp.where(hit1 | hit2, 1.0, 0.0)
    before = _dot(member.astype(BF16), tri_ref[...]) + carry[...]
    r1 = jnp.sum(jnp.where(hit1, before, 0.0), axis=0, keepdims=True)
    r2 = jnp.sum(jnp.where(hit2, before, 0.0), axis=0, keepdims=True)
    carry[...] = carry[...] + jnp.sum(member, axis=1, keepdims=True)
    oi_ref[...] = jnp.zeros_like(oi_ref)
    oi_ref[0:1, :] = e1
    oi_ref[1:2, :] = e2
    oi_ref[2:3, :] = r1.astype(I32)
    oi_ref[3:4, :] = r2.astype(I32)
    of_ref[...] = jnp.zeros_like(of_ref)
    of_ref[0:1, :] = g_w / den
    of_ref[1:2, :] = g_w * ex / den
    cnt_ref[...] = jnp.broadcast_to(carry[...], cnt_ref.shape).astype(I32)


def _route(x2d, gain, router_group, router_group_b, router_expert, router_expert_b):
    T, D = x2d.shape
    tm = ROUTE_TILE
    pad = ROUTE_ROWS - MOE_EXPERTS - MOE_GROUPS
    wr = jnp.concatenate([router_expert.T, router_group.T, jnp.zeros((pad, D), F32)], axis=0)
    br = jnp.concatenate([router_expert_b, router_group_b, jnp.zeros((pad,), F32)]).reshape(ROUTE_ROWS, 1)
    tri = (jnp.arange(tm)[:, None] < jnp.arange(tm)[None, :]).astype(BF16)
    return pl.pallas_call(
        _route_kernel,
        grid=(T // tm,),
        in_specs=[pl.BlockSpec((tm, D), lambda i: (i, 0)), _full((1, D)), _full((ROUTE_ROWS, D)),
                  _full((ROUTE_ROWS, 1)), _full((tm, tm))],
        out_specs=[pl.BlockSpec((tm, SLABS, LANES), lambda i: (i, 0, 0)), pl.BlockSpec((8, tm), lambda i: (0, i)),
                   pl.BlockSpec((8, tm), lambda i: (0, i)), _full((MOE_EXPERTS, LANES))],
        out_shape=[jax.ShapeDtypeStruct((T, SLABS, LANES), F32), jax.ShapeDtypeStruct((8, T), I32),
                   jax.ShapeDtypeStruct((8, T), F32), jax.ShapeDtypeStruct((MOE_EXPERTS, LANES), I32)],
        scratch_shapes=[pltpu.VMEM((MOE_EXPERTS, 1), F32)],
        compiler_params=_params("arbitrary"),
        name="moe_route",
    )(x2d, gain.reshape(1, D), wr, br, tri)


DISPATCH_TILE = 512


def _dispatch_kernel(dest_ref, hn_ref, xs_init_ref, xs_ref, sem):
    del xs_init_ref
    tm = hn_ref.shape[0]
    base = pl.program_id(0) * (2 * tm)

    def copy(t, k):
        return pltpu.make_async_copy(hn_ref.at[t], xs_ref.at[dest_ref[base + 2 * t + k]], sem)

    def issue(t, carry):
        copy(t, 0).start()
        copy(t, 1).start()
        return carry

    def drain(t, carry):
        copy(t, 0).wait()
        copy(t, 1).wait()
        return carry

    lax.fori_loop(0, tm, issue, 0)
    lax.fori_loop(0, tm, drain, 0)


def _dispatch(dest, hn3, n_rows):
    T = hn3.shape[0]
    tm = DISPATCH_TILE
    return pl.pallas_call(
        _dispatch_kernel,
        grid_spec=pltpu.PrefetchScalarGridSpec(
            num_scalar_prefetch=1,
            grid=(T // tm,),
            in_specs=[pl.BlockSpec((tm, SLABS, LANES), lambda i, d: (i, 0, 0)), pl.BlockSpec(memory_space=pl.ANY)],
            out_specs=pl.BlockSpec(memory_space=pl.ANY),
            scratch_shapes=[pltpu.SemaphoreType.DMA(())],
        ),
        out_shape=jax.ShapeDtypeStruct((n_rows, SLABS, LANES), F32),
        input_output_aliases={2: 0},
        compiler_params=_params("arbitrary"),
        name="moe_dispatch",
    )(dest, hn3, jnp.zeros((n_rows, SLABS, LANES), F32))


def _expert_kernel(blk_e_ref, n_used_ref, xs_ref, wgu_ref, wd_ref, ys_ref):
    del blk_e_ref

    @pl.when(pl.program_id(0) < n_used_ref[0])
    def _():
        xrow = jnp.concatenate([xs_ref[:, j, :] for j in range(SLABS)], axis=1).astype(BF16)
        gu = _dot(xrow, wgu_ref[0])
        gate = gu[:, :MOE_HIDDEN]
        hid = (gate * jax.nn.sigmoid(gate) * gu[:, MOE_HIDDEN:]).astype(BF16)
        y = _dot(hid, wd_ref[0])
        for j in range(SLABS):
            ys_ref[:, j, :] = y[:, j * LANES:(j + 1) * LANES]

    @pl.when(pl.program_id(0) >= n_used_ref[0])
    def _():
        ys_ref[...] = jnp.zeros_like(ys_ref)


def _experts(blk_expert, n_used, xs, w_gu, w_d):
    R = xs.shape[0]
    rb = MOE_ROW_BLOCK
    D = w_gu.shape[1]
    return pl.pallas_call(
        _expert_kernel,
        grid_spec=pltpu.PrefetchScalarGridSpec(
            num_scalar_prefetch=2,
            grid=(R // rb,),
            in_specs=[pl.BlockSpec((rb, SLABS, LANES), lambda b, be, nu: (jnp.minimum(b, nu[0] - 1), 0, 0)),
                      pl.BlockSpec((1, D, 2 * MOE_HIDDEN), lambda b, be, nu: (be[b], 0, 0)),
                      pl.BlockSpec((1, MOE_HIDDEN, D), lambda b, be, nu: (be[b], 0, 0))],
            out_specs=pl.BlockSpec((rb, SLABS, LANES), lambda b, be, nu: (b, 0, 0)),
        ),
        out_shape=jax.ShapeDtypeStruct((R, SLABS, LANES), F32),
        compiler_params=_params("arbitrary"),
        name="moe_experts",
    )(blk_expert, n_used, xs, w_gu, w_d)


def _combine_kernel(dest_ref, x_ref, gate_ref, ys_ref, out_ref, buf0, buf1, sem):
    tm = x_ref.shape[0]
    base = pl.program_id(0) * (2 * tm)

    def copy(t, k):
        return pltpu.make_async_copy(ys_ref.at[dest_ref[base + 2 * t + k]], (buf0, buf1)[k].at[t], sem)

    def issue(t, carry):
        copy(t, 0).start()
        copy(t, 1).start()
        return carry

    def drain(t, carry):
        copy(t, 0).wait()
        copy(t, 1).wait()
        return carry

    lax.fori_loop(0, tm, issue, 0)
    lax.fori_loop(0, tm, drain, 0)
    g0 = gate_ref[:, 0:1]
    g1 = gate_ref[:, 1:2]
    for j in range(SLABS):
        sl = slice(j * LANES, (j + 1) * LANES)
        out_ref[:, sl] = x_ref[:, sl] + g0 * buf0[:, j, :] + g1 * buf1[:, j, :]


def _combine(dest, x2d, gates_col, ys):
    T, D = x2d.shape
    tm = DISPATCH_TILE
    return pl.pallas_call(
        _combine_kernel,
        grid_spec=pltpu.PrefetchScalarGridSpec(
            num_scalar_prefetch=1,
            grid=(T // tm,),
            in_specs=[pl.BlockSpec((tm, D), lambda i, d: (i, 0)), pl.BlockSpec((tm, 2), lambda i, d: (i, 0)),
                      pl.BlockSpec(memory_space=pl.ANY)],
            out_specs=pl.BlockSpec((tm, D), lambda i, d: (i, 0)),
            scratch_shapes=[pltpu.VMEM((tm, SLABS, LANES), F32), pltpu.VMEM((tm, SLABS, LANES), F32),
                            pltpu.SemaphoreType.DMA(())],
        ),
        out_shape=jax.ShapeDtypeStruct((T, D), F32),
        compiler_params=_params("arbitrary"),
        name="moe_combine",
    )(dest, x2d, gates_col, ys)


def _moe(x, gain, router_group, router_group_b, router_expert, router_expert_b, w_gate, w_up, w_down):
    B, S, D = x.shape
    T = B * S
    x2d = x.reshape(T, D)
    hn3, oi, of, cnt = _route(x2d, gain, router_group, router_group_b, router_expert, router_expert_b)
    rb = MOE_ROW_BLOCK
    counts = cnt[:, 0]
    cap = (counts + rb - 1) // rb * rb
    cap_end = jnp.cumsum(cap)
    start = cap_end - cap
    dest = (start[oi[0:2]] + oi[2:4]).T.reshape(2 * T).astype(I32)
    n_rows = 2 * T + MOE_EXPERTS * rb
    n_blk = n_rows // rb
    blk_first_row = jnp.arange(n_blk, dtype=I32) * rb
    blk_expert = jnp.minimum(jnp.sum(cap_end[None, :] <= blk_first_row[:, None], axis=1), MOE_EXPERTS - 1)
    n_used = (cap_end[-1:] // rb).astype(I32)
    xs = _dispatch(dest, hn3, n_rows)
    w_gu = jnp.concatenate([w_gate, w_up], axis=2).astype(BF16)
    ys = _experts(blk_expert.astype(I32), n_used, xs, w_gu, w_down.astype(BF16))
    out = _combine(dest, x2d, of[0:2].T, ys)
    return out.reshape(B, S, D)


NSA_TQ = 128
NSA_PAIRS = NSA_HPG // 2
N_SEL = 32


def _pair_norm(t, gain2, lane_lo):
    t2 = t * t
    s_lo = jnp.sum(jnp.where(lane_lo, t2, 0.0), axis=-1, keepdims=True)
    s_hi = jnp.sum(jnp.where(lane_lo, 0.0, t2), axis=-1, keepdims=True)
    inv = jnp.where(lane_lo, lax.rsqrt(s_lo * (1.0 / NSA_DK) + 1e-6), lax.rsqrt(s_hi * (1.0 / NSA_DK) + 1e-6))
    return t * inv * gain2


def _inproj1_kernel(x_ref, g_ref, win_ref, qg_ref, ksg_ref, kwg_ref,
                    q_out, kc_out, vc_out, ks_out, vs_out, kw_out, vw_out, gate_out):
    h = _rms(x_ref[...]) * g_ref[...]
    proj = _dot(h.astype(BF16), win_ref[...])
    lane_lo = lax.broadcasted_iota(I32, (1, LANES), 1) < NSA_DK
    nq = NSA_HEADS * NSA_DK
    for p in range(nq // LANES):
        sl = slice(p * LANES, (p + 1) * LANES)
        q_out[:, sl] = (_pair_norm(proj[:, sl], qg_ref[...], lane_lo) * (NSA_DK ** -0.5)).astype(BF16)
    part = lambda i: proj[:, nq + i * LANES:nq + (i + 1) * LANES]
    kc_out[...] = part(0)
    vc_out[...] = part(1)

    def lo_hi(t):
        r = pltpu.roll(t, NSA_DK, 1)
        return jnp.concatenate([jnp.where(lane_lo, t, 0.0), jnp.where(lane_lo, 0.0, r),
                                jnp.where(lane_lo, r, 0.0), jnp.where(lane_lo, 0.0, t)], axis=1).astype(BF16)

    ks_out[...] = lo_hi(_pair_norm(part(2), ksg_ref[...], lane_lo))
    vs_out[...] = lo_hi(part(3))
    kw_out[...] = lo_hi(_pair_norm(part(4), kwg_ref[...], lane_lo))
    vw_out[...] = lo_hi(part(5))
    gate_out[...] = jax.nn.sigmoid(part(6))


def _inproj1(x2d, gain, w_in, q_norm, k_norm_slc, k_norm_win, tm=256):
    T, D = x2d.shape
    n_in = w_in.shape[1]
    n_pad = -n_in % LANES
    w_in_p = jnp.pad(w_in, ((0, 0), (0, n_pad))).astype(BF16)
    two = lambda g: jnp.tile(g, 2).reshape(1, LANES)
    row = lambda n: pl.BlockSpec((tm, n), lambda i: (i, 0))
    nq = NSA_HEADS * NSA_DK
    bf = lambda n: jax.ShapeDtypeStruct((T, n), BF16)
    f32 = lambda n: jax.ShapeDtypeStruct((T, n), F32)
    return pl.pallas_call(
        _inproj1_kernel,
        grid=(T // tm,),
        in_specs=[row(D), _full((1, D)), _full((D, n_in + n_pad)), _full((1, LANES)), _full((1, LANES)),
                  _full((1, LANES))],
        out_specs=[row(nq), row(LANES), row(LANES), row(4 * LANES), row(4 * LANES), row(4 * LANES), row(4 * LANES),
                   row(LANES)],
        out_shape=[bf(nq), f32(LANES), f32(LANES), bf(4 * LANES), bf(4 * LANES), bf(4 * LANES), bf(4 * LANES),
                   f32(LANES)],
        compiler_params=_params("parallel"),
        name="inproj1",
    )(x2d, gain.reshape(1, D), w_in_p, two(q_norm), two(k_norm_slc), two(k_norm_win))


def _stack_pairs(q_ref, rows):
    return jnp.concatenate([q_ref[0, rows, p * LANES:(p + 1) * LANES] for p in range(NSA_PAIRS)], axis=0)


def _gate_tile(gt, branch, p, lane_lo):
    c = branch * NSA_HPG + 2 * p
    return jnp.where(lane_lo, gt[:, c:c + 1], gt[:, c + 1:c + 2])


def _cmp_kernel(q_ref, kch_ref, vch_ref, pos3_ref, pend_ref, gate_ref, slope_ref,
                pek_ref, w1k_ref, w2k_ref, kcg_ref, pev_ref, w1v_ref, w2v_ref, ovl_ref, eye_ref,
                o_ref, sel_ref, kc_s, vc_s):
    tq = NSA_TQ
    S = q_ref.shape[1]

    def compress(ch_ref, pe_ref, w1_ref, w2_ref):
        a = ch_ref[0, 0]
        h_lo = _dot((a + pe_ref[0:1, :]).astype(BF16), w1_ref[0])
        h_hi = _dot((a + pe_ref[1:2, :]).astype(BF16), w1_ref[1])
        n = h_hi.shape[0]
        hid = jax.nn.gelu(h_lo + pltpu.roll(h_hi, n - 1, 0)).astype(BF16)
        return _dot(hid, w2_ref[0]), _dot(hid, w2_ref[1])

    k_lo, k_hi = compress(kch_ref, pek_ref, w1k_ref, w2k_ref)
    for e, kk in enumerate((k_lo, k_hi)):
        kk = kk * lax.rsqrt(jnp.sum(kk * kk, axis=-1, keepdims=True) * (1.0 / NSA_DK) + 1e-6) * kcg_ref[e:e + 1, :]
        kc_s[e] = kk.astype(BF16)
    v_lo, v_hi = compress(vch_ref, pev_ref, w1v_ref, w2v_ref)
    vc_s[0] = v_lo.astype(BF16)
    vc_s[1] = v_hi.astype(BF16)

    n_cmp = (S - CMP_BLOCK) // CMP_STRIDE + 1
    lane = lax.broadcasted_iota(I32, (1, LANES), 1)
    lane_lo = lane < NSA_DK
    blk_row = lax.broadcasted_iota(I32, (N_SEL, 1), 0)

    def q_block(qi, carry):
        t0 = pl.multiple_of(qi * tq, tq)
        rows = pl.ds(t0, tq)
        qs = _stack_pairs(q_ref, rows)
        pos_q0 = pos3_ref[0, pl.ds(qi, 1), :][:, 0:1]
        posrel = (pend_ref[0] - pos_q0).astype(F32)
        tcol = t0 + lax.broadcasted_iota(I32, (tq, 1), 0)
        valid = ((tcol >= CMP_STRIDE * lane + (CMP_BLOCK - 1)) & (lane < n_cmp))[None]
        psum = jnp.zeros((tq, LANES), F32)
        o_pairs = jnp.zeros((NSA_PAIRS * tq, LANES), F32)
        for e in range(2):
            s = _dot_nt(qs, kc_s[e]).reshape(NSA_PAIRS, tq, LANES)
            s = s + slope_ref[0, NSA_PAIRS * e:NSA_PAIRS * (e + 1)][:, None, :] * posrel[None]
            s = jnp.where(valid, s, NEG_INF)
            m = jnp.max(s, axis=-1, keepdims=True)
            p = jnp.where(valid, jnp.exp(s - m), 0.0)
            p = p / jnp.maximum(jnp.sum(p, axis=-1, keepdims=True), 1e-20)
            psum = psum + jnp.sum(p, axis=0)
            o_pairs = o_pairs + _dot(p.reshape(NSA_PAIRS * tq, LANES).astype(BF16), vc_s[e])
        imp = _dot_nt(ovl_ref[...], psum, precision=lax.Precision.HIGHEST)
        cur = (t0 + lax.broadcasted_iota(I32, (1, tq), 1)) // SLC_BLOCK
        forced = (blk_row == 0) | (blk_row == cur) | (blk_row == cur - 1)
        imp = jnp.where(forced, FORCE_SCORE, jnp.where(blk_row <= cur, imp, -1.0))
        rank = jnp.zeros((N_SEL, tq), I32)
        for i in range(N_SEL):
            ri = imp[i:i + 1, :]
            rank = rank + jnp.where((ri > imp) | ((ri == imp) & (blk_row > i)), 1, 0)
        sel_t = jnp.where(rank < SLC_TOP_N, 1.0, 0.0).astype(BF16)
        sel_ref[0, 0, rows, :] = _dot_nt(eye_ref[...], sel_t).astype(BF16)
        gt = gate_ref[0, 0, rows, :]
        o3 = o_pairs.reshape(NSA_PAIRS, tq, LANES)
        for p in range(NSA_PAIRS):
            o_ref[0, rows, p * LANES:(p + 1) * LANES] = (o3[p] * _gate_tile(gt, 0, p, lane_lo)).astype(BF16)
        return carry

    lax.fori_loop(0, S // tq, q_block, 0)


def _lo_hi_cols(w):
    z = jnp.zeros_like(w)
    return jnp.stack([jnp.concatenate([w, z], axis=1), jnp.concatenate([z, w], axis=1)])


def _nsa_cmp(q, kch, vch, pos3, pend, gates_g, slopes_t, pe_k, w1_k, w2_k, k_norm_cmp, pe_v, w1_v, w2_v):
    B, S, _ = q.shape
    G = NSA_G
    nch = S // CMP_STRIDE
    half = CMP_STRIDE * NSA_DK
    pe2 = lambda pe: pe.reshape(2, half)
    w1_2 = lambda w: w.reshape(2, half, CMP_HIDDEN).astype(BF16)
    n_sel = S // SLC_BLOCK
    n_cmp = (S - CMP_BLOCK) // CMP_STRIDE + 1
    cmp_start = np.arange(LANES) * CMP_STRIDE
    slc_start = np.arange(n_sel) * SLC_BLOCK
    overlap = ((cmp_start[None, :] < slc_start[:, None] + SLC_BLOCK) & (cmp_start[None, :] + CMP_BLOCK > slc_start[:, None])
               & (np.arange(LANES)[None, :] < n_cmp))
    ovl = jnp.asarray(overlap, F32)
    eye = jnp.eye(NSA_TQ, dtype=BF16)
    qspec = pl.BlockSpec((1, S, NSA_PAIRS * LANES), lambda b, g: (b, 0, g))
    chspec = pl.BlockSpec((1, 1, nch, half), lambda b, g: (b, g, 0, 0))
    return pl.pallas_call(
        _cmp_kernel,
        grid=(B, G),
        in_specs=[qspec, chspec, chspec,
                  pl.BlockSpec((1, S // LANES, LANES), lambda b, g: (b, 0, 0)),
                  pl.BlockSpec((1, 1, LANES), lambda b, g: (b, 0, 0)),
                  pl.BlockSpec((1, 1, S, 3 * NSA_HPG), lambda b, g: (b, g, 0, 0)),
                  pl.BlockSpec((1, NSA_HPG, LANES), lambda b, g: (g, 0, 0)),
                  _full((2, half)), _full((2, half, CMP_HIDDEN)), _full((2, CMP_HIDDEN, LANES)), _full((2, LANES)),
                  _full((2, half)), _full((2, half, CMP_HIDDEN)), _full((2, CMP_HIDDEN, LANES)),
                  _full((n_sel, LANES)), _full((NSA_TQ, NSA_TQ))],
        out_specs=[qspec, pl.BlockSpec((1, 1, S, n_sel), lambda b, g: (b, g, 0, 0))],
        out_shape=[jax.ShapeDtypeStruct((B, S, NSA_HEADS * NSA_DK), BF16),
                   jax.ShapeDtypeStruct((B, G, S, n_sel), BF16)],
        scratch_shapes=[pltpu.VMEM((2, nch, LANES), BF16), pltpu.VMEM((2, nch, LANES), BF16)],
        compiler_params=_params("parallel", "parallel"),
        name="nsa_cmp",
    )(q, kch, vch, pos3, pend, gates_g, slopes_t, pe2(pe_k), w1_2(w1_k), _lo_hi_cols(w2_k).astype(BF16),
      _lo_hi_cols(k_norm_cmp.reshape(1, -1)).reshape(2, LANES), pe2(pe_v), w1_2(w1_v),
      _lo_hi_cols(w2_v).astype(BF16), ovl, eye)


def _slc_win_kernel(q_ref, ks_ref, vs_ref, kw_ref, vw_ref, sel_ref, bmap_ref, pos3_ref, gate_ref, slope_ref,
                    ocmp_ref, o_ref):
    tq = NSA_TQ
    S = q_ref.shape[1]
    lane = lax.broadcasted_iota(I32, (1, LANES), 1)
    lane_lo = lane < NSA_DK

    def q_block(qi, carry):
        t0 = pl.multiple_of(qi * tq, tq)
        rows = pl.ds(t0, tq)
        qs = _stack_pairs(q_ref, rows)
        pos_q0 = pos3_ref[0, pl.ds(qi, 1), :][:, 0:1]
        tcol = t0 + lax.broadcasted_iota(I32, (tq, 1), 0)
        sel = sel_ref[0, 0, rows, :]

        def slc_mask(j):
            chosen = _dot(sel, bmap_ref[j]) > 0.5
            return chosen & (j * tq + lane <= tcol)

        def win_mask(j):
            kidx = j * tq + lane
            rel = tcol - kidx
            return (kidx >= 0) & (rel >= 0) & (rel < WINDOW)

        def attend(k_ref, v_ref, e, lo, hi, mask_fn):
            cols = slice(e * LANES, (e + 1) * LANES)
            slopes = slope_ref[0, NSA_PAIRS * e:NSA_PAIRS * (e + 1)][:, None, :]

            def step(j, state):
                m, l, acc = state
                jc = jnp.maximum(j, 0)
                k0 = pl.multiple_of(jc * tq, tq)
                posrel = (pos3_ref[0, pl.ds(jc, 1), :] - pos_q0).astype(F32)
                s = _dot_nt(qs, k_ref[0, pl.ds(k0, tq), cols]).reshape(NSA_PAIRS, tq, LANES) + slopes * posrel[None]
                s = jnp.where(mask_fn(j)[None], s, NEG_INF)
                m_new = jnp.maximum(m, jnp.max(s, axis=-1, keepdims=True))
                alpha = jnp.exp(m - m_new)
                p = jnp.exp(s - m_new)
                l = alpha * l + jnp.sum(p, axis=-1, keepdims=True)
                pv = _dot(p.reshape(NSA_PAIRS * tq, LANES).astype(BF16), v_ref[0, pl.ds(k0, tq), cols])
                acc = alpha * acc + pv.reshape(NSA_PAIRS, tq, LANES)
                return m_new, l, acc

            init = (jnp.full((NSA_PAIRS, tq, 1), NEG_INF, F32), jnp.zeros((NSA_PAIRS, tq, 1), F32),
                    jnp.zeros((NSA_PAIRS, tq, LANES), F32))
            m, l, acc = lax.fori_loop(lo, hi, step, init)
            return acc / l

        slc = attend(ks_ref, vs_ref, 0, 0, qi + 1, slc_mask) + attend(ks_ref, vs_ref, 1, 0, qi + 1, slc_mask)
        win = (attend(kw_ref, vw_ref, 0, qi - WINDOW // tq, qi + 1, win_mask)
               + attend(kw_ref, vw_ref, 1, qi - WINDOW // tq, qi + 1, win_mask))
        gt = gate_ref[0, 0, rows, :]
        for p in range(NSA_PAIRS):
            sl = slice(p * LANES, (p + 1) * LANES)
            o = (ocmp_ref[0, rows, sl].astype(F32) + slc[p] * _gate_tile(gt, 1, p, lane_lo)
                 + win[p] * _gate_tile(gt, 2, p, lane_lo))
            o_ref[0, rows, sl] = o.astype(BF16)
        return carry

    lax.fori_loop(0, S // tq, q_block, 0)


def _nsa_slc_win(q, ks4, vs4, kw4, vw4, sel, pos3, gates_g, slopes_t, o_cmp):
    B, S, _ = q.shape
    n_sel = S // SLC_BLOCK
    nch = S // NSA_TQ
    key_blk = np.arange(S) // SLC_BLOCK
    bmap = jnp.asarray((np.arange(n_sel)[:, None] == key_blk[None, :]).reshape(n_sel, nch, NSA_TQ).transpose(1, 0, 2), BF16)
    qspec = pl.BlockSpec((1, S, NSA_PAIRS * LANES), lambda b, g: (b, 0, g))
    kvspec = pl.BlockSpec((1, S, 2 * LANES), lambda b, g: (b, 0, g))
    return pl.pallas_call(
        _slc_win_kernel,
        grid=(B, NSA_G),
        in_specs=[qspec, kvspec, kvspec, kvspec, kvspec,
                  pl.BlockSpec((1, 1, S, n_sel), lambda b, g: (b, g, 0, 0)),
                  _full((nch, n_sel, NSA_TQ)),
                  pl.BlockSpec((1, S // LANES, LANES), lambda b, g: (b, 0, 0)),
                  pl.BlockSpec((1, 1, S, 3 * NSA_HPG), lambda b, g: (b, g, 0, 0)),
                  pl.BlockSpec((1, NSA_HPG, LANES), lambda b, g: (g, 0, 0)),
                  qspec],
        out_specs=qspec,
        out_shape=jax.ShapeDtypeStruct((B, S, NSA_HEADS * NSA_DK), BF16),
        compiler_params=_params("parallel", "parallel"),
        name="nsa_slc_win",
    )(q, ks4, vs4, kw4, vw4, sel, bmap, pos3, gates_g, slopes_t, o_cmp)


def _outproj_kernel(o_ref, x_ref, w_ref, out_ref):
    out_ref[...] = x_ref[...] + _dot(o_ref[...], w_ref[...])


def _outproj(o2d, x2d, w_out, tm=512):
    T, D = x2d.shape
    K = o2d.shape[1]
    return pl.pallas_call(
        _outproj_kernel,
        grid=(T // tm,),
        in_specs=[pl.BlockSpec((tm, K), lambda i: (i, 0)), pl.BlockSpec((tm, D), lambda i: (i, 0)), _full((K, D))],
        out_specs=pl.BlockSpec((tm, D), lambda i: (i, 0)),
        out_shape=jax.ShapeDtypeStruct((T, D), F32),
        compiler_params=_params("parallel"),
        name="outproj1",
    )(o2d, x2d, w_out.astype(BF16))


def _mixer1(x, positions, gain, w_in, q_norm, k_norm_cmp, k_norm_slc, k_norm_win,
            pe_k, w1_k, w2_k, pe_v, w1_v, w2_v, w_out):
    B, S, D = x.shape
    T = B * S
    G, HPG, DK = NSA_G, NSA_HPG, NSA_DK
    assert S // SLC_BLOCK == N_SEL and S % NSA_TQ == 0 and NSA_TQ == LANES
    x2d = x.reshape(T, D)
    q, kc, vc, ks4, vs4, kw4, vw4, gates = _inproj1(x2d, gain, w_in, q_norm, k_norm_slc, k_norm_win)
    b3 = lambda t: t.reshape(B, S, -1)
    chunks = lambda t: t.reshape(B, S, G, DK).transpose(0, 2, 1, 3).reshape(B, G, S // CMP_STRIDE, CMP_STRIDE * DK)
    gates_g = gates[:, :3 * NSA_HEADS].reshape(B, S, 3, G, HPG).transpose(0, 3, 1, 2, 4).reshape(B, G, S, 3 * HPG)
    n_cmp = (S - CMP_BLOCK) // CMP_STRIDE + 1
    pend = jnp.pad(positions[:, CMP_BLOCK - 1::CMP_STRIDE][:, :n_cmp], ((0, 0), (0, LANES - n_cmp))).reshape(B, 1, LANES)
    pos3 = positions.reshape(B, S // LANES, LANES)
    slopes = jnp.asarray(2.0 ** (-8.0 * np.arange(1, NSA_HEADS + 1) / NSA_HEADS), F32).reshape(G, NSA_PAIRS, 2)
    slopes_t = jnp.broadcast_to(slopes.transpose(0, 2, 1).reshape(G, HPG, 1), (G, HPG, LANES))
    o_cmp, sel = _nsa_cmp(b3(q), chunks(kc), chunks(vc), pos3, pend, gates_g, slopes_t,
                          pe_k, w1_k, w2_k, k_norm_cmp, pe_v, w1_v, w2_v)
    o = _nsa_slc_win(b3(q), b3(ks4), b3(vs4), b3(kw4), b3(vw4), sel, pos3, gates_g, slopes_t, o_cmp)
    return _outproj(o.reshape(T, -1), x2d, w_out).reshape(B, S, D)


def kernel(x, positions, norm_mix, norm_ffn, mix0_w_in, mla_q_a_norm, mla_w_uq, mla_kv_a_norm, mla_w_ukv, mla_q_norm, mla_k_norm, conv_dw_w, conv_dw_b, conv_ln_g, conv_ln_b, mix0_w_out, nsa_w_in, nsa_q_norm, nsa_k_norm_cmp, nsa_k_norm_slc, nsa_k_norm_win, nsa_cmp_pe_k, nsa_cmp_w1_k, nsa_cmp_w2_k, nsa_cmp_pe_v, nsa_cmp_w1_v, nsa_cmp_w2_v, nsa_w_out, moe_router_group, moe_router_group_b, moe_router_expert, moe_router_expert_b, moe_w_gate, moe_w_up, moe_w_down):
    def moe(x, layer):
        return _moe(x, norm_ffn[layer], moe_router_group[layer], moe_router_group_b[layer], moe_router_expert[layer],
                    moe_router_expert_b[layer], moe_w_gate[layer], moe_w_up[layer], moe_w_down[layer])

    x = _mixer0(x, positions, norm_mix[0], mix0_w_in[0], mla_q_a_norm[0], mla_w_uq[0], mla_kv_a_norm[0], mla_w_ukv[0],
                mla_q_norm[0], mla_k_norm[0], conv_dw_w[0], conv_dw_b[0], conv_ln_g[0], conv_ln_b[0], mix0_w_out[0])
    x = moe(x, 0)
    x = _mixer1(x, positions, norm_mix[1], nsa_w_in[0], nsa_q_norm[0], nsa_k_norm_cmp[0], nsa_k_norm_slc[0],
                nsa_k_norm_win[0], nsa_cmp_pe_k[0], nsa_cmp_w1_k[0], nsa_cmp_w2_k[0], nsa_cmp_pe_v[0],
                nsa_cmp_w1_v[0], nsa_cmp_w2_v[0], nsa_w_out[0])
    return moe(x, 1)
```

```python
import functools

import numpy as np
import jax
import jax.numpy as jnp
from jax import lax
from jax.experimental import pallas as pl
from jax.experimental.pallas import tpu as pltpu

F32 = jnp.float32
BF16 = jnp.bfloat16
I32 = jnp.int32

LANES = 128
NEG_INF = -1e30
Q_BLOCK = 128
MLA_HEADS = 8
MLA_Q_LORA = 256
MLA_KV_LORA = 128
MLA_NOPE = 64
MLA_ROPE = 32
MLA_V = 64
MLA_QK = MLA_NOPE + MLA_ROPE
ROPE_THETA = 10000.0
CONV_CH = 512
CONV_W = 31
NSA_HEADS = 16
NSA_G = 2
NSA_HPG = NSA_HEADS // NSA_G
NSA_DK = 64
NSA_KVW = NSA_G * NSA_DK
CMP_BLOCK = 32
CMP_STRIDE = 16
CMP_HIDDEN = 128
SLC_BLOCK = 64
SLC_TOP_N = 8
WINDOW = 256
FORCE_SCORE = 1e4
MOE_GROUPS = 4
MOE_EPG = 8
MOE_EXPERTS = MOE_GROUPS * MOE_EPG
MOE_HIDDEN = 256
MOE_ROW_BLOCK = 256

VMEM_LIMIT = 56 * 1024 * 1024


def _params(*sem):
    return pltpu.CompilerParams(dimension_semantics=sem, vmem_limit_bytes=VMEM_LIMIT)


def _full(shape):
    n = len(shape)
    return pl.BlockSpec(shape, lambda *_: (0,) * n)


def _rms(x, eps=1e-6):
    return x * lax.rsqrt(jnp.mean(x * x, axis=-1, keepdims=True) + eps)


def _dot(a, b):
    return jnp.dot(a, b, preferred_element_type=F32)


def _dot_nt(a, b, **kw):
    return lax.dot_general(a, b, (((1,), (1,)), ((), ())), preferred_element_type=F32, **kw)


def _inproj0_kernel(x_ref, pos_ref, gmix_ref, win_ref, qan_ref, wuq_ref, kvan_ref, wuk_ref, wuv_ref,
                    qg_ref, kg_ref, invf_ref, sgn_ref, q_out, k_out, v_out, u_out):
    h = _rms(x_ref[...]) * gmix_ref[...]
    proj = _dot(h.astype(BF16), win_ref[...])
    c_q = proj[:, :MLA_Q_LORA]
    c_kv = proj[:, MLA_Q_LORA:MLA_Q_LORA + MLA_KV_LORA]
    k_rope = proj[:, 384:512]
    a = proj[:, 512:512 + CONV_CH]
    g = proj[:, 512 + CONV_CH:]
    u_out[...] = a * jax.nn.sigmoid(g)
    cqn = (_rms(c_q) * qan_ref[...]).astype(BF16)
    ckvn = (_rms(c_kv) * kvan_ref[...]).astype(BF16)
    q = _dot(cqn, wuq_ref[...])
    kn = _dot(ckvn, wuk_ref[...])
    v_out[...] = _dot(ckvn, wuv_ref[...]).astype(BF16)
    ang = pos_ref[...].astype(F32) * invf_ref[...]
    cos = jnp.cos(ang)
    sin = jnp.sin(ang) * sgn_ref[...]
    lane = lax.broadcasted_iota(I32, (1, LANES), 1)
    first_half = (lane >= MLA_NOPE) & (lane < MLA_NOPE + MLA_ROPE // 2)
    scale = MLA_QK ** -0.5

    def norm_rope(t, gain):
        t = t * lax.rsqrt(jnp.sum(t * t, axis=-1, keepdims=True) * (1.0 / MLA_QK) + 1e-6) * gain
        partner = jnp.where(first_half, pltpu.roll(t, LANES - MLA_ROPE // 2, 1), pltpu.roll(t, MLA_ROPE // 2, 1))
        return t * cos + partner * sin

    for hd in range(MLA_HEADS):
        sl = slice(hd * LANES, (hd + 1) * LANES)
        q_out[:, sl] = (norm_rope(q[:, sl], qg_ref[...]) * scale).astype(BF16)
        k_out[:, sl] = norm_rope(kn[:, sl] + k_rope, kg_ref[...]).astype(BF16)


def _head_slots(w, n_heads, width, offset=0):
    k = w.shape[0]
    w = w.reshape(k, n_heads, width)
    w = jnp.pad(w, ((0, 0), (0, 0), (offset, LANES - width - offset)))
    return w.reshape(k, n_heads * LANES)


def _inproj0(x2d, pos_col, gmix, w_in, q_a_norm, w_uq, kv_a_norm, w_ukv, q_norm, k_norm, tm=256):
    T, D = x2d.shape
    H = MLA_HEADS
    w_krope = jnp.pad(w_in[:, 384:416], ((0, 0), (MLA_NOPE, LANES - MLA_NOPE - MLA_ROPE)))
    w_in_p = jnp.concatenate([w_in[:, :384], w_krope, w_in[:, 416:]], axis=1).astype(BF16)
    w_uq_p = _head_slots(w_uq, H, MLA_QK).astype(BF16)
    w_ukv3 = w_ukv.reshape(MLA_KV_LORA, H, MLA_NOPE + MLA_V)
    w_uk_p = _head_slots(w_ukv3[:, :, :MLA_NOPE].reshape(MLA_KV_LORA, H * MLA_NOPE), H, MLA_NOPE).astype(BF16)
    w_uv = w_ukv3[:, :, MLA_NOPE:].reshape(MLA_KV_LORA, H * MLA_V).astype(BF16)
    pad = LANES - MLA_QK
    qg = jnp.pad(q_norm, (0, pad)).reshape(1, LANES)
    kg = jnp.pad(k_norm, (0, pad)).reshape(1, LANES)
    half = MLA_ROPE // 2
    inv_freq = ROPE_THETA ** (-jnp.arange(half, dtype=F32) / half)
    invf = jnp.zeros((LANES,), F32).at[MLA_NOPE:MLA_NOPE + half].set(inv_freq).at[MLA_NOPE + half:MLA_QK].set(inv_freq)
    sgn = jnp.zeros((LANES,), F32).at[MLA_NOPE:MLA_NOPE + half].set(-1.0).at[MLA_NOPE + half:MLA_QK].set(1.0)
    row = lambda n: pl.BlockSpec((tm, n), lambda i: (i, 0))
    n_in = w_in_p.shape[1]
    return pl.pallas_call(
        _inproj0_kernel,
        grid=(T // tm,),
        in_specs=[row(D), row(1), _full((1, D)), _full((D, n_in)), _full((1, MLA_Q_LORA)),
                  _full((MLA_Q_LORA, H * LANES)), _full((1, MLA_KV_LORA)), _full((MLA_KV_LORA, H * LANES)),
                  _full((MLA_KV_LORA, H * MLA_V)), _full((1, LANES)), _full((1, LANES)), _full((1, LANES)),
                  _full((1, LANES))],
        out_specs=[row(H * LANES), row(H * LANES), row(H * MLA_V), row(CONV_CH)],
        out_shape=[jax.ShapeDtypeStruct((T, H * LANES), BF16), jax.ShapeDtypeStruct((T, H * LANES), BF16),
                   jax.ShapeDtypeStruct((T, H * MLA_V), BF16), jax.ShapeDtypeStruct((T, CONV_CH), F32)],
        compiler_params=_params("parallel"),
        name="inproj0",
    )(x2d, pos_col, gmix.reshape(1, D), w_in_p, q_a_norm.reshape(1, -1), w_uq_p, kv_a_norm.reshape(1, -1),
      w_uk_p, w_uv, qg, kg, invf.reshape(1, LANES), sgn.reshape(1, LANES))


def _mla_attn_kernel(q_ref, k_ref, v_ref, o_ref, *, blk):
    S = q_ref.shape[1]
    lane = lax.broadcasted_iota(I32, (1, LANES), 1)
    row = lax.broadcasted_iota(I32, (blk, 1), 0)
    col = lax.broadcasted_iota(I32, (1, blk), 1)

    def q_block(qi, carry):
        q0 = pl.multiple_of(qi * blk, blk)
        outs = []
        for hh in range(2):
            hs = slice(hh * LANES, (hh + 1) * LANES)
            q = q_ref[0, pl.ds(q0, blk), hs]

            def kv_step(j, state, masked):
                m, l, acc = state
                k0 = pl.multiple_of(j * blk, blk)
                s = _dot_nt(q, k_ref[0, pl.ds(k0, blk), hs])
                if masked:
                    s = jnp.where(row >= col, s, NEG_INF)
                m_new = jnp.maximum(m, jnp.max(s, axis=-1, keepdims=True))
                alpha = jnp.exp(m - m_new)
                p = jnp.exp(s - m_new)
                l = alpha * l + jnp.sum(p, axis=-1, keepdims=True)
                acc = alpha * acc + _dot(p.astype(BF16), v_ref[0, pl.ds(k0, blk), :])
                return m_new, l, acc

            init = (jnp.full((blk, 1), NEG_INF, F32), jnp.zeros((blk, 1), F32), jnp.zeros((blk, LANES), F32))
            state = lax.fori_loop(0, qi, functools.partial(kv_step, masked=False), init)
            m, l, acc = kv_step(qi, state, True)
            outs.append(acc / l)
        o_ref[0, pl.ds(q0, blk), :] = jnp.where(lane < MLA_V, outs[0], outs[1]).astype(BF16)
        return carry

    lax.fori_loop(0, S // blk, q_block, 0)


def _mla_attn(q, k, v, blk=256):
    B, S, _ = q.shape
    return pl.pallas_call(
        functools.partial(_mla_attn_kernel, blk=blk),
        grid=(B, MLA_HEADS // 2),
        in_specs=[pl.BlockSpec((1, S, 2 * LANES), lambda b, h: (b, 0, h)),
                  pl.BlockSpec((1, S, 2 * LANES), lambda b, h: (b, 0, h)),
                  pl.BlockSpec((1, S, 2 * MLA_V), lambda b, h: (b, 0, h))],
        out_specs=pl.BlockSpec((1, S, 2 * MLA_V), lambda b, h: (b, 0, h)),
        out_shape=jax.ShapeDtypeStruct((B, S, MLA_HEADS * MLA_V), BF16),
        compiler_params=_params("parallel", "parallel"),
        name="mla_attn",
    )(q, k, v)


CONV_TILE = 64
CONV_PAD = 32


def _conv_out0_kernel(u_ref, o_ref, x_ref, dww_ref, dwb_ref, lng_ref, lnb_ref, wo_ref, out_ref, upad, act):
    S = u_ref.shape[1]
    upad[0:CONV_PAD, :] = jnp.zeros((CONV_PAD, CONV_CH), F32)
    upad[CONV_PAD:, :] = u_ref[0]
    lead = CONV_PAD - (CONV_W - 1)

    def tile(i, carry):
        t0 = pl.multiple_of(i * CONV_TILE, CONV_TILE)
        win = upad[pl.ds(t0, CONV_TILE + CONV_PAD), :]
        acc = jnp.zeros((CONV_TILE, CONV_CH), F32) + dwb_ref[...]
        for j in range(CONV_W):
            acc = acc + win[lead + j:lead + j + CONV_TILE, :] * dww_ref[j:j + 1, :]
        mu = jnp.mean(acc, axis=-1, keepdims=True)
        xc = acc - mu
        var = jnp.mean(xc * xc, axis=-1, keepdims=True)
        y = xc * lax.rsqrt(var + 1e-5) * lng_ref[...] + lnb_ref[...]
        act[pl.ds(t0, CONV_TILE), :] = (y * jax.nn.sigmoid(y)).astype(BF16)
        return carry

    lax.fori_loop(0, S // CONV_TILE, tile, 0)
    n_o = o_ref.shape[2]
    rows = 512
    for r in range(S // rows):
        rs = slice(r * rows, (r + 1) * rows)
        y = _dot(o_ref[0, rs, :], wo_ref[:n_o, :]) + _dot(act[rs, :], wo_ref[n_o:, :])
        out_ref[0, rs, :] = x_ref[0, rs, :] + y


def _conv_out0(u, o_mla, x, dw_w, dw_b, ln_g, ln_b, w_out):
    B, S, D = x.shape
    n_o = o_mla.shape[2]
    bspec = lambda n: pl.BlockSpec((1, S, n), lambda b: (b, 0, 0))
    return pl.pallas_call(
        _conv_out0_kernel,
        grid=(B,),
        in_specs=[bspec(CONV_CH), bspec(n_o), bspec(D), _full((CONV_W, CONV_CH)), _full((1, CONV_CH)),
                  _full((1, CONV_CH)), _full((1, CONV_CH)), _full((n_o + CONV_CH, D))],
        out_specs=bspec(D),
        out_shape=jax.ShapeDtypeStruct((B, S, D), F32),
        scratch_shapes=[pltpu.VMEM((S + CONV_PAD, CONV_CH), F32), pltpu.VMEM((S, CONV_CH), BF16)],
        compiler_params=_params("parallel"),
        name="conv_out0",
    )(u, o_mla, x, dw_w, dw_b.reshape(1, -1), ln_g.reshape(1, -1), ln_b.reshape(1, -1), w_out.astype(BF16))


def _mixer0(x, positions, gmix, w_in, q_a_norm, w_uq, kv_a_norm, w_ukv, q_norm, k_norm,
            dw_w, dw_b, ln_g, ln_b, w_out):
    B, S, D = x.shape
    T = B * S
    q, k, v, u = _inproj0(x.reshape(T, D), positions.reshape(T, 1), gmix, w_in, q_a_norm, w_uq, kv_a_norm, w_ukv,
                          q_norm, k_norm)
    o = _mla_attn(q.reshape(B, S, -1), k.reshape(B, S, -1), v.reshape(B, S, -1))
    return _conv_out0(u.reshape(B, S, -1), o, x, dw_w, dw_b, ln_g, ln_b, w_out)


ROUTE_TILE = 512
ROUTE_ROWS = 40
SLABS = 8


def _route_kernel(x_ref, g_ref, wr_ref, br_ref, tri_ref, hn_ref, oi_ref, of_ref, cnt_ref, carry):
    @pl.when(pl.program_id(0) == 0)
    def _():
        carry[...] = jnp.zeros_like(carry)

    hn = _rms(x_ref[...]) * g_ref[...]
    for j in range(SLABS):
        hn_ref[:, j, :] = hn[:, j * LANES:(j + 1) * LANES]
    tm = hn.shape[0]
    logits = _dot_nt(wr_ref[...], hn, precision=lax.Precision.HIGHEST) + br_ref[...]
    gl = logits[MOE_EXPERTS:MOE_EXPERTS + MOE_GROUPS]
    rid_g = lax.broadcasted_iota(I32, (MOE_GROUPS, tm), 0)
    gmax = jnp.max(gl, axis=0, keepdims=True)
    grp = jnp.min(jnp.where(gl == gmax, rid_g, MOE_GROUPS), axis=0, keepdims=True)
    g_w = 1.0 / jnp.sum(jnp.exp(gl - gmax), axis=0, keepdims=True)
    e_in = jnp.zeros((MOE_EPG, tm), F32)
    for gi in range(MOE_GROUPS):
        e_in = jnp.where(grp == gi, logits[gi * MOE_EPG:(gi + 1) * MOE_EPG], e_in)
    rid_e = lax.broadcasted_iota(I32, (MOE_EPG, tm), 0)
    v1 = jnp.max(e_in, axis=0, keepdims=True)
    i1 = jnp.min(jnp.where(e_in == v1, rid_e, MOE_EPG), axis=0, keepdims=True)
    rest = jnp.where(rid_e == i1, -jnp.inf, e_in)
    v2 = jnp.max(rest, axis=0, keepdims=True)
    i2 = jnp.min(jnp.where(rest == v2, rid_e, MOE_EPG), axis=0, keepdims=True)
    ex = jnp.exp(v2 - v1)
    den = 1.0 + ex
    e1 = grp * MOE_EPG + i1
    e2 = grp * MOE_EPG + i2
    rid = lax.broadcasted_iota(I32, (MOE_EXPERTS, tm), 0)
    hit1 = rid == e1
    hit2 = rid == e2
    member = jnp.where(hit1 | hit2, 1.0, 0.0)
    before = _dot(member.astype(BF16), tri_ref[...]) + carry[...]
    r1 = jnp.sum(jnp.where(hit1, before, 0.0), axis=0, keepdims=True)
    r2 = jnp.sum(jnp.where(hit2, before, 0.0), axis=0, keepdims=True)
    carry[...] = carry[...] + jnp.sum(member, axis=1, keepdims=True)
    oi_ref[...] = jnp.zeros_like(oi_ref)
    oi_ref[0:1, :] = e1
    oi_ref[1:2, :] = e2
    oi_ref[2:3, :] = r1.astype(I32)
    oi_ref[3:4, :] = r2.astype(I32)
    of_ref[...] = jnp.zeros_like(of_ref)
    of_ref[0:1, :] = g_w / den
    of_ref[1:2, :] = g_w * ex / den
    cnt_ref[...] = jnp.broadcast_to(carry[...], cnt_ref.shape).astype(I32)


def _route(x2d, gain, router_group, router_group_b, router_expert, router_expert_b):
    T, D = x2d.shape
    tm = ROUTE_TILE
    pad = ROUTE_ROWS - MOE_EXPERTS - MOE_GROUPS
    wr = jnp.concatenate([router_expert.T, router_group.T, jnp.zeros((pad, D), F32)], axis=0)
    br = jnp.concatenate([router_expert_b, router_group_b, jnp.zeros((pad,), F32)]).reshape(ROUTE_ROWS, 1)
    tri = (jnp.arange(tm)[:, None] < jnp.arange(tm)[None, :]).astype(BF16)
    return pl.pallas_call(
        _route_kernel,
        grid=(T // tm,),
        in_specs=[pl.BlockSpec((tm, D), lambda i: (i, 0)), _full((1, D)), _full((ROUTE_ROWS, D)),
                  _full((ROUTE_ROWS, 1)), _full((tm, tm))],
        out_specs=[pl.BlockSpec((tm, SLABS, LANES), lambda i: (i, 0, 0)), pl.BlockSpec((8, tm), lambda i: (0, i)),
                   pl.BlockSpec((8, tm), lambda i: (0, i)), _full((MOE_EXPERTS, LANES))],
        out_shape=[jax.ShapeDtypeStruct((T, SLABS, LANES), F32), jax.ShapeDtypeStruct((8, T), I32),
                   jax.ShapeDtypeStruct((8, T), F32), jax.ShapeDtypeStruct((MOE_EXPERTS, LANES), I32)],
        scratch_shapes=[pltpu.VMEM((MOE_EXPERTS, 1), F32)],
        compiler_params=_params("arbitrary"),
        name="moe_route",
    )(x2d, gain.reshape(1, D), wr, br, tri)


DISPATCH_TILE = 512


def _dispatch_kernel(dest_ref, hn_ref, xs_init_ref, xs_ref, sem):
    del xs_init_ref
    tm = hn_ref.shape[0]
    base = pl.program_id(0) * (2 * tm)

    def copy(t, k):
        return pltpu.make_async_copy(hn_ref.at[t], xs_ref.at[dest_ref[base + 2 * t + k]], sem)

    def issue(t, carry):
        copy(t, 0).start()
        copy(t, 1).start()
        return carry

    def drain(t, carry):
        copy(t, 0).wait()
        copy(t, 1).wait()
        return carry

    lax.fori_loop(0, tm, issue, 0)
    lax.fori_loop(0, tm, drain, 0)


def _dispatch(dest, hn3, n_rows):
    T = hn3.shape[0]
    tm = DISPATCH_TILE
    return pl.pallas_call(
        _dispatch_kernel,
        grid_spec=pltpu.PrefetchScalarGridSpec(
            num_scalar_prefetch=1,
            grid=(T // tm,),
            in_specs=[pl.BlockSpec((tm, SLABS, LANES), lambda i, d: (i, 0, 0)), pl.BlockSpec(memory_space=pl.ANY)],
            out_specs=pl.BlockSpec(memory_space=pl.ANY),
            scratch_shapes=[pltpu.SemaphoreType.DMA(())],
        ),
        out_shape=jax.ShapeDtypeStruct((n_rows, SLABS, LANES), F32),
        input_output_aliases={2: 0},
        compiler_params=_params("arbitrary"),
        name="moe_dispatch",
    )(dest, hn3, jnp.zeros((n_rows, SLABS, LANES), F32))


def _expert_kernel(blk_e_ref, n_used_ref, xs_ref, wgu_ref, wd_ref, ys_ref):
    del blk_e_ref

    @pl.when(pl.program_id(0) < n_used_ref[0])
    def _():
        xrow = jnp.concatenate([xs_ref[:, j, :] for j in range(SLABS)], axis=1).astype(BF16)
        gu = _dot(xrow, wgu_ref[0])
        gate = gu[:, :MOE_HIDDEN]
        hid = (gate * jax.nn.sigmoid(gate) * gu[:, MOE_HIDDEN:]).astype(BF16)
        y = _dot(hid, wd_ref[0])
        for j in range(SLABS):
            ys_ref[:, j, :] = y[:, j * LANES:(j + 1) * LANES]

    @pl.when(pl.program_id(0) >= n_used_ref[0])
    def _():
        ys_ref[...] = jnp.zeros_like(ys_ref)


def _experts(blk_expert, n_used, xs, w_gu, w_d):
    R = xs.shape[0]
    rb = MOE_ROW_BLOCK
    D = w_gu.shape[1]
    return pl.pallas_call(
        _expert_kernel,
        grid_spec=pltpu.PrefetchScalarGridSpec(
            num_scalar_prefetch=2,
            grid=(R // rb,),
            in_specs=[pl.BlockSpec((rb, SLABS, LANES), lambda b, be, nu: (jnp.minimum(b, nu[0] - 1), 0, 0)),
                      pl.BlockSpec((1, D, 2 * MOE_HIDDEN), lambda b, be, nu: (be[b], 0, 0)),
                      pl.BlockSpec((1, MOE_HIDDEN, D), lambda b, be, nu: (be[b], 0, 0))],
            out_specs=pl.BlockSpec((rb, SLABS, LANES), lambda b, be, nu: (b, 0, 0)),
        ),
        out_shape=jax.ShapeDtypeStruct((R, SLABS, LANES), F32),
        compiler_params=_params("arbitrary"),
        name="moe_experts",
    )(blk_expert, n_used, xs, w_gu, w_d)


def _combine_kernel(dest_ref, x_ref, gate_ref, ys_ref, out_ref, buf0, buf1, sem):
    tm = x_ref.shape[0]
    base = pl.program_id(0) * (2 * tm)

    def copy(t, k):
        return pltpu.make_async_copy(ys_ref.at[dest_ref[base + 2 * t + k]], (buf0, buf1)[k].at[t], sem)

    def issue(t, carry):
        copy(t, 0).start()
        copy(t, 1).start()
        return carry

    def drain(t, carry):
        copy(t, 0).wait()
        copy(t, 1).wait()
        return carry

    lax.fori_loop(0, tm, issue, 0)
    lax.fori_loop(0, tm, drain, 0)
    g0 = gate_ref[:, 0:1]
    g1 = gate_ref[:, 1:2]
    for j in range(SLABS):
        sl = slice(j * LANES, (j + 1) * LANES)
        out_ref[:, sl] = x_ref[:, sl] + g0 * buf0[:, j, :] + g1 * buf1[:, j, :]


def _combine(dest, x2d, gates_col, ys):
    T, D = x2d.shape
    tm = DISPATCH_TILE
    return pl.pallas_call(
        _combine_kernel,
        grid_spec=pltpu.PrefetchScalarGridSpec(
            num_scalar_prefetch=1,
            grid=(T // tm,),
            in_specs=[pl.BlockSpec((tm, D), lambda i, d: (i, 0)), pl.BlockSpec((tm, 2), lambda i, d: (i, 0)),
                      pl.BlockSpec(memory_space=pl.ANY)],
            out_specs=pl.BlockSpec((tm, D), lambda i, d: (i, 0)),
            scratch_shapes=[pltpu.VMEM((tm, SLABS, LANES), F32), pltpu.VMEM((tm, SLABS, LANES), F32),
                            pltpu.SemaphoreType.DMA(())],
        ),
        out_shape=jax.ShapeDtypeStruct((T, D), F32),
        compiler_params=_params("arbitrary"),
        name="moe_combine",
    )(dest, x2d, gates_col, ys)


def _moe(x, gain, router_group, router_group_b, router_expert, router_expert_b, w_gate, w_up, w_down):
    B, S, D = x.shape
    T = B * S
    x2d = x.reshape(T, D)
    hn3, oi, of, cnt = _route(x2d, gain, router_group, router_group_b, router_expert, router_expert_b)
    rb = MOE_ROW_BLOCK
    counts = cnt[:, 0]
    cap = (counts + rb - 1) // rb * rb
    cap_end = jnp.cumsum(cap)
    start = cap_end - cap
    eid = oi[0:2].T
    first = jnp.sum(jnp.where(eid[:, :, None] == jnp.arange(MOE_EXPERTS), start, 0), axis=-1)
    dest = (first + oi[2:4].T).reshape(2 * T).astype(I32)
    n_rows = 2 * T + MOE_EXPERTS * rb
    n_blk = n_rows // rb
    blk_first_row = jnp.arange(n_blk, dtype=I32) * rb
    blk_expert = jnp.minimum(jnp.sum(cap_end[None, :] <= blk_first_row[:, None], axis=1), MOE_EXPERTS - 1)
    n_used = (cap_end[-1:] // rb).astype(I32)
    xs = _dispatch(dest, hn3, n_rows)
    w_gu = jnp.concatenate([w_gate, w_up], axis=2).astype(BF16)
    ys = _experts(blk_expert.astype(I32), n_used, xs, w_gu, w_down.astype(BF16))
    out = _combine(dest, x2d, of[0:2].T, ys)
    return out.reshape(B, S, D)


NSA_TQ = 128
NSA_PAIRS = NSA_HPG // 2
N_SEL = 32
LOG2E = 1.4426950408889634
NSA_Q_SCALE = NSA_DK ** -0.5 * LOG2E
LO_ONE = LANES - 1
HI_ONE = 0
SLC_TK = 512
WIN_SPAN = WINDOW + NSA_TQ
AUG_POS = N_SEL
MASK_BIG = 1e30


def _pair_norm(t, gain2, lane_lo):
    t2 = t * t
    s_lo = jnp.sum(jnp.where(lane_lo, t2, 0.0), axis=-1, keepdims=True)
    s_hi = jnp.sum(jnp.where(lane_lo, 0.0, t2), axis=-1, keepdims=True)
    inv = jnp.where(lane_lo, lax.rsqrt(s_lo * (1.0 / NSA_DK) + 1e-6), lax.rsqrt(s_hi * (1.0 / NSA_DK) + 1e-6))
    return t * inv * gain2


def _inproj1_kernel(x_ref, kaug_ref, g_ref, win_ref, qg_ref, ksg_ref, kwg_ref,
                    q_out, kc_out, vc_out, ks_out, vs_out, kw_out, vw_out, gate_out):
    h = _rms(x_ref[...]) * g_ref[...]
    proj = _dot(h.astype(BF16), win_ref[...])
    lane = lax.broadcasted_iota(I32, (1, LANES), 1)
    lane_lo = lane < NSA_DK
    nq = NSA_HEADS * NSA_DK
    for p in range(nq // LANES):
        sl = slice(p * LANES, (p + 1) * LANES)
        q_out[:, sl] = (_pair_norm(proj[:, sl], qg_ref[...], lane_lo) * NSA_Q_SCALE).astype(BF16)
    part = lambda i: proj[:, nq + i * LANES:nq + (i + 1) * LANES]
    kc_out[...] = part(0)
    vc_out[...] = part(1)

    def lo_hi(t, lo_pad, hi_pad):
        r = pltpu.roll(t, NSA_DK, 1)
        return jnp.concatenate([jnp.where(lane_lo, t, lo_pad), jnp.where(lane_lo, hi_pad, r),
                                jnp.where(lane_lo, r, lo_pad), jnp.where(lane_lo, hi_pad, t)], axis=1).astype(BF16)

    kaug = kaug_ref[...].astype(F32)
    one_lo = jnp.where(lane == LO_ONE, 1.0, 0.0)
    one_hi = jnp.where(lane == HI_ONE, 1.0, 0.0)
    ks_out[...] = lo_hi(_pair_norm(part(2), ksg_ref[...], lane_lo), kaug, kaug)
    vs_out[...] = lo_hi(part(3), one_lo, one_hi)
    kw_out[...] = lo_hi(_pair_norm(part(4), kwg_ref[...], lane_lo), kaug, kaug)
    vw_out[...] = lo_hi(part(5), one_lo, one_hi)
    gate_out[...] = jax.nn.sigmoid(part(6))


def _inproj1(x2d, kaug, gain, w_in, q_norm, k_norm_slc, k_norm_win, tm=256):
    T, D = x2d.shape
    n_in = w_in.shape[1]
    n_pad = -n_in % LANES
    w_in_p = jnp.pad(w_in, ((0, 0), (0, n_pad))).astype(BF16)
    two = lambda g: jnp.tile(g, 2).reshape(1, LANES)
    row = lambda n: pl.BlockSpec((tm, n), lambda i: (i, 0))
    nq = NSA_HEADS * NSA_DK
    bf = lambda n: jax.ShapeDtypeStruct((T, n), BF16)
    f32 = lambda n: jax.ShapeDtypeStruct((T, n), F32)
    return pl.pallas_call(
        _inproj1_kernel,
        grid=(T // tm,),
        in_specs=[row(D), row(LANES), _full((1, D)), _full((D, n_in + n_pad)), _full((1, LANES)), _full((1, LANES)),
                  _full((1, LANES))],
        out_specs=[row(nq), row(LANES), row(LANES), row(4 * LANES), row(4 * LANES), row(4 * LANES), row(4 * LANES),
                   row(LANES)],
        out_shape=[bf(nq), f32(LANES), f32(LANES), bf(4 * LANES), bf(4 * LANES), bf(4 * LANES), bf(4 * LANES),
                   f32(LANES)],
        compiler_params=_params("parallel"),
        name="inproj1",
    )(x2d, kaug, gain.reshape(1, D), w_in_p, two(q_norm), two(k_norm_slc), two(k_norm_win))


def _stack_pairs(q_ref, rows):
    return jnp.concatenate([q_ref[0, rows, p * LANES:(p + 1) * LANES] for p in range(NSA_PAIRS)], axis=0)


def _gate_tile(gt, branch, p, lane_lo):
    c = branch * NSA_HPG + 2 * p
    return jnp.where(lane_lo, gt[:, c:c + 1], gt[:, c + 1:c + 2])


def _cmp_kernel(q_ref, kch_ref, vch_ref, pos3_ref, pend_ref, gate_ref, slope_ref,
                pek_ref, w1k_ref, w2k_ref, kcg_ref, pev_ref, w1v_ref, w2v_ref, ovl_ref, eye_ref,
                o_ref, sel_ref, kc_s, vc_s):
    tq = NSA_TQ
    S = q_ref.shape[1]

    def compress(ch_ref, pe_ref, w1_ref, w2_ref):
        a = ch_ref[0, 0]
        h_lo = _dot((a + pe_ref[0:1, :]).astype(BF16), w1_ref[0])
        h_hi = _dot((a + pe_ref[1:2, :]).astype(BF16), w1_ref[1])
        n = h_hi.shape[0]
        hid = jax.nn.gelu(h_lo + pltpu.roll(h_hi, n - 1, 0)).astype(BF16)
        return _dot(hid, w2_ref[0]), _dot(hid, w2_ref[1])

    k_lo, k_hi = compress(kch_ref, pek_ref, w1k_ref, w2k_ref)
    for e, kk in enumerate((k_lo, k_hi)):
        kk = kk * lax.rsqrt(jnp.sum(kk * kk, axis=-1, keepdims=True) * (1.0 / NSA_DK) + 1e-6) * kcg_ref[e:e + 1, :]
        kc_s[e] = kk.astype(BF16)
    v_lo, v_hi = compress(vch_ref, pev_ref, w1v_ref, w2v_ref)
    vc_s[0] = v_lo.astype(BF16)
    vc_s[1] = v_hi.astype(BF16)

    n_cmp = (S - CMP_BLOCK) // CMP_STRIDE + 1
    lane = lax.broadcasted_iota(I32, (1, LANES), 1)
    lane_lo = lane < NSA_DK
    blk_row = lax.broadcasted_iota(I32, (N_SEL, 1), 0)

    def q_block(qi, carry):
        t0 = pl.multiple_of(qi * tq, tq)
        rows = pl.ds(t0, tq)
        qs = _stack_pairs(q_ref, rows)
        pos_q0 = pos3_ref[0, pl.ds(qi, 1), :][:, 0:1]
        posrel = (pend_ref[0] - pos_q0).astype(F32)
        tcol = t0 + lax.broadcasted_iota(I32, (tq, 1), 0)
        valid = ((tcol >= CMP_STRIDE * lane + (CMP_BLOCK - 1)) & (lane < n_cmp))[None]
        psum = jnp.zeros((tq, LANES), F32)
        o_pairs = jnp.zeros((NSA_PAIRS * tq, LANES), F32)
        for e in range(2):
            s = _dot_nt(qs, kc_s[e]).reshape(NSA_PAIRS, tq, LANES)
            s = s + slope_ref[0, NSA_PAIRS * e:NSA_PAIRS * (e + 1)][:, None, :] * posrel[None]
            s = jnp.where(valid, s, NEG_INF)
            m = jnp.max(s, axis=-1, keepdims=True)
            p = jnp.where(valid, jnp.exp2(s - m), 0.0)
            p = p / jnp.maximum(jnp.sum(p, axis=-1, keepdims=True), 1e-20)
            psum = psum + jnp.sum(p, axis=0)
            o_pairs = o_pairs + _dot(p.reshape(NSA_PAIRS * tq, LANES).astype(BF16), vc_s[e])
        imp = _dot_nt(ovl_ref[...], psum, precision=lax.Precision.HIGHEST)
        cur = (t0 + lax.broadcasted_iota(I32, (1, tq), 1)) // SLC_BLOCK
        forced = (blk_row == 0) | (blk_row == cur) | (blk_row == cur - 1)
        imp = jnp.where(forced, FORCE_SCORE, jnp.where(blk_row <= cur, imp, -1.0))
        rank = jnp.zeros((N_SEL, tq), I32)
        for i in range(N_SEL):
            ri = imp[i:i + 1, :]
            rank = rank + jnp.where((ri > imp) | ((ri == imp) & (blk_row > i)), 1, 0)
        sel_t = jnp.where(rank < SLC_TOP_N, 1.0, 0.0).astype(BF16)
        gap = jnp.zeros((NSA_DK - N_SEL, tq), BF16)
        sel_t = jnp.concatenate([sel_t, gap, sel_t, gap], axis=0)
        sel = _dot_nt(eye_ref[...], sel_t)
        sel_ref[0, 0, rows, :] = jnp.where((lane & (NSA_DK - 1)) < N_SEL, sel - 1.0, 0.0).astype(BF16)
        gt = gate_ref[0, 0, rows, :]
        o3 = o_pairs.reshape(NSA_PAIRS, tq, LANES)
        for p in range(NSA_PAIRS):
            o_ref[0, rows, p * LANES:(p + 1) * LANES] = (o3[p] * _gate_tile(gt, 0, p, lane_lo)).astype(BF16)
        return carry

    lax.fori_loop(0, S // tq, q_block, 0)


def _lo_hi_cols(w):
    z = jnp.zeros_like(w)
    return jnp.stack([jnp.concatenate([w, z], axis=1), jnp.concatenate([z, w], axis=1)])


def _nsa_cmp(q, kch, vch, pos3, pend, gates_g, slopes_t, pe_k, w1_k, w2_k, k_norm_cmp, pe_v, w1_v, w2_v):
    B, S, _ = q.shape
    G = NSA_G
    nch = S // CMP_STRIDE
    half = CMP_STRIDE * NSA_DK
    pe2 = lambda pe: pe.reshape(2, half)
    w1_2 = lambda w: w.reshape(2, half, CMP_HIDDEN).astype(BF16)
    n_sel = S // SLC_BLOCK
    n_cmp = (S - CMP_BLOCK) // CMP_STRIDE + 1
    cmp_start = np.arange(LANES) * CMP_STRIDE
    slc_start = np.arange(n_sel) * SLC_BLOCK
    overlap = ((cmp_start[None, :] < slc_start[:, None] + SLC_BLOCK) & (cmp_start[None, :] + CMP_BLOCK > slc_start[:, None])
               & (np.arange(LANES)[None, :] < n_cmp))
    ovl = jnp.asarray(overlap, F32)
    eye = jnp.eye(NSA_TQ, dtype=BF16)
    qspec = pl.BlockSpec((1, S, NSA_PAIRS * LANES), lambda b, g: (b, 0, g))
    chspec = pl.BlockSpec((1, 1, nch, half), lambda b, g: (b, g, 0, 0))
    return pl.pallas_call(
        _cmp_kernel,
        grid=(B, G),
        in_specs=[qspec, chspec, chspec,
                  pl.BlockSpec((1, S // LANES, LANES), lambda b, g: (b, 0, 0)),
                  pl.BlockSpec((1, 1, LANES), lambda b, g: (b, 0, 0)),
                  pl.BlockSpec((1, 1, S, 3 * NSA_HPG), lambda b, g: (b, g, 0, 0)),
                  pl.BlockSpec((1, NSA_HPG, LANES), lambda b, g: (g, 0, 0)),
                  _full((2, half)), _full((2, half, CMP_HIDDEN)), _full((2, CMP_HIDDEN, LANES)), _full((2, LANES)),
                  _full((2, half)), _full((2, half, CMP_HIDDEN)), _full((2, CMP_HIDDEN, LANES)),
                  _full((n_sel, LANES)), _full((NSA_TQ, NSA_TQ))],
        out_specs=[qspec, pl.BlockSpec((1, 1, S, LANES), lambda b, g: (b, g, 0, 0))],
        out_shape=[jax.ShapeDtypeStruct((B, S, NSA_HEADS * NSA_DK), BF16),
                   jax.ShapeDtypeStruct((B, G, S, LANES), BF16)],
        scratch_shapes=[pltpu.VMEM((2, nch, LANES), BF16), pltpu.VMEM((2, nch, LANES), BF16)],
        compiler_params=_params("parallel", "parallel"),
        name="nsa_cmp",
    )(q, kch, vch, pos3, pend, gates_g, slopes_t, pe2(pe_k), w1_2(w1_k), _lo_hi_cols(w2_k).astype(BF16),
      _lo_hi_cols(k_norm_cmp.reshape(1, -1)).reshape(2, LANES), pe2(pe_v), w1_2(w1_v),
      _lo_hi_cols(w2_v).astype(BF16), ovl, eye)


def _slc_win_kernel(q_ref, ks_ref, vs_ref, kw_ref, vw_ref, sel_ref, gate_ref, slopeq_ref, ocmp_ref, o_ref):
    tq = NSA_TQ
    S = q_ref.shape[1]
    lane = lax.broadcasted_iota(I32, (1, LANES), 1)
    lane_lo = lane < NSA_DK
    ones_lane = (LO_ONE, HI_ONE)

    def q_block(qi, carry):
        t0 = pl.multiple_of(qi * tq, tq)
        rows = pl.ds(t0, tq)
        qs = _stack_pairs(q_ref, rows)
        tcol = t0 + lax.broadcasted_iota(I32, (tq, 1), 0)
        selm1 = sel_ref[0, 0, rows, :]

        def q_aug(e, with_sel):
            feats = [slopeq_ref[0, NSA_PAIRS * e + p:NSA_PAIRS * e + p + 1, :] for p in range(NSA_PAIRS)]
            if with_sel:
                extra = jnp.concatenate([selm1 + f for f in feats], axis=0)
            else:
                extra = jnp.concatenate([jnp.broadcast_to(f, (tq, LANES)) for f in feats], axis=0)
            return jnp.where(lane_lo, qs, extra) if e == 0 else jnp.where(lane_lo, extra, qs)

        def k_aug(k_ref, e, r0, n):
            return k_ref[0, pl.ds(r0, n), e * LANES:(e + 1) * LANES]

        def normalise(accs):
            outs = [acc * (1.0 / acc[:, :, ones_lane[e]:ones_lane[e] + 1]) for e, acc in enumerate(accs)]
            return jnp.where(lane_lo, outs[0], outs[1])

        qa = [q_aug(0, True), q_aug(1, True)]
        kcol = lax.broadcasted_iota(I32, (1, SLC_TK), 1)

        def slc_step(j, state, last):
            r0 = pl.multiple_of(j * SLC_TK, SLC_TK)
            out = []
            for e in range(2):
                m, acc = state[e]
                s = _dot_nt(qa[e], k_aug(ks_ref, e, r0, SLC_TK)).reshape(NSA_PAIRS, tq, SLC_TK)
                if last:
                    s = jnp.where((r0 + kcol <= tcol)[None], s, NEG_INF)
                m_new = jnp.maximum(m, jnp.max(s, axis=-1, keepdims=True))
                p = jnp.exp2(s - m_new).reshape(NSA_PAIRS * tq, SLC_TK).astype(BF16)
                pv = _dot(p, vs_ref[0, pl.ds(r0, SLC_TK), e * LANES:(e + 1) * LANES])
                out.append((m_new, jnp.exp2(m - m_new) * acc + pv.reshape(NSA_PAIRS, tq, LANES)))
            return tuple(out)

        init = (jnp.full((NSA_PAIRS, tq, 1), NEG_INF, F32), jnp.zeros((NSA_PAIRS, tq, LANES), F32))
        n_full = qi // (SLC_TK // tq)
        state = lax.fori_loop(0, n_full, functools.partial(slc_step, last=False), (init, init))
        state = slc_step(n_full, state, True)
        slc = normalise([state[0][1], state[1][1]])

        w0 = pl.multiple_of(jnp.maximum(t0 - WINDOW, 0), tq)
        rel = tcol - (w0 + lax.broadcasted_iota(I32, (1, WIN_SPAN), 1))
        wmask = ((rel >= 0) & (rel < WINDOW))[None]
        accs = []
        for e in range(2):
            s = _dot_nt(q_aug(e, False), k_aug(kw_ref, e, w0, WIN_SPAN)).reshape(NSA_PAIRS, tq, WIN_SPAN)
            s = jnp.where(wmask, s, NEG_INF)
            p = jnp.exp2(s - jnp.max(s, axis=-1, keepdims=True)).reshape(NSA_PAIRS * tq, WIN_SPAN).astype(BF16)
            accs.append(_dot(p, vw_ref[0, pl.ds(w0, WIN_SPAN), e * LANES:(e + 1) * LANES]).reshape(NSA_PAIRS, tq, LANES))
        win = normalise(accs)

        gt = gate_ref[0, 0, rows, :]
        for p in range(NSA_PAIRS):
            sl = slice(p * LANES, (p + 1) * LANES)
            o = (ocmp_ref[0, rows, sl].astype(F32) + slc[p] * _gate_tile(gt, 1, p, lane_lo)
                 + win[p] * _gate_tile(gt, 2, p, lane_lo))
            o_ref[0, rows, sl] = o.astype(BF16)
        return carry

    lax.fori_loop(0, S // tq, q_block, 0)


def _key_aug(positions):
    B, S = positions.shape
    prel = positions - positions[:, :1]
    byte = lambda k: ((prel >> (8 * k)) & 255).astype(F32)
    pos_bytes = jnp.stack([byte(2), byte(1), byte(0)] * 2, axis=-1)
    onehot = (np.arange(S)[:, None] // SLC_BLOCK == np.arange(N_SEL)[None, :]) * MASK_BIG
    blocks = jnp.broadcast_to(jnp.asarray(onehot, F32), (B, S, N_SEL))
    pad = jnp.zeros((B, S, NSA_DK - N_SEL - 6), F32)
    half = jnp.concatenate([blocks, pos_bytes, pad], axis=-1)
    return jnp.concatenate([half, half], axis=-1).astype(BF16).reshape(B * S, LANES)


def _slope_aug(slopes_eo):
    hi = slopes_eo.astype(BF16).astype(F32)
    lo = (slopes_eo - hi).astype(BF16).astype(F32)
    w = jnp.asarray([65536.0, 256.0, 1.0], F32)
    feats = jnp.concatenate([hi[..., None] * w, lo[..., None] * w], axis=-1)
    half = jnp.pad(feats, ((0, 0), (0, 0), (AUG_POS, NSA_DK - AUG_POS - 6)))
    return jnp.concatenate([half, half], axis=-1).astype(BF16)


def _nsa_slc_win(q, ks4, vs4, kw4, vw4, sel, gates_g, slopeq, o_cmp):
    B, S, _ = q.shape
    qspec = pl.BlockSpec((1, S, NSA_PAIRS * LANES), lambda b, g: (b, 0, g))
    kvspec = pl.BlockSpec((1, S, 2 * LANES), lambda b, g: (b, 0, g))
    return pl.pallas_call(
        _slc_win_kernel,
        grid=(B, NSA_G),
        in_specs=[qspec, kvspec, kvspec, kvspec, kvspec,
                  pl.BlockSpec((1, 1, S, LANES), lambda b, g: (b, g, 0, 0)),
                  pl.BlockSpec((1, 1, S, 3 * NSA_HPG), lambda b, g: (b, g, 0, 0)),
                  pl.BlockSpec((1, NSA_HPG, LANES), lambda b, g: (g, 0, 0)),
                  qspec],
        out_specs=qspec,
        out_shape=jax.ShapeDtypeStruct((B, S, NSA_HEADS * NSA_DK), BF16),
        compiler_params=_params("parallel", "parallel"),
        name="nsa_slc_win",
    )(q, ks4, vs4, kw4, vw4, sel, gates_g, slopeq, o_cmp)


def _outproj_kernel(o_ref, x_ref, w_ref, out_ref):
    out_ref[...] = x_ref[...] + _dot(o_ref[...], w_ref[...])


def _outproj(o2d, x2d, w_out, tm=512):
    T, D = x2d.shape
    K = o2d.shape[1]
    return pl.pallas_call(
        _outproj_kernel,
        grid=(T // tm,),
        in_specs=[pl.BlockSpec((tm, K), lambda i: (i, 0)), pl.BlockSpec((tm, D), lambda i: (i, 0)), _full((K, D))],
        out_specs=pl.BlockSpec((tm, D), lambda i: (i, 0)),
        out_shape=jax.ShapeDtypeStruct((T, D), F32),
        compiler_params=_params("parallel"),
        name="outproj1",
    )(o2d, x2d, w_out.astype(BF16))


def _mixer1(x, positions, gain, w_in, q_norm, k_norm_cmp, k_norm_slc, k_norm_win,
            pe_k, w1_k, w2_k, pe_v, w1_v, w2_v, w_out):
    B, S, D = x.shape
    T = B * S
    G, HPG, DK = NSA_G, NSA_HPG, NSA_DK
    assert S // SLC_BLOCK == N_SEL and S % NSA_TQ == 0 and NSA_TQ == LANES
    x2d = x.reshape(T, D)
    q, kc, vc, ks4, vs4, kw4, vw4, gates = _inproj1(x2d, _key_aug(positions), gain, w_in, q_norm, k_norm_slc,
                                                    k_norm_win)
    b3 = lambda t: t.reshape(B, S, -1)
    chunks = lambda t: t.reshape(B, S, G, DK).transpose(0, 2, 1, 3).reshape(B, G, S // CMP_STRIDE, CMP_STRIDE * DK)
    gates_g = gates[:, :3 * NSA_HEADS].reshape(B, S, 3, G, HPG).transpose(0, 3, 1, 2, 4).reshape(B, G, S, 3 * HPG)
    n_cmp = (S - CMP_BLOCK) // CMP_STRIDE + 1
    pend = jnp.pad(positions[:, CMP_BLOCK - 1::CMP_STRIDE][:, :n_cmp], ((0, 0), (0, LANES - n_cmp))).reshape(B, 1, LANES)
    pos3 = positions.reshape(B, S // LANES, LANES)
    slopes = jnp.asarray(2.0 ** (-8.0 * np.arange(1, NSA_HEADS + 1) / NSA_HEADS), F32) * LOG2E
    slopes_eo = slopes.reshape(G, NSA_PAIRS, 2).transpose(0, 2, 1).reshape(G, HPG)
    slopes_t = jnp.broadcast_to(slopes_eo[:, :, None], (G, HPG, LANES))
    o_cmp, sel = _nsa_cmp(b3(q), chunks(kc), chunks(vc), pos3, pend, gates_g, slopes_t,
                          pe_k, w1_k, w2_k, k_norm_cmp, pe_v, w1_v, w2_v)
    o = _nsa_slc_win(b3(q), b3(ks4), b3(vs4), b3(kw4), b3(vw4), sel, gates_g, _slope_aug(slopes_eo), o_cmp)
    return _outproj(o.reshape(T, -1), x2d, w_out).reshape(B, S, D)


def kernel(x, positions, norm_mix, norm_ffn, mix0_w_in, mla_q_a_norm, mla_w_uq, mla_kv_a_norm, mla_w_ukv, mla_q_norm, mla_k_norm, conv_dw_w, conv_dw_b, conv_ln_g, conv_ln_b, mix0_w_out, nsa_w_in, nsa_q_norm, nsa_k_norm_cmp, nsa_k_norm_slc, nsa_k_norm_win, nsa_cmp_pe_k, nsa_cmp_w1_k, nsa_cmp_w2_k, nsa_cmp_pe_v, nsa_cmp_w1_v, nsa_cmp_w2_v, nsa_w_out, moe_router_group, moe_router_group_b, moe_router_expert, moe_router_expert_b, moe_w_gate, moe_w_up, moe_w_down):
    def moe(x, layer):
        return _moe(x, norm_ffn[layer], moe_router_group[layer], moe_router_group_b[layer], moe_router_expert[layer],
                    moe_router_expert_b[layer], moe_w_gate[layer], moe_w_up[layer], moe_w_down[layer])

    x = _mixer0(x, positions, norm_mix[0], mix0_w_in[0], mla_q_a_norm[0], mla_w_uq[0], mla_kv_a_norm[0], mla_w_ukv[0],
                mla_q_norm[0], mla_k_norm[0], conv_dw_w[0], conv_dw_b[0], conv_ln_g[0], conv_ln_b[0], mix0_w_out[0])
    x = moe(x, 0)
    x = _mixer1(x, positions, norm_mix[1], nsa_w_in[0], nsa_q_norm[0], nsa_k_norm_cmp[0], nsa_k_norm_slc[0],
                nsa_k_norm_win[0], nsa_cmp_pe_k[0], nsa_cmp_w1_k[0], nsa_cmp_w2_k[0], nsa_cmp_pe_v[0],
                nsa_cmp_w1_v[0], nsa_cmp_w2_v[0], nsa_w_out[0])
    return moe(x, 1)
```

```python
import functools

import numpy as np
import jax
import jax.numpy as jnp
from jax import lax
from jax.experimental import pallas as pl
from jax.experimental.pallas import tpu as pltpu

F32 = jnp.float32
BF16 = jnp.bfloat16
I32 = jnp.int32

LANES = 128
NEG_INF = -1e30
LOG2E = 1.4426950408889634
Q_BLOCK = 128
MLA_HEADS = 8
MLA_Q_LORA = 256
MLA_KV_LORA = 128
MLA_NOPE = 64
MLA_ROPE = 32
MLA_V = 64
MLA_QK = MLA_NOPE + MLA_ROPE
ROPE_THETA = 10000.0
CONV_CH = 512
CONV_W = 31
NSA_HEADS = 16
NSA_G = 2
NSA_HPG = NSA_HEADS // NSA_G
NSA_DK = 64
NSA_KVW = NSA_G * NSA_DK
CMP_BLOCK = 32
CMP_STRIDE = 16
CMP_HIDDEN = 128
SLC_BLOCK = 64
SLC_TOP_N = 8
WINDOW = 256
FORCE_SCORE = 1e4
MOE_GROUPS = 4
MOE_EPG = 8
MOE_EXPERTS = MOE_GROUPS * MOE_EPG
MOE_HIDDEN = 256
MOE_ROW_BLOCK = 256

VMEM_LIMIT = 56 * 1024 * 1024


def _params(*sem):
    return pltpu.CompilerParams(dimension_semantics=sem, vmem_limit_bytes=VMEM_LIMIT)


def _full(shape):
    n = len(shape)
    return pl.BlockSpec(shape, lambda *_: (0,) * n)


def _rms(x, eps=1e-6):
    return x * lax.rsqrt(jnp.mean(x * x, axis=-1, keepdims=True) + eps)


def _dot(a, b):
    return jnp.dot(a, b, preferred_element_type=F32)


def _dot_nt(a, b, **kw):
    return lax.dot_general(a, b, (((1,), (1,)), ((), ())), preferred_element_type=F32, **kw)


def _inproj0_kernel(x_ref, pos_ref, gmix_ref, win_ref, qan_ref, wuq_ref, kvan_ref, wuk_ref, wuv_ref,
                    qg_ref, kg_ref, invf_ref, sgn_ref, q_out, k_out, v_out, u_out):
    h = _rms(x_ref[...]) * gmix_ref[...]
    proj = _dot(h.astype(BF16), win_ref[...])
    c_q = proj[:, :MLA_Q_LORA]
    c_kv = proj[:, MLA_Q_LORA:MLA_Q_LORA + MLA_KV_LORA]
    k_rope = proj[:, 384:512]
    a = proj[:, 512:512 + CONV_CH]
    g = proj[:, 512 + CONV_CH:]
    u_out[...] = a * jax.nn.sigmoid(g)
    cqn = (_rms(c_q) * qan_ref[...]).astype(BF16)
    ckvn = (_rms(c_kv) * kvan_ref[...]).astype(BF16)
    q = _dot(cqn, wuq_ref[...])
    kn = _dot(ckvn, wuk_ref[...])
    slot_row = lax.broadcasted_iota(I32, (MLA_HEADS * LANES, 1), 0) & (LANES - 1)
    v_out[...] = (_dot_nt(wuv_ref[...], ckvn) + jnp.where(slot_row == MLA_V, 1.0, 0.0)).astype(BF16)
    ang = pos_ref[...].astype(F32) * invf_ref[...]
    cos = jnp.cos(ang)
    sin = jnp.sin(ang) * sgn_ref[...]
    lane = lax.broadcasted_iota(I32, (1, LANES), 1)
    first_half = (lane >= MLA_NOPE) & (lane < MLA_NOPE + MLA_ROPE // 2)
    scale = MLA_QK ** -0.5 * LOG2E

    def norm_rope(t, gain):
        t = t * lax.rsqrt(jnp.sum(t * t, axis=-1, keepdims=True) * (1.0 / MLA_QK) + 1e-6) * gain
        partner = jnp.where(first_half, pltpu.roll(t, LANES - MLA_ROPE // 2, 1), pltpu.roll(t, MLA_ROPE // 2, 1))
        return t * cos + partner * sin

    for hd in range(MLA_HEADS):
        sl = slice(hd * LANES, (hd + 1) * LANES)
        q_out[:, sl] = (norm_rope(q[:, sl], qg_ref[...]) * scale).astype(BF16)
        k_out[:, sl] = norm_rope(kn[:, sl] + k_rope, kg_ref[...]).astype(BF16)


def _head_slots(w, n_heads, width, offset=0):
    k = w.shape[0]
    w = w.reshape(k, n_heads, width)
    w = jnp.pad(w, ((0, 0), (0, 0), (offset, LANES - width - offset)))
    return w.reshape(k, n_heads * LANES)


def _inproj0(x2d, pos_col, gmix, w_in, q_a_norm, w_uq, kv_a_norm, w_ukv, q_norm, k_norm, tm=256):
    T, D = x2d.shape
    H = MLA_HEADS
    w_krope = jnp.pad(w_in[:, 384:416], ((0, 0), (MLA_NOPE, LANES - MLA_NOPE - MLA_ROPE)))
    w_in_p = jnp.concatenate([w_in[:, :384], w_krope, w_in[:, 416:]], axis=1).astype(BF16)
    w_uq_p = _head_slots(w_uq, H, MLA_QK).astype(BF16)
    w_ukv3 = w_ukv.reshape(MLA_KV_LORA, H, MLA_NOPE + MLA_V)
    w_uk_p = _head_slots(w_ukv3[:, :, :MLA_NOPE].reshape(MLA_KV_LORA, H * MLA_NOPE), H, MLA_NOPE).astype(BF16)
    w_uv = _head_slots(w_ukv3[:, :, MLA_NOPE:].reshape(MLA_KV_LORA, H * MLA_V), H, MLA_V).T.astype(BF16)
    pad = LANES - MLA_QK
    qg = jnp.pad(q_norm, (0, pad)).reshape(1, LANES)
    kg = jnp.pad(k_norm, (0, pad)).reshape(1, LANES)
    half = MLA_ROPE // 2
    inv_freq = ROPE_THETA ** (-jnp.arange(half, dtype=F32) / half)
    invf = jnp.zeros((LANES,), F32).at[MLA_NOPE:MLA_NOPE + half].set(inv_freq).at[MLA_NOPE + half:MLA_QK].set(inv_freq)
    sgn = jnp.zeros((LANES,), F32).at[MLA_NOPE:MLA_NOPE + half].set(-1.0).at[MLA_NOPE + half:MLA_QK].set(1.0)
    row = lambda n: pl.BlockSpec((tm, n), lambda i: (i, 0))
    n_in = w_in_p.shape[1]
    return pl.pallas_call(
        _inproj0_kernel,
        grid=(T // tm,),
        in_specs=[row(D), row(1), _full((1, D)), _full((D, n_in)), _full((1, MLA_Q_LORA)),
                  _full((MLA_Q_LORA, H * LANES)), _full((1, MLA_KV_LORA)), _full((MLA_KV_LORA, H * LANES)),
                  _full((H * LANES, MLA_KV_LORA)), _full((1, LANES)), _full((1, LANES)), _full((1, LANES)),
                  _full((1, LANES))],
        out_specs=[row(H * LANES), row(H * LANES), pl.BlockSpec((H * LANES, tm), lambda i: (0, i)), row(CONV_CH)],
        out_shape=[jax.ShapeDtypeStruct((T, H * LANES), BF16), jax.ShapeDtypeStruct((T, H * LANES), BF16),
                   jax.ShapeDtypeStruct((H * LANES, T), BF16), jax.ShapeDtypeStruct((T, CONV_CH), F32)],
        compiler_params=_params("parallel"),
        name="inproj0",
    )(x2d, pos_col, gmix.reshape(1, D), w_in_p, q_a_norm.reshape(1, -1), w_uq_p, kv_a_norm.reshape(1, -1),
      w_uk_p, w_uv, qg, kg, invf.reshape(1, LANES), sgn.reshape(1, LANES))


MLA_TQ = 512
MLA_TK = 512
MLA_ONE = MLA_V


def _mla_attn_kernel(q_ref, k_ref, vt_ref, o_ref):
    S = q_ref.shape[1]
    tq, tk = MLA_TQ, MLA_TK
    krow = lax.broadcasted_iota(I32, (tk, 1), 0)
    qcol = lax.broadcasted_iota(I32, (1, tq), 1)

    def q_block(qi, carry):
        q0 = pl.multiple_of(qi * tq, tq)
        qs = [q_ref[0, pl.ds(q0, tq), hh * LANES:(hh + 1) * LANES] for hh in range(2)]

        def kv_step(j, state, masked):
            k0 = pl.multiple_of(j * tk, tk)
            hs = [slice(hh * LANES, (hh + 1) * LANES) for hh in range(2)]
            ss = [_dot_nt(k_ref[0, pl.ds(k0, tk), hs[hh]], qs[hh]) for hh in range(2)]
            out = []
            for hh in range(2):
                m, acc = state[hh]
                s = jnp.where(k0 + krow <= q0 + qcol, ss[hh], NEG_INF) if masked else ss[hh]
                m_new = jnp.maximum(m, jnp.max(s, axis=0, keepdims=True))
                p = jnp.exp2(s - m_new).astype(BF16)
                out.append((m_new, jnp.exp2(m - m_new) * acc + _dot(vt_ref[hs[hh], pl.ds(k0, tk)], p)))
            return tuple(out)

        init = (jnp.full((1, tq), NEG_INF, F32), jnp.zeros((LANES, tq), F32))
        n_full = (qi * tq) // tk
        state = lax.fori_loop(0, n_full, functools.partial(kv_step, masked=False), (init, init))
        state = kv_step(n_full, state, True)
        for hh in range(2):
            acc = state[hh][1]
            o_t = acc * (1.0 / acc[MLA_ONE:MLA_ONE + 1, :])
            o_ref[0, pl.ds(q0, tq), hh * LANES:(hh + 1) * LANES] = o_t.T.astype(BF16)
        return carry

    lax.fori_loop(0, S // tq, q_block, 0)


def _mla_attn(q, k, vt):
    B, S, _ = q.shape
    spec = pl.BlockSpec((1, S, 2 * LANES), lambda b, h: (b, 0, h))
    return pl.pallas_call(
        _mla_attn_kernel,
        grid=(B, MLA_HEADS // 2),
        in_specs=[spec, spec, pl.BlockSpec((2 * LANES, S), lambda b, h: (h, b))],
        out_specs=spec,
        out_shape=jax.ShapeDtypeStruct((B, S, MLA_HEADS * LANES), BF16),
        compiler_params=_params("parallel", "parallel"),
        name="mla_attn",
    )(q, k, vt)


CONV_TILE = 64
CONV_PAD = 32


CONV_ROWS = 512


def _conv_out0_kernel(u_ref, o_ref, x_ref, dww_ref, dwb_ref, lng_ref, lnb_ref, wo_ref, out_ref, upad, act):
    step = pl.program_id(1)

    @pl.when(step == 0)
    def _():
        upad[0:CONV_PAD, :] = jnp.zeros((CONV_PAD, CONV_CH), F32)
        upad[CONV_PAD:, :] = u_ref[0]

    lead = CONV_PAD - (CONV_W - 1)
    base = step * CONV_ROWS

    def tile(i, carry):
        t0 = pl.multiple_of(i * CONV_TILE, CONV_TILE)
        win = upad[pl.ds(pl.multiple_of(base + t0, CONV_TILE), CONV_TILE + CONV_PAD), :]
        acc = jnp.zeros((CONV_TILE, CONV_CH), F32) + dwb_ref[...]
        for j in range(CONV_W):
            acc = acc + win[lead + j:lead + j + CONV_TILE, :] * dww_ref[j:j + 1, :]
        mu = jnp.mean(acc, axis=-1, keepdims=True)
        xc = acc - mu
        var = jnp.mean(xc * xc, axis=-1, keepdims=True)
        y = xc * lax.rsqrt(var + 1e-5) * lng_ref[...] + lnb_ref[...]
        act[pl.ds(t0, CONV_TILE), :] = (y * jax.nn.sigmoid(y)).astype(BF16)
        return carry

    lax.fori_loop(0, CONV_ROWS // CONV_TILE, tile, 0)
    n_o = o_ref.shape[2]
    y = _dot(o_ref[0], wo_ref[:n_o, :]) + _dot(act[...], wo_ref[n_o:, :])
    out_ref[0] = x_ref[0] + y


def _conv_out0(u, o_mla, x, dw_w, dw_b, ln_g, ln_b, w_out):
    B, S, D = x.shape
    n_o = o_mla.shape[2]
    n_v = MLA_HEADS * MLA_V
    w_attn = _head_slots(w_out[:n_v].T, MLA_HEADS, MLA_V).T
    w_out_p = jnp.concatenate([w_attn, w_out[n_v:]], axis=0).astype(BF16)
    tspec = lambda n: pl.BlockSpec((1, CONV_ROWS, n), lambda b, t: (b, t, 0))
    return pl.pallas_call(
        _conv_out0_kernel,
        grid=(B, S // CONV_ROWS),
        in_specs=[pl.BlockSpec((1, S, CONV_CH), lambda b, t: (b, 0, 0)), tspec(n_o), tspec(D),
                  _full((CONV_W, CONV_CH)), _full((1, CONV_CH)), _full((1, CONV_CH)), _full((1, CONV_CH)),
                  _full((n_o + CONV_CH, D))],
        out_specs=tspec(D),
        out_shape=jax.ShapeDtypeStruct((B, S, D), F32),
        scratch_shapes=[pltpu.VMEM((S + CONV_PAD, CONV_CH), F32), pltpu.VMEM((CONV_ROWS, CONV_CH), BF16)],
        compiler_params=_params("parallel", "arbitrary"),
        name="conv_out0",
    )(u, o_mla, x, dw_w, dw_b.reshape(1, -1), ln_g.reshape(1, -1), ln_b.reshape(1, -1), w_out_p)


def _mixer0(x, positions, gmix, w_in, q_a_norm, w_uq, kv_a_norm, w_ukv, q_norm, k_norm,
            dw_w, dw_b, ln_g, ln_b, w_out):
    B, S, D = x.shape
    T = B * S
    q, k, v, u = _inproj0(x.reshape(T, D), positions.reshape(T, 1), gmix, w_in, q_a_norm, w_uq, kv_a_norm, w_ukv,
                          q_norm, k_norm)
    o = _mla_attn(q.reshape(B, S, -1), k.reshape(B, S, -1), v)
    return _conv_out0(u.reshape(B, S, -1), o, x, dw_w, dw_b, ln_g, ln_b, w_out)


ROUTE_TILE = 512
ROUTE_ROWS = 40
SLABS = 8


def _route_kernel(x_ref, g_ref, wr_ref, br_ref, tri_ref, hn_ref, oi_ref, of_ref, cnt_ref, carry):
    @pl.when(pl.program_id(0) == 0)
    def _():
        carry[...] = jnp.zeros_like(carry)

    hn = _rms(x_ref[...]) * g_ref[...]
    for j in range(SLABS):
        hn_ref[:, j, :] = hn[:, j * LANES:(j + 1) * LANES]
    tm = hn.shape[0]
    logits = _dot_nt(wr_ref[...], hn, precision=lax.Precision.HIGHEST) + br_ref[...]
    gl = logits[MOE_EXPERTS:MOE_EXPERTS + MOE_GROUPS]
    rid_g = lax.broadcasted_iota(I32, (MOE_GROUPS, tm), 0)
    gmax = jnp.max(gl, axis=0, keepdims=True)
    grp = jnp.min(jnp.where(gl == gmax, rid_g, MOE_GROUPS), axis=0, keepdims=True)
    g_w = 1.0 / jnp.sum(jnp.exp(gl - gmax), axis=0, keepdims=True)
    e_in = jnp.zeros((MOE_EPG, tm), F32)
    for gi in range(MOE_GROUPS):
        e_in = jnp.where(grp == gi, logits[gi * MOE_EPG:(gi + 1) * MOE_EPG], e_in)
    rid_e = lax.broadcasted_iota(I32, (MOE_EPG, tm), 0)
    v1 = jnp.max(e_in, axis=0, keepdims=True)
    i1 = jnp.min(jnp.where(e_in == v1, rid_e, MOE_EPG), axis=0, keepdims=True)
    rest = jnp.where(rid_e == i1, -jnp.inf, e_in)
    v2 = jnp.max(rest, axis=0, keepdims=True)
    i2 = jnp.min(jnp.where(rest == v2, rid_e, MOE_EPG), axis=0, keepdims=True)
    ex = jnp.exp(v2 - v1)
    den = 1.0 + ex
    e1 = grp * MOE_EPG + i1
    e2 = grp * MOE_EPG + i2
    rid = lax.broadcasted_iota(I32, (MOE_EXPERTS, tm), 0)
    hit1 = rid == e1
    hit2 = rid == e2
    member = jnp.where(hit1 | hit2, 1.0, 0.0)
    before = _dot(member.astype(BF16), tri_ref[...]) + carry[...]
    r1 = jnp.sum(jnp.where(hit1, before, 0.0), axis=0, keepdims=True)
    r2 = jnp.sum(jnp.where(hit2, before, 0.0), axis=0, keepdims=True)
    carry[...] = carry[...] + jnp.sum(member, axis=1, keepdims=True)
    oi_ref[...] = jnp.zeros_like(oi_ref)
    oi_ref[0:1, :] = e1
    oi_ref[1:2, :] = e2
    oi_ref[2:3, :] = r1.astype(I32)
    oi_ref[3:4, :] = r2.astype(I32)
    of_ref[...] = jnp.zeros_like(of_ref)
    of_ref[0:1, :] = g_w / den
    of_ref[1:2, :] = g_w * ex / den
    cnt_ref[...] = jnp.broadcast_to(carry[...], cnt_ref.shape).astype(I32)


def _route(x2d, gain, router_group, router_group_b, router_expert, router_expert_b):
    T, D = x2d.shape
    tm = ROUTE_TILE
    pad = ROUTE_ROWS - MOE_EXPERTS - MOE_GROUPS
    wr = jnp.concatenate([router_expert.T, router_group.T, jnp.zeros((pad, D), F32)], axis=0)
    br = jnp.concatenate([router_expert_b, router_group_b, jnp.zeros((pad,), F32)]).reshape(ROUTE_ROWS, 1)
    tri = (jnp.arange(tm)[:, None] < jnp.arange(tm)[None, :]).astype(BF16)
    return pl.pallas_call(
        _route_kernel,
        grid=(T // tm,),
        in_specs=[pl.BlockSpec((tm, D), lambda i: (i, 0)), _full((1, D)), _full((ROUTE_ROWS, D)),
                  _full((ROUTE_ROWS, 1)), _full((tm, tm))],
        out_specs=[pl.BlockSpec((tm, SLABS, LANES), lambda i: (i, 0, 0)), pl.BlockSpec((8, tm), lambda i: (0, i)),
                   pl.BlockSpec((8, tm), lambda i: (0, i)), _full((MOE_EXPERTS, LANES))],
        out_shape=[jax.ShapeDtypeStruct((T, SLABS, LANES), F32), jax.ShapeDtypeStruct((8, T), I32),
                   jax.ShapeDtypeStruct((8, T), F32), jax.ShapeDtypeStruct((MOE_EXPERTS, LANES), I32)],
        scratch_shapes=[pltpu.VMEM((MOE_EXPERTS, 1), F32)],
        compiler_params=_params("arbitrary"),
        name="moe_route",
    )(x2d, gain.reshape(1, D), wr, br, tri)


DISPATCH_TILE = 512


def _dispatch_kernel(dest_ref, hn_ref, xs_init_ref, xs_ref, sem):
    del xs_init_ref
    tm = hn_ref.shape[0]
    base = pl.program_id(0) * (2 * tm)

    def copy(t, k):
        return pltpu.make_async_copy(hn_ref.at[t], xs_ref.at[dest_ref[base + 2 * t + k]], sem)

    def issue(t, carry):
        copy(t, 0).start()
        copy(t, 1).start()
        return carry

    def drain(t, carry):
        copy(t, 0).wait()
        copy(t, 1).wait()
        return carry

    lax.fori_loop(0, tm, issue, 0)
    lax.fori_loop(0, tm, drain, 0)


def _dispatch(dest, hn3, n_rows):
    T = hn3.shape[0]
    tm = DISPATCH_TILE
    return pl.pallas_call(
        _dispatch_kernel,
        grid_spec=pltpu.PrefetchScalarGridSpec(
            num_scalar_prefetch=1,
            grid=(T // tm,),
            in_specs=[pl.BlockSpec((tm, SLABS, LANES), lambda i, d: (i, 0, 0)), pl.BlockSpec(memory_space=pl.ANY)],
            out_specs=pl.BlockSpec(memory_space=pl.ANY),
            scratch_shapes=[pltpu.SemaphoreType.DMA(())],
        ),
        out_shape=jax.ShapeDtypeStruct((n_rows, SLABS, LANES), F32),
        input_output_aliases={2: 0},
        compiler_params=_params("arbitrary"),
        name="moe_dispatch",
    )(dest, hn3, jnp.zeros((n_rows, SLABS, LANES), F32))


def _expert_kernel(blk_e_ref, n_used_ref, xs_ref, wgu_ref, wd_ref, ys_ref):
    del blk_e_ref

    @pl.when(pl.program_id(0) < n_used_ref[0])
    def _():
        xrow = jnp.concatenate([xs_ref[:, j, :] for j in range(SLABS)], axis=1).astype(BF16)
        gu = _dot(xrow, wgu_ref[0])
        gate = gu[:, :MOE_HIDDEN]
        hid = (gate * jax.nn.sigmoid(gate) * gu[:, MOE_HIDDEN:]).astype(BF16)
        y = _dot(hid, wd_ref[0])
        for j in range(SLABS):
            ys_ref[:, j, :] = y[:, j * LANES:(j + 1) * LANES]

    @pl.when(pl.program_id(0) >= n_used_ref[0])
    def _():
        ys_ref[...] = jnp.zeros_like(ys_ref)


def _experts(blk_expert, n_used, xs, w_gu, w_d):
    R = xs.shape[0]
    rb = MOE_ROW_BLOCK
    D = w_gu.shape[1]
    return pl.pallas_call(
        _expert_kernel,
        grid_spec=pltpu.PrefetchScalarGridSpec(
            num_scalar_prefetch=2,
            grid=(R // rb,),
            in_specs=[pl.BlockSpec((rb, SLABS, LANES), lambda b, be, nu: (jnp.minimum(b, nu[0] - 1), 0, 0)),
                      pl.BlockSpec((1, D, 2 * MOE_HIDDEN), lambda b, be, nu: (be[b], 0, 0)),
                      pl.BlockSpec((1, MOE_HIDDEN, D), lambda b, be, nu: (be[b], 0, 0))],
            out_specs=pl.BlockSpec((rb, SLABS, LANES), lambda b, be, nu: (b, 0, 0)),
        ),
        out_shape=jax.ShapeDtypeStruct((R, SLABS, LANES), F32),
        compiler_params=_params("arbitrary"),
        name="moe_experts",
    )(blk_expert, n_used, xs, w_gu, w_d)


def _combine_kernel(dest_ref, x_ref, gate_ref, ys_ref, out_ref, buf0, buf1, sem):
    tm = x_ref.shape[0]
    base = pl.program_id(0) * (2 * tm)

    def copy(t, k):
        return pltpu.make_async_copy(ys_ref.at[dest_ref[base + 2 * t + k]], (buf0, buf1)[k].at[t], sem)

    def issue(t, carry):
        copy(t, 0).start()
        copy(t, 1).start()
        return carry

    def drain(t, carry):
        copy(t, 0).wait()
        copy(t, 1).wait()
        return carry

    lax.fori_loop(0, tm, issue, 0)
    lax.fori_loop(0, tm, drain, 0)
    g0 = gate_ref[:, 0:1]
    g1 = gate_ref[:, 1:2]
    for j in range(SLABS):
        sl = slice(j * LANES, (j + 1) * LANES)
        out_ref[:, sl] = x_ref[:, sl] + g0 * buf0[:, j, :] + g1 * buf1[:, j, :]


def _combine(dest, x2d, gates_col, ys):
    T, D = x2d.shape
    tm = DISPATCH_TILE
    return pl.pallas_call(
        _combine_kernel,
        grid_spec=pltpu.PrefetchScalarGridSpec(
            num_scalar_prefetch=1,
            grid=(T // tm,),
            in_specs=[pl.BlockSpec((tm, D), lambda i, d: (i, 0)), pl.BlockSpec((tm, 2), lambda i, d: (i, 0)),
                      pl.BlockSpec(memory_space=pl.ANY)],
            out_specs=pl.BlockSpec((tm, D), lambda i, d: (i, 0)),
            scratch_shapes=[pltpu.VMEM((tm, SLABS, LANES), F32), pltpu.VMEM((tm, SLABS, LANES), F32),
                            pltpu.SemaphoreType.DMA(())],
        ),
        out_shape=jax.ShapeDtypeStruct((T, D), F32),
        compiler_params=_params("arbitrary"),
        name="moe_combine",
    )(dest, x2d, gates_col, ys)


def _moe(x, gain, router_group, router_group_b, router_expert, router_expert_b, w_gate, w_up, w_down):
    B, S, D = x.shape
    T = B * S
    x2d = x.reshape(T, D)
    hn3, oi, of, cnt = _route(x2d, gain, router_group, router_group_b, router_expert, router_expert_b)
    rb = MOE_ROW_BLOCK
    counts = cnt[:, 0]
    cap = (counts + rb - 1) // rb * rb
    cap_end = jnp.cumsum(cap)
    start = cap_end - cap
    eid = oi[0:2].T
    first = jnp.sum(jnp.where(eid[:, :, None] == jnp.arange(MOE_EXPERTS), start, 0), axis=-1)
    dest = (first + oi[2:4].T).reshape(2 * T).astype(I32)
    n_rows = 2 * T + MOE_EXPERTS * rb
    n_blk = n_rows // rb
    blk_first_row = jnp.arange(n_blk, dtype=I32) * rb
    blk_expert = jnp.minimum(jnp.sum(cap_end[None, :] <= blk_first_row[:, None], axis=1), MOE_EXPERTS - 1)
    n_used = (cap_end[-1:] // rb).astype(I32)
    xs = _dispatch(dest, hn3, n_rows)
    w_gu = jnp.concatenate([w_gate, w_up], axis=2).astype(BF16)
    ys = _experts(blk_expert.astype(I32), n_used, xs, w_gu, w_down.astype(BF16))
    out = _combine(dest, x2d, of[0:2].T, ys)
    return out.reshape(B, S, D)


NSA_TQ = 128
NSA_PAIRS = NSA_HPG // 2
N_SEL = 32
NSA_Q_SCALE = NSA_DK ** -0.5 * LOG2E
LO_ONE = LANES - 1
HI_ONE = 0
SLC_TK = 512
WIN_SPAN = WINDOW + NSA_TQ
AUG_POS = N_SEL
MASK_BIG = 1e30


def _pair_norm(t, gain2, lane_lo):
    t2 = t * t
    s_lo = jnp.sum(jnp.where(lane_lo, t2, 0.0), axis=-1, keepdims=True)
    s_hi = jnp.sum(jnp.where(lane_lo, 0.0, t2), axis=-1, keepdims=True)
    inv = jnp.where(lane_lo, lax.rsqrt(s_lo * (1.0 / NSA_DK) + 1e-6), lax.rsqrt(s_hi * (1.0 / NSA_DK) + 1e-6))
    return t * inv * gain2


def _inproj1_kernel(x_ref, kaug_ref, g_ref, win_ref, qg_ref, ksg_ref, kwg_ref,
                    q_out, kc_out, vc_out, ks_out, vs_out, kw_out, vw_out, gate_out):
    h = _rms(x_ref[...]) * g_ref[...]
    proj = _dot(h.astype(BF16), win_ref[...])
    lane = lax.broadcasted_iota(I32, (1, LANES), 1)
    lane_lo = lane < NSA_DK
    nq = NSA_HEADS * NSA_DK
    for p in range(nq // LANES):
        sl = slice(p * LANES, (p + 1) * LANES)
        q_out[:, sl] = (_pair_norm(proj[:, sl], qg_ref[...], lane_lo) * NSA_Q_SCALE).astype(BF16)
    part = lambda i: proj[:, nq + i * LANES:nq + (i + 1) * LANES]
    kc_out[...] = part(0)
    vc_out[...] = part(1)

    def lo_hi(t, lo_pad, hi_pad):
        r = pltpu.roll(t, NSA_DK, 1)
        return jnp.concatenate([jnp.where(lane_lo, t, lo_pad), jnp.where(lane_lo, hi_pad, r),
                                jnp.where(lane_lo, r, lo_pad), jnp.where(lane_lo, hi_pad, t)], axis=1).astype(BF16)

    kaug = kaug_ref[...].astype(F32)
    one_lo = jnp.where(lane == LO_ONE, 1.0, 0.0)
    one_hi = jnp.where(lane == HI_ONE, 1.0, 0.0)
    ks_out[...] = lo_hi(_pair_norm(part(2), ksg_ref[...], lane_lo), kaug, kaug)
    vs_out[...] = lo_hi(part(3), one_lo, one_hi)
    kw_out[...] = lo_hi(_pair_norm(part(4), kwg_ref[...], lane_lo), kaug, kaug)
    vw_out[...] = lo_hi(part(5), one_lo, one_hi)
    gate_out[...] = jax.nn.sigmoid(part(6))


def _inproj1(x2d, kaug, gain, w_in, q_norm, k_norm_slc, k_norm_win, tm=256):
    T, D = x2d.shape
    n_in = w_in.shape[1]
    n_pad = -n_in % LANES
    w_in_p = jnp.pad(w_in, ((0, 0), (0, n_pad))).astype(BF16)
    two = lambda g: jnp.tile(g, 2).reshape(1, LANES)
    row = lambda n: pl.BlockSpec((tm, n), lambda i: (i, 0))
    nq = NSA_HEADS * NSA_DK
    bf = lambda n: jax.ShapeDtypeStruct((T, n), BF16)
    f32 = lambda n: jax.ShapeDtypeStruct((T, n), F32)
    return pl.pallas_call(
        _inproj1_kernel,
        grid=(T // tm,),
        in_specs=[row(D), row(LANES), _full((1, D)), _full((D, n_in + n_pad)), _full((1, LANES)), _full((1, LANES)),
                  _full((1, LANES))],
        out_specs=[row(nq), row(LANES), row(LANES), row(4 * LANES), row(4 * LANES), row(4 * LANES), row(4 * LANES),
                   row(LANES)],
        out_shape=[bf(nq), f32(LANES), f32(LANES), bf(4 * LANES), bf(4 * LANES), bf(4 * LANES), bf(4 * LANES),
                   f32(LANES)],
        compiler_params=_params("parallel"),
        name="inproj1",
    )(x2d, kaug, gain.reshape(1, D), w_in_p, two(q_norm), two(k_norm_slc), two(k_norm_win))


def _stack_pairs(q_ref, rows):
    return jnp.concatenate([q_ref[0, rows, p * LANES:(p + 1) * LANES] for p in range(NSA_PAIRS)], axis=0)


def _gate_tile(gt, branch, p, lane_lo):
    c = branch * NSA_HPG + 2 * p
    return jnp.where(lane_lo, gt[:, c:c + 1], gt[:, c + 1:c + 2])


def _cmp_kernel(q_ref, kch_ref, vch_ref, pos3_ref, pend_ref, gate_ref, slope_ref,
                pek_ref, w1k_ref, w2k_ref, kcg_ref, pev_ref, w1v_ref, w2v_ref, ovl_ref, eye_ref,
                o_ref, sel_ref, kc_s, vc_s):
    tq = NSA_TQ
    S = q_ref.shape[1]

    def compress(ch_ref, pe_ref, w1_ref, w2_ref):
        a = ch_ref[0, 0]
        h_lo = _dot((a + pe_ref[0:1, :]).astype(BF16), w1_ref[0])
        h_hi = _dot((a + pe_ref[1:2, :]).astype(BF16), w1_ref[1])
        n = h_hi.shape[0]
        hid = jax.nn.gelu(h_lo + pltpu.roll(h_hi, n - 1, 0)).astype(BF16)
        return _dot(hid, w2_ref[0]), _dot(hid, w2_ref[1])

    k_lo, k_hi = compress(kch_ref, pek_ref, w1k_ref, w2k_ref)
    for e, kk in enumerate((k_lo, k_hi)):
        kk = kk * lax.rsqrt(jnp.sum(kk * kk, axis=-1, keepdims=True) * (1.0 / NSA_DK) + 1e-6) * kcg_ref[e:e + 1, :]
        kc_s[e] = kk.astype(BF16)
    v_lo, v_hi = compress(vch_ref, pev_ref, w1v_ref, w2v_ref)
    vc_s[0] = v_lo.astype(BF16)
    vc_s[1] = v_hi.astype(BF16)

    n_cmp = (S - CMP_BLOCK) // CMP_STRIDE + 1
    lane = lax.broadcasted_iota(I32, (1, LANES), 1)
    lane_lo = lane < NSA_DK
    blk_row = lax.broadcasted_iota(I32, (N_SEL, 1), 0)

    def q_block(qi, carry):
        t0 = pl.multiple_of(qi * tq, tq)
        rows = pl.ds(t0, tq)
        qs = _stack_pairs(q_ref, rows)
        pos_q0 = pos3_ref[0, pl.ds(qi, 1), :][:, 0:1]
        posrel = (pend_ref[0] - pos_q0).astype(F32)
        tcol = t0 + lax.broadcasted_iota(I32, (tq, 1), 0)
        valid = ((tcol >= CMP_STRIDE * lane + (CMP_BLOCK - 1)) & (lane < n_cmp))[None]
        psum = jnp.zeros((tq, LANES), F32)
        o_pairs = jnp.zeros((NSA_PAIRS * tq, LANES), F32)
        for e in range(2):
            s = _dot_nt(qs, kc_s[e]).reshape(NSA_PAIRS, tq, LANES)
            s = s + slope_ref[0, NSA_PAIRS * e:NSA_PAIRS * (e + 1)][:, None, :] * posrel[None]
            s = jnp.where(valid, s, NEG_INF)
            m = jnp.max(s, axis=-1, keepdims=True)
            p = jnp.where(valid, jnp.exp2(s - m), 0.0)
            p = p / jnp.maximum(jnp.sum(p, axis=-1, keepdims=True), 1e-20)
            psum = psum + jnp.sum(p, axis=0)
            o_pairs = o_pairs + _dot(p.reshape(NSA_PAIRS * tq, LANES).astype(BF16), vc_s[e])
        imp = _dot_nt(ovl_ref[...], psum, precision=lax.Precision.HIGHEST)
        cur = (t0 + lax.broadcasted_iota(I32, (1, tq), 1)) // SLC_BLOCK
        forced = (blk_row == 0) | (blk_row == cur) | (blk_row == cur - 1)
        imp = jnp.where(forced, FORCE_SCORE, jnp.where(blk_row <= cur, imp, -1.0))
        rank = jnp.zeros((N_SEL, tq), I32)
        for i in range(N_SEL):
            ri = imp[i:i + 1, :]
            rank = rank + jnp.where((ri > imp) | ((ri == imp) & (blk_row > i)), 1, 0)
        sel_t = jnp.where(rank < SLC_TOP_N, 1.0, 0.0).astype(BF16)
        gap = jnp.zeros((NSA_DK - N_SEL, tq), BF16)
        sel_t = jnp.concatenate([sel_t, gap, sel_t, gap], axis=0)
        sel = _dot_nt(eye_ref[...], sel_t)
        sel_ref[0, 0, rows, :] = jnp.where((lane & (NSA_DK - 1)) < N_SEL, sel - 1.0, 0.0).astype(BF16)
        gt = gate_ref[0, 0, rows, :]
        o3 = o_pairs.reshape(NSA_PAIRS, tq, LANES)
        for p in range(NSA_PAIRS):
            o_ref[0, rows, p * LANES:(p + 1) * LANES] = (o3[p] * _gate_tile(gt, 0, p, lane_lo)).astype(BF16)
        return carry

    lax.fori_loop(0, S // tq, q_block, 0)


def _lo_hi_cols(w):
    z = jnp.zeros_like(w)
    return jnp.stack([jnp.concatenate([w, z], axis=1), jnp.concatenate([z, w], axis=1)])


def _nsa_cmp(q, kch, vch, pos3, pend, gates_g, slopes_t, pe_k, w1_k, w2_k, k_norm_cmp, pe_v, w1_v, w2_v):
    B, S, _ = q.shape
    G = NSA_G
    nch = S // CMP_STRIDE
    half = CMP_STRIDE * NSA_DK
    pe2 = lambda pe: pe.reshape(2, half)
    w1_2 = lambda w: w.reshape(2, half, CMP_HIDDEN).astype(BF16)
    n_sel = S // SLC_BLOCK
    n_cmp = (S - CMP_BLOCK) // CMP_STRIDE + 1
    cmp_start = np.arange(LANES) * CMP_STRIDE
    slc_start = np.arange(n_sel) * SLC_BLOCK
    overlap = ((cmp_start[None, :] < slc_start[:, None] + SLC_BLOCK) & (cmp_start[None, :] + CMP_BLOCK > slc_start[:, None])
               & (np.arange(LANES)[None, :] < n_cmp))
    ovl = jnp.asarray(overlap, F32)
    eye = jnp.eye(NSA_TQ, dtype=BF16)
    qspec = pl.BlockSpec((1, S, NSA_PAIRS * LANES), lambda b, g: (b, 0, g))
    chspec = pl.BlockSpec((1, 1, nch, half), lambda b, g: (b, g, 0, 0))
    return pl.pallas_call(
        _cmp_kernel,
        grid=(B, G),
        in_specs=[qspec, chspec, chspec,
                  pl.BlockSpec((1, S // LANES, LANES), lambda b, g: (b, 0, 0)),
                  pl.BlockSpec((1, 1, LANES), lambda b, g: (b, 0, 0)),
                  pl.BlockSpec((1, 1, S, 3 * NSA_HPG), lambda b, g: (b, g, 0, 0)),
                  pl.BlockSpec((1, NSA_HPG, LANES), lambda b, g: (g, 0, 0)),
                  _full((2, half)), _full((2, half, CMP_HIDDEN)), _full((2, CMP_HIDDEN, LANES)), _full((2, LANES)),
                  _full((2, half)), _full((2, half, CMP_HIDDEN)), _full((2, CMP_HIDDEN, LANES)),
                  _full((n_sel, LANES)), _full((NSA_TQ, NSA_TQ))],
        out_specs=[qspec, pl.BlockSpec((1, 1, S, LANES), lambda b, g: (b, g, 0, 0))],
        out_shape=[jax.ShapeDtypeStruct((B, S, NSA_HEADS * NSA_DK), BF16),
                   jax.ShapeDtypeStruct((B, G, S, LANES), BF16)],
        scratch_shapes=[pltpu.VMEM((2, nch, LANES), BF16), pltpu.VMEM((2, nch, LANES), BF16)],
        compiler_params=_params("parallel", "parallel"),
        name="nsa_cmp",
    )(q, kch, vch, pos3, pend, gates_g, slopes_t, pe2(pe_k), w1_2(w1_k), _lo_hi_cols(w2_k).astype(BF16),
      _lo_hi_cols(k_norm_cmp.reshape(1, -1)).reshape(2, LANES), pe2(pe_v), w1_2(w1_v),
      _lo_hi_cols(w2_v).astype(BF16), ovl, eye)


def _slc_win_kernel(q_ref, ks_ref, vs_ref, kw_ref, vw_ref, sel_ref, gate_ref, slopeq_ref, ocmp_ref, o_ref):
    tq = NSA_TQ
    S = q_ref.shape[1]
    lane = lax.broadcasted_iota(I32, (1, LANES), 1)
    lane_lo = lane < NSA_DK
    ones_lane = (LO_ONE, HI_ONE)

    def q_block(qi, carry):
        t0 = pl.multiple_of(qi * tq, tq)
        rows = pl.ds(t0, tq)
        qs = _stack_pairs(q_ref, rows)
        tcol = t0 + lax.broadcasted_iota(I32, (tq, 1), 0)
        selm1 = sel_ref[0, 0, rows, :]

        def q_aug(e, with_sel):
            feats = [slopeq_ref[0, NSA_PAIRS * e + p:NSA_PAIRS * e + p + 1, :] for p in range(NSA_PAIRS)]
            if with_sel:
                extra = jnp.concatenate([selm1 + f for f in feats], axis=0)
            else:
                extra = jnp.concatenate([jnp.broadcast_to(f, (tq, LANES)) for f in feats], axis=0)
            return jnp.where(lane_lo, qs, extra) if e == 0 else jnp.where(lane_lo, extra, qs)

        def k_aug(k_ref, e, r0, n):
            return k_ref[0, pl.ds(r0, n), e * LANES:(e + 1) * LANES]

        def normalise(accs):
            outs = [acc * (1.0 / acc[:, :, ones_lane[e]:ones_lane[e] + 1]) for e, acc in enumerate(accs)]
            return jnp.where(lane_lo, outs[0], outs[1])

        qa = [q_aug(0, True), q_aug(1, True)]
        kcol = lax.broadcasted_iota(I32, (1, SLC_TK), 1)

        def slc_step(j, state, last):
            r0 = pl.multiple_of(j * SLC_TK, SLC_TK)
            out = []
            ss = [_dot_nt(qa[e], k_aug(ks_ref, e, r0, SLC_TK)).reshape(NSA_PAIRS, tq, SLC_TK) for e in range(2)]
            for e in range(2):
                m, acc = state[e]
                s = ss[e]
                if last:
                    s = jnp.where((r0 + kcol <= tcol)[None], s, NEG_INF)
                m_new = jnp.maximum(m, jnp.max(s, axis=-1, keepdims=True))
                p = jnp.exp2(s - m_new).reshape(NSA_PAIRS * tq, SLC_TK).astype(BF16)
                pv = _dot(p, vs_ref[0, pl.ds(r0, SLC_TK), e * LANES:(e + 1) * LANES])
                out.append((m_new, jnp.exp2(m - m_new) * acc + pv.reshape(NSA_PAIRS, tq, LANES)))
            return tuple(out)

        init = (jnp.full((NSA_PAIRS, tq, 1), NEG_INF, F32), jnp.zeros((NSA_PAIRS, tq, LANES), F32))
        n_full = qi // (SLC_TK // tq)
        state = lax.fori_loop(0, n_full, functools.partial(slc_step, last=False), (init, init))
        state = slc_step(n_full, state, True)
        slc = normalise([state[0][1], state[1][1]])

        w0 = pl.multiple_of(jnp.maximum(t0 - WINDOW, 0), tq)
        rel = tcol - (w0 + lax.broadcasted_iota(I32, (1, WIN_SPAN), 1))
        wmask = ((rel >= 0) & (rel < WINDOW))[None]
        accs = []
        for e in range(2):
            s = _dot_nt(q_aug(e, False), k_aug(kw_ref, e, w0, WIN_SPAN)).reshape(NSA_PAIRS, tq, WIN_SPAN)
            s = jnp.where(wmask, s, NEG_INF)
            p = jnp.exp2(s - jnp.max(s, axis=-1, keepdims=True)).reshape(NSA_PAIRS * tq, WIN_SPAN).astype(BF16)
            accs.append(_dot(p, vw_ref[0, pl.ds(w0, WIN_SPAN), e * LANES:(e + 1) * LANES]).reshape(NSA_PAIRS, tq, LANES))
        win = normalise(accs)

        gt = gate_ref[0, 0, rows, :]
        for p in range(NSA_PAIRS):
            sl = slice(p * LANES, (p + 1) * LANES)
            o = (ocmp_ref[0, rows, sl].astype(F32) + slc[p] * _gate_tile(gt, 1, p, lane_lo)
                 + win[p] * _gate_tile(gt, 2, p, lane_lo))
            o_ref[0, rows, sl] = o.astype(BF16)
        return carry

    lax.fori_loop(0, S // tq, q_block, 0)


def _key_aug(positions):
    B, S = positions.shape
    prel = positions - positions[:, :1]
    byte = lambda k: ((prel >> (8 * k)) & 255).astype(F32)
    pos_bytes = jnp.stack([byte(2), byte(1), byte(0)] * 2, axis=-1)
    onehot = (np.arange(S)[:, None] // SLC_BLOCK == np.arange(N_SEL)[None, :]) * MASK_BIG
    blocks = jnp.broadcast_to(jnp.asarray(onehot, F32), (B, S, N_SEL))
    pad = jnp.zeros((B, S, NSA_DK - N_SEL - 6), F32)
    half = jnp.concatenate([blocks, pos_bytes, pad], axis=-1)
    return jnp.concatenate([half, half], axis=-1).astype(BF16).reshape(B * S, LANES)


def _slope_aug(slopes_eo):
    hi = slopes_eo.astype(BF16).astype(F32)
    lo = (slopes_eo - hi).astype(BF16).astype(F32)
    w = jnp.asarray([65536.0, 256.0, 1.0], F32)
    feats = jnp.concatenate([hi[..., None] * w, lo[..., None] * w], axis=-1)
    half = jnp.pad(feats, ((0, 0), (0, 0), (AUG_POS, NSA_DK - AUG_POS - 6)))
    return jnp.concatenate([half, half], axis=-1).astype(BF16)


def _nsa_slc_win(q, ks4, vs4, kw4, vw4, sel, gates_g, slopeq, o_cmp):
    B, S, _ = q.shape
    qspec = pl.BlockSpec((1, S, NSA_PAIRS * LANES), lambda b, g: (b, 0, g))
    kvspec = pl.BlockSpec((1, S, 2 * LANES), lambda b, g: (b, 0, g))
    return pl.pallas_call(
        _slc_win_kernel,
        grid=(B, NSA_G),
        in_specs=[qspec, kvspec, kvspec, kvspec, kvspec,
                  pl.BlockSpec((1, 1, S, LANES), lambda b, g: (b, g, 0, 0)),
                  pl.BlockSpec((1, 1, S, 3 * NSA_HPG), lambda b, g: (b, g, 0, 0)),
                  pl.BlockSpec((1, NSA_HPG, LANES), lambda b, g: (g, 0, 0)),
                  qspec],
        out_specs=qspec,
        out_shape=jax.ShapeDtypeStruct((B, S, NSA_HEADS * NSA_DK), BF16),
        compiler_params=_params("parallel", "parallel"),
        name="nsa_slc_win",
    )(q, ks4, vs4, kw4, vw4, sel, gates_g, slopeq, o_cmp)


def _outproj_kernel(o_ref, x_ref, w_ref, out_ref):
    out_ref[...] = x_ref[...] + _dot(o_ref[...], w_ref[...])


def _outproj(o2d, x2d, w_out, tm=512):
    T, D = x2d.shape
    K = o2d.shape[1]
    return pl.pallas_call(
        _outproj_kernel,
        grid=(T // tm,),
        in_specs=[pl.BlockSpec((tm, K), lambda i: (i, 0)), pl.BlockSpec((tm, D), lambda i: (i, 0)), _full((K, D))],
        out_specs=pl.BlockSpec((tm, D), lambda i: (i, 0)),
        out_shape=jax.ShapeDtypeStruct((T, D), F32),
        compiler_params=_params("parallel"),
        name="outproj1",
    )(o2d, x2d, w_out.astype(BF16))


def _mixer1(x, positions, gain, w_in, q_norm, k_norm_cmp, k_norm_slc, k_norm_win,
            pe_k, w1_k, w2_k, pe_v, w1_v, w2_v, w_out):
    B, S, D = x.shape
    T = B * S
    G, HPG, DK = NSA_G, NSA_HPG, NSA_DK
    assert S // SLC_BLOCK == N_SEL and S % NSA_TQ == 0 and NSA_TQ == LANES
    x2d = x.reshape(T, D)
    q, kc, vc, ks4, vs4, kw4, vw4, gates = _inproj1(x2d, _key_aug(positions), gain, w_in, q_norm, k_norm_slc,
                                                    k_norm_win)
    b3 = lambda t: t.reshape(B, S, -1)
    chunks = lambda t: t.reshape(B, S, G, DK).transpose(0, 2, 1, 3).reshape(B, G, S // CMP_STRIDE, CMP_STRIDE * DK)
    gates_g = gates[:, :3 * NSA_HEADS].reshape(B, S, 3, G, HPG).transpose(0, 3, 1, 2, 4).reshape(B, G, S, 3 * HPG)
    n_cmp = (S - CMP_BLOCK) // CMP_STRIDE + 1
    pend = jnp.pad(positions[:, CMP_BLOCK - 1::CMP_STRIDE][:, :n_cmp], ((0, 0), (0, LANES - n_cmp))).reshape(B, 1, LANES)
    pos3 = positions.reshape(B, S // LANES, LANES)
    slopes = jnp.asarray(2.0 ** (-8.0 * np.arange(1, NSA_HEADS + 1) / NSA_HEADS), F32) * LOG2E
    slopes_eo = slopes.reshape(G, NSA_PAIRS, 2).transpose(0, 2, 1).reshape(G, HPG)
    slopes_t = jnp.broadcast_to(slopes_eo[:, :, None], (G, HPG, LANES))
    o_cmp, sel = _nsa_cmp(b3(q), chunks(kc), chunks(vc), pos3, pend, gates_g, slopes_t,
                          pe_k, w1_k, w2_k, k_norm_cmp, pe_v, w1_v, w2_v)
    o = _nsa_slc_win(b3(q), b3(ks4), b3(vs4), b3(kw4), b3(vw4), sel, gates_g, _slope_aug(slopes_eo), o_cmp)
    return _outproj(o.reshape(T, -1), x2d, w_out).reshape(B, S, D)


def kernel(x, positions, norm_mix, norm_ffn, mix0_w_in, mla_q_a_norm, mla_w_uq, mla_kv_a_norm, mla_w_ukv, mla_q_norm, mla_k_norm, conv_dw_w, conv_dw_b, conv_ln_g, conv_ln_b, mix0_w_out, nsa_w_in, nsa_q_norm, nsa_k_norm_cmp, nsa_k_norm_slc, nsa_k_norm_win, nsa_cmp_pe_k, nsa_cmp_w1_k, nsa_cmp_w2_k, nsa_cmp_pe_v, nsa_cmp_w1_v, nsa_cmp_w2_v, nsa_w_out, moe_router_group, moe_router_group_b, moe_router_expert, moe_router_expert_b, moe_w_gate, moe_w_up, moe_w_down):
    def moe(x, layer):
        return _moe(x, norm_ffn[layer], moe_router_group[layer], moe_router_group_b[layer], moe_router_expert[layer],
                    moe_router_expert_b[layer], moe_w_gate[layer], moe_w_up[layer], moe_w_down[layer])

    x = _mixer0(x, positions, norm_mix[0], mix0_w_in[0], mla_q_a_norm[0], mla_w_uq[0], mla_kv_a_norm[0], mla_w_ukv[0],
                mla_q_norm[0], mla_k_norm[0], conv_dw_w[0], conv_dw_b[0], conv_ln_g[0], conv_ln_b[0], mix0_w_out[0])
    x = moe(x, 0)
    x = _mixer1(x, positions, norm_mix[1], nsa_w_in[0], nsa_q_norm[0], nsa_k_norm_cmp[0], nsa_k_norm_slc[0],
                nsa_k_norm_win[0], nsa_cmp_pe_k[0], nsa_cmp_w1_k[0], nsa_cmp_w2_k[0], nsa_cmp_pe_v[0],
                nsa_cmp_w1_v[0], nsa_cmp_w2_v[0], nsa_w_out[0])
    return moe(x, 1)
```

```python
import functools

import numpy as np
import jax
import jax.numpy as jnp
from jax import lax
from jax.experimental import pallas as pl
from jax.experimental.pallas import tpu as pltpu

F32 = jnp.float32
BF16 = jnp.bfloat16
I32 = jnp.int32

LANES = 128
NEG_INF = -1e30
LOG2E = 1.4426950408889634
Q_BLOCK = 128
MLA_HEADS = 8
MLA_Q_LORA = 256
MLA_KV_LORA = 128
MLA_NOPE = 64
MLA_ROPE = 32
MLA_V = 64
MLA_QK = MLA_NOPE + MLA_ROPE
ROPE_THETA = 10000.0
CONV_CH = 512
CONV_W = 31
NSA_HEADS = 16
NSA_G = 2
NSA_HPG = NSA_HEADS // NSA_G
NSA_DK = 64
NSA_KVW = NSA_G * NSA_DK
CMP_BLOCK = 32
CMP_STRIDE = 16
CMP_HIDDEN = 128
SLC_BLOCK = 64
SLC_TOP_N = 8
WINDOW = 256
FORCE_SCORE = 1e4
MOE_GROUPS = 4
MOE_EPG = 8
MOE_EXPERTS = MOE_GROUPS * MOE_EPG
MOE_HIDDEN = 256
MOE_ROW_BLOCK = 256

VMEM_LIMIT = 56 * 1024 * 1024


def _params(*sem):
    return pltpu.CompilerParams(dimension_semantics=sem, vmem_limit_bytes=VMEM_LIMIT)


def _full(shape):
    n = len(shape)
    return pl.BlockSpec(shape, lambda *_: (0,) * n)


def _rms(x, eps=1e-6):
    return x * lax.rsqrt(jnp.mean(x * x, axis=-1, keepdims=True) + eps)


def _dot(a, b):
    return jnp.dot(a, b, preferred_element_type=F32)


def _dot_nt(a, b, **kw):
    return lax.dot_general(a, b, (((1,), (1,)), ((), ())), preferred_element_type=F32, **kw)


def _inproj0_kernel(x_ref, pos_ref, gmix_ref, win_ref, qan_ref, wuq_ref, kvan_ref, wuk_ref, wuv_ref,
                    qg_ref, kg_ref, invf_ref, sgn_ref, q_out, k_out, v_out, u_out):
    h = _rms(x_ref[...]) * gmix_ref[...]
    proj = _dot(h.astype(BF16), win_ref[...])
    c_q = proj[:, :MLA_Q_LORA]
    c_kv = proj[:, MLA_Q_LORA:MLA_Q_LORA + MLA_KV_LORA]
    k_rope = proj[:, 384:512]
    a = proj[:, 512:512 + CONV_CH]
    g = proj[:, 512 + CONV_CH:]
    u_out[...] = a * jax.nn.sigmoid(g)
    cqn = (_rms(c_q) * qan_ref[...]).astype(BF16)
    ckvn = (_rms(c_kv) * kvan_ref[...]).astype(BF16)
    q = _dot(cqn, wuq_ref[...])
    kn = _dot(ckvn, wuk_ref[...])
    slot_row = lax.broadcasted_iota(I32, (MLA_HEADS * LANES, 1), 0) & (LANES - 1)
    v_out[...] = (_dot_nt(wuv_ref[...], ckvn) + jnp.where(slot_row == MLA_V, 1.0, 0.0)).astype(BF16)
    ang = pos_ref[...].astype(F32) * invf_ref[...]
    cos = jnp.cos(ang)
    sin = jnp.sin(ang) * sgn_ref[...]
    lane = lax.broadcasted_iota(I32, (1, LANES), 1)
    first_half = (lane >= MLA_NOPE) & (lane < MLA_NOPE + MLA_ROPE // 2)
    scale = MLA_QK ** -0.5 * LOG2E

    def norm_rope(t, gain):
        t = t * lax.rsqrt(jnp.sum(t * t, axis=-1, keepdims=True) * (1.0 / MLA_QK) + 1e-6) * gain
        partner = jnp.where(first_half, pltpu.roll(t, LANES - MLA_ROPE // 2, 1), pltpu.roll(t, MLA_ROPE // 2, 1))
        return t * cos + partner * sin

    for hd in range(MLA_HEADS):
        sl = slice(hd * LANES, (hd + 1) * LANES)
        q_out[:, sl] = (norm_rope(q[:, sl], qg_ref[...]) * scale).astype(BF16)
        k_out[:, sl] = norm_rope(kn[:, sl] + k_rope, kg_ref[...]).astype(BF16)


def _head_slots(w, n_heads, width, offset=0):
    k = w.shape[0]
    w = w.reshape(k, n_heads, width)
    w = jnp.pad(w, ((0, 0), (0, 0), (offset, LANES - width - offset)))
    return w.reshape(k, n_heads * LANES)


def _inproj0(x2d, pos_col, gmix, w_in, q_a_norm, w_uq, kv_a_norm, w_ukv, q_norm, k_norm, tm=256):
    T, D = x2d.shape
    H = MLA_HEADS
    w_krope = jnp.pad(w_in[:, 384:416], ((0, 0), (MLA_NOPE, LANES - MLA_NOPE - MLA_ROPE)))
    w_in_p = jnp.concatenate([w_in[:, :384], w_krope, w_in[:, 416:]], axis=1).astype(BF16)
    w_uq_p = _head_slots(w_uq, H, MLA_QK).astype(BF16)
    w_ukv3 = w_ukv.reshape(MLA_KV_LORA, H, MLA_NOPE + MLA_V)
    w_uk_p = _head_slots(w_ukv3[:, :, :MLA_NOPE].reshape(MLA_KV_LORA, H * MLA_NOPE), H, MLA_NOPE).astype(BF16)
    w_uv = _head_slots(w_ukv3[:, :, MLA_NOPE:].reshape(MLA_KV_LORA, H * MLA_V), H, MLA_V).T.astype(BF16)
    pad = LANES - MLA_QK
    qg = jnp.pad(q_norm, (0, pad)).reshape(1, LANES)
    kg = jnp.pad(k_norm, (0, pad)).reshape(1, LANES)
    half = MLA_ROPE // 2
    inv_freq = ROPE_THETA ** (-jnp.arange(half, dtype=F32) / half)
    invf = jnp.zeros((LANES,), F32).at[MLA_NOPE:MLA_NOPE + half].set(inv_freq).at[MLA_NOPE + half:MLA_QK].set(inv_freq)
    sgn = jnp.zeros((LANES,), F32).at[MLA_NOPE:MLA_NOPE + half].set(-1.0).at[MLA_NOPE + half:MLA_QK].set(1.0)
    row = lambda n: pl.BlockSpec((tm, n), lambda i: (i, 0))
    n_in = w_in_p.shape[1]
    return pl.pallas_call(
        _inproj0_kernel,
        grid=(T // tm,),
        in_specs=[row(D), row(1), _full((1, D)), _full((D, n_in)), _full((1, MLA_Q_LORA)),
                  _full((MLA_Q_LORA, H * LANES)), _full((1, MLA_KV_LORA)), _full((MLA_KV_LORA, H * LANES)),
                  _full((H * LANES, MLA_KV_LORA)), _full((1, LANES)), _full((1, LANES)), _full((1, LANES)),
                  _full((1, LANES))],
        out_specs=[row(H * LANES), row(H * LANES), pl.BlockSpec((H * LANES, tm), lambda i: (0, i)), row(CONV_CH)],
        out_shape=[jax.ShapeDtypeStruct((T, H * LANES), BF16), jax.ShapeDtypeStruct((T, H * LANES), BF16),
                   jax.ShapeDtypeStruct((H * LANES, T), BF16), jax.ShapeDtypeStruct((T, CONV_CH), F32)],
        compiler_params=_params("parallel"),
        name="inproj0",
    )(x2d, pos_col, gmix.reshape(1, D), w_in_p, q_a_norm.reshape(1, -1), w_uq_p, kv_a_norm.reshape(1, -1),
      w_uk_p, w_uv, qg, kg, invf.reshape(1, LANES), sgn.reshape(1, LANES))


MLA_TQ = 512
MLA_TK = 512
MLA_ONE = MLA_V


def _mla_attn_kernel(q_ref, k_ref, vt_ref, o_ref):
    S = q_ref.shape[1]
    tq, tk = MLA_TQ, MLA_TK
    krow = lax.broadcasted_iota(I32, (tk, 1), 0)
    qcol = lax.broadcasted_iota(I32, (1, tq), 1)

    def q_block(qi, carry):
        q0 = pl.multiple_of(qi * tq, tq)
        qs = [q_ref[0, pl.ds(q0, tq), hh * LANES:(hh + 1) * LANES] for hh in range(2)]

        def kv_step(j, state, masked):
            k0 = pl.multiple_of(j * tk, tk)
            hs = [slice(hh * LANES, (hh + 1) * LANES) for hh in range(2)]
            ss = [_dot_nt(k_ref[0, pl.ds(k0, tk), hs[hh]], qs[hh]) for hh in range(2)]
            out = []
            for hh in range(2):
                m, acc = state[hh]
                s = jnp.where(k0 + krow <= q0 + qcol, ss[hh], NEG_INF) if masked else ss[hh]
                m_new = jnp.maximum(m, jnp.max(s, axis=0, keepdims=True))
                p = jnp.exp2(s - m_new).astype(BF16)
                out.append((m_new, jnp.exp2(m - m_new) * acc + _dot(vt_ref[hs[hh], pl.ds(k0, tk)], p)))
            return tuple(out)

        init = (jnp.full((1, tq), NEG_INF, F32), jnp.zeros((LANES, tq), F32))
        n_full = (qi * tq) // tk
        state = lax.fori_loop(0, n_full, functools.partial(kv_step, masked=False), (init, init))
        state = kv_step(n_full, state, True)
        for hh in range(2):
            acc = state[hh][1]
            o_t = acc * (1.0 / acc[MLA_ONE:MLA_ONE + 1, :])
            o_ref[0, pl.ds(q0, tq), hh * LANES:(hh + 1) * LANES] = o_t.T.astype(BF16)
        return carry

    lax.fori_loop(0, S // tq, q_block, 0)


def _mla_attn(q, k, vt):
    B, S, _ = q.shape
    spec = pl.BlockSpec((1, S, 2 * LANES), lambda b, h: (b, 0, h))
    return pl.pallas_call(
        _mla_attn_kernel,
        grid=(B, MLA_HEADS // 2),
        in_specs=[spec, spec, pl.BlockSpec((2 * LANES, S), lambda b, h: (h, b))],
        out_specs=spec,
        out_shape=jax.ShapeDtypeStruct((B, S, MLA_HEADS * LANES), BF16),
        compiler_params=_params("parallel", "parallel"),
        name="mla_attn",
    )(q, k, vt)


CONV_TILE = 64
CONV_PAD = 32


CONV_ROWS = 512


def _conv_out0_kernel(u_ref, o_ref, x_ref, dww_ref, dwb_ref, lng_ref, lnb_ref, wo_ref, out_ref, upad, act):
    step = pl.program_id(1)

    @pl.when(step == 0)
    def _():
        upad[0:CONV_PAD, :] = jnp.zeros((CONV_PAD, CONV_CH), F32)
        upad[CONV_PAD:, :] = u_ref[0]

    lead = CONV_PAD - (CONV_W - 1)
    base = step * CONV_ROWS

    def tile(i, carry):
        t0 = pl.multiple_of(i * CONV_TILE, CONV_TILE)
        win = upad[pl.ds(pl.multiple_of(base + t0, CONV_TILE), CONV_TILE + CONV_PAD), :]
        acc = jnp.zeros((CONV_TILE, CONV_CH), F32) + dwb_ref[...]
        for j in range(CONV_W):
            acc = acc + win[lead + j:lead + j + CONV_TILE, :] * dww_ref[j:j + 1, :]
        mu = jnp.mean(acc, axis=-1, keepdims=True)
        xc = acc - mu
        var = jnp.mean(xc * xc, axis=-1, keepdims=True)
        y = xc * lax.rsqrt(var + 1e-5) * lng_ref[...] + lnb_ref[...]
        act[pl.ds(t0, CONV_TILE), :] = (y * jax.nn.sigmoid(y)).astype(BF16)
        return carry

    lax.fori_loop(0, CONV_ROWS // CONV_TILE, tile, 0)
    n_o = o_ref.shape[2]
    y = _dot(o_ref[0], wo_ref[:n_o, :]) + _dot(act[...], wo_ref[n_o:, :])
    out_ref[0] = x_ref[0] + y


def _conv_out0(u, o_mla, x, dw_w, dw_b, ln_g, ln_b, w_out):
    B, S, D = x.shape
    n_o = o_mla.shape[2]
    n_v = MLA_HEADS * MLA_V
    w_attn = _head_slots(w_out[:n_v].T, MLA_HEADS, MLA_V).T
    w_out_p = jnp.concatenate([w_attn, w_out[n_v:]], axis=0).astype(BF16)
    tspec = lambda n: pl.BlockSpec((1, CONV_ROWS, n), lambda b, t: (b, t, 0))
    return pl.pallas_call(
        _conv_out0_kernel,
        grid=(B, S // CONV_ROWS),
        in_specs=[pl.BlockSpec((1, S, CONV_CH), lambda b, t: (b, 0, 0)), tspec(n_o), tspec(D),
                  _full((CONV_W, CONV_CH)), _full((1, CONV_CH)), _full((1, CONV_CH)), _full((1, CONV_CH)),
                  _full((n_o + CONV_CH, D))],
        out_specs=tspec(D),
        out_shape=jax.ShapeDtypeStruct((B, S, D), F32),
        scratch_shapes=[pltpu.VMEM((S + CONV_PAD, CONV_CH), F32), pltpu.VMEM((CONV_ROWS, CONV_CH), BF16)],
        compiler_params=_params("parallel", "arbitrary"),
        name="conv_out0",
    )(u, o_mla, x, dw_w, dw_b.reshape(1, -1), ln_g.reshape(1, -1), ln_b.reshape(1, -1), w_out_p)


def _mixer0(x, positions, gmix, w_in, q_a_norm, w_uq, kv_a_norm, w_ukv, q_norm, k_norm,
            dw_w, dw_b, ln_g, ln_b, w_out):
    B, S, D = x.shape
    T = B * S
    q, k, v, u = _inproj0(x.reshape(T, D), positions.reshape(T, 1), gmix, w_in, q_a_norm, w_uq, kv_a_norm, w_ukv,
                          q_norm, k_norm)
    o = _mla_attn(q.reshape(B, S, -1), k.reshape(B, S, -1), v)
    return _conv_out0(u.reshape(B, S, -1), o, x, dw_w, dw_b, ln_g, ln_b, w_out)


ROUTE_TILE = 512
ROUTE_ROWS = 40


def _route_kernel(x_ref, g_ref, wr_ref, br_ref, tri_ref, hn_ref, oi_ref, of_ref, cnt_ref, carry):
    @pl.when(pl.program_id(0) == 0)
    def _():
        carry[...] = jnp.zeros_like(carry)

    hn = _rms(x_ref[...]) * g_ref[...]
    _slab_store(hn_ref, hn)
    tm = hn.shape[0]
    logits = _dot_nt(wr_ref[...], hn, precision=lax.Precision.HIGHEST) + br_ref[...]
    gl = logits[MOE_EXPERTS:MOE_EXPERTS + MOE_GROUPS]
    rid_g = lax.broadcasted_iota(I32, (MOE_GROUPS, tm), 0)
    gmax = jnp.max(gl, axis=0, keepdims=True)
    grp = jnp.min(jnp.where(gl == gmax, rid_g, MOE_GROUPS), axis=0, keepdims=True)
    g_w = 1.0 / jnp.sum(jnp.exp(gl - gmax), axis=0, keepdims=True)
    e_in = jnp.zeros((MOE_EPG, tm), F32)
    for gi in range(MOE_GROUPS):
        e_in = jnp.where(grp == gi, logits[gi * MOE_EPG:(gi + 1) * MOE_EPG], e_in)
    rid_e = lax.broadcasted_iota(I32, (MOE_EPG, tm), 0)
    v1 = jnp.max(e_in, axis=0, keepdims=True)
    i1 = jnp.min(jnp.where(e_in == v1, rid_e, MOE_EPG), axis=0, keepdims=True)
    rest = jnp.where(rid_e == i1, -jnp.inf, e_in)
    v2 = jnp.max(rest, axis=0, keepdims=True)
    i2 = jnp.min(jnp.where(rest == v2, rid_e, MOE_EPG), axis=0, keepdims=True)
    ex = jnp.exp(v2 - v1)
    den = 1.0 + ex
    e1 = grp * MOE_EPG + i1
    e2 = grp * MOE_EPG + i2
    rid = lax.broadcasted_iota(I32, (MOE_EXPERTS, tm), 0)
    hit1 = rid == e1
    hit2 = rid == e2
    member = jnp.where(hit1 | hit2, 1.0, 0.0)
    before = _dot(member.astype(BF16), tri_ref[...]) + carry[...]
    r1 = jnp.sum(jnp.where(hit1, before, 0.0), axis=0, keepdims=True)
    r2 = jnp.sum(jnp.where(hit2, before, 0.0), axis=0, keepdims=True)
    carry[...] = carry[...] + jnp.sum(member, axis=1, keepdims=True)
    oi_ref[...] = jnp.zeros_like(oi_ref)
    oi_ref[0:1, :] = e1
    oi_ref[1:2, :] = e2
    oi_ref[2:3, :] = r1.astype(I32)
    oi_ref[3:4, :] = r2.astype(I32)
    of_ref[...] = jnp.zeros_like(of_ref)
    of_ref[0:1, :] = g_w / den
    of_ref[1:2, :] = g_w * ex / den
    cnt_ref[...] = jnp.broadcast_to(carry[...], cnt_ref.shape).astype(I32)


def _route(x2d, gain, router_group, router_group_b, router_expert, router_expert_b):
    T, D = x2d.shape
    tm = ROUTE_TILE
    pad = ROUTE_ROWS - MOE_EXPERTS - MOE_GROUPS
    wr = jnp.concatenate([router_expert.T, router_group.T, jnp.zeros((pad, D), F32)], axis=0)
    br = jnp.concatenate([router_expert_b, router_group_b, jnp.zeros((pad,), F32)]).reshape(ROUTE_ROWS, 1)
    tri = (jnp.arange(tm)[:, None] < jnp.arange(tm)[None, :]).astype(BF16)
    return pl.pallas_call(
        _route_kernel,
        grid=(T // tm,),
        in_specs=[pl.BlockSpec((tm, D), lambda i: (i, 0)), _full((1, D)), _full((ROUTE_ROWS, D)),
                  _full((ROUTE_ROWS, 1)), _full((tm, tm))],
        out_specs=[pl.BlockSpec((tm * SLAB, LANES), lambda i: (i, 0)), pl.BlockSpec((8, tm), lambda i: (0, i)),
                   pl.BlockSpec((8, tm), lambda i: (0, i)), _full((MOE_EXPERTS, LANES))],
        out_shape=[jax.ShapeDtypeStruct((T * SLAB, LANES), F32), jax.ShapeDtypeStruct((8, T), I32),
                   jax.ShapeDtypeStruct((8, T), F32), jax.ShapeDtypeStruct((MOE_EXPERTS, LANES), I32)],
        scratch_shapes=[pltpu.VMEM((MOE_EXPERTS, 1), F32)],
        compiler_params=_params("arbitrary"),
        name="moe_route",
    )(x2d, gain.reshape(1, D), wr, br, tri)


MOVE_CHUNK = 512
MOVE_UNROLL = 8


SLAB = 8


def _slab_load(ref, n, first=0, stride=SLAB):
    return jnp.concatenate([ref[pl.ds(first + j, n, stride=stride), :] for j in range(SLAB)], axis=1)


def _slab_store(ref, value):
    n = value.shape[0]
    for j in range(SLAB):
        ref[pl.ds(j, n, stride=SLAB), :] = value[:, j * LANES:(j + 1) * LANES]


def _slab_move_kernel(idx_ref, src_ref, *rest, scatter):
    dst_ref, sem = rest[-2:]

    def copy(t, k):
        a = 2 * t + k
        there = pl.ds(pl.multiple_of(idx_ref[a], SLAB), SLAB)
        if scatter:
            return pltpu.make_async_copy(src_ref.at[pl.ds(pl.multiple_of(t * SLAB, SLAB), SLAB)], dst_ref.at[there], sem)
        return pltpu.make_async_copy(src_ref.at[there], dst_ref.at[pl.ds(pl.multiple_of(a * SLAB, SLAB), SLAB)], sem)

    def issue(lo):
        def body(i, carry):
            copy(lo + i, 0).start()
            copy(lo + i, 1).start()
            return carry
        lax.fori_loop(0, MOVE_CHUNK, body, 0, unroll=MOVE_UNROLL)

    def drain(lo):
        def body(i, carry):
            copy(lo + i, 0).wait()
            copy(lo + i, 1).wait()
            return carry
        lax.fori_loop(0, MOVE_CHUNK, body, 0, unroll=MOVE_UNROLL)

    c = pl.program_id(0)
    issue(c * MOVE_CHUNK)

    @pl.when(c > 0)
    def _():
        drain((c - 1) * MOVE_CHUNK)

    @pl.when(c == pl.num_programs(0) - 1)
    def _():
        drain(c * MOVE_CHUNK)


def _slab_move(idx, src, n_out, *, scatter, name):
    any_spec = pl.BlockSpec(memory_space=pl.ANY)
    shape = jax.ShapeDtypeStruct((n_out * SLAB, LANES), src.dtype)
    operands = (idx, src, jnp.zeros(shape.shape, shape.dtype)) if scatter else (idx, src)
    return pl.pallas_call(
        functools.partial(_slab_move_kernel, scatter=scatter),
        grid_spec=pltpu.PrefetchScalarGridSpec(
            num_scalar_prefetch=1,
            grid=(idx.shape[0] // (2 * MOVE_CHUNK),),
            in_specs=[any_spec] * (len(operands) - 1),
            out_specs=any_spec,
            scratch_shapes=[pltpu.SemaphoreType.DMA(())],
        ),
        out_shape=shape,
        input_output_aliases={2: 0} if scatter else {},
        compiler_params=_params("arbitrary"),
        name=name,
    )(*operands)


def _expert_kernel(blk_e_ref, n_used_ref, xs_ref, wgu_ref, wd_ref, ys_ref):
    del blk_e_ref

    @pl.when(pl.program_id(0) < n_used_ref[0])
    def _():
        gu = _dot(_slab_load(xs_ref, MOE_ROW_BLOCK).astype(BF16), wgu_ref[0])
        gate = gu[:, :MOE_HIDDEN]
        hid = (gate * jax.nn.sigmoid(gate) * gu[:, MOE_HIDDEN:]).astype(BF16)
        _slab_store(ys_ref, _dot(hid, wd_ref[0]))

    @pl.when(pl.program_id(0) >= n_used_ref[0])
    def _():
        ys_ref[...] = jnp.zeros_like(ys_ref)


def _experts(blk_expert, n_used, xs, w_gu, w_d):
    R = xs.shape[0] // SLAB
    rb = MOE_ROW_BLOCK
    D = w_gu.shape[1]
    return pl.pallas_call(
        _expert_kernel,
        grid_spec=pltpu.PrefetchScalarGridSpec(
            num_scalar_prefetch=2,
            grid=(R // rb,),
            in_specs=[pl.BlockSpec((rb * SLAB, LANES), lambda b, be, nu: (jnp.minimum(b, nu[0] - 1), 0)),
                      pl.BlockSpec((1, D, 2 * MOE_HIDDEN), lambda b, be, nu: (be[b], 0, 0)),
                      pl.BlockSpec((1, MOE_HIDDEN, D), lambda b, be, nu: (be[b], 0, 0))],
            out_specs=pl.BlockSpec((rb * SLAB, LANES), lambda b, be, nu: (b, 0)),
        ),
        out_shape=jax.ShapeDtypeStruct((R * SLAB, LANES), F32),
        compiler_params=_params("arbitrary"),
        name="moe_experts",
    )(blk_expert, n_used, xs, w_gu, w_d)


def _combine_kernel(x_ref, gate_ref, y_ref, out_ref):
    tm = x_ref.shape[0]
    y0 = _slab_load(y_ref, tm, 0, 2 * SLAB)
    y1 = _slab_load(y_ref, tm, SLAB, 2 * SLAB)
    out_ref[...] = x_ref[...] + gate_ref[:, 0:1] * y0 + gate_ref[:, 1:2] * y1


def _combine(x2d, gates_col, y_slabs, tm=512):
    T, D = x2d.shape
    return pl.pallas_call(
        _combine_kernel,
        grid=(T // tm,),
        in_specs=[pl.BlockSpec((tm, D), lambda i: (i, 0)), pl.BlockSpec((tm, 2), lambda i: (i, 0)),
                  pl.BlockSpec((tm * 2 * SLAB, LANES), lambda i: (i, 0))],
        out_specs=pl.BlockSpec((tm, D), lambda i: (i, 0)),
        out_shape=jax.ShapeDtypeStruct((T, D), F32),
        compiler_params=_params("parallel"),
        name="moe_combine",
    )(x2d, gates_col, y_slabs)


def _moe(x, gain, router_group, router_group_b, router_expert, router_expert_b, w_gate, w_up, w_down):
    B, S, D = x.shape
    T = B * S
    x2d = x.reshape(T, D)
    hn, oi, of, cnt = _route(x2d, gain, router_group, router_group_b, router_expert, router_expert_b)
    rb = MOE_ROW_BLOCK
    counts = cnt[:, 0]
    cap = (counts + rb - 1) // rb * rb
    cap_end = jnp.cumsum(cap)
    start = cap_end - cap
    eid = oi[0:2].T
    first = jnp.sum(jnp.where(eid[:, :, None] == jnp.arange(MOE_EXPERTS), start, 0), axis=-1)
    dest = ((first + oi[2:4].T) * SLAB).reshape(2 * T).astype(I32)
    n_rows = 2 * T + MOE_EXPERTS * rb
    n_blk = n_rows // rb
    blk_first_row = jnp.arange(n_blk, dtype=I32) * rb
    blk_expert = jnp.minimum(jnp.sum(cap_end[None, :] <= blk_first_row[:, None], axis=1), MOE_EXPERTS - 1)
    n_used = (cap_end[-1:] // rb).astype(I32)
    xs = _slab_move(dest, hn, n_rows, scatter=True, name="moe_dispatch")
    w_gu = jnp.concatenate([w_gate, w_up], axis=2).astype(BF16)
    ys = _experts(blk_expert.astype(I32), n_used, xs, w_gu, w_down.astype(BF16))
    y_slabs = _slab_move(dest, ys, 2 * T, scatter=False, name="moe_gather")
    out = _combine(x2d, of[0:2].T, y_slabs)
    return out.reshape(B, S, D)


NSA_TQ = 128
NSA_PAIRS = NSA_HPG // 2
N_SEL = 32
NSA_Q_SCALE = NSA_DK ** -0.5 * LOG2E
LO_ONE = LANES - 1
HI_ONE = 0
SLC_TK = 512
WIN_SPAN = WINDOW + NSA_TQ
AUG_POS = N_SEL
MASK_BIG = 1e30


def _pair_norm(t, gain2, lane_lo):
    t2 = t * t
    s_lo = jnp.sum(jnp.where(lane_lo, t2, 0.0), axis=-1, keepdims=True)
    s_hi = jnp.sum(jnp.where(lane_lo, 0.0, t2), axis=-1, keepdims=True)
    inv = jnp.where(lane_lo, lax.rsqrt(s_lo * (1.0 / NSA_DK) + 1e-6), lax.rsqrt(s_hi * (1.0 / NSA_DK) + 1e-6))
    return t * inv * gain2


def _inproj1_kernel(x_ref, kaug_ref, g_ref, win_ref, qg_ref, ksg_ref, kwg_ref,
                    q_out, kc_out, vc_out, ks_out, vs_out, kw_out, vw_out, gate_out):
    h = _rms(x_ref[...]) * g_ref[...]
    proj = _dot(h.astype(BF16), win_ref[...])
    lane = lax.broadcasted_iota(I32, (1, LANES), 1)
    lane_lo = lane < NSA_DK
    nq = NSA_HEADS * NSA_DK
    for p in range(nq // LANES):
        sl = slice(p * LANES, (p + 1) * LANES)
        q_out[:, sl] = (_pair_norm(proj[:, sl], qg_ref[...], lane_lo) * NSA_Q_SCALE).astype(BF16)
    part = lambda i: proj[:, nq + i * LANES:nq + (i + 1) * LANES]
    kc_out[...] = part(0)
    vc_out[...] = part(1)

    def lo_hi(t, lo_pad, hi_pad):
        r = pltpu.roll(t, NSA_DK, 1)
        return jnp.concatenate([jnp.where(lane_lo, t, lo_pad), jnp.where(lane_lo, hi_pad, r),
                                jnp.where(lane_lo, r, lo_pad), jnp.where(lane_lo, hi_pad, t)], axis=1).astype(BF16)

    kaug = kaug_ref[...].astype(F32)
    one_lo = jnp.where(lane == LO_ONE, 1.0, 0.0)
    one_hi = jnp.where(lane == HI_ONE, 1.0, 0.0)
    ks_out[...] = lo_hi(_pair_norm(part(2), ksg_ref[...], lane_lo), kaug, kaug)
    vs_out[...] = lo_hi(part(3), one_lo, one_hi)
    kw_out[...] = lo_hi(_pair_norm(part(4), kwg_ref[...], lane_lo), kaug, kaug)
    vw_out[...] = lo_hi(part(5), one_lo, one_hi)
    gate_out[...] = jax.nn.sigmoid(part(6))


def _inproj1(x2d, kaug, gain, w_in, q_norm, k_norm_slc, k_norm_win, tm=256):
    T, D = x2d.shape
    n_in = w_in.shape[1]
    n_pad = -n_in % LANES
    w_in_p = jnp.pad(w_in, ((0, 0), (0, n_pad))).astype(BF16)
    two = lambda g: jnp.tile(g, 2).reshape(1, LANES)
    row = lambda n: pl.BlockSpec((tm, n), lambda i: (i, 0))
    nq = NSA_HEADS * NSA_DK
    bf = lambda n: jax.ShapeDtypeStruct((T, n), BF16)
    f32 = lambda n: jax.ShapeDtypeStruct((T, n), F32)
    return pl.pallas_call(
        _inproj1_kernel,
        grid=(T // tm,),
        in_specs=[row(D), row(LANES), _full((1, D)), _full((D, n_in + n_pad)), _full((1, LANES)), _full((1, LANES)),
                  _full((1, LANES))],
        out_specs=[row(nq), row(LANES), row(LANES), row(4 * LANES), row(4 * LANES), row(4 * LANES), row(4 * LANES),
                   row(LANES)],
        out_shape=[bf(nq), f32(LANES), f32(LANES), bf(4 * LANES), bf(4 * LANES), bf(4 * LANES), bf(4 * LANES),
                   f32(LANES)],
        compiler_params=_params("parallel"),
        name="inproj1",
    )(x2d, kaug, gain.reshape(1, D), w_in_p, two(q_norm), two(k_norm_slc), two(k_norm_win))


def _stack_pairs(q_ref, rows):
    return jnp.concatenate([q_ref[0, rows, p * LANES:(p + 1) * LANES] for p in range(NSA_PAIRS)], axis=0)


def _gate_tile(gt, branch, p, lane_lo):
    c = branch * NSA_HPG + 2 * p
    return jnp.where(lane_lo, gt[:, c:c + 1], gt[:, c + 1:c + 2])


def _cmp_kernel(q_ref, kch_ref, vch_ref, pos3_ref, pend_ref, gate_ref, slope_ref,
                pek_ref, w1k_ref, w2k_ref, kcg_ref, pev_ref, w1v_ref, w2v_ref, ovl_ref, eye_ref,
                o_ref, sel_ref, kc_s, vc_s):
    tq = NSA_TQ
    S = q_ref.shape[1]

    def compress(ch_ref, pe_ref, w1_ref, w2_ref):
        a = ch_ref[0, 0]
        h_lo = _dot((a + pe_ref[0:1, :]).astype(BF16), w1_ref[0])
        h_hi = _dot((a + pe_ref[1:2, :]).astype(BF16), w1_ref[1])
        n = h_hi.shape[0]
        hid = jax.nn.gelu(h_lo + pltpu.roll(h_hi, n - 1, 0)).astype(BF16)
        return _dot(hid, w2_ref[0]), _dot(hid, w2_ref[1])

    k_lo, k_hi = compress(kch_ref, pek_ref, w1k_ref, w2k_ref)
    for e, kk in enumerate((k_lo, k_hi)):
        kk = kk * lax.rsqrt(jnp.sum(kk * kk, axis=-1, keepdims=True) * (1.0 / NSA_DK) + 1e-6) * kcg_ref[e:e + 1, :]
        kc_s[e] = kk.astype(BF16)
    v_lo, v_hi = compress(vch_ref, pev_ref, w1v_ref, w2v_ref)
    vc_s[0] = v_lo.astype(BF16)
    vc_s[1] = v_hi.astype(BF16)

    n_cmp = (S - CMP_BLOCK) // CMP_STRIDE + 1
    lane = lax.broadcasted_iota(I32, (1, LANES), 1)
    lane_lo = lane < NSA_DK
    blk_row = lax.broadcasted_iota(I32, (N_SEL, 1), 0)

    def q_block(qi, carry):
        t0 = pl.multiple_of(qi * tq, tq)
        rows = pl.ds(t0, tq)
        qs = _stack_pairs(q_ref, rows)
        pos_q0 = pos3_ref[0, pl.ds(qi, 1), :][:, 0:1]
        posrel = (pend_ref[0] - pos_q0).astype(F32)
        tcol = t0 + lax.broadcasted_iota(I32, (tq, 1), 0)
        valid = ((tcol >= CMP_STRIDE * lane + (CMP_BLOCK - 1)) & (lane < n_cmp))[None]
        psum = jnp.zeros((tq, LANES), F32)
        o_pairs = jnp.zeros((NSA_PAIRS * tq, LANES), F32)
        for e in range(2):
            s = _dot_nt(qs, kc_s[e]).reshape(NSA_PAIRS, tq, LANES)
            s = s + slope_ref[0, NSA_PAIRS * e:NSA_PAIRS * (e + 1)][:, None, :] * posrel[None]
            s = jnp.where(valid, s, NEG_INF)
            m = jnp.max(s, axis=-1, keepdims=True)
            p = jnp.where(valid, jnp.exp2(s - m), 0.0)
            p = p / jnp.maximum(jnp.sum(p, axis=-1, keepdims=True), 1e-20)
            psum = psum + jnp.sum(p, axis=0)
            o_pairs = o_pairs + _dot(p.reshape(NSA_PAIRS * tq, LANES).astype(BF16), vc_s[e])
        imp = _dot_nt(ovl_ref[...], psum, precision=lax.Precision.HIGHEST)
        cur = (t0 + lax.broadcasted_iota(I32, (1, tq), 1)) // SLC_BLOCK
        forced = (blk_row == 0) | (blk_row == cur) | (blk_row == cur - 1)
        imp = jnp.where(forced, FORCE_SCORE, jnp.where(blk_row <= cur, imp, -1.0))
        rank = jnp.zeros((N_SEL, tq), I32)
        for i in range(N_SEL):
            ri = imp[i:i + 1, :]
            rank = rank + jnp.where((ri > imp) | ((ri == imp) & (blk_row > i)), 1, 0)
        sel_t = jnp.where(rank < SLC_TOP_N, 1.0, 0.0).astype(BF16)
        gap = jnp.zeros((NSA_DK - N_SEL, tq), BF16)
        sel_t = jnp.concatenate([sel_t, gap, sel_t, gap], axis=0)
        sel = _dot_nt(eye_ref[...], sel_t)
        sel_ref[0, 0, rows, :] = jnp.where((lane & (NSA_DK - 1)) < N_SEL, sel - 1.0, 0.0).astype(BF16)
        gt = gate_ref[0, 0, rows, :]
        o3 = o_pairs.reshape(NSA_PAIRS, tq, LANES)
        for p in range(NSA_PAIRS):
            o_ref[0, rows, p * LANES:(p + 1) * LANES] = (o3[p] * _gate_tile(gt, 0, p, lane_lo)).astype(BF16)
        return carry

    lax.fori_loop(0, S // tq, q_block, 0)


def _lo_hi_cols(w):
    z = jnp.zeros_like(w)
    return jnp.stack([jnp.concatenate([w, z], axis=1), jnp.concatenate([z, w], axis=1)])


def _nsa_cmp(q, kch, vch, pos3, pend, gates_g, slopes_t, pe_k, w1_k, w2_k, k_norm_cmp, pe_v, w1_v, w2_v):
    B, S, _ = q.shape
    G = NSA_G
    nch = S // CMP_STRIDE
    half = CMP_STRIDE * NSA_DK
    pe2 = lambda pe: pe.reshape(2, half)
    w1_2 = lambda w: w.reshape(2, half, CMP_HIDDEN).astype(BF16)
    n_sel = S // SLC_BLOCK
    n_cmp = (S - CMP_BLOCK) // CMP_STRIDE + 1
    cmp_start = np.arange(LANES) * CMP_STRIDE
    slc_start = np.arange(n_sel) * SLC_BLOCK
    overlap = ((cmp_start[None, :] < slc_start[:, None] + SLC_BLOCK) & (cmp_start[None, :] + CMP_BLOCK > slc_start[:, None])
               & (np.arange(LANES)[None, :] < n_cmp))
    ovl = jnp.asarray(overlap, F32)
    eye = jnp.eye(NSA_TQ, dtype=BF16)
    qspec = pl.BlockSpec((1, S, NSA_PAIRS * LANES), lambda b, g: (b, 0, g))
    chspec = pl.BlockSpec((1, 1, nch, half), lambda b, g: (b, g, 0, 0))
    return pl.pallas_call(
        _cmp_kernel,
        grid=(B, G),
        in_specs=[qspec, chspec, chspec,
                  pl.BlockSpec((1, S // LANES, LANES), lambda b, g: (b, 0, 0)),
                  pl.BlockSpec((1, 1, LANES), lambda b, g: (b, 0, 0)),
                  pl.BlockSpec((1, 1, S, 3 * NSA_HPG), lambda b, g: (b, g, 0, 0)),
                  pl.BlockSpec((1, NSA_HPG, LANES), lambda b, g: (g, 0, 0)),
                  _full((2, half)), _full((2, half, CMP_HIDDEN)), _full((2, CMP_HIDDEN, LANES)), _full((2, LANES)),
                  _full((2, half)), _full((2, half, CMP_HIDDEN)), _full((2, CMP_HIDDEN, LANES)),
                  _full((n_sel, LANES)), _full((NSA_TQ, NSA_TQ))],
        out_specs=[qspec, pl.BlockSpec((1, 1, S, LANES), lambda b, g: (b, g, 0, 0))],
        out_shape=[jax.ShapeDtypeStruct((B, S, NSA_HEADS * NSA_DK), BF16),
                   jax.ShapeDtypeStruct((B, G, S, LANES), BF16)],
        scratch_shapes=[pltpu.VMEM((2, nch, LANES), BF16), pltpu.VMEM((2, nch, LANES), BF16)],
        compiler_params=_params("parallel", "parallel"),
        name="nsa_cmp",
    )(q, kch, vch, pos3, pend, gates_g, slopes_t, pe2(pe_k), w1_2(w1_k), _lo_hi_cols(w2_k).astype(BF16),
      _lo_hi_cols(k_norm_cmp.reshape(1, -1)).reshape(2, LANES), pe2(pe_v), w1_2(w1_v),
      _lo_hi_cols(w2_v).astype(BF16), ovl, eye)


def _slc_win_kernel(q_ref, ks_ref, vs_ref, kw_ref, vw_ref, sel_ref, gate_ref, slopeq_ref, ocmp_ref, o_ref):
    tq = NSA_TQ
    S = q_ref.shape[1]
    lane = lax.broadcasted_iota(I32, (1, LANES), 1)
    lane_lo = lane < NSA_DK
    ones_lane = (LO_ONE, HI_ONE)

    def q_block(qi, carry):
        t0 = pl.multiple_of(qi * tq, tq)
        rows = pl.ds(t0, tq)
        qs = _stack_pairs(q_ref, rows)
        tcol = t0 + lax.broadcasted_iota(I32, (tq, 1), 0)
        selm1 = sel_ref[0, 0, rows, :]

        def q_aug(e, with_sel):
            feats = [slopeq_ref[0, NSA_PAIRS * e + p:NSA_PAIRS * e + p + 1, :] for p in range(NSA_PAIRS)]
            if with_sel:
                extra = jnp.concatenate([selm1 + f for f in feats], axis=0)
            else:
                extra = jnp.concatenate([jnp.broadcast_to(f, (tq, LANES)) for f in feats], axis=0)
            return jnp.where(lane_lo, qs, extra) if e == 0 else jnp.where(lane_lo, extra, qs)

        def k_aug(k_ref, e, r0, n):
            return k_ref[0, pl.ds(r0, n), e * LANES:(e + 1) * LANES]

        def normalise(accs):
            outs = [acc * (1.0 / acc[:, :, ones_lane[e]:ones_lane[e] + 1]) for e, acc in enumerate(accs)]
            return jnp.where(lane_lo, outs[0], outs[1])

        qa = [q_aug(0, True), q_aug(1, True)]
        kcol = lax.broadcasted_iota(I32, (1, SLC_TK), 1)

        def slc_step(j, state, last):
            r0 = pl.multiple_of(j * SLC_TK, SLC_TK)
            out = []
            ss = [_dot_nt(qa[e], k_aug(ks_ref, e, r0, SLC_TK)).reshape(NSA_PAIRS, tq, SLC_TK) for e in range(2)]
            for e in range(2):
                m, acc = state[e]
                s = ss[e]
                if last:
                    s = jnp.where((r0 + kcol <= tcol)[None], s, NEG_INF)
                m_new = jnp.maximum(m, jnp.max(s, axis=-1, keepdims=True))
                p = jnp.exp2(s - m_new).reshape(NSA_PAIRS * tq, SLC_TK).astype(BF16)
                pv = _dot(p, vs_ref[0, pl.ds(r0, SLC_TK), e * LANES:(e + 1) * LANES])
                out.append((m_new, jnp.exp2(m - m_new) * acc + pv.reshape(NSA_PAIRS, tq, LANES)))
            return tuple(out)

        init = (jnp.full((NSA_PAIRS, tq, 1), NEG_INF, F32), jnp.zeros((NSA_PAIRS, tq, LANES), F32))
        n_full = qi // (SLC_TK // tq)
        state = lax.fori_loop(0, n_full, functools.partial(slc_step, last=False), (init, init))
        state = slc_step(n_full, state, True)
        slc = normalise([state[0][1], state[1][1]])

        w0 = pl.multiple_of(jnp.maximum(t0 - WINDOW, 0), tq)
        rel = tcol - (w0 + lax.broadcasted_iota(I32, (1, WIN_SPAN), 1))
        wmask = ((rel >= 0) & (rel < WINDOW))[None]
        accs = []
        for e in range(2):
            s = _dot_nt(q_aug(e, False), k_aug(kw_ref, e, w0, WIN_SPAN)).reshape(NSA_PAIRS, tq, WIN_SPAN)
            s = jnp.where(wmask, s, NEG_INF)
            p = jnp.exp2(s - jnp.max(s, axis=-1, keepdims=True)).reshape(NSA_PAIRS * tq, WIN_SPAN).astype(BF16)
            accs.append(_dot(p, vw_ref[0, pl.ds(w0, WIN_SPAN), e * LANES:(e + 1) * LANES]).reshape(NSA_PAIRS, tq, LANES))
        win = normalise(accs)

        gt = gate_ref[0, 0, rows, :]
        for p in range(NSA_PAIRS):
            sl = slice(p * LANES, (p + 1) * LANES)
            o = (ocmp_ref[0, rows, sl].astype(F32) + slc[p] * _gate_tile(gt, 1, p, lane_lo)
                 + win[p] * _gate_tile(gt, 2, p, lane_lo))
            o_ref[0, rows, sl] = o.astype(BF16)
        return carry

    lax.fori_loop(0, S // tq, q_block, 0)


def _key_aug(positions):
    B, S = positions.shape
    prel = positions - positions[:, :1]
    byte = lambda k: ((prel >> (8 * k)) & 255).astype(F32)
    pos_bytes = jnp.stack([byte(2), byte(1), byte(0)] * 2, axis=-1)
    onehot = (np.arange(S)[:, None] // SLC_BLOCK == np.arange(N_SEL)[None, :]) * MASK_BIG
    blocks = jnp.broadcast_to(jnp.asarray(onehot, F32), (B, S, N_SEL))
    pad = jnp.zeros((B, S, NSA_DK - N_SEL - 6), F32)
    half = jnp.concatenate([blocks, pos_bytes, pad], axis=-1)
    return jnp.concatenate([half, half], axis=-1).astype(BF16).reshape(B * S, LANES)


def _slope_aug(slopes_eo):
    hi = slopes_eo.astype(BF16).astype(F32)
    lo = (slopes_eo - hi).astype(BF16).astype(F32)
    w = jnp.asarray([65536.0, 256.0, 1.0], F32)
    feats = jnp.concatenate([hi[..., None] * w, lo[..., None] * w], axis=-1)
    half = jnp.pad(feats, ((0, 0), (0, 0), (AUG_POS, NSA_DK - AUG_POS - 6)))
    return jnp.concatenate([half, half], axis=-1).astype(BF16)


def _nsa_slc_win(q, ks4, vs4, kw4, vw4, sel, gates_g, slopeq, o_cmp):
    B, S, _ = q.shape
    qspec = pl.BlockSpec((1, S, NSA_PAIRS * LANES), lambda b, g: (b, 0, g))
    kvspec = pl.BlockSpec((1, S, 2 * LANES), lambda b, g: (b, 0, g))
    return pl.pallas_call(
        _slc_win_kernel,
        grid=(B, NSA_G),
        in_specs=[qspec, kvspec, kvspec, kvspec, kvspec,
                  pl.BlockSpec((1, 1, S, LANES), lambda b, g: (b, g, 0, 0)),
                  pl.BlockSpec((1, 1, S, 3 * NSA_HPG), lambda b, g: (b, g, 0, 0)),
                  pl.BlockSpec((1, NSA_HPG, LANES), lambda b, g: (g, 0, 0)),
                  qspec],
        out_specs=qspec,
        out_shape=jax.ShapeDtypeStruct((B, S, NSA_HEADS * NSA_DK), BF16),
        compiler_params=_params("parallel", "parallel"),
        name="nsa_slc_win",
    )(q, ks4, vs4, kw4, vw4, sel, gates_g, slopeq, o_cmp)


def _outproj_kernel(o_ref, x_ref, w_ref, out_ref):
    out_ref[...] = x_ref[...] + _dot(o_ref[...], w_ref[...])


def _outproj(o2d, x2d, w_out, tm=512):
    T, D = x2d.shape
    K = o2d.shape[1]
    return pl.pallas_call(
        _outproj_kernel,
        grid=(T // tm,),
        in_specs=[pl.BlockSpec((tm, K), lambda i: (i, 0)), pl.BlockSpec((tm, D), lambda i: (i, 0)), _full((K, D))],
        out_specs=pl.BlockSpec((tm, D), lambda i: (i, 0)),
        out_shape=jax.ShapeDtypeStruct((T, D), F32),
        compiler_params=_params("parallel"),
        name="outproj1",
    )(o2d, x2d, w_out.astype(BF16))


def _mixer1(x, positions, gain, w_in, q_norm, k_norm_cmp, k_norm_slc, k_norm_win,
            pe_k, w1_k, w2_k, pe_v, w1_v, w2_v, w_out):
    B, S, D = x.shape
    T = B * S
    G, HPG, DK = NSA_G, NSA_HPG, NSA_DK
    assert S // SLC_BLOCK == N_SEL and S % NSA_TQ == 0 and NSA_TQ == LANES
    x2d = x.reshape(T, D)
    q, kc, vc, ks4, vs4, kw4, vw4, gates = _inproj1(x2d, _key_aug(positions), gain, w_in, q_norm, k_norm_slc,
                                                    k_norm_win)
    b3 = lambda t: t.reshape(B, S, -1)
    chunks = lambda t: t.reshape(B, S, G, DK).transpose(0, 2, 1, 3).reshape(B, G, S // CMP_STRIDE, CMP_STRIDE * DK)
    gates_g = gates[:, :3 * NSA_HEADS].reshape(B, S, 3, G, HPG).transpose(0, 3, 1, 2, 4).reshape(B, G, S, 3 * HPG)
    n_cmp = (S - CMP_BLOCK) // CMP_STRIDE + 1
    pend = jnp.pad(positions[:, CMP_BLOCK - 1::CMP_STRIDE][:, :n_cmp], ((0, 0), (0, LANES - n_cmp))).reshape(B, 1, LANES)
    pos3 = positions.reshape(B, S // LANES, LANES)
    slopes = jnp.asarray(2.0 ** (-8.0 * np.arange(1, NSA_HEADS + 1) / NSA_HEADS), F32) * LOG2E
    slopes_eo = slopes.reshape(G, NSA_PAIRS, 2).transpose(0, 2, 1).reshape(G, HPG)
    slopes_t = jnp.broadcast_to(slopes_eo[:, :, None], (G, HPG, LANES))
    o_cmp, sel = _nsa_cmp(b3(q), chunks(kc), chunks(vc), pos3, pend, gates_g, slopes_t,
                          pe_k, w1_k, w2_k, k_norm_cmp, pe_v, w1_v, w2_v)
    o = _nsa_slc_win(b3(q), b3(ks4), b3(vs4), b3(kw4), b3(vw4), sel, gates_g, _slope_aug(slopes_eo), o_cmp)
    return _outproj(o.reshape(T, -1), x2d, w_out).reshape(B, S, D)


def kernel(x, positions, norm_mix, norm_ffn, mix0_w_in, mla_q_a_norm, mla_w_uq, mla_kv_a_norm, mla_w_ukv, mla_q_norm, mla_k_norm, conv_dw_w, conv_dw_b, conv_ln_g, conv_ln_b, mix0_w_out, nsa_w_in, nsa_q_norm, nsa_k_norm_cmp, nsa_k_norm_slc, nsa_k_norm_win, nsa_cmp_pe_k, nsa_cmp_w1_k, nsa_cmp_w2_k, nsa_cmp_pe_v, nsa_cmp_w1_v, nsa_cmp_w2_v, nsa_w_out, moe_router_group, moe_router_group_b, moe_router_expert, moe_router_expert_b, moe_w_gate, moe_w_up, moe_w_down):
    def moe(x, layer):
        return _moe(x, norm_ffn[layer], moe_router_group[layer], moe_router_group_b[layer], moe_router_expert[layer],
                    moe_router_expert_b[layer], moe_w_gate[layer], moe_w_up[layer], moe_w_down[layer])

    x = _mixer0(x, positions, norm_mix[0], mix0_w_in[0], mla_q_a_norm[0], mla_w_uq[0], mla_kv_a_norm[0], mla_w_ukv[0],
                mla_q_norm[0], mla_k_norm[0], conv_dw_w[0], conv_dw_b[0], conv_ln_g[0], conv_ln_b[0], mix0_w_out[0])
    x = moe(x, 0)
    x = _mixer1(x, positions, norm_mix[1], nsa_w_in[0], nsa_q_norm[0], nsa_k_norm_cmp[0], nsa_k_norm_slc[0],
                nsa_k_norm_win[0], nsa_cmp_pe_k[0], nsa_cmp_w1_k[0], nsa_cmp_w2_k[0], nsa_cmp_pe_v[0],
                nsa_cmp_w1_v[0], nsa_cmp_w2_v[0], nsa_w_out[0])
    return moe(x, 1)
```

```python
import functools

import numpy as np
import jax
import jax.numpy as jnp
from jax import lax
from jax.experimental import pallas as pl
from jax.experimental.pallas import tpu as pltpu

F32 = jnp.float32
BF16 = jnp.bfloat16
I32 = jnp.int32

LANES = 128
NEG_INF = -1e30
LOG2E = 1.4426950408889634
Q_BLOCK = 128
MLA_HEADS = 8
MLA_Q_LORA = 256
MLA_KV_LORA = 128
MLA_NOPE = 64
MLA_ROPE = 32
MLA_V = 64
MLA_QK = MLA_NOPE + MLA_ROPE
ROPE_THETA = 10000.0
CONV_CH = 512
CONV_W = 31
NSA_HEADS = 16
NSA_G = 2
NSA_HPG = NSA_HEADS // NSA_G
NSA_DK = 64
NSA_KVW = NSA_G * NSA_DK
CMP_BLOCK = 32
CMP_STRIDE = 16
CMP_HIDDEN = 128
SLC_BLOCK = 64
SLC_TOP_N = 8
WINDOW = 256
FORCE_SCORE = 1e4
MOE_GROUPS = 4
MOE_EPG = 8
MOE_EXPERTS = MOE_GROUPS * MOE_EPG
MOE_HIDDEN = 256
MOE_ROW_BLOCK = 256

VMEM_LIMIT = 56 * 1024 * 1024


def _params(*sem):
    return pltpu.CompilerParams(dimension_semantics=sem, vmem_limit_bytes=VMEM_LIMIT)


def _full(shape):
    n = len(shape)
    return pl.BlockSpec(shape, lambda *_: (0,) * n)


def _rms(x, eps=1e-6):
    return x * lax.rsqrt(jnp.mean(x * x, axis=-1, keepdims=True) + eps)


def _dot(a, b):
    return jnp.dot(a, b, preferred_element_type=F32)


def _dot_nt(a, b, **kw):
    return lax.dot_general(a, b, (((1,), (1,)), ((), ())), preferred_element_type=F32, **kw)


def _inproj0_kernel(x_ref, pos_ref, gmix_ref, win_ref, qan_ref, wuq_ref, kvan_ref, wuk_ref, wuv_ref,
                    qg_ref, kg_ref, invf_ref, sgn_ref, q_out, k_out, v_out, u_out):
    h = _rms(x_ref[...]) * gmix_ref[...]
    proj = _dot(h.astype(BF16), win_ref[...])
    c_q = proj[:, :MLA_Q_LORA]
    c_kv = proj[:, MLA_Q_LORA:MLA_Q_LORA + MLA_KV_LORA]
    k_rope = proj[:, 384:512]
    a = proj[:, 512:512 + CONV_CH]
    g = proj[:, 512 + CONV_CH:]
    u_out[...] = a * jax.nn.sigmoid(g)
    cqn = (_rms(c_q) * qan_ref[...]).astype(BF16)
    ckvn = (_rms(c_kv) * kvan_ref[...]).astype(BF16)
    q = _dot(cqn, wuq_ref[...])
    kn = _dot(ckvn, wuk_ref[...])
    slot_row = lax.broadcasted_iota(I32, (MLA_HEADS * LANES, 1), 0) & (LANES - 1)
    v_out[...] = (_dot_nt(wuv_ref[...], ckvn) + jnp.where(slot_row == MLA_V, 1.0, 0.0)).astype(BF16)
    ang = pos_ref[...].astype(F32) * invf_ref[...]
    cos = jnp.cos(ang)
    sin = jnp.sin(ang) * sgn_ref[...]
    lane = lax.broadcasted_iota(I32, (1, LANES), 1)
    first_half = (lane >= MLA_NOPE) & (lane < MLA_NOPE + MLA_ROPE // 2)
    scale = MLA_QK ** -0.5 * LOG2E

    def norm_rope(t, gain):
        t = t * lax.rsqrt(jnp.sum(t * t, axis=-1, keepdims=True) * (1.0 / MLA_QK) + 1e-6) * gain
        partner = jnp.where(first_half, pltpu.roll(t, LANES - MLA_ROPE // 2, 1), pltpu.roll(t, MLA_ROPE // 2, 1))
        return t * cos + partner * sin

    for hd in range(MLA_HEADS):
        sl = slice(hd * LANES, (hd + 1) * LANES)
        q_out[:, sl] = (norm_rope(q[:, sl], qg_ref[...]) * scale).astype(BF16)
        k_out[:, sl] = norm_rope(kn[:, sl] + k_rope, kg_ref[...]).astype(BF16)


def _head_slots(w, n_heads, width, offset=0):
    k = w.shape[0]
    w = w.reshape(k, n_heads, width)
    w = jnp.pad(w, ((0, 0), (0, 0), (offset, LANES - width - offset)))
    return w.reshape(k, n_heads * LANES)


def _inproj0(x2d, pos_col, gmix, w_in, q_a_norm, w_uq, kv_a_norm, w_ukv, q_norm, k_norm, tm=256):
    T, D = x2d.shape
    H = MLA_HEADS
    w_krope = jnp.pad(w_in[:, 384:416], ((0, 0), (MLA_NOPE, LANES - MLA_NOPE - MLA_ROPE)))
    w_in_p = jnp.concatenate([w_in[:, :384], w_krope, w_in[:, 416:]], axis=1).astype(BF16)
    w_uq_p = _head_slots(w_uq, H, MLA_QK).astype(BF16)
    w_ukv3 = w_ukv.reshape(MLA_KV_LORA, H, MLA_NOPE + MLA_V)
    w_uk_p = _head_slots(w_ukv3[:, :, :MLA_NOPE].reshape(MLA_KV_LORA, H * MLA_NOPE), H, MLA_NOPE).astype(BF16)
    w_uv = _head_slots(w_ukv3[:, :, MLA_NOPE:].reshape(MLA_KV_LORA, H * MLA_V), H, MLA_V).T.astype(BF16)
    pad = LANES - MLA_QK
    qg = jnp.pad(q_norm, (0, pad)).reshape(1, LANES)
    kg = jnp.pad(k_norm, (0, pad)).reshape(1, LANES)
    half = MLA_ROPE // 2
    inv_freq = ROPE_THETA ** (-jnp.arange(half, dtype=F32) / half)
    invf = jnp.zeros((LANES,), F32).at[MLA_NOPE:MLA_NOPE + half].set(inv_freq).at[MLA_NOPE + half:MLA_QK].set(inv_freq)
    sgn = jnp.zeros((LANES,), F32).at[MLA_NOPE:MLA_NOPE + half].set(-1.0).at[MLA_NOPE + half:MLA_QK].set(1.0)
    row = lambda n: pl.BlockSpec((tm, n), lambda i: (i, 0))
    n_in = w_in_p.shape[1]
    return pl.pallas_call(
        _inproj0_kernel,
        grid=(T // tm,),
        in_specs=[row(D), row(1), _full((1, D)), _full((D, n_in)), _full((1, MLA_Q_LORA)),
                  _full((MLA_Q_LORA, H * LANES)), _full((1, MLA_KV_LORA)), _full((MLA_KV_LORA, H * LANES)),
                  _full((H * LANES, MLA_KV_LORA)), _full((1, LANES)), _full((1, LANES)), _full((1, LANES)),
                  _full((1, LANES))],
        out_specs=[row(H * LANES), row(H * LANES), pl.BlockSpec((H * LANES, tm), lambda i: (0, i)), row(CONV_CH)],
        out_shape=[jax.ShapeDtypeStruct((T, H * LANES), BF16), jax.ShapeDtypeStruct((T, H * LANES), BF16),
                   jax.ShapeDtypeStruct((H * LANES, T), BF16), jax.ShapeDtypeStruct((T, CONV_CH), F32)],
        compiler_params=_params("parallel"),
        name="inproj0",
    )(x2d, pos_col, gmix.reshape(1, D), w_in_p, q_a_norm.reshape(1, -1), w_uq_p, kv_a_norm.reshape(1, -1),
      w_uk_p, w_uv, qg, kg, invf.reshape(1, LANES), sgn.reshape(1, LANES))


MLA_TQ = 512
MLA_TK = 512
MLA_ONE = MLA_V


def _mla_attn_kernel(q_ref, k_ref, vt_ref, o_ref):
    S = q_ref.shape[1]
    tq, tk = MLA_TQ, MLA_TK
    krow = lax.broadcasted_iota(I32, (tk, 1), 0)
    qcol = lax.broadcasted_iota(I32, (1, tq), 1)

    def q_block(qi, carry):
        q0 = pl.multiple_of(qi * tq, tq)
        qs = [q_ref[0, pl.ds(q0, tq), hh * LANES:(hh + 1) * LANES] for hh in range(2)]

        def kv_step(j, state, masked):
            k0 = pl.multiple_of(j * tk, tk)
            hs = [slice(hh * LANES, (hh + 1) * LANES) for hh in range(2)]
            ss = [_dot_nt(k_ref[0, pl.ds(k0, tk), hs[hh]], qs[hh]) for hh in range(2)]
            out = []
            for hh in range(2):
                m, acc = state[hh]
                s = jnp.where(k0 + krow <= q0 + qcol, ss[hh], NEG_INF) if masked else ss[hh]
                m_new = jnp.maximum(m, jnp.max(s, axis=0, keepdims=True))
                p = jnp.exp2(s - m_new).astype(BF16)
                out.append((m_new, jnp.exp2(m - m_new) * acc + _dot(vt_ref[hs[hh], pl.ds(k0, tk)], p)))
            return tuple(out)

        init = (jnp.full((1, tq), NEG_INF, F32), jnp.zeros((LANES, tq), F32))
        n_full = (qi * tq) // tk
        state = lax.fori_loop(0, n_full, functools.partial(kv_step, masked=False), (init, init))
        state = kv_step(n_full, state, True)
        for hh in range(2):
            acc = state[hh][1]
            o_t = acc * (1.0 / acc[MLA_ONE:MLA_ONE + 1, :])
            o_ref[0, pl.ds(q0, tq), hh * LANES:(hh + 1) * LANES] = o_t.T.astype(BF16)
        return carry

    lax.fori_loop(0, S // tq, q_block, 0)


def _mla_attn(q, k, vt):
    B, S, _ = q.shape
    spec = pl.BlockSpec((1, S, 2 * LANES), lambda b, h: (b, 0, h))
    return pl.pallas_call(
        _mla_attn_kernel,
        grid=(B, MLA_HEADS // 2),
        in_specs=[spec, spec, pl.BlockSpec((2 * LANES, S), lambda b, h: (h, b))],
        out_specs=spec,
        out_shape=jax.ShapeDtypeStruct((B, S, MLA_HEADS * LANES), BF16),
        compiler_params=_params("parallel", "parallel"),
        name="mla_attn",
    )(q, k, vt)


CONV_TILE = 64
CONV_PAD = 32


CONV_ROWS = 512


def _conv_out0_kernel(u_ref, o_ref, x_ref, dww_ref, dwb_ref, lng_ref, lnb_ref, wo_ref, out_ref, upad, act):
    step = pl.program_id(1)

    @pl.when(step == 0)
    def _():
        upad[0:CONV_PAD, :] = jnp.zeros((CONV_PAD, CONV_CH), F32)
        upad[CONV_PAD:, :] = u_ref[0]

    lead = CONV_PAD - (CONV_W - 1)
    base = step * CONV_ROWS

    def tile(i, carry):
        t0 = pl.multiple_of(i * CONV_TILE, CONV_TILE)
        win = upad[pl.ds(pl.multiple_of(base + t0, CONV_TILE), CONV_TILE + CONV_PAD), :]
        acc = jnp.zeros((CONV_TILE, CONV_CH), F32) + dwb_ref[...]
        for j in range(CONV_W):
            acc = acc + win[lead + j:lead + j + CONV_TILE, :] * dww_ref[j:j + 1, :]
        mu = jnp.mean(acc, axis=-1, keepdims=True)
        xc = acc - mu
        var = jnp.mean(xc * xc, axis=-1, keepdims=True)
        y = xc * lax.rsqrt(var + 1e-5) * lng_ref[...] + lnb_ref[...]
        act[pl.ds(t0, CONV_TILE), :] = (y * jax.nn.sigmoid(y)).astype(BF16)
        return carry

    lax.fori_loop(0, CONV_ROWS // CONV_TILE, tile, 0)
    n_o = o_ref.shape[2]
    y = _dot(o_ref[0], wo_ref[:n_o, :]) + _dot(act[...], wo_ref[n_o:, :])
    out_ref[0] = x_ref[0] + y


def _conv_out0(u, o_mla, x, dw_w, dw_b, ln_g, ln_b, w_out):
    B, S, D = x.shape
    n_o = o_mla.shape[2]
    n_v = MLA_HEADS * MLA_V
    w_attn = _head_slots(w_out[:n_v].T, MLA_HEADS, MLA_V).T
    w_out_p = jnp.concatenate([w_attn, w_out[n_v:]], axis=0).astype(BF16)
    tspec = lambda n: pl.BlockSpec((1, CONV_ROWS, n), lambda b, t: (b, t, 0))
    return pl.pallas_call(
        _conv_out0_kernel,
        grid=(B, S // CONV_ROWS),
        in_specs=[pl.BlockSpec((1, S, CONV_CH), lambda b, t: (b, 0, 0)), tspec(n_o), tspec(D),
                  _full((CONV_W, CONV_CH)), _full((1, CONV_CH)), _full((1, CONV_CH)), _full((1, CONV_CH)),
                  _full((n_o + CONV_CH, D))],
        out_specs=tspec(D),
        out_shape=jax.ShapeDtypeStruct((B, S, D), F32),
        scratch_shapes=[pltpu.VMEM((S + CONV_PAD, CONV_CH), F32), pltpu.VMEM((CONV_ROWS, CONV_CH), BF16)],
        compiler_params=_params("parallel", "arbitrary"),
        name="conv_out0",
    )(u, o_mla, x, dw_w, dw_b.reshape(1, -1), ln_g.reshape(1, -1), ln_b.reshape(1, -1), w_out_p)


def _mixer0(x, positions, gmix, w_in, q_a_norm, w_uq, kv_a_norm, w_ukv, q_norm, k_norm,
            dw_w, dw_b, ln_g, ln_b, w_out):
    B, S, D = x.shape
    T = B * S
    q, k, v, u = _inproj0(x.reshape(T, D), positions.reshape(T, 1), gmix, w_in, q_a_norm, w_uq, kv_a_norm, w_ukv,
                          q_norm, k_norm)
    o = _mla_attn(q.reshape(B, S, -1), k.reshape(B, S, -1), v)
    return _conv_out0(u.reshape(B, S, -1), o, x, dw_w, dw_b, ln_g, ln_b, w_out)


ROUTE_TILE = 512
ROUTE_ROWS = 40


def _route_kernel(x_ref, g_ref, wr_ref, br_ref, tri_ref, hn_ref, oi_ref, of_ref, cnt_ref, carry):
    @pl.when(pl.program_id(0) == 0)
    def _():
        carry[...] = jnp.zeros_like(carry)

    hn = _rms(x_ref[...]) * g_ref[...]
    _slab_store(hn_ref, hn)
    tm = hn.shape[0]
    logits = _dot_nt(wr_ref[...], hn, precision=lax.Precision.HIGHEST) + br_ref[...]
    gl = logits[MOE_EXPERTS:MOE_EXPERTS + MOE_GROUPS]
    rid_g = lax.broadcasted_iota(I32, (MOE_GROUPS, tm), 0)
    gmax = jnp.max(gl, axis=0, keepdims=True)
    grp = jnp.min(jnp.where(gl == gmax, rid_g, MOE_GROUPS), axis=0, keepdims=True)
    g_w = 1.0 / jnp.sum(jnp.exp(gl - gmax), axis=0, keepdims=True)
    e_in = jnp.zeros((MOE_EPG, tm), F32)
    for gi in range(MOE_GROUPS):
        e_in = jnp.where(grp == gi, logits[gi * MOE_EPG:(gi + 1) * MOE_EPG], e_in)
    rid_e = lax.broadcasted_iota(I32, (MOE_EPG, tm), 0)
    v1 = jnp.max(e_in, axis=0, keepdims=True)
    i1 = jnp.min(jnp.where(e_in == v1, rid_e, MOE_EPG), axis=0, keepdims=True)
    rest = jnp.where(rid_e == i1, -jnp.inf, e_in)
    v2 = jnp.max(rest, axis=0, keepdims=True)
    i2 = jnp.min(jnp.where(rest == v2, rid_e, MOE_EPG), axis=0, keepdims=True)
    ex = jnp.exp(v2 - v1)
    den = 1.0 + ex
    e1 = grp * MOE_EPG + i1
    e2 = grp * MOE_EPG + i2
    rid = lax.broadcasted_iota(I32, (MOE_EXPERTS, tm), 0)
    hit1 = rid == e1
    hit2 = rid == e2
    member = jnp.where(hit1 | hit2, 1.0, 0.0)
    before = _dot(member.astype(BF16), tri_ref[...]) + carry[...]
    r1 = jnp.sum(jnp.where(hit1, before, 0.0), axis=0, keepdims=True)
    r2 = jnp.sum(jnp.where(hit2, before, 0.0), axis=0, keepdims=True)
    carry[...] = carry[...] + jnp.sum(member, axis=1, keepdims=True)
    oi_ref[...] = jnp.zeros_like(oi_ref)
    oi_ref[0:1, :] = e1
    oi_ref[1:2, :] = e2
    oi_ref[2:3, :] = r1.astype(I32)
    oi_ref[3:4, :] = r2.astype(I32)
    of_ref[...] = jnp.zeros_like(of_ref)
    of_ref[0:1, :] = g_w / den
    of_ref[1:2, :] = g_w * ex / den
    cnt_ref[...] = jnp.broadcast_to(carry[...], cnt_ref.shape).astype(I32)


def _route(x2d, gain, router_group, router_group_b, router_expert, router_expert_b):
    T, D = x2d.shape
    tm = ROUTE_TILE
    pad = ROUTE_ROWS - MOE_EXPERTS - MOE_GROUPS
    wr = jnp.concatenate([router_expert.T, router_group.T, jnp.zeros((pad, D), F32)], axis=0)
    br = jnp.concatenate([router_expert_b, router_group_b, jnp.zeros((pad,), F32)]).reshape(ROUTE_ROWS, 1)
    tri = (jnp.arange(tm)[:, None] < jnp.arange(tm)[None, :]).astype(BF16)
    return pl.pallas_call(
        _route_kernel,
        grid=(T // tm,),
        in_specs=[pl.BlockSpec((tm, D), lambda i: (i, 0)), _full((1, D)), _full((ROUTE_ROWS, D)),
                  _full((ROUTE_ROWS, 1)), _full((tm, tm))],
        out_specs=[pl.BlockSpec((tm * SLAB, LANES), lambda i: (i, 0)), pl.BlockSpec((8, tm), lambda i: (0, i)),
                   pl.BlockSpec((8, tm), lambda i: (0, i)), _full((MOE_EXPERTS, LANES))],
        out_shape=[jax.ShapeDtypeStruct((T * SLAB, LANES), F32), jax.ShapeDtypeStruct((8, T), I32),
                   jax.ShapeDtypeStruct((8, T), F32), jax.ShapeDtypeStruct((MOE_EXPERTS, LANES), I32)],
        scratch_shapes=[pltpu.VMEM((MOE_EXPERTS, 1), F32)],
        compiler_params=_params("arbitrary"),
        name="moe_route",
    )(x2d, gain.reshape(1, D), wr, br, tri)


MOVE_CHUNK = 512
MOVE_UNROLL = 8


SLAB = 8


def _slab_load(ref, n, first=0, stride=SLAB):
    return jnp.concatenate([ref[pl.ds(first + j, n, stride=stride), :] for j in range(SLAB)], axis=1)


def _slab_store(ref, value):
    n = value.shape[0]
    for j in range(SLAB):
        ref[pl.ds(j, n, stride=SLAB), :] = value[:, j * LANES:(j + 1) * LANES]


def _slab(row):
    return pl.ds(pl.multiple_of(row, SLAB), SLAB)


def _for_tokens(n, fn):
    def body(t, carry):
        fn(t, 0)
        fn(t, 1)
        return carry
    lax.fori_loop(0, n, body, 0, unroll=MOVE_UNROLL)


def _dispatch_kernel(dest_ref, hn_ref, xs_init_ref, xs_ref, sem):
    del xs_init_ref
    base = pl.program_id(0) * (2 * MOVE_CHUNK)

    def copy(t, k):
        return pltpu.make_async_copy(hn_ref.at[_slab(t * SLAB)], xs_ref.at[_slab(dest_ref[base + 2 * t + k])], sem)

    _for_tokens(MOVE_CHUNK, lambda t, k: copy(t, k).start())
    _for_tokens(MOVE_CHUNK, lambda t, k: copy(t, k).wait())


def _dispatch(dest, hn, n_rows):
    T = hn.shape[0] // SLAB
    tm = MOVE_CHUNK
    return pl.pallas_call(
        _dispatch_kernel,
        grid_spec=pltpu.PrefetchScalarGridSpec(
            num_scalar_prefetch=1,
            grid=(T // tm,),
            in_specs=[pl.BlockSpec((tm * SLAB, LANES), lambda i, d: (i, 0)), pl.BlockSpec(memory_space=pl.ANY)],
            out_specs=pl.BlockSpec(memory_space=pl.ANY),
            scratch_shapes=[pltpu.SemaphoreType.DMA(())],
        ),
        out_shape=jax.ShapeDtypeStruct((n_rows * SLAB, LANES), F32),
        input_output_aliases={2: 0},
        compiler_params=_params("arbitrary"),
        name="moe_dispatch",
    )(dest, hn, jnp.zeros((n_rows * SLAB, LANES), F32))


def _expert_kernel(blk_e_ref, n_used_ref, xs_ref, wgu_ref, wd_ref, ys_ref):
    del blk_e_ref

    @pl.when(pl.program_id(0) < n_used_ref[0])
    def _():
        gu = _dot(_slab_load(xs_ref, MOE_ROW_BLOCK).astype(BF16), wgu_ref[0])
        gate = gu[:, :MOE_HIDDEN]
        hid = (gate * jax.nn.sigmoid(gate) * gu[:, MOE_HIDDEN:]).astype(BF16)
        _slab_store(ys_ref, _dot(hid, wd_ref[0]))

    @pl.when(pl.program_id(0) >= n_used_ref[0])
    def _():
        ys_ref[...] = jnp.zeros_like(ys_ref)


def _experts(blk_expert, n_used, xs, w_gu, w_d):
    R = xs.shape[0] // SLAB
    rb = MOE_ROW_BLOCK
    D = w_gu.shape[1]
    return pl.pallas_call(
        _expert_kernel,
        grid_spec=pltpu.PrefetchScalarGridSpec(
            num_scalar_prefetch=2,
            grid=(R // rb,),
            in_specs=[pl.BlockSpec((rb * SLAB, LANES), lambda b, be, nu: (jnp.minimum(b, nu[0] - 1), 0)),
                      pl.BlockSpec((1, D, 2 * MOE_HIDDEN), lambda b, be, nu: (be[b], 0, 0)),
                      pl.BlockSpec((1, MOE_HIDDEN, D), lambda b, be, nu: (be[b], 0, 0))],
            out_specs=pl.BlockSpec((rb * SLAB, LANES), lambda b, be, nu: (b, 0)),
        ),
        out_shape=jax.ShapeDtypeStruct((R * SLAB, LANES), F32),
        compiler_params=_params("arbitrary"),
        name="moe_experts",
    )(blk_expert, n_used, xs, w_gu, w_d)


def _combine_kernel(dest_ref, x_ref, gate_ref, ys_ref, out_ref, buf, sem):
    tm = x_ref.shape[0]
    step = pl.program_id(0)

    def copy(s, t, k):
        a = 2 * t + k
        slot = s % 2
        return pltpu.make_async_copy(ys_ref.at[_slab(dest_ref[s * (2 * tm) + a])], buf.at[slot, _slab(a * SLAB)],
                                     sem.at[slot])

    @pl.when(step == 0)
    def _():
        _for_tokens(tm, lambda t, k: copy(step, t, k).start())

    @pl.when(step + 1 < pl.num_programs(0))
    def _():
        _for_tokens(tm, lambda t, k: copy(step + 1, t, k).start())

    _for_tokens(tm, lambda t, k: copy(step, t, k).wait())
    mine = buf.at[step % 2]
    y0 = _slab_load(mine, tm, 0, 2 * SLAB)
    y1 = _slab_load(mine, tm, SLAB, 2 * SLAB)
    out_ref[...] = x_ref[...] + gate_ref[:, 0:1] * y0 + gate_ref[:, 1:2] * y1


def _combine(dest, x2d, gates_col, ys, tm=MOVE_CHUNK):
    T, D = x2d.shape
    return pl.pallas_call(
        _combine_kernel,
        grid_spec=pltpu.PrefetchScalarGridSpec(
            num_scalar_prefetch=1,
            grid=(T // tm,),
            in_specs=[pl.BlockSpec((tm, D), lambda i, d: (i, 0)), pl.BlockSpec((tm, 2), lambda i, d: (i, 0)),
                      pl.BlockSpec(memory_space=pl.ANY)],
            out_specs=pl.BlockSpec((tm, D), lambda i, d: (i, 0)),
            scratch_shapes=[pltpu.VMEM((2, tm * 2 * SLAB, LANES), F32), pltpu.SemaphoreType.DMA((2,))],
        ),
        out_shape=jax.ShapeDtypeStruct((T, D), F32),
        compiler_params=_params("arbitrary"),
        name="moe_combine",
    )(dest, x2d, gates_col, ys)


def _moe(x, gain, router_group, router_group_b, router_expert, router_expert_b, w_gate, w_up, w_down):
    B, S, D = x.shape
    T = B * S
    x2d = x.reshape(T, D)
    hn, oi, of, cnt = _route(x2d, gain, router_group, router_group_b, router_expert, router_expert_b)
    rb = MOE_ROW_BLOCK
    counts = cnt[:, 0]
    cap = (counts + rb - 1) // rb * rb
    cap_end = jnp.cumsum(cap)
    start = cap_end - cap
    eid = oi[0:2].T
    first = jnp.sum(jnp.where(eid[:, :, None] == jnp.arange(MOE_EXPERTS), start, 0), axis=-1)
    dest = ((first + oi[2:4].T) * SLAB).reshape(2 * T).astype(I32)
    n_rows = 2 * T + MOE_EXPERTS * rb
    n_blk = n_rows // rb
    blk_first_row = jnp.arange(n_blk, dtype=I32) * rb
    blk_expert = jnp.minimum(jnp.sum(cap_end[None, :] <= blk_first_row[:, None], axis=1), MOE_EXPERTS - 1)
    n_used = (cap_end[-1:] // rb).astype(I32)
    xs = _dispatch(dest, hn, n_rows)
    w_gu = jnp.concatenate([w_gate, w_up], axis=2).astype(BF16)
    ys = _experts(blk_expert.astype(I32), n_used, xs, w_gu, w_down.astype(BF16))
    out = _combine(dest, x2d, of[0:2].T, ys)
    return out.reshape(B, S, D)


NSA_TQ = 128
NSA_PAIRS = NSA_HPG // 2
N_SEL = 32
NSA_Q_SCALE = NSA_DK ** -0.5 * LOG2E
LO_ONE = LANES - 1
HI_ONE = 0
SLC_TK = 512
WIN_SPAN = WINDOW + NSA_TQ
AUG_POS = N_SEL
MASK_BIG = 1e30


def _pair_norm(t, gain2, lane_lo):
    t2 = t * t
    s_lo = jnp.sum(jnp.where(lane_lo, t2, 0.0), axis=-1, keepdims=True)
    s_hi = jnp.sum(jnp.where(lane_lo, 0.0, t2), axis=-1, keepdims=True)
    inv = jnp.where(lane_lo, lax.rsqrt(s_lo * (1.0 / NSA_DK) + 1e-6), lax.rsqrt(s_hi * (1.0 / NSA_DK) + 1e-6))
    return t * inv * gain2


def _inproj1_kernel(x_ref, kaug_ref, g_ref, win_ref, qg_ref, ksg_ref, kwg_ref,
                    q_out, kc_out, vc_out, ks_out, vs_out, kw_out, vw_out, gate_out):
    h = _rms(x_ref[...]) * g_ref[...]
    proj = _dot(h.astype(BF16), win_ref[...])
    lane = lax.broadcasted_iota(I32, (1, LANES), 1)
    lane_lo = lane < NSA_DK
    nq = NSA_HEADS * NSA_DK
    for p in range(nq // LANES):
        sl = slice(p * LANES, (p + 1) * LANES)
        q_out[:, sl] = (_pair_norm(proj[:, sl], qg_ref[...], lane_lo) * NSA_Q_SCALE).astype(BF16)
    part = lambda i: proj[:, nq + i * LANES:nq + (i + 1) * LANES]
    kc_out[...] = part(0)
    vc_out[...] = part(1)

    def lo_hi(t, lo_pad, hi_pad):
        r = pltpu.roll(t, NSA_DK, 1)
        return jnp.concatenate([jnp.where(lane_lo, t, lo_pad), jnp.where(lane_lo, hi_pad, r),
                                jnp.where(lane_lo, r, lo_pad), jnp.where(lane_lo, hi_pad, t)], axis=1).astype(BF16)

    kaug = kaug_ref[...].astype(F32)
    one_lo = jnp.where(lane == LO_ONE, 1.0, 0.0)
    one_hi = jnp.where(lane == HI_ONE, 1.0, 0.0)
    ks_out[...] = lo_hi(_pair_norm(part(2), ksg_ref[...], lane_lo), kaug, kaug)
    vs_out[...] = lo_hi(part(3), one_lo, one_hi)
    kw_out[...] = lo_hi(_pair_norm(part(4), kwg_ref[...], lane_lo), kaug, kaug)
    vw_out[...] = lo_hi(part(5), one_lo, one_hi)
    gate_out[...] = jax.nn.sigmoid(part(6))


def _inproj1(x2d, kaug, gain, w_in, q_norm, k_norm_slc, k_norm_win, tm=256):
    T, D = x2d.shape
    n_in = w_in.shape[1]
    n_pad = -n_in % LANES
    w_in_p = jnp.pad(w_in, ((0, 0), (0, n_pad))).astype(BF16)
    two = lambda g: jnp.tile(g, 2).reshape(1, LANES)
    row = lambda n: pl.BlockSpec((tm, n), lambda i: (i, 0))
    nq = NSA_HEADS * NSA_DK
    bf = lambda n: jax.ShapeDtypeStruct((T, n), BF16)
    f32 = lambda n: jax.ShapeDtypeStruct((T, n), F32)
    return pl.pallas_call(
        _inproj1_kernel,
        grid=(T // tm,),
        in_specs=[row(D), row(LANES), _full((1, D)), _full((D, n_in + n_pad)), _full((1, LANES)), _full((1, LANES)),
                  _full((1, LANES))],
        out_specs=[row(nq), row(LANES), row(LANES), row(4 * LANES), row(4 * LANES), row(4 * LANES), row(4 * LANES),
                   row(LANES)],
        out_shape=[bf(nq), f32(LANES), f32(LANES), bf(4 * LANES), bf(4 * LANES), bf(4 * LANES), bf(4 * LANES),
                   f32(LANES)],
        compiler_params=_params("parallel"),
        name="inproj1",
    )(x2d, kaug, gain.reshape(1, D), w_in_p, two(q_norm), two(k_norm_slc), two(k_norm_win))


def _stack_pairs(q_ref, rows):
    return jnp.concatenate([q_ref[0, rows, p * LANES:(p + 1) * LANES] for p in range(NSA_PAIRS)], axis=0)


def _gate_tile(gt, branch, p, lane_lo):
    c = branch * NSA_HPG + 2 * p
    return jnp.where(lane_lo, gt[:, c:c + 1], gt[:, c + 1:c + 2])


def _cmp_kernel(q_ref, kch_ref, vch_ref, pos3_ref, pend_ref, gate_ref, slope_ref,
                pek_ref, w1k_ref, w2k_ref, kcg_ref, pev_ref, w1v_ref, w2v_ref, ovl_ref, eye_ref,
                o_ref, sel_ref, kc_s, vc_s):
    tq = NSA_TQ
    S = q_ref.shape[1]

    def compress(ch_ref, pe_ref, w1_ref, w2_ref):
        a = ch_ref[0, 0]
        h_lo = _dot((a + pe_ref[0:1, :]).astype(BF16), w1_ref[0])
        h_hi = _dot((a + pe_ref[1:2, :]).astype(BF16), w1_ref[1])
        n = h_hi.shape[0]
        hid = jax.nn.gelu(h_lo + pltpu.roll(h_hi, n - 1, 0)).astype(BF16)
        return _dot(hid, w2_ref[0]), _dot(hid, w2_ref[1])

    k_lo, k_hi = compress(kch_ref, pek_ref, w1k_ref, w2k_ref)
    for e, kk in enumerate((k_lo, k_hi)):
        kk = kk * lax.rsqrt(jnp.sum(kk * kk, axis=-1, keepdims=True) * (1.0 / NSA_DK) + 1e-6) * kcg_ref[e:e + 1, :]
        kc_s[e] = kk.astype(BF16)
    v_lo, v_hi = compress(vch_ref, pev_ref, w1v_ref, w2v_ref)
    vc_s[0] = v_lo.astype(BF16)
    vc_s[1] = v_hi.astype(BF16)

    n_cmp = (S - CMP_BLOCK) // CMP_STRIDE + 1
    lane = lax.broadcasted_iota(I32, (1, LANES), 1)
    lane_lo = lane < NSA_DK
    blk_row = lax.broadcasted_iota(I32, (N_SEL, 1), 0)

    def q_block(qi, carry):
        t0 = pl.multiple_of(qi * tq, tq)
        rows = pl.ds(t0, tq)
        qs = _stack_pairs(q_ref, rows)
        pos_q0 = pos3_ref[0, pl.ds(qi, 1), :][:, 0:1]
        posrel = (pend_ref[0] - pos_q0).astype(F32)
        tcol = t0 + lax.broadcasted_iota(I32, (tq, 1), 0)
        valid = ((tcol >= CMP_STRIDE * lane + (CMP_BLOCK - 1)) & (lane < n_cmp))[None]
        psum = jnp.zeros((tq, LANES), F32)
        o_pairs = jnp.zeros((NSA_PAIRS * tq, LANES), F32)
        for e in range(2):
            s = _dot_nt(qs, kc_s[e]).reshape(NSA_PAIRS, tq, LANES)
            s = s + slope_ref[0, NSA_PAIRS * e:NSA_PAIRS * (e + 1)][:, None, :] * posrel[None]
            s = jnp.where(valid, s, NEG_INF)
            m = jnp.max(s, axis=-1, keepdims=True)
            p = jnp.where(valid, jnp.exp2(s - m), 0.0)
            p = p / jnp.maximum(jnp.sum(p, axis=-1, keepdims=True), 1e-20)
            psum = psum + jnp.sum(p, axis=0)
            o_pairs = o_pairs + _dot(p.reshape(NSA_PAIRS * tq, LANES).astype(BF16), vc_s[e])
        imp = _dot_nt(ovl_ref[...], psum, precision=lax.Precision.HIGHEST)
        cur = (t0 + lax.broadcasted_iota(I32, (1, tq), 1)) // SLC_BLOCK
        forced = (blk_row == 0) | (blk_row == cur) | (blk_row == cur - 1)
        imp = jnp.where(forced, FORCE_SCORE, jnp.where(blk_row <= cur, imp, -1.0))
        rank = jnp.zeros((N_SEL, tq), I32)
        for i in range(N_SEL):
            ri = imp[i:i + 1, :]
            rank = rank + jnp.where((ri > imp) | ((ri == imp) & (blk_row > i)), 1, 0)
        sel_t = jnp.where(rank < SLC_TOP_N, 1.0, 0.0).astype(BF16)
        gap = jnp.zeros((NSA_DK - N_SEL, tq), BF16)
        sel_t = jnp.concatenate([sel_t, gap, sel_t, gap], axis=0)
        sel = _dot_nt(eye_ref[...], sel_t)
        sel_ref[0, 0, rows, :] = jnp.where((lane & (NSA_DK - 1)) < N_SEL, sel - 1.0, 0.0).astype(BF16)
        gt = gate_ref[0, 0, rows, :]
        o3 = o_pairs.reshape(NSA_PAIRS, tq, LANES)
        for p in range(NSA_PAIRS):
            o_ref[0, rows, p * LANES:(p + 1) * LANES] = (o3[p] * _gate_tile(gt, 0, p, lane_lo)).astype(BF16)
        return carry

    lax.fori_loop(0, S // tq, q_block, 0)


def _lo_hi_cols(w):
    z = jnp.zeros_like(w)
    return jnp.stack([jnp.concatenate([w, z], axis=1), jnp.concatenate([z, w], axis=1)])


def _nsa_cmp(q, kch, vch, pos3, pend, gates_g, slopes_t, pe_k, w1_k, w2_k, k_norm_cmp, pe_v, w1_v, w2_v):
    B, S, _ = q.shape
    G = NSA_G
    nch = S // CMP_STRIDE
    half = CMP_STRIDE * NSA_DK
    pe2 = lambda pe: pe.reshape(2, half)
    w1_2 = lambda w: w.reshape(2, half, CMP_HIDDEN).astype(BF16)
    n_sel = S // SLC_BLOCK
    n_cmp = (S - CMP_BLOCK) // CMP_STRIDE + 1
    cmp_start = np.arange(LANES) * CMP_STRIDE
    slc_start = np.arange(n_sel) * SLC_BLOCK
    overlap = ((cmp_start[None, :] < slc_start[:, None] + SLC_BLOCK) & (cmp_start[None, :] + CMP_BLOCK > slc_start[:, None])
               & (np.arange(LANES)[None, :] < n_cmp))
    ovl = jnp.asarray(overlap, F32)
    eye = jnp.eye(NSA_TQ, dtype=BF16)
    qspec = pl.BlockSpec((1, S, NSA_PAIRS * LANES), lambda b, g: (b, 0, g))
    chspec = pl.BlockSpec((1, 1, nch, half), lambda b, g: (b, g, 0, 0))
    return pl.pallas_call(
        _cmp_kernel,
        grid=(B, G),
        in_specs=[qspec, chspec, chspec,
                  pl.BlockSpec((1, S // LANES, LANES), lambda b, g: (b, 0, 0)),
                  pl.BlockSpec((1, 1, LANES), lambda b, g: (b, 0, 0)),
                  pl.BlockSpec((1, 1, S, 3 * NSA_HPG), lambda b, g: (b, g, 0, 0)),
                  pl.BlockSpec((1, NSA_HPG, LANES), lambda b, g: (g, 0, 0)),
                  _full((2, half)), _full((2, half, CMP_HIDDEN)), _full((2, CMP_HIDDEN, LANES)), _full((2, LANES)),
                  _full((2, half)), _full((2, half, CMP_HIDDEN)), _full((2, CMP_HIDDEN, LANES)),
                  _full((n_sel, LANES)), _full((NSA_TQ, NSA_TQ))],
        out_specs=[qspec, pl.BlockSpec((1, 1, S, LANES), lambda b, g: (b, g, 0, 0))],
        out_shape=[jax.ShapeDtypeStruct((B, S, NSA_HEADS * NSA_DK), BF16),
                   jax.ShapeDtypeStruct((B, G, S, LANES), BF16)],
        scratch_shapes=[pltpu.VMEM((2, nch, LANES), BF16), pltpu.VMEM((2, nch, LANES), BF16)],
        compiler_params=_params("parallel", "parallel"),
        name="nsa_cmp",
    )(q, kch, vch, pos3, pend, gates_g, slopes_t, pe2(pe_k), w1_2(w1_k), _lo_hi_cols(w2_k).astype(BF16),
      _lo_hi_cols(k_norm_cmp.reshape(1, -1)).reshape(2, LANES), pe2(pe_v), w1_2(w1_v),
      _lo_hi_cols(w2_v).astype(BF16), ovl, eye)


def _slc_win_kernel(q_ref, ks_ref, vs_ref, kw_ref, vw_ref, sel_ref, gate_ref, slopeq_ref, ocmp_ref, o_ref):
    tq = NSA_TQ
    S = q_ref.shape[1]
    lane = lax.broadcasted_iota(I32, (1, LANES), 1)
    lane_lo = lane < NSA_DK
    ones_lane = (LO_ONE, HI_ONE)

    def q_block(qi, carry):
        t0 = pl.multiple_of(qi * tq, tq)
        rows = pl.ds(t0, tq)
        qs = _stack_pairs(q_ref, rows)
        tcol = t0 + lax.broadcasted_iota(I32, (tq, 1), 0)
        selm1 = sel_ref[0, 0, rows, :]

        def q_aug(e, with_sel):
            feats = [slopeq_ref[0, NSA_PAIRS * e + p:NSA_PAIRS * e + p + 1, :] for p in range(NSA_PAIRS)]
            if with_sel:
                extra = jnp.concatenate([selm1 + f for f in feats], axis=0)
            else:
                extra = jnp.concatenate([jnp.broadcast_to(f, (tq, LANES)) for f in feats], axis=0)
            return jnp.where(lane_lo, qs, extra) if e == 0 else jnp.where(lane_lo, extra, qs)

        def k_aug(k_ref, e, r0, n):
            return k_ref[0, pl.ds(r0, n), e * LANES:(e + 1) * LANES]

        def normalise(accs):
            outs = [acc * (1.0 / acc[:, :, ones_lane[e]:ones_lane[e] + 1]) for e, acc in enumerate(accs)]
            return jnp.where(lane_lo, outs[0], outs[1])

        qa = [q_aug(0, True), q_aug(1, True)]
        kcol = lax.broadcasted_iota(I32, (1, SLC_TK), 1)

        def slc_step(j, state, last):
            r0 = pl.multiple_of(j * SLC_TK, SLC_TK)
            out = []
            ss = [_dot_nt(qa[e], k_aug(ks_ref, e, r0, SLC_TK)).reshape(NSA_PAIRS, tq, SLC_TK) for e in range(2)]
            for e in range(2):
                m, acc = state[e]
                s = ss[e]
                if last:
                    s = jnp.where((r0 + kcol <= tcol)[None], s, NEG_INF)
                m_new = jnp.maximum(m, jnp.max(s, axis=-1, keepdims=True))
                p = jnp.exp2(s - m_new).reshape(NSA_PAIRS * tq, SLC_TK).astype(BF16)
                pv = _dot(p, vs_ref[0, pl.ds(r0, SLC_TK), e * LANES:(e + 1) * LANES])
                out.append((m_new, jnp.exp2(m - m_new) * acc + pv.reshape(NSA_PAIRS, tq, LANES)))
            return tuple(out)

        init = (jnp.full((NSA_PAIRS, tq, 1), NEG_INF, F32), jnp.zeros((NSA_PAIRS, tq, LANES), F32))
        n_full = qi // (SLC_TK // tq)
        state = lax.fori_loop(0, n_full, functools.partial(slc_step, last=False), (init, init))
        state = slc_step(n_full, state, True)
        slc = normalise([state[0][1], state[1][1]])

        w0 = pl.multiple_of(jnp.maximum(t0 - WINDOW, 0), tq)
        rel = tcol - (w0 + lax.broadcasted_iota(I32, (1, WIN_SPAN), 1))
        wmask = ((rel >= 0) & (rel < WINDOW))[None]
        accs = []
        for e in range(2):
            s = _dot_nt(q_aug(e, False), k_aug(kw_ref, e, w0, WIN_SPAN)).reshape(NSA_PAIRS, tq, WIN_SPAN)
            s = jnp.where(wmask, s, NEG_INF)
            p = jnp.exp2(s - jnp.max(s, axis=-1, keepdims=True)).reshape(NSA_PAIRS * tq, WIN_SPAN).astype(BF16)
            accs.append(_dot(p, vw_ref[0, pl.ds(w0, WIN_SPAN), e * LANES:(e + 1) * LANES]).reshape(NSA_PAIRS, tq, LANES))
        win = normalise(accs)

        gt = gate_ref[0, 0, rows, :]
        for p in range(NSA_PAIRS):
            sl = slice(p * LANES, (p + 1) * LANES)
            o = (ocmp_ref[0, rows, sl].astype(F32) + slc[p] * _gate_tile(gt, 1, p, lane_lo)
                 + win[p] * _gate_tile(gt, 2, p, lane_lo))
            o_ref[0, rows, sl] = o.astype(BF16)
        return carry

    lax.fori_loop(0, S // tq, q_block, 0)


def _key_aug(positions):
    B, S = positions.shape
    prel = positions - positions[:, :1]
    byte = lambda k: ((prel >> (8 * k)) & 255).astype(F32)
    pos_bytes = jnp.stack([byte(2), byte(1), byte(0)] * 2, axis=-1)
    onehot = (np.arange(S)[:, None] // SLC_BLOCK == np.arange(N_SEL)[None, :]) * MASK_BIG
    blocks = jnp.broadcast_to(jnp.asarray(onehot, F32), (B, S, N_SEL))
    pad = jnp.zeros((B, S, NSA_DK - N_SEL - 6), F32)
    half = jnp.concatenate([blocks, pos_bytes, pad], axis=-1)
    return jnp.concatenate([half, half], axis=-1).astype(BF16).reshape(B * S, LANES)


def _slope_aug(slopes_eo):
    hi = slopes_eo.astype(BF16).astype(F32)
    lo = (slopes_eo - hi).astype(BF16).astype(F32)
    w = jnp.asarray([65536.0, 256.0, 1.0], F32)
    feats = jnp.concatenate([hi[..., None] * w, lo[..., None] * w], axis=-1)
    half = jnp.pad(feats, ((0, 0), (0, 0), (AUG_POS, NSA_DK - AUG_POS - 6)))
    return jnp.concatenate([half, half], axis=-1).astype(BF16)


def _nsa_slc_win(q, ks4, vs4, kw4, vw4, sel, gates_g, slopeq, o_cmp):
    B, S, _ = q.shape
    qspec = pl.BlockSpec((1, S, NSA_PAIRS * LANES), lambda b, g: (b, 0, g))
    kvspec = pl.BlockSpec((1, S, 2 * LANES), lambda b, g: (b, 0, g))
    return pl.pallas_call(
        _slc_win_kernel,
        grid=(B, NSA_G),
        in_specs=[qspec, kvspec, kvspec, kvspec, kvspec,
                  pl.BlockSpec((1, 1, S, LANES), lambda b, g: (b, g, 0, 0)),
                  pl.BlockSpec((1, 1, S, 3 * NSA_HPG), lambda b, g: (b, g, 0, 0)),
                  pl.BlockSpec((1, NSA_HPG, LANES), lambda b, g: (g, 0, 0)),
                  qspec],
        out_specs=qspec,
        out_shape=jax.ShapeDtypeStruct((B, S, NSA_HEADS * NSA_DK), BF16),
        compiler_params=_params("parallel", "parallel"),
        name="nsa_slc_win",
    )(q, ks4, vs4, kw4, vw4, sel, gates_g, slopeq, o_cmp)


def _outproj_kernel(o_ref, x_ref, w_ref, out_ref):
    out_ref[...] = x_ref[...] + _dot(o_ref[...], w_ref[...])


def _outproj(o2d, x2d, w_out, tm=512):
    T, D = x2d.shape
    K = o2d.shape[1]
    return pl.pallas_call(
        _outproj_kernel,
        grid=(T // tm,),
        in_specs=[pl.BlockSpec((tm, K), lambda i: (i, 0)), pl.BlockSpec((tm, D), lambda i: (i, 0)), _full((K, D))],
        out_specs=pl.BlockSpec((tm, D), lambda i: (i, 0)),
        out_shape=jax.ShapeDtypeStruct((T, D), F32),
        compiler_params=_params("parallel"),
        name="outproj1",
    )(o2d, x2d, w_out.astype(BF16))


def _mixer1(x, positions, gain, w_in, q_norm, k_norm_cmp, k_norm_slc, k_norm_win,
            pe_k, w1_k, w2_k, pe_v, w1_v, w2_v, w_out):
    B, S, D = x.shape
    T = B * S
    G, HPG, DK = NSA_G, NSA_HPG, NSA_DK
    assert S // SLC_BLOCK == N_SEL and S % NSA_TQ == 0 and NSA_TQ == LANES
    x2d = x.reshape(T, D)
    q, kc, vc, ks4, vs4, kw4, vw4, gates = _inproj1(x2d, _key_aug(positions), gain, w_in, q_norm, k_norm_slc,
                                                    k_norm_win)
    b3 = lambda t: t.reshape(B, S, -1)
    chunks = lambda t: t.reshape(B, S, G, DK).transpose(0, 2, 1, 3).reshape(B, G, S // CMP_STRIDE, CMP_STRIDE * DK)
    gates_g = gates[:, :3 * NSA_HEADS].reshape(B, S, 3, G, HPG).transpose(0, 3, 1, 2, 4).reshape(B, G, S, 3 * HPG)
    n_cmp = (S - CMP_BLOCK) // CMP_STRIDE + 1
    pend = jnp.pad(positions[:, CMP_BLOCK - 1::CMP_STRIDE][:, :n_cmp], ((0, 0), (0, LANES - n_cmp))).reshape(B, 1, LANES)
    pos3 = positions.reshape(B, S // LANES, LANES)
    slopes = jnp.asarray(2.0 ** (-8.0 * np.arange(1, NSA_HEADS + 1) / NSA_HEADS), F32) * LOG2E
    slopes_eo = slopes.reshape(G, NSA_PAIRS, 2).transpose(0, 2, 1).reshape(G, HPG)
    slopes_t = jnp.broadcast_to(slopes_eo[:, :, None], (G, HPG, LANES))
    o_cmp, sel = _nsa_cmp(b3(q), chunks(kc), chunks(vc), pos3, pend, gates_g, slopes_t,
                          pe_k, w1_k, w2_k, k_norm_cmp, pe_v, w1_v, w2_v)
    o = _nsa_slc_win(b3(q), b3(ks4), b3(vs4), b3(kw4), b3(vw4), sel, gates_g, _slope_aug(slopes_eo), o_cmp)
    return _outproj(o.reshape(T, -1), x2d, w_out).reshape(B, S, D)


def kernel(x, positions, norm_mix, norm_ffn, mix0_w_in, mla_q_a_norm, mla_w_uq, mla_kv_a_norm, mla_w_ukv, mla_q_norm, mla_k_norm, conv_dw_w, conv_dw_b, conv_ln_g, conv_ln_b, mix0_w_out, nsa_w_in, nsa_q_norm, nsa_k_norm_cmp, nsa_k_norm_slc, nsa_k_norm_win, nsa_cmp_pe_k, nsa_cmp_w1_k, nsa_cmp_w2_k, nsa_cmp_pe_v, nsa_cmp_w1_v, nsa_cmp_w2_v, nsa_w_out, moe_router_group, moe_router_group_b, moe_router_expert, moe_router_expert_b, moe_w_gate, moe_w_up, moe_w_down):
    def moe(x, layer):
        return _moe(x, norm_ffn[layer], moe_router_group[layer], moe_router_group_b[layer], moe_router_expert[layer],
                    moe_router_expert_b[layer], moe_w_gate[layer], moe_w_up[layer], moe_w_down[layer])

    x = _mixer0(x, positions, norm_mix[0], mix0_w_in[0], mla_q_a_norm[0], mla_w_uq[0], mla_kv_a_norm[0], mla_w_ukv[0],
                mla_q_norm[0], mla_k_norm[0], conv_dw_w[0], conv_dw_b[0], conv_ln_g[0], conv_ln_b[0], mix0_w_out[0])
    x = moe(x, 0)
    x = _mixer1(x, positions, norm_mix[1], nsa_w_in[0], nsa_q_norm[0], nsa_k_norm_cmp[0], nsa_k_norm_slc[0],
                nsa_k_norm_win[0], nsa_cmp_pe_k[0], nsa_cmp_w1_k[0], nsa_cmp_w2_k[0], nsa_cmp_pe_v[0],
                nsa_cmp_w1_v[0], nsa_cmp_w2_v[0], nsa_w_out[0])
    return moe(x, 1)
```

```python
import functools

import numpy as np
import jax
import jax.numpy as jnp
from jax import lax
from jax.experimental import pallas as pl
from jax.experimental.pallas import tpu as pltpu

F32 = jnp.float32
BF16 = jnp.bfloat16
I32 = jnp.int32

LANES = 128
NEG_INF = -1e30
LOG2E = 1.4426950408889634
Q_BLOCK = 128
MLA_HEADS = 8
MLA_Q_LORA = 256
MLA_KV_LORA = 128
MLA_NOPE = 64
MLA_ROPE = 32
MLA_V = 64
MLA_QK = MLA_NOPE + MLA_ROPE
ROPE_THETA = 10000.0
CONV_CH = 512
CONV_W = 31
NSA_HEADS = 16
NSA_G = 2
NSA_HPG = NSA_HEADS // NSA_G
NSA_DK = 64
NSA_KVW = NSA_G * NSA_DK
CMP_BLOCK = 32
CMP_STRIDE = 16
CMP_HIDDEN = 128
SLC_BLOCK = 64
SLC_TOP_N = 8
WINDOW = 256
FORCE_SCORE = 1e4
MOE_GROUPS = 4
MOE_EPG = 8
MOE_EXPERTS = MOE_GROUPS * MOE_EPG
MOE_HIDDEN = 256
MOE_ROW_BLOCK = 256

VMEM_LIMIT = 56 * 1024 * 1024


def _params(*sem):
    return pltpu.CompilerParams(dimension_semantics=sem, vmem_limit_bytes=VMEM_LIMIT)


def _full(shape):
    n = len(shape)
    return pl.BlockSpec(shape, lambda *_: (0,) * n)


def _rms(x, eps=1e-6):
    return x * lax.rsqrt(jnp.mean(x * x, axis=-1, keepdims=True) + eps)


def _dot(a, b):
    return jnp.dot(a, b, preferred_element_type=F32)


def _dot_nt(a, b, **kw):
    return lax.dot_general(a, b, (((1,), (1,)), ((), ())), preferred_element_type=F32, **kw)


def _inproj0_kernel(x_ref, pos_ref, gmix_ref, win_ref, qan_ref, wuq_ref, kvan_ref, wuk_ref, wuv_ref,
                    qg_ref, kg_ref, invf_ref, sgn_ref, q_out, k_out, v_out, u_out):
    h = _rms(x_ref[...]) * gmix_ref[...]
    proj = _dot(h.astype(BF16), win_ref[...])
    c_q = proj[:, :MLA_Q_LORA]
    c_kv = proj[:, MLA_Q_LORA:MLA_Q_LORA + MLA_KV_LORA]
    k_rope = proj[:, 384:512]
    a = proj[:, 512:512 + CONV_CH]
    g = proj[:, 512 + CONV_CH:]
    u_out[...] = a * jax.nn.sigmoid(g)
    cqn = (_rms(c_q) * qan_ref[...]).astype(BF16)
    ckvn = (_rms(c_kv) * kvan_ref[...]).astype(BF16)
    q = _dot(cqn, wuq_ref[...])
    kn = _dot(ckvn, wuk_ref[...])
    slot_row = lax.broadcasted_iota(I32, (MLA_HEADS * LANES, 1), 0) & (LANES - 1)
    v_out[...] = (_dot_nt(wuv_ref[...], ckvn) + jnp.where(slot_row == MLA_V, 1.0, 0.0)).astype(BF16)
    ang = pos_ref[...].astype(F32) * invf_ref[...]
    cos = jnp.cos(ang)
    sin = jnp.sin(ang) * sgn_ref[...]
    lane = lax.broadcasted_iota(I32, (1, LANES), 1)
    first_half = (lane >= MLA_NOPE) & (lane < MLA_NOPE + MLA_ROPE // 2)
    scale = MLA_QK ** -0.5 * LOG2E

    def norm_rope(t, gain):
        t = t * lax.rsqrt(jnp.sum(t * t, axis=-1, keepdims=True) * (1.0 / MLA_QK) + 1e-6) * gain
        partner = jnp.where(first_half, pltpu.roll(t, LANES - MLA_ROPE // 2, 1), pltpu.roll(t, MLA_ROPE // 2, 1))
        return t * cos + partner * sin

    for hd in range(MLA_HEADS):
        sl = slice(hd * LANES, (hd + 1) * LANES)
        q_out[:, sl] = (norm_rope(q[:, sl], qg_ref[...]) * scale).astype(BF16)
        k_out[:, sl] = norm_rope(kn[:, sl] + k_rope, kg_ref[...]).astype(BF16)


def _head_slots(w, n_heads, width, offset=0):
    k = w.shape[0]
    w = w.reshape(k, n_heads, width)
    w = jnp.pad(w, ((0, 0), (0, 0), (offset, LANES - width - offset)))
    return w.reshape(k, n_heads * LANES)


def _inproj0(x2d, pos_col, gmix, w_in, q_a_norm, w_uq, kv_a_norm, w_ukv, q_norm, k_norm, tm=256):
    T, D = x2d.shape
    H = MLA_HEADS
    w_krope = jnp.pad(w_in[:, 384:416], ((0, 0), (MLA_NOPE, LANES - MLA_NOPE - MLA_ROPE)))
    w_in_p = jnp.concatenate([w_in[:, :384], w_krope, w_in[:, 416:]], axis=1).astype(BF16)
    w_uq_p = _head_slots(w_uq, H, MLA_QK).astype(BF16)
    w_ukv3 = w_ukv.reshape(MLA_KV_LORA, H, MLA_NOPE + MLA_V)
    w_uk_p = _head_slots(w_ukv3[:, :, :MLA_NOPE].reshape(MLA_KV_LORA, H * MLA_NOPE), H, MLA_NOPE).astype(BF16)
    w_uv = _head_slots(w_ukv3[:, :, MLA_NOPE:].reshape(MLA_KV_LORA, H * MLA_V), H, MLA_V).T.astype(BF16)
    pad = LANES - MLA_QK
    qg = jnp.pad(q_norm, (0, pad)).reshape(1, LANES)
    kg = jnp.pad(k_norm, (0, pad)).reshape(1, LANES)
    half = MLA_ROPE // 2
    inv_freq = ROPE_THETA ** (-jnp.arange(half, dtype=F32) / half)
    invf = jnp.zeros((LANES,), F32).at[MLA_NOPE:MLA_NOPE + half].set(inv_freq).at[MLA_NOPE + half:MLA_QK].set(inv_freq)
    sgn = jnp.zeros((LANES,), F32).at[MLA_NOPE:MLA_NOPE + half].set(-1.0).at[MLA_NOPE + half:MLA_QK].set(1.0)
    row = lambda n: pl.BlockSpec((tm, n), lambda i: (i, 0))
    n_in = w_in_p.shape[1]
    return pl.pallas_call(
        _inproj0_kernel,
        grid=(T // tm,),
        in_specs=[row(D), row(1), _full((1, D)), _full((D, n_in)), _full((1, MLA_Q_LORA)),
                  _full((MLA_Q_LORA, H * LANES)), _full((1, MLA_KV_LORA)), _full((MLA_KV_LORA, H * LANES)),
                  _full((H * LANES, MLA_KV_LORA)), _full((1, LANES)), _full((1, LANES)), _full((1, LANES)),
                  _full((1, LANES))],
        out_specs=[row(H * LANES), row(H * LANES), pl.BlockSpec((H * LANES, tm), lambda i: (0, i)), row(CONV_CH)],
        out_shape=[jax.ShapeDtypeStruct((T, H * LANES), BF16), jax.ShapeDtypeStruct((T, H * LANES), BF16),
                   jax.ShapeDtypeStruct((H * LANES, T), BF16), jax.ShapeDtypeStruct((T, CONV_CH), F32)],
        compiler_params=_params("parallel"),
        name="inproj0",
    )(x2d, pos_col, gmix.reshape(1, D), w_in_p, q_a_norm.reshape(1, -1), w_uq_p, kv_a_norm.reshape(1, -1),
      w_uk_p, w_uv, qg, kg, invf.reshape(1, LANES), sgn.reshape(1, LANES))


MLA_TQ = 512
MLA_TK = 512
MLA_ONE = MLA_V


def _mla_attn_kernel(q_ref, k_ref, vt_ref, o_ref):
    S = q_ref.shape[1]
    tq, tk = MLA_TQ, MLA_TK
    krow = lax.broadcasted_iota(I32, (tk, 1), 0)
    qcol = lax.broadcasted_iota(I32, (1, tq), 1)

    def q_block(qi, carry):
        q0 = pl.multiple_of(qi * tq, tq)
        qs = [q_ref[0, pl.ds(q0, tq), hh * LANES:(hh + 1) * LANES] for hh in range(2)]

        def kv_step(j, state, masked):
            k0 = pl.multiple_of(j * tk, tk)
            hs = [slice(hh * LANES, (hh + 1) * LANES) for hh in range(2)]
            ss = [_dot_nt(k_ref[0, pl.ds(k0, tk), hs[hh]], qs[hh]) for hh in range(2)]
            out = []
            for hh in range(2):
                m, acc = state[hh]
                s = jnp.where(k0 + krow <= q0 + qcol, ss[hh], NEG_INF) if masked else ss[hh]
                m_new = jnp.maximum(m, jnp.max(s, axis=0, keepdims=True))
                p = jnp.exp2(s - m_new).astype(BF16)
                out.append((m_new, jnp.exp2(m - m_new) * acc + _dot(vt_ref[hs[hh], pl.ds(k0, tk)], p)))
            return tuple(out)

        init = (jnp.full((1, tq), NEG_INF, F32), jnp.zeros((LANES, tq), F32))
        n_full = (qi * tq) // tk
        state = lax.fori_loop(0, n_full, functools.partial(kv_step, masked=False), (init, init))
        state = kv_step(n_full, state, True)
        for hh in range(2):
            acc = state[hh][1]
            o_t = acc * (1.0 / acc[MLA_ONE:MLA_ONE + 1, :])
            o_ref[0, pl.ds(q0, tq), hh * LANES:(hh + 1) * LANES] = o_t.T.astype(BF16)
        return carry

    lax.fori_loop(0, S // tq, q_block, 0)


def _mla_attn(q, k, vt):
    B, S, _ = q.shape
    spec = pl.BlockSpec((1, S, 2 * LANES), lambda b, h: (b, 0, h))
    return pl.pallas_call(
        _mla_attn_kernel,
        grid=(B, MLA_HEADS // 2),
        in_specs=[spec, spec, pl.BlockSpec((2 * LANES, S), lambda b, h: (h, b))],
        out_specs=spec,
        out_shape=jax.ShapeDtypeStruct((B, S, MLA_HEADS * LANES), BF16),
        compiler_params=_params("parallel", "parallel"),
        name="mla_attn",
    )(q, k, vt)


CONV_TILE = 64
CONV_PAD = 32


CONV_ROWS = 512


def _conv_out0_kernel(u_ref, o_ref, x_ref, dww_ref, dwb_ref, lng_ref, lnb_ref, wo_ref, out_ref, upad, act):
    step = pl.program_id(1)

    @pl.when(step == 0)
    def _():
        upad[0:CONV_PAD, :] = jnp.zeros((CONV_PAD, CONV_CH), F32)
        upad[CONV_PAD:, :] = u_ref[0]

    lead = CONV_PAD - (CONV_W - 1)
    base = step * CONV_ROWS

    def tile(i, carry):
        t0 = pl.multiple_of(i * CONV_TILE, CONV_TILE)
        win = upad[pl.ds(pl.multiple_of(base + t0, CONV_TILE), CONV_TILE + CONV_PAD), :]
        acc = jnp.zeros((CONV_TILE, CONV_CH), F32) + dwb_ref[...]
        for j in range(CONV_W):
            acc = acc + win[lead + j:lead + j + CONV_TILE, :] * dww_ref[j:j + 1, :]
        mu = jnp.mean(acc, axis=-1, keepdims=True)
        xc = acc - mu
        var = jnp.mean(xc * xc, axis=-1, keepdims=True)
        y = xc * lax.rsqrt(var + 1e-5) * lng_ref[...] + lnb_ref[...]
        act[pl.ds(t0, CONV_TILE), :] = (y * jax.nn.sigmoid(y)).astype(BF16)
        return carry

    lax.fori_loop(0, CONV_ROWS // CONV_TILE, tile, 0)
    n_o = o_ref.shape[2]
    y = _dot(o_ref[0], wo_ref[:n_o, :]) + _dot(act[...], wo_ref[n_o:, :])
    out_ref[0] = x_ref[0] + y


def _conv_out0(u, o_mla, x, dw_w, dw_b, ln_g, ln_b, w_out):
    B, S, D = x.shape
    n_o = o_mla.shape[2]
    n_v = MLA_HEADS * MLA_V
    w_attn = _head_slots(w_out[:n_v].T, MLA_HEADS, MLA_V).T
    w_out_p = jnp.concatenate([w_attn, w_out[n_v:]], axis=0).astype(BF16)
    tspec = lambda n: pl.BlockSpec((1, CONV_ROWS, n), lambda b, t: (b, t, 0))
    return pl.pallas_call(
        _conv_out0_kernel,
        grid=(B, S // CONV_ROWS),
        in_specs=[pl.BlockSpec((1, S, CONV_CH), lambda b, t: (b, 0, 0)), tspec(n_o), tspec(D),
                  _full((CONV_W, CONV_CH)), _full((1, CONV_CH)), _full((1, CONV_CH)), _full((1, CONV_CH)),
                  _full((n_o + CONV_CH, D))],
        out_specs=tspec(D),
        out_shape=jax.ShapeDtypeStruct((B, S, D), F32),
        scratch_shapes=[pltpu.VMEM((S + CONV_PAD, CONV_CH), F32), pltpu.VMEM((CONV_ROWS, CONV_CH), BF16)],
        compiler_params=_params("parallel", "arbitrary"),
        name="conv_out0",
    )(u, o_mla, x, dw_w, dw_b.reshape(1, -1), ln_g.reshape(1, -1), ln_b.reshape(1, -1), w_out_p)


def _mixer0(x, positions, gmix, w_in, q_a_norm, w_uq, kv_a_norm, w_ukv, q_norm, k_norm,
            dw_w, dw_b, ln_g, ln_b, w_out):
    B, S, D = x.shape
    T = B * S
    q, k, v, u = _inproj0(x.reshape(T, D), positions.reshape(T, 1), gmix, w_in, q_a_norm, w_uq, kv_a_norm, w_ukv,
                          q_norm, k_norm)
    o = _mla_attn(q.reshape(B, S, -1), k.reshape(B, S, -1), v)
    return _conv_out0(u.reshape(B, S, -1), o, x, dw_w, dw_b, ln_g, ln_b, w_out)


ROUTE_TILE = 512
ROUTE_ROWS = 40


def _route_kernel(x_ref, g_ref, wr_ref, br_ref, tri_ref, hn_ref, oi_ref, of_ref, cnt_ref, carry):
    @pl.when(pl.program_id(0) == 0)
    def _():
        carry[...] = jnp.zeros_like(carry)

    hn = _rms(x_ref[...]) * g_ref[...]
    _slab_store(hn_ref, hn)
    tm = hn.shape[0]
    logits = _dot_nt(wr_ref[...], hn, precision=lax.Precision.HIGHEST) + br_ref[...]
    gl = logits[MOE_EXPERTS:MOE_EXPERTS + MOE_GROUPS]
    rid_g = lax.broadcasted_iota(I32, (MOE_GROUPS, tm), 0)
    gmax = jnp.max(gl, axis=0, keepdims=True)
    grp = jnp.min(jnp.where(gl == gmax, rid_g, MOE_GROUPS), axis=0, keepdims=True)
    g_w = 1.0 / jnp.sum(jnp.exp(gl - gmax), axis=0, keepdims=True)
    e_in = jnp.zeros((MOE_EPG, tm), F32)
    for gi in range(MOE_GROUPS):
        e_in = jnp.where(grp == gi, logits[gi * MOE_EPG:(gi + 1) * MOE_EPG], e_in)
    rid_e = lax.broadcasted_iota(I32, (MOE_EPG, tm), 0)
    v1 = jnp.max(e_in, axis=0, keepdims=True)
    i1 = jnp.min(jnp.where(e_in == v1, rid_e, MOE_EPG), axis=0, keepdims=True)
    rest = jnp.where(rid_e == i1, -jnp.inf, e_in)
    v2 = jnp.max(rest, axis=0, keepdims=True)
    i2 = jnp.min(jnp.where(rest == v2, rid_e, MOE_EPG), axis=0, keepdims=True)
    ex = jnp.exp(v2 - v1)
    den = 1.0 + ex
    e1 = grp * MOE_EPG + i1
    e2 = grp * MOE_EPG + i2
    rid = lax.broadcasted_iota(I32, (MOE_EXPERTS, tm), 0)
    hit1 = rid == e1
    hit2 = rid == e2
    member = jnp.where(hit1 | hit2, 1.0, 0.0)
    before = _dot(member.astype(BF16), tri_ref[...]) + carry[...]
    r1 = jnp.sum(jnp.where(hit1, before, 0.0), axis=0, keepdims=True)
    r2 = jnp.sum(jnp.where(hit2, before, 0.0), axis=0, keepdims=True)
    carry[...] = carry[...] + jnp.sum(member, axis=1, keepdims=True)
    oi_ref[...] = jnp.zeros_like(oi_ref)
    oi_ref[0:1, :] = e1
    oi_ref[1:2, :] = e2
    oi_ref[2:3, :] = r1.astype(I32)
    oi_ref[3:4, :] = r2.astype(I32)
    of_ref[...] = jnp.zeros_like(of_ref)
    of_ref[0:1, :] = g_w / den
    of_ref[1:2, :] = g_w * ex / den
    cnt_ref[...] = jnp.broadcast_to(carry[...], cnt_ref.shape).astype(I32)


def _route(x2d, gain, router_group, router_group_b, router_expert, router_expert_b):
    T, D = x2d.shape
    tm = ROUTE_TILE
    pad = ROUTE_ROWS - MOE_EXPERTS - MOE_GROUPS
    wr = jnp.concatenate([router_expert.T, router_group.T, jnp.zeros((pad, D), F32)], axis=0)
    br = jnp.concatenate([router_expert_b, router_group_b, jnp.zeros((pad,), F32)]).reshape(ROUTE_ROWS, 1)
    tri = (jnp.arange(tm)[:, None] < jnp.arange(tm)[None, :]).astype(BF16)
    return pl.pallas_call(
        _route_kernel,
        grid=(T // tm,),
        in_specs=[pl.BlockSpec((tm, D), lambda i: (i, 0)), _full((1, D)), _full((ROUTE_ROWS, D)),
                  _full((ROUTE_ROWS, 1)), _full((tm, tm))],
        out_specs=[pl.BlockSpec((tm * SLAB, LANES), lambda i: (i, 0)), pl.BlockSpec((8, tm), lambda i: (0, i)),
                   pl.BlockSpec((8, tm), lambda i: (0, i)), _full((MOE_EXPERTS, LANES))],
        out_shape=[jax.ShapeDtypeStruct((T * SLAB, LANES), F32), jax.ShapeDtypeStruct((8, T), I32),
                   jax.ShapeDtypeStruct((8, T), F32), jax.ShapeDtypeStruct((MOE_EXPERTS, LANES), I32)],
        scratch_shapes=[pltpu.VMEM((MOE_EXPERTS, 1), F32)],
        compiler_params=_params("arbitrary"),
        name="moe_route",
    )(x2d, gain.reshape(1, D), wr, br, tri)


MOVE_CHUNK = 512
MOVE_UNROLL = 8


SLAB = 8


def _slab_load(ref, n, first=0, stride=SLAB):
    return jnp.concatenate([ref[pl.ds(first + j, n, stride=stride), :] for j in range(SLAB)], axis=1)


def _slab_store(ref, value):
    n = value.shape[0]
    for j in range(SLAB):
        ref[pl.ds(j, n, stride=SLAB), :] = value[:, j * LANES:(j + 1) * LANES]


def _slab(row):
    return pl.ds(pl.multiple_of(row, SLAB), SLAB)


def _for_tokens(n, fn):
    def body(t, carry):
        fn(t, 0)
        fn(t, 1)
        return carry
    lax.fori_loop(0, n, body, 0, unroll=MOVE_UNROLL)


def _dispatch_kernel(dest_ref, hn_ref, xs_init_ref, xs_ref, sem):
    del xs_init_ref
    base = pl.program_id(0) * (2 * MOVE_CHUNK)

    def copy(t, k):
        return pltpu.make_async_copy(hn_ref.at[_slab(t * SLAB)], xs_ref.at[_slab(dest_ref[base + 2 * t + k])], sem)

    _for_tokens(MOVE_CHUNK, lambda t, k: copy(t, k).start())
    _for_tokens(MOVE_CHUNK, lambda t, k: copy(t, k).wait())


def _dispatch(dest, hn, n_rows):
    T = hn.shape[0] // SLAB
    tm = MOVE_CHUNK
    return pl.pallas_call(
        _dispatch_kernel,
        grid_spec=pltpu.PrefetchScalarGridSpec(
            num_scalar_prefetch=1,
            grid=(T // tm,),
            in_specs=[pl.BlockSpec((tm * SLAB, LANES), lambda i, d: (i, 0)), pl.BlockSpec(memory_space=pl.ANY)],
            out_specs=pl.BlockSpec(memory_space=pl.ANY),
            scratch_shapes=[pltpu.SemaphoreType.DMA(())],
        ),
        out_shape=jax.ShapeDtypeStruct((n_rows * SLAB, LANES), F32),
        input_output_aliases={2: 0},
        compiler_params=_params("arbitrary"),
        name="moe_dispatch",
    )(dest, hn, jnp.zeros((n_rows * SLAB, LANES), F32))


def _expert_kernel(blk_e_ref, n_used_ref, xs_ref, wgu_ref, wd_ref, ys_ref):
    del blk_e_ref

    @pl.when(pl.program_id(0) < n_used_ref[0])
    def _():
        gu = _dot(_slab_load(xs_ref, MOE_ROW_BLOCK).astype(BF16), wgu_ref[0])
        gate = gu[:, :MOE_HIDDEN]
        hid = (gate * jax.nn.sigmoid(gate) * gu[:, MOE_HIDDEN:]).astype(BF16)
        _slab_store(ys_ref, _dot(hid, wd_ref[0]))

    @pl.when(pl.program_id(0) >= n_used_ref[0])
    def _():
        ys_ref[...] = jnp.zeros_like(ys_ref)


def _experts(blk_expert, n_used, xs, w_gu, w_d):
    R = xs.shape[0] // SLAB
    rb = MOE_ROW_BLOCK
    D = w_gu.shape[1]
    return pl.pallas_call(
        _expert_kernel,
        grid_spec=pltpu.PrefetchScalarGridSpec(
            num_scalar_prefetch=2,
            grid=(R // rb,),
            in_specs=[pl.BlockSpec((rb * SLAB, LANES), lambda b, be, nu: (jnp.minimum(b, nu[0] - 1), 0)),
                      pl.BlockSpec((1, D, 2 * MOE_HIDDEN), lambda b, be, nu: (be[b], 0, 0)),
                      pl.BlockSpec((1, MOE_HIDDEN, D), lambda b, be, nu: (be[b], 0, 0))],
            out_specs=pl.BlockSpec((rb * SLAB, LANES), lambda b, be, nu: (b, 0)),
        ),
        out_shape=jax.ShapeDtypeStruct((R * SLAB, LANES), F32),
        compiler_params=_params("arbitrary"),
        name="moe_experts",
    )(blk_expert, n_used, xs, w_gu, w_d)


def _combine_kernel(dest_ref, x_ref, gate_ref, ys_ref, out_ref, buf, sem):
    tm = x_ref.shape[0]
    step = pl.program_id(0)

    def copy(s, t, k):
        a = 2 * t + k
        slot = s % 2
        return pltpu.make_async_copy(ys_ref.at[_slab(dest_ref[s * (2 * tm) + a])], buf.at[slot, _slab(a * SLAB)],
                                     sem.at[slot])

    @pl.when(step == 0)
    def _():
        _for_tokens(tm, lambda t, k: copy(step, t, k).start())

    @pl.when(step + 1 < pl.num_programs(0))
    def _():
        _for_tokens(tm, lambda t, k: copy(step + 1, t, k).start())

    _for_tokens(tm, lambda t, k: copy(step, t, k).wait())
    mine = buf.at[step % 2]
    y0 = _slab_load(mine, tm, 0, 2 * SLAB)
    y1 = _slab_load(mine, tm, SLAB, 2 * SLAB)
    out_ref[...] = x_ref[...] + gate_ref[:, 0:1] * y0 + gate_ref[:, 1:2] * y1


def _combine(dest, x2d, gates_col, ys, tm=MOVE_CHUNK):
    T, D = x2d.shape
    return pl.pallas_call(
        _combine_kernel,
        grid_spec=pltpu.PrefetchScalarGridSpec(
            num_scalar_prefetch=1,
            grid=(T // tm,),
            in_specs=[pl.BlockSpec((tm, D), lambda i, d: (i, 0)), pl.BlockSpec((tm, 2), lambda i, d: (i, 0)),
                      pl.BlockSpec(memory_space=pl.ANY)],
            out_specs=pl.BlockSpec((tm, D), lambda i, d: (i, 0)),
            scratch_shapes=[pltpu.VMEM((2, tm * 2 * SLAB, LANES), F32), pltpu.SemaphoreType.DMA((2,))],
        ),
        out_shape=jax.ShapeDtypeStruct((T, D), F32),
        compiler_params=_params("arbitrary"),
        name="moe_combine",
    )(dest, x2d, gates_col, ys)


def _moe(x, gain, router_group, router_group_b, router_expert, router_expert_b, w_gate, w_up, w_down):
    B, S, D = x.shape
    T = B * S
    x2d = x.reshape(T, D)
    hn, oi, of, cnt = _route(x2d, gain, router_group, router_group_b, router_expert, router_expert_b)
    rb = MOE_ROW_BLOCK
    counts = cnt[:, 0]
    cap = (counts + rb - 1) // rb * rb
    cap_end = jnp.cumsum(cap)
    start = cap_end - cap
    eid = oi[0:2].T
    first = jnp.sum(jnp.where(eid[:, :, None] == jnp.arange(MOE_EXPERTS), start, 0), axis=-1)
    dest = ((first + oi[2:4].T) * SLAB).reshape(2 * T).astype(I32)
    n_rows = 2 * T + MOE_EXPERTS * rb
    n_blk = n_rows // rb
    blk_first_row = jnp.arange(n_blk, dtype=I32) * rb
    blk_expert = jnp.minimum(jnp.sum(cap_end[None, :] <= blk_first_row[:, None], axis=1), MOE_EXPERTS - 1)
    n_used = (cap_end[-1:] // rb).astype(I32)
    xs = _dispatch(dest, hn, n_rows)
    w_gu = jnp.concatenate([w_gate, w_up], axis=2).astype(BF16)
    ys = _experts(blk_expert.astype(I32), n_used, xs, w_gu, w_down.astype(BF16))
    out = _combine(dest, x2d, of[0:2].T, ys)
    return out.reshape(B, S, D)


NSA_TQ = 128
SLC_TQ = 256
NSA_PAIRS = NSA_HPG // 2
N_SEL = 32
NSA_Q_SCALE = NSA_DK ** -0.5 * LOG2E
LO_ONE = LANES - 1
HI_ONE = 0
SLC_TK = 512
WIN_SPAN = WINDOW + SLC_TQ
AUG_POS = N_SEL
MASK_BIG = 1e30


def _pair_norm(t, gain2, lane_lo):
    t2 = t * t
    s_lo = jnp.sum(jnp.where(lane_lo, t2, 0.0), axis=-1, keepdims=True)
    s_hi = jnp.sum(jnp.where(lane_lo, 0.0, t2), axis=-1, keepdims=True)
    inv = jnp.where(lane_lo, lax.rsqrt(s_lo * (1.0 / NSA_DK) + 1e-6), lax.rsqrt(s_hi * (1.0 / NSA_DK) + 1e-6))
    return t * inv * gain2


def _inproj1_kernel(x_ref, prel_ref, blocks_ref, g_ref, win_ref, wvt_ref, qg_ref, ksg_ref, kwg_ref,
                    q_out, kc_out, vc_out, ks_out, vs_out, kw_out, vw_out, gate_out):
    h = _rms(x_ref[...]) * g_ref[...]
    hb = h.astype(BF16)
    proj = _dot(hb, win_ref[...])
    lane = lax.broadcasted_iota(I32, (1, LANES), 1)
    lane_lo = lane < NSA_DK
    nq = NSA_HEADS * NSA_DK
    for p in range(nq // LANES):
        sl = slice(p * LANES, (p + 1) * LANES)
        q_out[:, sl] = (_pair_norm(proj[:, sl], qg_ref[...], lane_lo) * NSA_Q_SCALE).astype(BF16)
    part = lambda i: proj[:, nq + i * LANES:nq + (i + 1) * LANES]
    kc_out[...] = part(0)
    vc_out[...] = part(1)

    def lo_hi(t, lo_pad, hi_pad):
        r = pltpu.roll(t, NSA_DK, 1)
        return jnp.concatenate([jnp.where(lane_lo, t, lo_pad), jnp.where(lane_lo, hi_pad, r),
                                jnp.where(lane_lo, r, lo_pad), jnp.where(lane_lo, hi_pad, t)], axis=1).astype(BF16)

    prel = prel_ref[...]
    byte = lambda k: ((prel >> (8 * k)) & 255).astype(F32)
    half_lane = (lane & (NSA_DK - 1)) - AUG_POS
    kaug = blocks_ref[...] + jnp.where((half_lane == 0) | (half_lane == 3), byte(2),
                                       jnp.where((half_lane == 1) | (half_lane == 4), byte(1),
                                                 jnp.where((half_lane == 2) | (half_lane == 5), byte(0), 0.0)))
    ks_out[...] = lo_hi(_pair_norm(part(2), ksg_ref[...], lane_lo), kaug, kaug)
    kw_out[...] = lo_hi(_pair_norm(part(4), kwg_ref[...], lane_lo), kaug, kaug)
    gate_out[...] = jax.nn.sigmoid(part(6))

    vt = _dot_nt(wvt_ref[...], hb)
    tm = vt.shape[1]
    sub = lax.broadcasted_iota(I32, (NSA_DK, 1), 0)
    pad_lo = jnp.broadcast_to(jnp.where(sub == LO_ONE - NSA_DK, 1.0, 0.0), (NSA_DK, tm))
    pad_hi = jnp.broadcast_to(jnp.where(sub == HI_ONE, 1.0, 0.0), (NSA_DK, tm))
    for branch, out in enumerate((vs_out, vw_out)):
        pieces = []
        for grp in range(NSA_G):
            v = vt[(2 * branch + grp) * NSA_DK:(2 * branch + grp + 1) * NSA_DK, :]
            pieces += [v, pad_lo, pad_hi, v]
        out[...] = jnp.concatenate(pieces, axis=0).astype(BF16)


def _inproj1(x2d, positions, gain, w_in, q_norm, k_norm_slc, k_norm_win, tm=256):
    T, D = x2d.shape
    S = positions.shape[1]
    prel = (positions - positions[:, :1]).reshape(T, 1)
    own_block = np.arange(S)[:, None] // SLC_BLOCK == (np.arange(LANES)[None, :] % NSA_DK)
    blocks = jnp.asarray(own_block * MASK_BIG, F32)
    n_in = w_in.shape[1]
    n_pad = -n_in % LANES
    w_in_p = jnp.pad(w_in, ((0, 0), (0, n_pad))).astype(BF16)
    two = lambda g: jnp.tile(g, 2).reshape(1, LANES)
    row = lambda n: pl.BlockSpec((tm, n), lambda i: (i, 0))
    nq = NSA_HEADS * NSA_DK
    vcols = lambda i: w_in[:, nq + i * NSA_KVW:nq + (i + 1) * NSA_KVW]
    w_vt = jnp.concatenate([vcols(3), vcols(5)], axis=1).T.astype(BF16)
    bf = lambda n: jax.ShapeDtypeStruct((T, n), BF16)
    f32 = lambda n: jax.ShapeDtypeStruct((T, n), F32)
    col = pl.BlockSpec((4 * LANES, tm), lambda i: (0, i))
    vt_shape = jax.ShapeDtypeStruct((4 * LANES, T), BF16)
    return pl.pallas_call(
        _inproj1_kernel,
        grid=(T // tm,),
        in_specs=[row(D), row(1), pl.BlockSpec((tm, LANES), lambda i: (i % (S // tm), 0)), _full((1, D)),
                  _full((D, n_in + n_pad)), _full((2 * NSA_KVW, D)), _full((1, LANES)), _full((1, LANES)),
                  _full((1, LANES))],
        out_specs=[row(nq), row(LANES), row(LANES), row(4 * LANES), col, row(4 * LANES), col, row(LANES)],
        out_shape=[bf(nq), f32(LANES), f32(LANES), bf(4 * LANES), vt_shape, bf(4 * LANES), vt_shape, f32(LANES)],
        compiler_params=_params("parallel"),
        name="inproj1",
    )(x2d, prel, blocks, gain.reshape(1, D), w_in_p, w_vt, two(q_norm), two(k_norm_slc), two(k_norm_win))


def _stack_pairs(q_ref, rows):
    return jnp.concatenate([q_ref[0, rows, p * LANES:(p + 1) * LANES] for p in range(NSA_PAIRS)], axis=0)


def _gate_tile(gt, branch, p, lane_lo):
    c = branch * NSA_HPG + 2 * p
    return jnp.where(lane_lo, gt[:, c:c + 1], gt[:, c + 1:c + 2])


def _cmp_kernel(q_ref, kch_ref, vch_ref, pos3_ref, pend_ref, gate_ref, slope_ref,
                pek_ref, w1k_ref, w2k_ref, kcg_ref, pev_ref, w1v_ref, w2v_ref, ovl_ref, eye_ref,
                o_ref, sel_ref, kc_s, vc_s):
    tq = NSA_TQ
    S = q_ref.shape[1]

    def compress(ch_ref, pe_ref, w1_ref, w2_ref):
        a = ch_ref[0, 0]
        h_lo = _dot((a + pe_ref[0:1, :]).astype(BF16), w1_ref[0])
        h_hi = _dot((a + pe_ref[1:2, :]).astype(BF16), w1_ref[1])
        n = h_hi.shape[0]
        hid = jax.nn.gelu(h_lo + pltpu.roll(h_hi, n - 1, 0)).astype(BF16)
        return _dot(hid, w2_ref[0]), _dot(hid, w2_ref[1])

    k_lo, k_hi = compress(kch_ref, pek_ref, w1k_ref, w2k_ref)
    for e, kk in enumerate((k_lo, k_hi)):
        kk = kk * lax.rsqrt(jnp.sum(kk * kk, axis=-1, keepdims=True) * (1.0 / NSA_DK) + 1e-6) * kcg_ref[e:e + 1, :]
        kc_s[e] = kk.astype(BF16)
    v_lo, v_hi = compress(vch_ref, pev_ref, w1v_ref, w2v_ref)
    vc_s[0] = v_lo.astype(BF16)
    vc_s[1] = v_hi.astype(BF16)

    n_cmp = (S - CMP_BLOCK) // CMP_STRIDE + 1
    lane = lax.broadcasted_iota(I32, (1, LANES), 1)
    lane_lo = lane < NSA_DK
    blk_row = lax.broadcasted_iota(I32, (N_SEL, 1), 0)

    def q_block(qi, carry):
        t0 = pl.multiple_of(qi * tq, tq)
        rows = pl.ds(t0, tq)
        qs = _stack_pairs(q_ref, rows)
        pos_q0 = pos3_ref[0, pl.ds(qi, 1), :][:, 0:1]
        posrel = (pend_ref[0] - pos_q0).astype(F32)
        tcol = t0 + lax.broadcasted_iota(I32, (tq, 1), 0)
        valid = ((tcol >= CMP_STRIDE * lane + (CMP_BLOCK - 1)) & (lane < n_cmp))[None]
        psum = jnp.zeros((tq, LANES), F32)
        o_pairs = jnp.zeros((NSA_PAIRS * tq, LANES), F32)
        for e in range(2):
            s = _dot_nt(qs, kc_s[e]).reshape(NSA_PAIRS, tq, LANES)
            s = s + slope_ref[0, NSA_PAIRS * e:NSA_PAIRS * (e + 1)][:, None, :] * posrel[None]
            s = jnp.where(valid, s, NEG_INF)
            m = jnp.max(s, axis=-1, keepdims=True)
            p = jnp.where(valid, jnp.exp2(s - m), 0.0)
            p = p / jnp.maximum(jnp.sum(p, axis=-1, keepdims=True), 1e-20)
            psum = psum + jnp.sum(p, axis=0)
            o_pairs = o_pairs + _dot(p.reshape(NSA_PAIRS * tq, LANES).astype(BF16), vc_s[e])
        imp = _dot_nt(ovl_ref[...], psum, precision=lax.Precision.HIGHEST)
        cur = (t0 + lax.broadcasted_iota(I32, (1, tq), 1)) // SLC_BLOCK
        forced = (blk_row == 0) | (blk_row == cur) | (blk_row == cur - 1)
        imp = jnp.where(forced, FORCE_SCORE, jnp.where(blk_row <= cur, imp, -1.0))
        rank = jnp.zeros((N_SEL, tq), I32)
        for i in range(N_SEL):
            ri = imp[i:i + 1, :]
            rank = rank + jnp.where((ri > imp) | ((ri == imp) & (blk_row > i)), 1, 0)
        sel_t = jnp.where(rank < SLC_TOP_N, 1.0, 0.0).astype(BF16)
        gap = jnp.zeros((NSA_DK - N_SEL, tq), BF16)
        sel_t = jnp.concatenate([sel_t, gap, sel_t, gap], axis=0)
        sel = _dot_nt(eye_ref[...], sel_t)
        sel_ref[0, 0, rows, :] = jnp.where((lane & (NSA_DK - 1)) < N_SEL, sel - 1.0, 0.0).astype(BF16)
        gt = gate_ref[0, 0, rows, :]
        o3 = o_pairs.reshape(NSA_PAIRS, tq, LANES)
        for p in range(NSA_PAIRS):
            o_ref[0, rows, p * LANES:(p + 1) * LANES] = (o3[p] * _gate_tile(gt, 0, p, lane_lo)).astype(BF16)
        return carry

    lax.fori_loop(0, S // tq, q_block, 0)


def _lo_hi_cols(w):
    z = jnp.zeros_like(w)
    return jnp.stack([jnp.concatenate([w, z], axis=1), jnp.concatenate([z, w], axis=1)])


def _nsa_cmp(q, kch, vch, pos3, pend, gates_g, slopes_t, pe_k, w1_k, w2_k, k_norm_cmp, pe_v, w1_v, w2_v):
    B, S, _ = q.shape
    G = NSA_G
    nch = S // CMP_STRIDE
    half = CMP_STRIDE * NSA_DK
    pe2 = lambda pe: pe.reshape(2, half)
    w1_2 = lambda w: w.reshape(2, half, CMP_HIDDEN).astype(BF16)
    n_sel = S // SLC_BLOCK
    n_cmp = (S - CMP_BLOCK) // CMP_STRIDE + 1
    cmp_start = np.arange(LANES) * CMP_STRIDE
    slc_start = np.arange(n_sel) * SLC_BLOCK
    overlap = ((cmp_start[None, :] < slc_start[:, None] + SLC_BLOCK) & (cmp_start[None, :] + CMP_BLOCK > slc_start[:, None])
               & (np.arange(LANES)[None, :] < n_cmp))
    ovl = jnp.asarray(overlap, F32)
    eye = jnp.eye(NSA_TQ, dtype=BF16)
    qspec = pl.BlockSpec((1, S, NSA_PAIRS * LANES), lambda b, g: (b, 0, g))
    chspec = pl.BlockSpec((1, 1, nch, half), lambda b, g: (b, g, 0, 0))
    return pl.pallas_call(
        _cmp_kernel,
        grid=(B, G),
        in_specs=[qspec, chspec, chspec,
                  pl.BlockSpec((1, S // LANES, LANES), lambda b, g: (b, 0, 0)),
                  pl.BlockSpec((1, 1, LANES), lambda b, g: (b, 0, 0)),
                  pl.BlockSpec((1, 1, S, 3 * NSA_HPG), lambda b, g: (b, g, 0, 0)),
                  pl.BlockSpec((1, NSA_HPG, LANES), lambda b, g: (g, 0, 0)),
                  _full((2, half)), _full((2, half, CMP_HIDDEN)), _full((2, CMP_HIDDEN, LANES)), _full((2, LANES)),
                  _full((2, half)), _full((2, half, CMP_HIDDEN)), _full((2, CMP_HIDDEN, LANES)),
                  _full((n_sel, LANES)), _full((NSA_TQ, NSA_TQ))],
        out_specs=[qspec, pl.BlockSpec((1, 1, S, LANES), lambda b, g: (b, g, 0, 0))],
        out_shape=[jax.ShapeDtypeStruct((B, S, NSA_HEADS * NSA_DK), BF16),
                   jax.ShapeDtypeStruct((B, G, S, LANES), BF16)],
        scratch_shapes=[pltpu.VMEM((2, nch, LANES), BF16), pltpu.VMEM((2, nch, LANES), BF16)],
        compiler_params=_params("parallel", "parallel"),
        name="nsa_cmp",
    )(q, kch, vch, pos3, pend, gates_g, slopes_t, pe2(pe_k), w1_2(w1_k), _lo_hi_cols(w2_k).astype(BF16),
      _lo_hi_cols(k_norm_cmp.reshape(1, -1)).reshape(2, LANES), pe2(pe_v), w1_2(w1_v),
      _lo_hi_cols(w2_v).astype(BF16), ovl, eye)


def _slc_win_kernel(q_ref, ks_ref, vs_ref, kw_ref, vw_ref, sel_ref, gate_ref, slopeq_ref, ocmp_ref, o_ref):
    tq = SLC_TQ
    S = q_ref.shape[1]
    lane = lax.broadcasted_iota(I32, (1, LANES), 1)
    lane_lo = lane < NSA_DK
    sub_lo = lax.broadcasted_iota(I32, (LANES, 1), 0) < NSA_DK
    ones_row = (LO_ONE, HI_ONE)

    def q_block(qi, carry):
        t0 = pl.multiple_of(qi * tq, tq)
        rows = pl.ds(t0, tq)
        qs = _stack_pairs(q_ref, rows)
        selm1 = sel_ref[0, 0, rows, :]

        def q_aug(e, with_sel):
            feats = [slopeq_ref[0, NSA_PAIRS * e + p:NSA_PAIRS * e + p + 1, :] for p in range(NSA_PAIRS)]
            if with_sel:
                extra = jnp.concatenate([selm1 + f for f in feats], axis=0)
            else:
                extra = jnp.concatenate([jnp.broadcast_to(f, (tq, LANES)) for f in feats], axis=0)
            return jnp.where(lane_lo, qs, extra) if e == 0 else jnp.where(lane_lo, extra, qs)

        def k_aug(k_ref, e, r0, n):
            return k_ref[0, pl.ds(r0, n), e * LANES:(e + 1) * LANES]

        gt = gate_ref[0, 0, :, rows]
        tq_lane = t0 + (lax.broadcasted_iota(I32, (1, NSA_PAIRS * tq), 1) & (tq - 1))

        def gated(accs, branch):
            outs = []
            for e, acc in enumerate(accs):
                gate = jnp.concatenate([gt[branch * NSA_HPG + 2 * p + e:branch * NSA_HPG + 2 * p + e + 1, :]
                                        for p in range(NSA_PAIRS)], axis=1)
                outs.append(acc * (gate / acc[ones_row[e]:ones_row[e] + 1, :]))
            return jnp.where(sub_lo, outs[0], outs[1])

        qa = [q_aug(0, True), q_aug(1, True)]
        krow = lax.broadcasted_iota(I32, (SLC_TK, 1), 0)

        def slc_step(j, state, last):
            r0 = pl.multiple_of(j * SLC_TK, SLC_TK)
            ss = [_dot_nt(k_aug(ks_ref, e, r0, SLC_TK), qa[e]) for e in range(2)]
            out = []
            for e in range(2):
                m, acc = state[e]
                s = jnp.where(r0 + krow <= tq_lane, ss[e], NEG_INF) if last else ss[e]
                m_new = jnp.maximum(m, jnp.max(s, axis=0, keepdims=True))
                p = jnp.exp2(s - m_new).astype(BF16)
                pv = _dot(vs_ref[e * LANES:(e + 1) * LANES, pl.ds(r0, SLC_TK)], p)
                out.append((m_new, jnp.exp2(m - m_new) * acc + pv))
            return tuple(out)

        init = (jnp.full((1, NSA_PAIRS * tq), NEG_INF, F32), jnp.zeros((LANES, NSA_PAIRS * tq), F32))
        n_full = qi // (SLC_TK // tq)
        state = lax.fori_loop(0, n_full, functools.partial(slc_step, last=False), (init, init))
        state = slc_step(n_full, state, True)
        z = gated([state[0][1], state[1][1]], 1)

        w0 = pl.multiple_of(jnp.maximum(t0 - WINDOW, 0), tq)
        rel = tq_lane - (w0 + lax.broadcasted_iota(I32, (WIN_SPAN, 1), 0))
        wmask = (rel >= 0) & (rel < WINDOW)
        ss = [_dot_nt(k_aug(kw_ref, e, w0, WIN_SPAN), q_aug(e, False)) for e in range(2)]
        accs = []
        for e in range(2):
            s = jnp.where(wmask, ss[e], NEG_INF)
            p = jnp.exp2(s - jnp.max(s, axis=0, keepdims=True)).astype(BF16)
            accs.append(_dot(vw_ref[e * LANES:(e + 1) * LANES, pl.ds(w0, WIN_SPAN)], p))
        z = z + gated(accs, 2)

        for p in range(NSA_PAIRS):
            sl = slice(p * LANES, (p + 1) * LANES)
            o_ref[0, rows, sl] = (ocmp_ref[0, rows, sl].astype(F32) + z[:, p * tq:(p + 1) * tq].T).astype(BF16)
        return carry

    lax.fori_loop(0, S // tq, q_block, 0)


def _slope_aug(slopes_eo):
    hi = slopes_eo.astype(BF16).astype(F32)
    lo = (slopes_eo - hi).astype(BF16).astype(F32)
    w = jnp.asarray([65536.0, 256.0, 1.0], F32)
    feats = jnp.concatenate([hi[..., None] * w, lo[..., None] * w], axis=-1)
    half = jnp.pad(feats, ((0, 0), (0, 0), (AUG_POS, NSA_DK - AUG_POS - 6)))
    return jnp.concatenate([half, half], axis=-1).astype(BF16)


def _nsa_slc_win(q, ks4, vst4, kw4, vwt4, sel, gates_t, slopeq, o_cmp):
    B, S, _ = q.shape
    qspec = pl.BlockSpec((1, S, NSA_PAIRS * LANES), lambda b, g: (b, 0, g))
    kvspec = pl.BlockSpec((1, S, 2 * LANES), lambda b, g: (b, 0, g))
    vtspec = pl.BlockSpec((2 * LANES, S), lambda b, g: (g, b))
    return pl.pallas_call(
        _slc_win_kernel,
        grid=(B, NSA_G),
        in_specs=[qspec, kvspec, vtspec, kvspec, vtspec,
                  pl.BlockSpec((1, 1, S, LANES), lambda b, g: (b, g, 0, 0)),
                  pl.BlockSpec((1, 1, 3 * NSA_HPG, S), lambda b, g: (b, g, 0, 0)),
                  pl.BlockSpec((1, NSA_HPG, LANES), lambda b, g: (g, 0, 0)),
                  qspec],
        out_specs=qspec,
        out_shape=jax.ShapeDtypeStruct((B, S, NSA_HEADS * NSA_DK), BF16),
        compiler_params=_params("parallel", "parallel"),
        name="nsa_slc_win",
    )(q, ks4, vst4, kw4, vwt4, sel, gates_t, slopeq, o_cmp)


def _outproj_kernel(o_ref, x_ref, w_ref, out_ref):
    out_ref[...] = x_ref[...] + _dot(o_ref[...], w_ref[...])


def _outproj(o2d, x2d, w_out, tm=512):
    T, D = x2d.shape
    K = o2d.shape[1]
    return pl.pallas_call(
        _outproj_kernel,
        grid=(T // tm,),
        in_specs=[pl.BlockSpec((tm, K), lambda i: (i, 0)), pl.BlockSpec((tm, D), lambda i: (i, 0)), _full((K, D))],
        out_specs=pl.BlockSpec((tm, D), lambda i: (i, 0)),
        out_shape=jax.ShapeDtypeStruct((T, D), F32),
        compiler_params=_params("parallel"),
        name="outproj1",
    )(o2d, x2d, w_out.astype(BF16))


def _mixer1(x, positions, gain, w_in, q_norm, k_norm_cmp, k_norm_slc, k_norm_win,
            pe_k, w1_k, w2_k, pe_v, w1_v, w2_v, w_out):
    B, S, D = x.shape
    T = B * S
    G, HPG, DK = NSA_G, NSA_HPG, NSA_DK
    assert S // SLC_BLOCK == N_SEL and S % NSA_TQ == 0 and NSA_TQ == LANES
    x2d = x.reshape(T, D)
    q, kc, vc, ks4, vs4, kw4, vw4, gates = _inproj1(x2d, positions, gain, w_in, q_norm, k_norm_slc,
                                                    k_norm_win)
    b3 = lambda t: t.reshape(B, S, -1)
    chunks = lambda t: t.reshape(B, S, G, DK).transpose(0, 2, 1, 3).reshape(B, G, S // CMP_STRIDE, CMP_STRIDE * DK)
    gates_g = gates[:, :3 * NSA_HEADS].reshape(B, S, 3, G, HPG).transpose(0, 3, 1, 2, 4).reshape(B, G, S, 3 * HPG)
    n_cmp = (S - CMP_BLOCK) // CMP_STRIDE + 1
    pend = jnp.pad(positions[:, CMP_BLOCK - 1::CMP_STRIDE][:, :n_cmp], ((0, 0), (0, LANES - n_cmp))).reshape(B, 1, LANES)
    pos3 = positions.reshape(B, S // LANES, LANES)
    slopes = jnp.asarray(2.0 ** (-8.0 * np.arange(1, NSA_HEADS + 1) / NSA_HEADS), F32) * LOG2E
    slopes_eo = slopes.reshape(G, NSA_PAIRS, 2).transpose(0, 2, 1).reshape(G, HPG)
    slopes_t = jnp.broadcast_to(slopes_eo[:, :, None], (G, HPG, LANES))
    o_cmp, sel = _nsa_cmp(b3(q), chunks(kc), chunks(vc), pos3, pend, gates_g, slopes_t,
                          pe_k, w1_k, w2_k, k_norm_cmp, pe_v, w1_v, w2_v)
    o = _nsa_slc_win(b3(q), b3(ks4), vs4, b3(kw4), vw4, sel, gates_g.transpose(0, 1, 3, 2), _slope_aug(slopes_eo),
                     o_cmp)
    return _outproj(o.reshape(T, -1), x2d, w_out).reshape(B, S, D)


def kernel(x, positions, norm_mix, norm_ffn, mix0_w_in, mla_q_a_norm, mla_w_uq, mla_kv_a_norm, mla_w_ukv, mla_q_norm, mla_k_norm, conv_dw_w, conv_dw_b, conv_ln_g, conv_ln_b, mix0_w_out, nsa_w_in, nsa_q_norm, nsa_k_norm_cmp, nsa_k_norm_slc, nsa_k_norm_win, nsa_cmp_pe_k, nsa_cmp_w1_k, nsa_cmp_w2_k, nsa_cmp_pe_v, nsa_cmp_w1_v, nsa_cmp_w2_v, nsa_w_out, moe_router_group, moe_router_group_b, moe_router_expert, moe_router_expert_b, moe_w_gate, moe_w_up, moe_w_down):
    def moe(x, layer):
        return _moe(x, norm_ffn[layer], moe_router_group[layer], moe_router_group_b[layer], moe_router_expert[layer],
                    moe_router_expert_b[layer], moe_w_gate[layer], moe_w_up[layer], moe_w_down[layer])

    x = _mixer0(x, positions, norm_mix[0], mix0_w_in[0], mla_q_a_norm[0], mla_w_uq[0], mla_kv_a_norm[0], mla_w_ukv[0],
                mla_q_norm[0], mla_k_norm[0], conv_dw_w[0], conv_dw_b[0], conv_ln_g[0], conv_ln_b[0], mix0_w_out[0])
    x = moe(x, 0)
    x = _mixer1(x, positions, norm_mix[1], nsa_w_in[0], nsa_q_norm[0], nsa_k_norm_cmp[0], nsa_k_norm_slc[0],
                nsa_k_norm_win[0], nsa_cmp_pe_k[0], nsa_cmp_w1_k[0], nsa_cmp_w2_k[0], nsa_cmp_pe_v[0],
                nsa_cmp_w1_v[0], nsa_cmp_w2_v[0], nsa_w_out[0])
    return moe(x, 1)
```

```python
import functools

import numpy as np
import jax
import jax.numpy as jnp
from jax import lax
from jax.experimental import pallas as pl
from jax.experimental.pallas import tpu as pltpu

F32 = jnp.float32
BF16 = jnp.bfloat16
I32 = jnp.int32

LANES = 128
NEG_INF = -1e30
LOG2E = 1.4426950408889634
Q_BLOCK = 128
MLA_HEADS = 8
MLA_Q_LORA = 256
MLA_KV_LORA = 128
MLA_NOPE = 64
MLA_ROPE = 32
MLA_V = 64
MLA_QK = MLA_NOPE + MLA_ROPE
ROPE_THETA = 10000.0
CONV_CH = 512
CONV_W = 31
NSA_HEADS = 16
NSA_G = 2
NSA_HPG = NSA_HEADS // NSA_G
NSA_DK = 64
NSA_KVW = NSA_G * NSA_DK
CMP_BLOCK = 32
CMP_STRIDE = 16
CMP_HIDDEN = 128
SLC_BLOCK = 64
SLC_TOP_N = 8
WINDOW = 256
FORCE_SCORE = 1e4
MOE_GROUPS = 4
MOE_EPG = 8
MOE_EXPERTS = MOE_GROUPS * MOE_EPG
MOE_HIDDEN = 256
MOE_ROW_BLOCK = 256

VMEM_LIMIT = 56 * 1024 * 1024


def _params(*sem):
    return pltpu.CompilerParams(dimension_semantics=sem, vmem_limit_bytes=VMEM_LIMIT)


def _full(shape):
    n = len(shape)
    return pl.BlockSpec(shape, lambda *_: (0,) * n)


def _row_mean(x):
    n = x.shape[-1]
    folded = functools.reduce(jnp.add, [x[:, c:c + LANES] for c in range(0, n, LANES)])
    return jnp.sum(folded, axis=-1, keepdims=True) * (1.0 / n)


def _rms(x, eps=1e-6):
    return x * lax.rsqrt(_row_mean(x * x) + eps)


def _dot(a, b):
    return jnp.dot(a, b, preferred_element_type=F32)


def _group_matrix(groups):
    width = LANES // groups
    same = (lax.broadcasted_iota(I32, (LANES, LANES), 0) // width) == (lax.broadcasted_iota(I32, (LANES, LANES), 1) // width)
    return jnp.where(same, 1.0, 0.0).astype(BF16)


def _group_sums(t, group_matrix):
    hi = t.astype(BF16)
    lo = (t - hi.astype(F32)).astype(BF16)
    return _dot(hi, group_matrix) + _dot(lo, group_matrix)


def _dot_nt(a, b, **kw):
    return lax.dot_general(a, b, (((1,), (1,)), ((), ())), preferred_element_type=F32, **kw)


def _inproj0_kernel(x_ref, pos_ref, gmix_ref, win_ref, qan_ref, wuq_ref, kvan_ref, wuk_ref, wuv_ref,
                    qg_ref, kg_ref, invf_ref, sgn_ref, q_out, k_out, v_out, u_out):
    h = _rms(x_ref[...]) * gmix_ref[...]
    proj = _dot(h.astype(BF16), win_ref[...])
    c_q = proj[:, :MLA_Q_LORA]
    c_kv = proj[:, MLA_Q_LORA:MLA_Q_LORA + MLA_KV_LORA]
    k_rope = proj[:, 384:512]
    a = proj[:, 512:512 + CONV_CH]
    g = proj[:, 512 + CONV_CH:]
    u_out[...] = a * jax.nn.sigmoid(g)
    cqn = (_rms(c_q) * qan_ref[...]).astype(BF16)
    ckvn = (_rms(c_kv) * kvan_ref[...]).astype(BF16)
    q = _dot(cqn, wuq_ref[...])
    kn = _dot(ckvn, wuk_ref[...])
    slot_row = lax.broadcasted_iota(I32, (MLA_HEADS * LANES, 1), 0) & (LANES - 1)
    v_out[...] = (_dot_nt(wuv_ref[...], ckvn) + jnp.where(slot_row == MLA_V, 1.0, 0.0)).astype(BF16)
    ang = pos_ref[...].astype(F32) * invf_ref[...]
    cos = jnp.cos(ang)
    sin = jnp.sin(ang) * sgn_ref[...]
    lane = lax.broadcasted_iota(I32, (1, LANES), 1)
    first_half = (lane >= MLA_NOPE) & (lane < MLA_NOPE + MLA_ROPE // 2)
    scale = MLA_QK ** -0.5 * LOG2E

    all_lanes = _group_matrix(1)

    def norm_rope(t, gain):
        t = t * lax.rsqrt(_group_sums(t * t, all_lanes) * (1.0 / MLA_QK) + 1e-6) * gain
        partner = jnp.where(first_half, pltpu.roll(t, LANES - MLA_ROPE // 2, 1), pltpu.roll(t, MLA_ROPE // 2, 1))
        return t * cos + partner * sin

    for hd in range(MLA_HEADS):
        sl = slice(hd * LANES, (hd + 1) * LANES)
        q_out[:, sl] = (norm_rope(q[:, sl], qg_ref[...]) * scale).astype(BF16)
        k_out[:, sl] = norm_rope(kn[:, sl] + k_rope, kg_ref[...]).astype(BF16)


def _head_slots(w, n_heads, width, offset=0):
    k = w.shape[0]
    w = w.reshape(k, n_heads, width)
    w = jnp.pad(w, ((0, 0), (0, 0), (offset, LANES - width - offset)))
    return w.reshape(k, n_heads * LANES)


def _inproj0(x2d, pos_col, gmix, w_in, q_a_norm, w_uq, kv_a_norm, w_ukv, q_norm, k_norm, tm=512):
    T, D = x2d.shape
    H = MLA_HEADS
    w_krope = jnp.pad(w_in[:, 384:416], ((0, 0), (MLA_NOPE, LANES - MLA_NOPE - MLA_ROPE)))
    w_in_p = jnp.concatenate([w_in[:, :384], w_krope, w_in[:, 416:]], axis=1).astype(BF16)
    w_uq_p = _head_slots(w_uq, H, MLA_QK).astype(BF16)
    w_ukv3 = w_ukv.reshape(MLA_KV_LORA, H, MLA_NOPE + MLA_V)
    w_uk_p = _head_slots(w_ukv3[:, :, :MLA_NOPE].reshape(MLA_KV_LORA, H * MLA_NOPE), H, MLA_NOPE).astype(BF16)
    w_uv = _head_slots(w_ukv3[:, :, MLA_NOPE:].reshape(MLA_KV_LORA, H * MLA_V), H, MLA_V).T.astype(BF16)
    pad = LANES - MLA_QK
    qg = jnp.pad(q_norm, (0, pad)).reshape(1, LANES)
    kg = jnp.pad(k_norm, (0, pad)).reshape(1, LANES)
    half = MLA_ROPE // 2
    inv_freq = ROPE_THETA ** (-jnp.arange(half, dtype=F32) / half)
    invf = jnp.zeros((LANES,), F32).at[MLA_NOPE:MLA_NOPE + half].set(inv_freq).at[MLA_NOPE + half:MLA_QK].set(inv_freq)
    sgn = jnp.zeros((LANES,), F32).at[MLA_NOPE:MLA_NOPE + half].set(-1.0).at[MLA_NOPE + half:MLA_QK].set(1.0)
    row = lambda n: pl.BlockSpec((tm, n), lambda i: (i, 0))
    n_in = w_in_p.shape[1]
    return pl.pallas_call(
        _inproj0_kernel,
        grid=(T // tm,),
        in_specs=[row(D), row(1), _full((1, D)), _full((D, n_in)), _full((1, MLA_Q_LORA)),
                  _full((MLA_Q_LORA, H * LANES)), _full((1, MLA_KV_LORA)), _full((MLA_KV_LORA, H * LANES)),
                  _full((H * LANES, MLA_KV_LORA)), _full((1, LANES)), _full((1, LANES)), _full((1, LANES)),
                  _full((1, LANES))],
        out_specs=[row(H * LANES), row(H * LANES), pl.BlockSpec((H * LANES, tm), lambda i: (0, i)), row(CONV_CH)],
        out_shape=[jax.ShapeDtypeStruct((T, H * LANES), BF16), jax.ShapeDtypeStruct((T, H * LANES), BF16),
                   jax.ShapeDtypeStruct((H * LANES, T), BF16), jax.ShapeDtypeStruct((T, CONV_CH), F32)],
        compiler_params=_params("parallel"),
        name="inproj0",
    )(x2d, pos_col, gmix.reshape(1, D), w_in_p, q_a_norm.reshape(1, -1), w_uq_p, kv_a_norm.reshape(1, -1),
      w_uk_p, w_uv, qg, kg, invf.reshape(1, LANES), sgn.reshape(1, LANES))


MLA_TQ = 512
MLA_TK = 512
MLA_ONE = MLA_V


def _mla_attn_kernel(q_ref, k_ref, vt_ref, o_ref):
    S = q_ref.shape[1]
    tq, tk = MLA_TQ, MLA_TK
    krow = lax.broadcasted_iota(I32, (tk, 1), 0)
    qcol = lax.broadcasted_iota(I32, (1, tq), 1)

    def q_block(qi, carry):
        q0 = pl.multiple_of(qi * tq, tq)
        qs = [q_ref[0, pl.ds(q0, tq), hh * LANES:(hh + 1) * LANES] for hh in range(2)]

        def kv_step(j, state, masked):
            k0 = pl.multiple_of(j * tk, tk)
            hs = [slice(hh * LANES, (hh + 1) * LANES) for hh in range(2)]
            ss = [_dot_nt(k_ref[0, pl.ds(k0, tk), hs[hh]], qs[hh]) for hh in range(2)]
            out = []
            for hh in range(2):
                m, acc = state[hh]
                s = jnp.where(k0 + krow <= q0 + qcol, ss[hh], NEG_INF) if masked else ss[hh]
                m_new = jnp.maximum(m, jnp.max(s, axis=0, keepdims=True))
                p = jnp.exp2(s - m_new).astype(BF16)
                out.append((m_new, jnp.exp2(m - m_new) * acc + _dot(vt_ref[hs[hh], pl.ds(k0, tk)], p)))
            return tuple(out)

        init = (jnp.full((1, tq), NEG_INF, F32), jnp.zeros((LANES, tq), F32))
        n_full = (qi * tq) // tk
        state = lax.fori_loop(0, n_full, functools.partial(kv_step, masked=False), (init, init))
        state = kv_step(n_full, state, True)
        for hh in range(2):
            acc = state[hh][1]
            o_t = acc * (1.0 / acc[MLA_ONE:MLA_ONE + 1, :])
            o_ref[0, pl.ds(q0, tq), hh * LANES:(hh + 1) * LANES] = o_t.T.astype(BF16)
        return carry

    lax.fori_loop(0, S // tq, q_block, 0)


def _mla_attn(q, k, vt):
    B, S, _ = q.shape
    spec = pl.BlockSpec((1, S, 2 * LANES), lambda b, h: (b, 0, h))
    return pl.pallas_call(
        _mla_attn_kernel,
        grid=(B, MLA_HEADS // 2),
        in_specs=[spec, spec, pl.BlockSpec((2 * LANES, S), lambda b, h: (h, b))],
        out_specs=spec,
        out_shape=jax.ShapeDtypeStruct((B, S, MLA_HEADS * LANES), BF16),
        compiler_params=_params("parallel", "parallel"),
        name="mla_attn",
    )(q, k, vt)


CONV_TILE = 64
CONV_PAD = 32


CONV_ROWS = 512


def _conv_out0_kernel(u_ref, o_ref, x_ref, dww_ref, dwb_ref, lng_ref, lnb_ref, wo_ref, out_ref, upad, act):
    step = pl.program_id(1)

    @pl.when(step == 0)
    def _():
        upad[0:CONV_PAD, :] = jnp.zeros((CONV_PAD, CONV_CH), F32)
        upad[CONV_PAD:, :] = u_ref[0]

    lead = CONV_PAD - (CONV_W - 1)
    base = step * CONV_ROWS

    def tile(i, carry):
        t0 = pl.multiple_of(i * CONV_TILE, CONV_TILE)
        win = upad[pl.ds(pl.multiple_of(base + t0, CONV_TILE), CONV_TILE + CONV_PAD), :]
        acc = jnp.zeros((CONV_TILE, CONV_CH), F32) + dwb_ref[...]
        for j in range(CONV_W):
            acc = acc + win[lead + j:lead + j + CONV_TILE, :] * dww_ref[j:j + 1, :]
        xc = acc - _row_mean(acc)
        var = _row_mean(xc * xc)
        y = xc * lax.rsqrt(var + 1e-5) * lng_ref[...] + lnb_ref[...]
        act[pl.ds(t0, CONV_TILE), :] = (y * jax.nn.sigmoid(y)).astype(BF16)
        return carry

    lax.fori_loop(0, CONV_ROWS // CONV_TILE, tile, 0)
    n_o = o_ref.shape[2]
    y = _dot(o_ref[0], wo_ref[:n_o, :]) + _dot(act[...], wo_ref[n_o:, :])
    out_ref[0] = x_ref[0] + y


def _conv_out0(u, o_mla, x, dw_w, dw_b, ln_g, ln_b, w_out):
    B, S, D = x.shape
    n_o = o_mla.shape[2]
    n_v = MLA_HEADS * MLA_V
    w_attn = _head_slots(w_out[:n_v].T, MLA_HEADS, MLA_V).T
    w_out_p = jnp.concatenate([w_attn, w_out[n_v:]], axis=0).astype(BF16)
    tspec = lambda n: pl.BlockSpec((1, CONV_ROWS, n), lambda b, t: (b, t, 0))
    return pl.pallas_call(
        _conv_out0_kernel,
        grid=(B, S // CONV_ROWS),
        in_specs=[pl.BlockSpec((1, S, CONV_CH), lambda b, t: (b, 0, 0)), tspec(n_o), tspec(D),
                  _full((CONV_W, CONV_CH)), _full((1, CONV_CH)), _full((1, CONV_CH)), _full((1, CONV_CH)),
                  _full((n_o + CONV_CH, D))],
        out_specs=tspec(D),
        out_shape=jax.ShapeDtypeStruct((B, S, D), F32),
        scratch_shapes=[pltpu.VMEM((S + CONV_PAD, CONV_CH), F32), pltpu.VMEM((CONV_ROWS, CONV_CH), BF16)],
        compiler_params=_params("parallel", "arbitrary"),
        name="conv_out0",
    )(u, o_mla, x, dw_w, dw_b.reshape(1, -1), ln_g.reshape(1, -1), ln_b.reshape(1, -1), w_out_p)


def _mixer0(x, positions, gmix, w_in, q_a_norm, w_uq, kv_a_norm, w_ukv, q_norm, k_norm,
            dw_w, dw_b, ln_g, ln_b, w_out):
    B, S, D = x.shape
    T = B * S
    q, k, v, u = _inproj0(x.reshape(T, D), positions.reshape(T, 1), gmix, w_in, q_a_norm, w_uq, kv_a_norm, w_ukv,
                          q_norm, k_norm)
    o = _mla_attn(q.reshape(B, S, -1), k.reshape(B, S, -1), v)
    return _conv_out0(u.reshape(B, S, -1), o, x, dw_w, dw_b, ln_g, ln_b, w_out)


ROUTE_TILE = 512
ROUTE_ROWS = 40


def _route_kernel(x_ref, g_ref, wr_ref, br_ref, tri_ref, hn_ref, oi_ref, of_ref, cnt_ref, carry):
    @pl.when(pl.program_id(0) == 0)
    def _():
        carry[...] = jnp.zeros_like(carry)

    hn = _rms(x_ref[...]) * g_ref[...]
    _slab_store(hn_ref, hn)
    tm = hn.shape[0]
    logits = _dot_nt(wr_ref[...], hn, precision=lax.Precision.HIGHEST) + br_ref[...]
    gl = logits[MOE_EXPERTS:MOE_EXPERTS + MOE_GROUPS]
    rid_g = lax.broadcasted_iota(I32, (MOE_GROUPS, tm), 0)
    gmax = jnp.max(gl, axis=0, keepdims=True)
    grp = jnp.min(jnp.where(gl == gmax, rid_g, MOE_GROUPS), axis=0, keepdims=True)
    g_w = 1.0 / jnp.sum(jnp.exp(gl - gmax), axis=0, keepdims=True)
    e_in = jnp.zeros((MOE_EPG, tm), F32)
    for gi in range(MOE_GROUPS):
        e_in = jnp.where(grp == gi, logits[gi * MOE_EPG:(gi + 1) * MOE_EPG], e_in)
    rid_e = lax.broadcasted_iota(I32, (MOE_EPG, tm), 0)
    v1 = jnp.max(e_in, axis=0, keepdims=True)
    i1 = jnp.min(jnp.where(e_in == v1, rid_e, MOE_EPG), axis=0, keepdims=True)
    rest = jnp.where(rid_e == i1, -jnp.inf, e_in)
    v2 = jnp.max(rest, axis=0, keepdims=True)
    i2 = jnp.min(jnp.where(rest == v2, rid_e, MOE_EPG), axis=0, keepdims=True)
    ex = jnp.exp(v2 - v1)
    den = 1.0 + ex
    e1 = grp * MOE_EPG + i1
    e2 = grp * MOE_EPG + i2
    rid = lax.broadcasted_iota(I32, (MOE_EXPERTS, tm), 0)
    hit1 = rid == e1
    hit2 = rid == e2
    member = jnp.where(hit1 | hit2, 1.0, 0.0)
    before = _dot(member.astype(BF16), tri_ref[...]) + carry[...]
    r1 = jnp.sum(jnp.where(hit1, before, 0.0), axis=0, keepdims=True)
    r2 = jnp.sum(jnp.where(hit2, before, 0.0), axis=0, keepdims=True)
    carry[...] = carry[...] + jnp.sum(member, axis=1, keepdims=True)
    oi_ref[...] = jnp.zeros_like(oi_ref)
    oi_ref[0:1, :] = e1
    oi_ref[1:2, :] = e2
    oi_ref[2:3, :] = r1.astype(I32)
    oi_ref[3:4, :] = r2.astype(I32)
    of_ref[...] = jnp.zeros_like(of_ref)
    of_ref[0:1, :] = g_w / den
    of_ref[1:2, :] = g_w * ex / den
    cnt_ref[...] = jnp.broadcast_to(carry[...], cnt_ref.shape).astype(I32)


def _route(x2d, gain, router_group, router_group_b, router_expert, router_expert_b):
    T, D = x2d.shape
    tm = ROUTE_TILE
    pad = ROUTE_ROWS - MOE_EXPERTS - MOE_GROUPS
    wr = jnp.concatenate([router_expert.T, router_group.T, jnp.zeros((pad, D), F32)], axis=0)
    br = jnp.concatenate([router_expert_b, router_group_b, jnp.zeros((pad,), F32)]).reshape(ROUTE_ROWS, 1)
    tri = (jnp.arange(tm)[:, None] < jnp.arange(tm)[None, :]).astype(BF16)
    return pl.pallas_call(
        _route_kernel,
        grid=(T // tm,),
        in_specs=[pl.BlockSpec((tm, D), lambda i: (i, 0)), _full((1, D)), _full((ROUTE_ROWS, D)),
                  _full((ROUTE_ROWS, 1)), _full((tm, tm))],
        out_specs=[pl.BlockSpec((tm * SLAB, LANES), lambda i: (i, 0)), pl.BlockSpec((8, tm), lambda i: (0, i)),
                   pl.BlockSpec((8, tm), lambda i: (0, i)), _full((MOE_EXPERTS, LANES))],
        out_shape=[jax.ShapeDtypeStruct((T * SLAB, LANES), F32), jax.ShapeDtypeStruct((8, T), I32),
                   jax.ShapeDtypeStruct((8, T), F32), jax.ShapeDtypeStruct((MOE_EXPERTS, LANES), I32)],
        scratch_shapes=[pltpu.VMEM((MOE_EXPERTS, 1), F32)],
        compiler_params=_params("arbitrary"),
        name="moe_route",
    )(x2d, gain.reshape(1, D), wr, br, tri)


MOVE_CHUNK = 512
MOVE_UNROLL = 8


SLAB = 8


def _slab_load(ref, n, first=0, stride=SLAB):
    return jnp.concatenate([ref[pl.ds(first + j, n, stride=stride), :] for j in range(SLAB)], axis=1)


def _slab_store(ref, value):
    n = value.shape[0]
    for j in range(SLAB):
        ref[pl.ds(j, n, stride=SLAB), :] = value[:, j * LANES:(j + 1) * LANES]


def _slab(row):
    return pl.ds(pl.multiple_of(row, SLAB), SLAB)


def _for_tokens(n, fn):
    def body(t, carry):
        fn(t, 0)
        fn(t, 1)
        return carry
    lax.fori_loop(0, n, body, 0, unroll=MOVE_UNROLL)


def _dispatch_kernel(dest_ref, hn_ref, xs_init_ref, xs_ref, sem):
    del xs_init_ref
    base = pl.program_id(0) * (2 * MOVE_CHUNK)

    def copy(t, k):
        return pltpu.make_async_copy(hn_ref.at[_slab(t * SLAB)], xs_ref.at[_slab(dest_ref[base + 2 * t + k])], sem)

    _for_tokens(MOVE_CHUNK, lambda t, k: copy(t, k).start())
    _for_tokens(MOVE_CHUNK, lambda t, k: copy(t, k).wait())


def _dispatch(dest, hn, n_rows):
    T = hn.shape[0] // SLAB
    tm = MOVE_CHUNK
    return pl.pallas_call(
        _dispatch_kernel,
        grid_spec=pltpu.PrefetchScalarGridSpec(
            num_scalar_prefetch=1,
            grid=(T // tm,),
            in_specs=[pl.BlockSpec((tm * SLAB, LANES), lambda i, d: (i, 0)), pl.BlockSpec(memory_space=pl.ANY)],
            out_specs=pl.BlockSpec(memory_space=pl.ANY),
            scratch_shapes=[pltpu.SemaphoreType.DMA(())],
        ),
        out_shape=jax.ShapeDtypeStruct((n_rows * SLAB, LANES), F32),
        input_output_aliases={2: 0},
        compiler_params=_params("arbitrary"),
        name="moe_dispatch",
    )(dest, hn, jnp.zeros((n_rows * SLAB, LANES), F32))


def _expert_kernel(blk_e_ref, n_used_ref, xs_ref, wg_ref, wu_ref, wd_ref, ys_ref, wg_s, wu_s, wd_s):
    b = pl.program_id(0)

    @pl.when((b == 0) | (blk_e_ref[b] != blk_e_ref[jnp.maximum(b - 1, 0)]))
    def _():
        wg_s[...] = wg_ref[0].astype(BF16)
        wu_s[...] = wu_ref[0].astype(BF16)
        wd_s[...] = wd_ref[0].astype(BF16)

    @pl.when(b < n_used_ref[0])
    def _():
        x = _slab_load(xs_ref, MOE_ROW_BLOCK).astype(BF16)
        gate = _dot(x, wg_s[...])
        hid = (gate * jax.nn.sigmoid(gate) * _dot(x, wu_s[...])).astype(BF16)
        _slab_store(ys_ref, _dot(hid, wd_s[...]))

    @pl.when(b >= n_used_ref[0])
    def _():
        ys_ref[...] = jnp.zeros_like(ys_ref)


def _experts(blk_expert, n_used, xs, w_gate, w_up, w_down):
    R = xs.shape[0] // SLAB
    rb = MOE_ROW_BLOCK
    D = w_gate.shape[1]
    up_spec = pl.BlockSpec((1, D, MOE_HIDDEN), lambda b, be, nu: (be[b], 0, 0))
    return pl.pallas_call(
        _expert_kernel,
        grid_spec=pltpu.PrefetchScalarGridSpec(
            num_scalar_prefetch=2,
            grid=(R // rb,),
            in_specs=[pl.BlockSpec((rb * SLAB, LANES), lambda b, be, nu: (jnp.minimum(b, nu[0] - 1), 0)),
                      up_spec, up_spec, pl.BlockSpec((1, MOE_HIDDEN, D), lambda b, be, nu: (be[b], 0, 0))],
            out_specs=pl.BlockSpec((rb * SLAB, LANES), lambda b, be, nu: (b, 0)),
            scratch_shapes=[pltpu.VMEM((D, MOE_HIDDEN), BF16), pltpu.VMEM((D, MOE_HIDDEN), BF16),
                            pltpu.VMEM((MOE_HIDDEN, D), BF16)],
        ),
        out_shape=jax.ShapeDtypeStruct((R * SLAB, LANES), F32),
        compiler_params=_params("arbitrary"),
        name="moe_experts",
    )(blk_expert, n_used, xs, w_gate, w_up, w_down)


def _combine_kernel(dest_ref, x_ref, gate_ref, ys_ref, out_ref, buf, sem):
    tm = x_ref.shape[0]
    step = pl.program_id(0)

    def copy(s, t, k):
        a = 2 * t + k
        slot = s % 2
        return pltpu.make_async_copy(ys_ref.at[_slab(dest_ref[s * (2 * tm) + a])], buf.at[slot, _slab(a * SLAB)],
                                     sem.at[slot])

    @pl.when(step == 0)
    def _():
        _for_tokens(tm, lambda t, k: copy(step, t, k).start())

    @pl.when(step + 1 < pl.num_programs(0))
    def _():
        _for_tokens(tm, lambda t, k: copy(step + 1, t, k).start())

    _for_tokens(tm, lambda t, k: copy(step, t, k).wait())
    mine = buf.at[step % 2]
    y0 = _slab_load(mine, tm, 0, 2 * SLAB)
    y1 = _slab_load(mine, tm, SLAB, 2 * SLAB)
    out_ref[...] = x_ref[...] + gate_ref[:, 0:1] * y0 + gate_ref[:, 1:2] * y1


def _combine(dest, x2d, gates_col, ys, tm=MOVE_CHUNK):
    T, D = x2d.shape
    return pl.pallas_call(
        _combine_kernel,
        grid_spec=pltpu.PrefetchScalarGridSpec(
            num_scalar_prefetch=1,
            grid=(T // tm,),
            in_specs=[pl.BlockSpec((tm, D), lambda i, d: (i, 0)), pl.BlockSpec((tm, 2), lambda i, d: (i, 0)),
                      pl.BlockSpec(memory_space=pl.ANY)],
            out_specs=pl.BlockSpec((tm, D), lambda i, d: (i, 0)),
            scratch_shapes=[pltpu.VMEM((2, tm * 2 * SLAB, LANES), F32), pltpu.SemaphoreType.DMA((2,))],
        ),
        out_shape=jax.ShapeDtypeStruct((T, D), F32),
        compiler_params=_params("arbitrary"),
        name="moe_combine",
    )(dest, x2d, gates_col, ys)


def _moe(x, gain, router_group, router_group_b, router_expert, router_expert_b, w_gate, w_up, w_down):
    B, S, D = x.shape
    T = B * S
    x2d = x.reshape(T, D)
    hn, oi, of, cnt = _route(x2d, gain, router_group, router_group_b, router_expert, router_expert_b)
    rb = MOE_ROW_BLOCK
    counts = cnt[:, 0]
    cap = (counts + rb - 1) // rb * rb
    cap_end = jnp.cumsum(cap)
    start = cap_end - cap
    eid = oi[0:2].T
    first = jnp.sum(jnp.where(eid[:, :, None] == jnp.arange(MOE_EXPERTS), start, 0), axis=-1)
    dest = ((first + oi[2:4].T) * SLAB).reshape(2 * T).astype(I32)
    n_rows = 2 * T + MOE_EXPERTS * rb
    n_blk = n_rows // rb
    blk_first_row = jnp.arange(n_blk, dtype=I32) * rb
    blk_expert = jnp.minimum(jnp.sum(cap_end[None, :] <= blk_first_row[:, None], axis=1), MOE_EXPERTS - 1)
    n_used = (cap_end[-1:] // rb).astype(I32)
    xs = _dispatch(dest, hn, n_rows)
    ys = _experts(blk_expert.astype(I32), n_used, xs, w_gate, w_up, w_down)
    out = _combine(dest, x2d, of[0:2].T, ys)
    return out.reshape(B, S, D)


NSA_TQ = 128
SLC_TQ = 256
NSA_PAIRS = NSA_HPG // 2
N_SEL = 32
NSA_Q_SCALE = NSA_DK ** -0.5 * LOG2E
LO_ONE = LANES - 1
HI_ONE = 0
SLC_TK = 512
WIN_SPAN = WINDOW + SLC_TQ
AUG_POS = N_SEL
MASK_BIG = 1e30


def _pair_norm(t, gain2, lane_lo):
    t2 = t * t
    s_lo = jnp.sum(jnp.where(lane_lo, t2, 0.0), axis=-1, keepdims=True)
    s_hi = jnp.sum(jnp.where(lane_lo, 0.0, t2), axis=-1, keepdims=True)
    inv = jnp.where(lane_lo, lax.rsqrt(s_lo * (1.0 / NSA_DK) + 1e-6), lax.rsqrt(s_hi * (1.0 / NSA_DK) + 1e-6))
    return t * inv * gain2


def _inproj1_kernel(x_ref, prel_ref, blocks_ref, g_ref, win_ref, wvt_ref, qg_ref, ksg_ref, kwg_ref,
                    q_out, kc_out, vc_out, ks_out, vs_out, kw_out, vw_out, gate_out):
    h = _rms(x_ref[...]) * g_ref[...]
    hb = h.astype(BF16)
    proj = _dot(hb, win_ref[...])
    lane = lax.broadcasted_iota(I32, (1, LANES), 1)
    lane_lo = lane < NSA_DK
    nq = NSA_HEADS * NSA_DK
    for p in range(nq // LANES):
        sl = slice(p * LANES, (p + 1) * LANES)
        q_out[:, sl] = (_pair_norm(proj[:, sl], qg_ref[...], lane_lo) * NSA_Q_SCALE).astype(BF16)
    part = lambda i: proj[:, nq + i * LANES:nq + (i + 1) * LANES]
    kc_out[...] = part(0)
    vc_out[...] = part(1)

    def lo_hi(t, lo_pad, hi_pad):
        r = pltpu.roll(t, NSA_DK, 1)
        return jnp.concatenate([jnp.where(lane_lo, t, lo_pad), jnp.where(lane_lo, hi_pad, r),
                                jnp.where(lane_lo, r, lo_pad), jnp.where(lane_lo, hi_pad, t)], axis=1).astype(BF16)

    prel = prel_ref[...]
    byte = lambda k: ((prel >> (8 * k)) & 255).astype(F32)
    half_lane = (lane & (NSA_DK - 1)) - AUG_POS
    kaug = blocks_ref[...] + jnp.where((half_lane == 0) | (half_lane == 3), byte(2),
                                       jnp.where((half_lane == 1) | (half_lane == 4), byte(1),
                                                 jnp.where((half_lane == 2) | (half_lane == 5), byte(0), 0.0)))
    ks_out[...] = lo_hi(_pair_norm(part(2), ksg_ref[...], lane_lo), kaug, kaug)
    kw_out[...] = lo_hi(_pair_norm(part(3), kwg_ref[...], lane_lo), kaug, kaug)
    gate_out[...] = jax.nn.sigmoid(part(4))

    vt = _dot_nt(wvt_ref[...], hb)
    tm = vt.shape[1]
    sub = lax.broadcasted_iota(I32, (NSA_DK, 1), 0)
    pad_lo = jnp.broadcast_to(jnp.where(sub == LO_ONE - NSA_DK, 1.0, 0.0), (NSA_DK, tm))
    pad_hi = jnp.broadcast_to(jnp.where(sub == HI_ONE, 1.0, 0.0), (NSA_DK, tm))
    for branch, out in enumerate((vs_out, vw_out)):
        pieces = []
        for grp in range(NSA_G):
            v = vt[(2 * branch + grp) * NSA_DK:(2 * branch + grp + 1) * NSA_DK, :]
            pieces += [v, pad_lo, pad_hi, v]
        out[...] = jnp.concatenate(pieces, axis=0).astype(BF16)


def _inproj1(x2d, positions, gain, w_in, q_norm, k_norm_slc, k_norm_win, tm=256):
    T, D = x2d.shape
    S = positions.shape[1]
    prel = (positions - positions[:, :1]).reshape(T, 1)
    own_block = np.arange(S)[:, None] // SLC_BLOCK == (np.arange(LANES)[None, :] % NSA_DK)
    blocks = jnp.asarray(own_block * MASK_BIG, F32)
    nq_, kvw = NSA_HEADS * NSA_DK, NSA_KVW
    keep = [w_in[:, :nq_ + 3 * kvw], w_in[:, nq_ + 4 * kvw:nq_ + 5 * kvw], w_in[:, nq_ + 6 * kvw:]]
    w_in_p = jnp.concatenate(keep, axis=1)
    n_in = w_in_p.shape[1]
    n_pad = -n_in % LANES
    w_in_p = jnp.pad(w_in_p, ((0, 0), (0, n_pad))).astype(BF16)
    two = lambda g: jnp.tile(g, 2).reshape(1, LANES)
    row = lambda n: pl.BlockSpec((tm, n), lambda i: (i, 0))
    nq = NSA_HEADS * NSA_DK
    vcols = lambda i: w_in[:, nq + i * NSA_KVW:nq + (i + 1) * NSA_KVW]
    w_vt = jnp.concatenate([vcols(3), vcols(5)], axis=1).T.astype(BF16)
    bf = lambda n: jax.ShapeDtypeStruct((T, n), BF16)
    f32 = lambda n: jax.ShapeDtypeStruct((T, n), F32)
    col = pl.BlockSpec((4 * LANES, tm), lambda i: (0, i))
    vt_shape = jax.ShapeDtypeStruct((4 * LANES, T), BF16)
    return pl.pallas_call(
        _inproj1_kernel,
        grid=(T // tm,),
        in_specs=[row(D), row(1), pl.BlockSpec((tm, LANES), lambda i: (i % (S // tm), 0)), _full((1, D)),
                  _full((D, n_in + n_pad)), _full((2 * NSA_KVW, D)), _full((1, LANES)), _full((1, LANES)),
                  _full((1, LANES))],
        out_specs=[row(nq), row(LANES), row(LANES), row(4 * LANES), col, row(4 * LANES), col, row(LANES)],
        out_shape=[bf(nq), f32(LANES), f32(LANES), bf(4 * LANES), vt_shape, bf(4 * LANES), vt_shape, f32(LANES)],
        compiler_params=_params("parallel"),
        name="inproj1",
    )(x2d, prel, blocks, gain.reshape(1, D), w_in_p, w_vt, two(q_norm), two(k_norm_slc), two(k_norm_win))


def _stack_pairs(q_ref, rows):
    return jnp.concatenate([q_ref[0, rows, p * LANES:(p + 1) * LANES] for p in range(NSA_PAIRS)], axis=0)


def _gate_tile(gt, branch, p, lane_lo):
    c = branch * NSA_HPG + 2 * p
    return jnp.where(lane_lo, gt[:, c:c + 1], gt[:, c + 1:c + 2])


def _cmp_kernel(q_ref, kch_ref, vch_ref, pos3_ref, pend_ref, gate_ref, slope_ref,
                pek_ref, w1k_ref, w2k_ref, kcg_ref, pev_ref, w1v_ref, w2v_ref, ovl_ref, eye_ref,
                o_ref, sel_ref, kc_s, vc_s):
    tq = NSA_TQ
    S = q_ref.shape[1]

    def compress(ch_ref, pe_ref, w1_ref, w2_ref):
        a = ch_ref[0, 0]
        h_lo = _dot((a + pe_ref[0:1, :]).astype(BF16), w1_ref[0])
        h_hi = _dot((a + pe_ref[1:2, :]).astype(BF16), w1_ref[1])
        n = h_hi.shape[0]
        hid = jax.nn.gelu(h_lo + pltpu.roll(h_hi, n - 1, 0)).astype(BF16)
        return _dot(hid, w2_ref[0]), _dot(hid, w2_ref[1])

    k_lo, k_hi = compress(kch_ref, pek_ref, w1k_ref, w2k_ref)
    for e, kk in enumerate((k_lo, k_hi)):
        kk = kk * lax.rsqrt(jnp.sum(kk * kk, axis=-1, keepdims=True) * (1.0 / NSA_DK) + 1e-6) * kcg_ref[e:e + 1, :]
        kc_s[e] = kk.astype(BF16)
    v_lo, v_hi = compress(vch_ref, pev_ref, w1v_ref, w2v_ref)
    vc_s[0] = v_lo.astype(BF16)
    vc_s[1] = v_hi.astype(BF16)

    n_cmp = (S - CMP_BLOCK) // CMP_STRIDE + 1
    lane = lax.broadcasted_iota(I32, (1, LANES), 1)
    lane_lo = lane < NSA_DK
    blk_row = lax.broadcasted_iota(I32, (N_SEL, 1), 0)

    def q_block(qi, carry):
        t0 = pl.multiple_of(qi * tq, tq)
        rows = pl.ds(t0, tq)
        qs = _stack_pairs(q_ref, rows)
        pos_q0 = pos3_ref[0, pl.ds(qi, 1), :][:, 0:1]
        posrel = (pend_ref[0] - pos_q0).astype(F32)
        tcol = t0 + lax.broadcasted_iota(I32, (tq, 1), 0)
        valid = ((tcol >= CMP_STRIDE * lane + (CMP_BLOCK - 1)) & (lane < n_cmp))[None]
        psum = jnp.zeros((tq, LANES), F32)
        o_pairs = jnp.zeros((NSA_PAIRS * tq, LANES), F32)
        for e in range(2):
            s = _dot_nt(qs, kc_s[e]).reshape(NSA_PAIRS, tq, LANES)
            s = s + slope_ref[0, NSA_PAIRS * e:NSA_PAIRS * (e + 1)][:, None, :] * posrel[None]
            s = jnp.where(valid, s, NEG_INF)
            m = jnp.max(s, axis=-1, keepdims=True)
            p = jnp.where(valid, jnp.exp2(s - m), 0.0)
            p = p / jnp.maximum(jnp.sum(p, axis=-1, keepdims=True), 1e-20)
            psum = psum + jnp.sum(p, axis=0)
            o_pairs = o_pairs + _dot(p.reshape(NSA_PAIRS * tq, LANES).astype(BF16), vc_s[e])
        imp = _dot_nt(ovl_ref[...], psum, precision=lax.Precision.HIGHEST)
        cur = (t0 + lax.broadcasted_iota(I32, (1, tq), 1)) // SLC_BLOCK
        forced = (blk_row == 0) | (blk_row == cur) | (blk_row == cur - 1)
        imp = jnp.where(forced, FORCE_SCORE, jnp.where(blk_row <= cur, imp, -1.0))
        rank = jnp.zeros((N_SEL, tq), I32)
        for i in range(N_SEL):
            ri = imp[i:i + 1, :]
            rank = rank + jnp.where((ri > imp) | ((ri == imp) & (blk_row > i)), 1, 0)
        sel_t = jnp.where(rank < SLC_TOP_N, 1.0, 0.0).astype(BF16)
        gap = jnp.zeros((NSA_DK - N_SEL, tq), BF16)
        sel_t = jnp.concatenate([sel_t, gap, sel_t, gap], axis=0)
        sel = _dot_nt(eye_ref[...], sel_t)
        sel_ref[0, 0, rows, :] = jnp.where((lane & (NSA_DK - 1)) < N_SEL, sel - 1.0, 0.0).astype(BF16)
        gt = gate_ref[0, 0, rows, :]
        o3 = o_pairs.reshape(NSA_PAIRS, tq, LANES)
        for p in range(NSA_PAIRS):
            o_ref[0, rows, p * LANES:(p + 1) * LANES] = (o3[p] * _gate_tile(gt, 0, p, lane_lo)).astype(BF16)
        return carry

    lax.fori_loop(0, S // tq, q_block, 0)


def _lo_hi_cols(w):
    z = jnp.zeros_like(w)
    return jnp.stack([jnp.concatenate([w, z], axis=1), jnp.concatenate([z, w], axis=1)])


def _nsa_cmp(q, kch, vch, pos3, pend, gates_g, slopes_t, pe_k, w1_k, w2_k, k_norm_cmp, pe_v, w1_v, w2_v):
    B, S, _ = q.shape
    G = NSA_G
    nch = S // CMP_STRIDE
    half = CMP_STRIDE * NSA_DK
    pe2 = lambda pe: pe.reshape(2, half)
    w1_2 = lambda w: w.reshape(2, half, CMP_HIDDEN).astype(BF16)
    n_sel = S // SLC_BLOCK
    n_cmp = (S - CMP_BLOCK) // CMP_STRIDE + 1
    cmp_start = np.arange(LANES) * CMP_STRIDE
    slc_start = np.arange(n_sel) * SLC_BLOCK
    overlap = ((cmp_start[None, :] < slc_start[:, None] + SLC_BLOCK) & (cmp_start[None, :] + CMP_BLOCK > slc_start[:, None])
               & (np.arange(LANES)[None, :] < n_cmp))
    ovl = jnp.asarray(overlap, F32)
    eye = jnp.eye(NSA_TQ, dtype=BF16)
    qspec = pl.BlockSpec((1, S, NSA_PAIRS * LANES), lambda b, g: (b, 0, g))
    chspec = pl.BlockSpec((1, 1, nch, half), lambda b, g: (b, g, 0, 0))
    return pl.pallas_call(
        _cmp_kernel,
        grid=(B, G),
        in_specs=[qspec, chspec, chspec,
                  pl.BlockSpec((1, S // LANES, LANES), lambda b, g: (b, 0, 0)),
                  pl.BlockSpec((1, 1, LANES), lambda b, g: (b, 0, 0)),
                  pl.BlockSpec((1, 1, S, 3 * NSA_HPG), lambda b, g: (b, g, 0, 0)),
                  pl.BlockSpec((1, NSA_HPG, LANES), lambda b, g: (g, 0, 0)),
                  _full((2, half)), _full((2, half, CMP_HIDDEN)), _full((2, CMP_HIDDEN, LANES)), _full((2, LANES)),
                  _full((2, half)), _full((2, half, CMP_HIDDEN)), _full((2, CMP_HIDDEN, LANES)),
                  _full((n_sel, LANES)), _full((NSA_TQ, NSA_TQ))],
        out_specs=[qspec, pl.BlockSpec((1, 1, S, LANES), lambda b, g: (b, g, 0, 0))],
        out_shape=[jax.ShapeDtypeStruct((B, S, NSA_HEADS * NSA_DK), BF16),
                   jax.ShapeDtypeStruct((B, G, S, LANES), BF16)],
        scratch_shapes=[pltpu.VMEM((2, nch, LANES), BF16), pltpu.VMEM((2, nch, LANES), BF16)],
        compiler_params=_params("parallel", "parallel"),
        name="nsa_cmp",
    )(q, kch, vch, pos3, pend, gates_g, slopes_t, pe2(pe_k), w1_2(w1_k), _lo_hi_cols(w2_k).astype(BF16),
      _lo_hi_cols(k_norm_cmp.reshape(1, -1)).reshape(2, LANES), pe2(pe_v), w1_2(w1_v),
      _lo_hi_cols(w2_v).astype(BF16), ovl, eye)


def _slc_win_kernel(q_ref, ks_ref, vs_ref, kw_ref, vw_ref, sel_ref, gate_ref, slopeq_ref, ocmp_ref, o_ref):
    tq = SLC_TQ
    S = q_ref.shape[1]
    lane = lax.broadcasted_iota(I32, (1, LANES), 1)
    lane_lo = lane < NSA_DK
    sub_lo = lax.broadcasted_iota(I32, (LANES, 1), 0) < NSA_DK
    ones_row = (LO_ONE, HI_ONE)

    def q_block(qi, carry):
        t0 = pl.multiple_of(qi * tq, tq)
        rows = pl.ds(t0, tq)
        qs = _stack_pairs(q_ref, rows)
        selm1 = sel_ref[0, 0, rows, :]

        def q_aug(e, with_sel):
            feats = [slopeq_ref[0, NSA_PAIRS * e + p:NSA_PAIRS * e + p + 1, :] for p in range(NSA_PAIRS)]
            if with_sel:
                extra = jnp.concatenate([selm1 + f for f in feats], axis=0)
            else:
                extra = jnp.concatenate([jnp.broadcast_to(f, (tq, LANES)) for f in feats], axis=0)
            return jnp.where(lane_lo, qs, extra) if e == 0 else jnp.where(lane_lo, extra, qs)

        def k_aug(k_ref, e, r0, n):
            return k_ref[0, pl.ds(r0, n), e * LANES:(e + 1) * LANES]

        gt = gate_ref[0, 0, :, rows]
        tq_lane = t0 + (lax.broadcasted_iota(I32, (1, NSA_PAIRS * tq), 1) & (tq - 1))

        def gated(accs, branch):
            outs = []
            for e, acc in enumerate(accs):
                gate = jnp.concatenate([gt[branch * NSA_HPG + 2 * p + e:branch * NSA_HPG + 2 * p + e + 1, :]
                                        for p in range(NSA_PAIRS)], axis=1)
                outs.append(acc * (gate / acc[ones_row[e]:ones_row[e] + 1, :]))
            return jnp.where(sub_lo, outs[0], outs[1])

        qa = [q_aug(0, True), q_aug(1, True)]
        krow = lax.broadcasted_iota(I32, (SLC_TK, 1), 0)

        def slc_step(j, state, last):
            r0 = pl.multiple_of(j * SLC_TK, SLC_TK)
            ss = [_dot_nt(k_aug(ks_ref, e, r0, SLC_TK), qa[e]) for e in range(2)]
            out = []
            for e in range(2):
                m, acc = state[e]
                s = jnp.where(r0 + krow <= tq_lane, ss[e], NEG_INF) if last else ss[e]
                m_new = jnp.maximum(m, jnp.max(s, axis=0, keepdims=True))
                p = jnp.exp2(s - m_new).astype(BF16)
                pv = _dot(vs_ref[e * LANES:(e + 1) * LANES, pl.ds(r0, SLC_TK)], p)
                out.append((m_new, jnp.exp2(m - m_new) * acc + pv))
            return tuple(out)

        init = (jnp.full((1, NSA_PAIRS * tq), NEG_INF, F32), jnp.zeros((LANES, NSA_PAIRS * tq), F32))
        n_full = qi // (SLC_TK // tq)
        state = lax.fori_loop(0, n_full, functools.partial(slc_step, last=False), (init, init))
        state = slc_step(n_full, state, True)
        z = gated([state[0][1], state[1][1]], 1)

        w0 = pl.multiple_of(jnp.maximum(t0 - WINDOW, 0), tq)
        rel = tq_lane - (w0 + lax.broadcasted_iota(I32, (WIN_SPAN, 1), 0))
        wmask = (rel >= 0) & (rel < WINDOW)
        ss = [_dot_nt(k_aug(kw_ref, e, w0, WIN_SPAN), q_aug(e, False)) for e in range(2)]
        accs = []
        for e in range(2):
            s = jnp.where(wmask, ss[e], NEG_INF)
            p = jnp.exp2(s - jnp.max(s, axis=0, keepdims=True)).astype(BF16)
            accs.append(_dot(vw_ref[e * LANES:(e + 1) * LANES, pl.ds(w0, WIN_SPAN)], p))
        z = z + gated(accs, 2)

        for p in range(NSA_PAIRS):
            sl = slice(p * LANES, (p + 1) * LANES)
            o_ref[0, rows, sl] = (ocmp_ref[0, rows, sl].astype(F32) + z[:, p * tq:(p + 1) * tq].T).astype(BF16)
        return carry

    lax.fori_loop(0, S // tq, q_block, 0)


def _slope_aug(slopes_eo):
    hi = slopes_eo.astype(BF16).astype(F32)
    lo = (slopes_eo - hi).astype(BF16).astype(F32)
    w = jnp.asarray([65536.0, 256.0, 1.0], F32)
    feats = jnp.concatenate([hi[..., None] * w, lo[..., None] * w], axis=-1)
    half = jnp.pad(feats, ((0, 0), (0, 0), (AUG_POS, NSA_DK - AUG_POS - 6)))
    return jnp.concatenate([half, half], axis=-1).astype(BF16)


def _nsa_slc_win(q, ks4, vst4, kw4, vwt4, sel, gates_t, slopeq, o_cmp):
    B, S, _ = q.shape
    qspec = pl.BlockSpec((1, S, NSA_PAIRS * LANES), lambda b, g: (b, 0, g))
    kvspec = pl.BlockSpec((1, S, 2 * LANES), lambda b, g: (b, 0, g))
    vtspec = pl.BlockSpec((2 * LANES, S), lambda b, g: (g, b))
    return pl.pallas_call(
        _slc_win_kernel,
        grid=(B, NSA_G),
        in_specs=[qspec, kvspec, vtspec, kvspec, vtspec,
                  pl.BlockSpec((1, 1, S, LANES), lambda b, g: (b, g, 0, 0)),
                  pl.BlockSpec((1, 1, 3 * NSA_HPG, S), lambda b, g: (b, g, 0, 0)),
                  pl.BlockSpec((1, NSA_HPG, LANES), lambda b, g: (g, 0, 0)),
                  qspec],
        out_specs=qspec,
        out_shape=jax.ShapeDtypeStruct((B, S, NSA_HEADS * NSA_DK), BF16),
        compiler_params=_params("parallel", "parallel"),
        name="nsa_slc_win",
    )(q, ks4, vst4, kw4, vwt4, sel, gates_t, slopeq, o_cmp)


def _outproj_kernel(o_ref, x_ref, w_ref, out_ref):
    out_ref[...] = x_ref[...] + _dot(o_ref[...], w_ref[...])


def _outproj(o2d, x2d, w_out, tm=512):
    T, D = x2d.shape
    K = o2d.shape[1]
    return pl.pallas_call(
        _outproj_kernel,
        grid=(T // tm,),
        in_specs=[pl.BlockSpec((tm, K), lambda i: (i, 0)), pl.BlockSpec((tm, D), lambda i: (i, 0)), _full((K, D))],
        out_specs=pl.BlockSpec((tm, D), lambda i: (i, 0)),
        out_shape=jax.ShapeDtypeStruct((T, D), F32),
        compiler_params=_params("parallel"),
        name="outproj1",
    )(o2d, x2d, w_out.astype(BF16))


def _mixer1(x, positions, gain, w_in, q_norm, k_norm_cmp, k_norm_slc, k_norm_win,
            pe_k, w1_k, w2_k, pe_v, w1_v, w2_v, w_out):
    B, S, D = x.shape
    T = B * S
    G, HPG, DK = NSA_G, NSA_HPG, NSA_DK
    assert S // SLC_BLOCK == N_SEL and S % NSA_TQ == 0 and NSA_TQ == LANES
    x2d = x.reshape(T, D)
    q, kc, vc, ks4, vs4, kw4, vw4, gates = _inproj1(x2d, positions, gain, w_in, q_norm, k_norm_slc,
                                                    k_norm_win)
    b3 = lambda t: t.reshape(B, S, -1)
    chunks = lambda t: t.reshape(B, S, G, DK).transpose(0, 2, 1, 3).reshape(B, G, S // CMP_STRIDE, CMP_STRIDE * DK)
    gates_g = gates[:, :3 * NSA_HEADS].reshape(B, S, 3, G, HPG).transpose(0, 3, 1, 2, 4).reshape(B, G, S, 3 * HPG)
    n_cmp = (S - CMP_BLOCK) // CMP_STRIDE + 1
    pend = jnp.pad(positions[:, CMP_BLOCK - 1::CMP_STRIDE][:, :n_cmp], ((0, 0), (0, LANES - n_cmp))).reshape(B, 1, LANES)
    pos3 = positions.reshape(B, S // LANES, LANES)
    slopes = jnp.asarray(2.0 ** (-8.0 * np.arange(1, NSA_HEADS + 1) / NSA_HEADS), F32) * LOG2E
    slopes_eo = slopes.reshape(G, NSA_PAIRS, 2).transpose(0, 2, 1).reshape(G, HPG)
    slopes_t = jnp.broadcast_to(slopes_eo[:, :, None], (G, HPG, LANES))
    o_cmp, sel = _nsa_cmp(b3(q), chunks(kc), chunks(vc), pos3, pend, gates_g, slopes_t,
                          pe_k, w1_k, w2_k, k_norm_cmp, pe_v, w1_v, w2_v)
    o = _nsa_slc_win(b3(q), b3(ks4), vs4, b3(kw4), vw4, sel, gates_g.transpose(0, 1, 3, 2), _slope_aug(slopes_eo),
                     o_cmp)
    return _outproj(o.reshape(T, -1), x2d, w_out).reshape(B, S, D)


def kernel(x, positions, norm_mix, norm_ffn, mix0_w_in, mla_q_a_norm, mla_w_uq, mla_kv_a_norm, mla_w_ukv, mla_q_norm, mla_k_norm, conv_dw_w, conv_dw_b, conv_ln_g, conv_ln_b, mix0_w_out, nsa_w_in, nsa_q_norm, nsa_k_norm_cmp, nsa_k_norm_slc, nsa_k_norm_win, nsa_cmp_pe_k, nsa_cmp_w1_k, nsa_cmp_w2_k, nsa_cmp_pe_v, nsa_cmp_w1_v, nsa_cmp_w2_v, nsa_w_out, moe_router_group, moe_router_group_b, moe_router_expert, moe_router_expert_b, moe_w_gate, moe_w_up, moe_w_down):
    def moe(x, layer):
        return _moe(x, norm_ffn[layer], moe_router_group[layer], moe_router_group_b[layer], moe_router_expert[layer],
                    moe_router_expert_b[layer], moe_w_gate[layer], moe_w_up[layer], moe_w_down[layer])

    x = _mixer0(x, positions, norm_mix[0], mix0_w_in[0], mla_q_a_norm[0], mla_w_uq[0], mla_kv_a_norm[0], mla_w_ukv[0],
                mla_q_norm[0], mla_k_norm[0], conv_dw_w[0], conv_dw_b[0], conv_ln_g[0], conv_ln_b[0], mix0_w_out[0])
    x = moe(x, 0)
    x = _mixer1(x, positions, norm_mix[1], nsa_w_in[0], nsa_q_norm[0], nsa_k_norm_cmp[0], nsa_k_norm_slc[0],
                nsa_k_norm_win[0], nsa_cmp_pe_k[0], nsa_cmp_w1_k[0], nsa_cmp_w2_k[0], nsa_cmp_pe_v[0],
                nsa_cmp_w1_v[0], nsa_cmp_w2_v[0], nsa_w_out[0])
    return moe(x, 1)
```

```python
import functools

import numpy as np
import jax
import jax.numpy as jnp
from jax import lax
from jax.experimental import pallas as pl
from jax.experimental.pallas import tpu as pltpu

F32 = jnp.float32
BF16 = jnp.bfloat16
I32 = jnp.int32

LANES = 128
NEG_INF = -1e30
LOG2E = 1.4426950408889634
Q_BLOCK = 128
MLA_HEADS = 8
MLA_Q_LORA = 256
MLA_KV_LORA = 128
MLA_NOPE = 64
MLA_ROPE = 32
MLA_V = 64
MLA_QK = MLA_NOPE + MLA_ROPE
ROPE_THETA = 10000.0
CONV_CH = 512
CONV_W = 31
NSA_HEADS = 16
NSA_G = 2
NSA_HPG = NSA_HEADS // NSA_G
NSA_DK = 64
NSA_KVW = NSA_G * NSA_DK
CMP_BLOCK = 32
CMP_STRIDE = 16
CMP_HIDDEN = 128
SLC_BLOCK = 64
SLC_TOP_N = 8
WINDOW = 256
FORCE_SCORE = 1e4
MOE_GROUPS = 4
MOE_EPG = 8
MOE_EXPERTS = MOE_GROUPS * MOE_EPG
MOE_HIDDEN = 256
MOE_ROW_BLOCK = 256

VMEM_LIMIT = 56 * 1024 * 1024


def _params(*sem):
    return pltpu.CompilerParams(dimension_semantics=sem, vmem_limit_bytes=VMEM_LIMIT)


def _full(shape):
    n = len(shape)
    return pl.BlockSpec(shape, lambda *_: (0,) * n)


def _row_mean(x):
    n = x.shape[-1]
    folded = functools.reduce(jnp.add, [x[:, c:c + LANES] for c in range(0, n, LANES)])
    return jnp.sum(folded, axis=-1, keepdims=True) * (1.0 / n)


def _rms(x, eps=1e-6):
    return x * lax.rsqrt(_row_mean(x * x) + eps)


def _dot(a, b):
    return jnp.dot(a, b, preferred_element_type=F32)


def _group_matrix(groups):
    width = LANES // groups
    same = (lax.broadcasted_iota(I32, (LANES, LANES), 0) // width) == (lax.broadcasted_iota(I32, (LANES, LANES), 1) // width)
    return jnp.where(same, 1.0, 0.0).astype(BF16)


def _group_sums(t, group_matrix):
    hi = t.astype(BF16)
    lo = (t - hi.astype(F32)).astype(BF16)
    return _dot(hi, group_matrix) + _dot(lo, group_matrix)


def _dot_nt(a, b, **kw):
    return lax.dot_general(a, b, (((1,), (1,)), ((), ())), preferred_element_type=F32, **kw)


def _inproj0_kernel(x_ref, pos_ref, gmix_ref, win_ref, qan_ref, wuq_ref, kvan_ref, wuk_ref, wuv_ref,
                    qg_ref, kg_ref, invf_ref, sgn_ref, q_out, k_out, v_out, u_out):
    h = _rms(x_ref[...]) * gmix_ref[...]
    proj = _dot(h.astype(BF16), win_ref[...])
    c_q = proj[:, :MLA_Q_LORA]
    c_kv = proj[:, MLA_Q_LORA:MLA_Q_LORA + MLA_KV_LORA]
    k_rope = proj[:, 384:512]
    a = proj[:, 512:512 + CONV_CH]
    g = proj[:, 512 + CONV_CH:]
    u_out[...] = a * jax.nn.sigmoid(g)
    cqn = (_rms(c_q) * qan_ref[...]).astype(BF16)
    ckvn = (_rms(c_kv) * kvan_ref[...]).astype(BF16)
    q = _dot(cqn, wuq_ref[...])
    kn = _dot(ckvn, wuk_ref[...])
    slot_row = lax.broadcasted_iota(I32, (MLA_HEADS * LANES, 1), 0) & (LANES - 1)
    v_out[...] = (_dot_nt(wuv_ref[...], ckvn) + jnp.where(slot_row == MLA_V, 1.0, 0.0)).astype(BF16)
    ang = pos_ref[...].astype(F32) * invf_ref[...]
    cos = jnp.cos(ang)
    sin = jnp.sin(ang) * sgn_ref[...]
    lane = lax.broadcasted_iota(I32, (1, LANES), 1)
    first_half = (lane >= MLA_NOPE) & (lane < MLA_NOPE + MLA_ROPE // 2)
    scale = MLA_QK ** -0.5 * LOG2E

    all_lanes = _group_matrix(1)

    def norm_rope(t, gain):
        t = t * lax.rsqrt(_group_sums(t * t, all_lanes) * (1.0 / MLA_QK) + 1e-6) * gain
        partner = jnp.where(first_half, pltpu.roll(t, LANES - MLA_ROPE // 2, 1), pltpu.roll(t, MLA_ROPE // 2, 1))
        return t * cos + partner * sin

    for hd in range(MLA_HEADS):
        sl = slice(hd * LANES, (hd + 1) * LANES)
        q_out[:, sl] = (norm_rope(q[:, sl], qg_ref[...]) * scale).astype(BF16)
        k_out[:, sl] = norm_rope(kn[:, sl] + k_rope, kg_ref[...]).astype(BF16)


def _head_slots(w, n_heads, width, offset=0):
    k = w.shape[0]
    w = w.reshape(k, n_heads, width)
    w = jnp.pad(w, ((0, 0), (0, 0), (offset, LANES - width - offset)))
    return w.reshape(k, n_heads * LANES)


def _inproj0(x2d, pos_col, gmix, w_in, q_a_norm, w_uq, kv_a_norm, w_ukv, q_norm, k_norm, tm=512):
    T, D = x2d.shape
    H = MLA_HEADS
    w_krope = jnp.pad(w_in[:, 384:416], ((0, 0), (MLA_NOPE, LANES - MLA_NOPE - MLA_ROPE)))
    w_in_p = jnp.concatenate([w_in[:, :384], w_krope, w_in[:, 416:]], axis=1).astype(BF16)
    w_uq_p = _head_slots(w_uq, H, MLA_QK).astype(BF16)
    w_ukv3 = w_ukv.reshape(MLA_KV_LORA, H, MLA_NOPE + MLA_V)
    w_uk_p = _head_slots(w_ukv3[:, :, :MLA_NOPE].reshape(MLA_KV_LORA, H * MLA_NOPE), H, MLA_NOPE).astype(BF16)
    w_uv = _head_slots(w_ukv3[:, :, MLA_NOPE:].reshape(MLA_KV_LORA, H * MLA_V), H, MLA_V).T.astype(BF16)
    pad = LANES - MLA_QK
    qg = jnp.pad(q_norm, (0, pad)).reshape(1, LANES)
    kg = jnp.pad(k_norm, (0, pad)).reshape(1, LANES)
    half = MLA_ROPE // 2
    inv_freq = ROPE_THETA ** (-jnp.arange(half, dtype=F32) / half)
    invf = jnp.zeros((LANES,), F32).at[MLA_NOPE:MLA_NOPE + half].set(inv_freq).at[MLA_NOPE + half:MLA_QK].set(inv_freq)
    sgn = jnp.zeros((LANES,), F32).at[MLA_NOPE:MLA_NOPE + half].set(-1.0).at[MLA_NOPE + half:MLA_QK].set(1.0)
    row = lambda n: pl.BlockSpec((tm, n), lambda i: (i, 0))
    n_in = w_in_p.shape[1]
    return pl.pallas_call(
        _inproj0_kernel,
        grid=(T // tm,),
        in_specs=[row(D), row(1), _full((1, D)), _full((D, n_in)), _full((1, MLA_Q_LORA)),
                  _full((MLA_Q_LORA, H * LANES)), _full((1, MLA_KV_LORA)), _full((MLA_KV_LORA, H * LANES)),
                  _full((H * LANES, MLA_KV_LORA)), _full((1, LANES)), _full((1, LANES)), _full((1, LANES)),
                  _full((1, LANES))],
        out_specs=[row(H * LANES), row(H * LANES), pl.BlockSpec((H * LANES, tm), lambda i: (0, i)), row(CONV_CH)],
        out_shape=[jax.ShapeDtypeStruct((T, H * LANES), BF16), jax.ShapeDtypeStruct((T, H * LANES), BF16),
                   jax.ShapeDtypeStruct((H * LANES, T), BF16), jax.ShapeDtypeStruct((T, CONV_CH), F32)],
        compiler_params=_params("parallel"),
        name="inproj0",
    )(x2d, pos_col, gmix.reshape(1, D), w_in_p, q_a_norm.reshape(1, -1), w_uq_p, kv_a_norm.reshape(1, -1),
      w_uk_p, w_uv, qg, kg, invf.reshape(1, LANES), sgn.reshape(1, LANES))


MLA_TQ = 512
MLA_TK = 512
MLA_ONE = MLA_V


def _mla_attn_kernel(q_ref, k_ref, vt_ref, o_ref):
    S = q_ref.shape[1]
    tq, tk = MLA_TQ, MLA_TK
    krow = lax.broadcasted_iota(I32, (tk, 1), 0)
    qcol = lax.broadcasted_iota(I32, (1, tq), 1)

    def q_block(qi, carry):
        q0 = pl.multiple_of(qi * tq, tq)
        qs = [q_ref[0, pl.ds(q0, tq), hh * LANES:(hh + 1) * LANES] for hh in range(2)]

        def kv_step(j, state, masked):
            k0 = pl.multiple_of(j * tk, tk)
            hs = [slice(hh * LANES, (hh + 1) * LANES) for hh in range(2)]
            ss = [_dot_nt(k_ref[0, pl.ds(k0, tk), hs[hh]], qs[hh]) for hh in range(2)]
            out = []
            for hh in range(2):
                m, acc = state[hh]
                s = jnp.where(k0 + krow <= q0 + qcol, ss[hh], NEG_INF) if masked else ss[hh]
                m_new = jnp.maximum(m, jnp.max(s, axis=0, keepdims=True))
                p = jnp.exp2(s - m_new).astype(BF16)
                out.append((m_new, jnp.exp2(m - m_new) * acc + _dot(vt_ref[hs[hh], pl.ds(k0, tk)], p)))
            return tuple(out)

        init = (jnp.full((1, tq), NEG_INF, F32), jnp.zeros((LANES, tq), F32))
        n_full = (qi * tq) // tk
        state = lax.fori_loop(0, n_full, functools.partial(kv_step, masked=False), (init, init))
        state = kv_step(n_full, state, True)
        outs = []
        for hh in range(2):
            acc = state[hh][1]
            outs.append((acc * (1.0 / acc[MLA_ONE:MLA_ONE + 1, :]))[:MLA_V, :])
        o_ref[0, pl.ds(q0, tq), :] = jnp.concatenate(outs, axis=0).T.astype(BF16)
        return carry

    lax.fori_loop(0, S // tq, q_block, 0)


def _mla_attn(q, k, vt):
    B, S, _ = q.shape
    spec = pl.BlockSpec((1, S, 2 * LANES), lambda b, h: (b, 0, h))
    return pl.pallas_call(
        _mla_attn_kernel,
        grid=(B, MLA_HEADS // 2),
        in_specs=[spec, spec, pl.BlockSpec((2 * LANES, S), lambda b, h: (h, b))],
        out_specs=pl.BlockSpec((1, S, 2 * MLA_V), lambda b, h: (b, 0, h)),
        out_shape=jax.ShapeDtypeStruct((B, S, MLA_HEADS * MLA_V), BF16),
        compiler_params=_params("parallel", "parallel"),
        name="mla_attn",
    )(q, k, vt)


CONV_TILE = 64
CONV_PAD = 32


CONV_ROWS = 512


def _conv_out0_kernel(u_ref, o_ref, x_ref, dww_ref, dwb_ref, lng_ref, lnb_ref, wo_ref, out_ref, upad, act, shifted):
    step = pl.program_id(1)

    @pl.when(step == 0)
    def _():
        upad[0:CONV_PAD, :] = jnp.zeros((CONV_PAD, CONV_CH), F32)
        upad[CONV_PAD:, :] = u_ref[0]

    lead = CONV_PAD - (CONV_W - 1)
    base = step * CONV_ROWS

    def tile(i, carry):
        t0 = pl.multiple_of(i * CONV_TILE, CONV_TILE)
        win = upad[pl.ds(pl.multiple_of(base + t0, CONV_TILE), CONV_TILE + CONV_PAD), :]
        acc = jnp.zeros((CONV_TILE, CONV_CH), F32) + dwb_ref[...]
        span = CONV_TILE + CONV_PAD - 8
        for r in range(1, 8):
            shifted[r - 1] = win[r:r + span, :]
        for j in range(CONV_W):
            r, k = (lead + j) % 8, (lead + j) // 8 * 8
            tap = win[k:k + CONV_TILE, :] if r == 0 else shifted[r - 1, k:k + CONV_TILE, :]
            acc = acc + tap * dww_ref[j:j + 1, :]
        xc = acc - _row_mean(acc)
        var = _row_mean(xc * xc)
        y = xc * lax.rsqrt(var + 1e-5) * lng_ref[...] + lnb_ref[...]
        act[pl.ds(t0, CONV_TILE), :] = (y * jax.nn.sigmoid(y)).astype(BF16)
        return carry

    lax.fori_loop(0, CONV_ROWS // CONV_TILE, tile, 0)
    n_o = o_ref.shape[2]
    y = _dot(o_ref[0], wo_ref[:n_o, :]) + _dot(act[...], wo_ref[n_o:, :])
    out_ref[0] = x_ref[0] + y


def _conv_out0(u, o_mla, x, dw_w, dw_b, ln_g, ln_b, w_out):
    B, S, D = x.shape
    n_o = o_mla.shape[2]
    w_out_p = w_out.astype(BF16)
    tspec = lambda n: pl.BlockSpec((1, CONV_ROWS, n), lambda b, t: (b, t, 0))
    return pl.pallas_call(
        _conv_out0_kernel,
        grid=(B, S // CONV_ROWS),
        in_specs=[pl.BlockSpec((1, S, CONV_CH), lambda b, t: (b, 0, 0)), tspec(n_o), tspec(D),
                  _full((CONV_W, CONV_CH)), _full((1, CONV_CH)), _full((1, CONV_CH)), _full((1, CONV_CH)),
                  _full((n_o + CONV_CH, D))],
        out_specs=tspec(D),
        out_shape=jax.ShapeDtypeStruct((B, S, D), F32),
        scratch_shapes=[pltpu.VMEM((S + CONV_PAD, CONV_CH), F32), pltpu.VMEM((CONV_ROWS, CONV_CH), BF16),
                        pltpu.VMEM((7, CONV_TILE + CONV_PAD - 8, CONV_CH), F32)],
        compiler_params=_params("parallel", "arbitrary"),
        name="conv_out0",
    )(u, o_mla, x, dw_w, dw_b.reshape(1, -1), ln_g.reshape(1, -1), ln_b.reshape(1, -1), w_out_p)


def _mixer0(x, positions, gmix, w_in, q_a_norm, w_uq, kv_a_norm, w_ukv, q_norm, k_norm,
            dw_w, dw_b, ln_g, ln_b, w_out):
    B, S, D = x.shape
    T = B * S
    q, k, v, u = _inproj0(x.reshape(T, D), positions.reshape(T, 1), gmix, w_in, q_a_norm, w_uq, kv_a_norm, w_ukv,
                          q_norm, k_norm)
    o = _mla_attn(q.reshape(B, S, -1), k.reshape(B, S, -1), v)
    return _conv_out0(u.reshape(B, S, -1), o, x, dw_w, dw_b, ln_g, ln_b, w_out)


ROUTE_TILE = 512
ROUTE_ROWS = 40


def _route_kernel(x_ref, g_ref, wr_ref, br_ref, tri_ref, hn_ref, oi_ref, of_ref, cnt_ref, carry):
    @pl.when(pl.program_id(0) == 0)
    def _():
        carry[...] = jnp.zeros_like(carry)

    hn = _rms(x_ref[...]) * g_ref[...]
    _slab_store(hn_ref, hn)
    tm = hn.shape[0]
    logits = _dot_nt(wr_ref[...], hn, precision=lax.Precision.HIGHEST) + br_ref[...]
    gl = logits[MOE_EXPERTS:MOE_EXPERTS + MOE_GROUPS]
    rid_g = lax.broadcasted_iota(I32, (MOE_GROUPS, tm), 0)
    gmax = jnp.max(gl, axis=0, keepdims=True)
    grp = jnp.min(jnp.where(gl == gmax, rid_g, MOE_GROUPS), axis=0, keepdims=True)
    g_w = 1.0 / jnp.sum(jnp.exp(gl - gmax), axis=0, keepdims=True)
    e_in = jnp.zeros((MOE_EPG, tm), F32)
    for gi in range(MOE_GROUPS):
        e_in = jnp.where(grp == gi, logits[gi * MOE_EPG:(gi + 1) * MOE_EPG], e_in)
    rid_e = lax.broadcasted_iota(I32, (MOE_EPG, tm), 0)
    v1 = jnp.max(e_in, axis=0, keepdims=True)
    i1 = jnp.min(jnp.where(e_in == v1, rid_e, MOE_EPG), axis=0, keepdims=True)
    rest = jnp.where(rid_e == i1, -jnp.inf, e_in)
    v2 = jnp.max(rest, axis=0, keepdims=True)
    i2 = jnp.min(jnp.where(rest == v2, rid_e, MOE_EPG), axis=0, keepdims=True)
    ex = jnp.exp(v2 - v1)
    den = 1.0 + ex
    e1 = grp * MOE_EPG + i1
    e2 = grp * MOE_EPG + i2
    rid = lax.broadcasted_iota(I32, (MOE_EXPERTS, tm), 0)
    hit1 = rid == e1
    hit2 = rid == e2
    member = jnp.where(hit1 | hit2, 1.0, 0.0)
    before = _dot(member.astype(BF16), tri_ref[...]) + carry[...]
    r1 = jnp.sum(jnp.where(hit1, before, 0.0), axis=0, keepdims=True)
    r2 = jnp.sum(jnp.where(hit2, before, 0.0), axis=0, keepdims=True)
    carry[...] = carry[...] + jnp.sum(member, axis=1, keepdims=True)
    oi_ref[...] = jnp.zeros_like(oi_ref)
    oi_ref[0:1, :] = e1
    oi_ref[1:2, :] = e2
    oi_ref[2:3, :] = r1.astype(I32)
    oi_ref[3:4, :] = r2.astype(I32)
    of_ref[...] = jnp.zeros_like(of_ref)
    of_ref[0:1, :] = g_w / den
    of_ref[1:2, :] = g_w * ex / den
    cnt_ref[...] = jnp.broadcast_to(carry[...], cnt_ref.shape).astype(I32)


def _route(x2d, gain, router_group, router_group_b, router_expert, router_expert_b):
    T, D = x2d.shape
    tm = ROUTE_TILE
    pad = ROUTE_ROWS - MOE_EXPERTS - MOE_GROUPS
    wr = jnp.concatenate([router_expert.T, router_group.T, jnp.zeros((pad, D), F32)], axis=0)
    br = jnp.concatenate([router_expert_b, router_group_b, jnp.zeros((pad,), F32)]).reshape(ROUTE_ROWS, 1)
    tri = (jnp.arange(tm)[:, None] < jnp.arange(tm)[None, :]).astype(BF16)
    return pl.pallas_call(
        _route_kernel,
        grid=(T // tm,),
        in_specs=[pl.BlockSpec((tm, D), lambda i: (i, 0)), _full((1, D)), _full((ROUTE_ROWS, D)),
                  _full((ROUTE_ROWS, 1)), _full((tm, tm))],
        out_specs=[pl.BlockSpec((tm * SLAB, LANES), lambda i: (i, 0)), pl.BlockSpec((8, tm), lambda i: (0, i)),
                   pl.BlockSpec((8, tm), lambda i: (0, i)), _full((MOE_EXPERTS, LANES))],
        out_shape=[jax.ShapeDtypeStruct((T * SLAB, LANES), F32), jax.ShapeDtypeStruct((8, T), I32),
                   jax.ShapeDtypeStruct((8, T), F32), jax.ShapeDtypeStruct((MOE_EXPERTS, LANES), I32)],
        scratch_shapes=[pltpu.VMEM((MOE_EXPERTS, 1), F32)],
        compiler_params=_params("arbitrary"),
        name="moe_route",
    )(x2d, gain.reshape(1, D), wr, br, tri)


MOVE_CHUNK = 512
MOVE_UNROLL = 8


SLAB = 8


def _slab_load(ref, n, first=0, stride=SLAB):
    return jnp.concatenate([ref[pl.ds(first + j, n, stride=stride), :] for j in range(SLAB)], axis=1)


def _slab_store(ref, value):
    n = value.shape[0]
    for j in range(SLAB):
        ref[pl.ds(j, n, stride=SLAB), :] = value[:, j * LANES:(j + 1) * LANES]


def _slab(row):
    return pl.ds(pl.multiple_of(row, SLAB), SLAB)


def _for_tokens(n, fn):
    def body(t, carry):
        fn(t, 0)
        fn(t, 1)
        return carry
    lax.fori_loop(0, n, body, 0, unroll=MOVE_UNROLL)


def _dispatch_kernel(dest_ref, hn_ref, xs_init_ref, xs_ref, sem):
    del xs_init_ref
    base = pl.program_id(0) * (2 * MOVE_CHUNK)

    def copy(t, k):
        return pltpu.make_async_copy(hn_ref.at[_slab(t * SLAB)], xs_ref.at[_slab(dest_ref[base + 2 * t + k])], sem)

    _for_tokens(MOVE_CHUNK, lambda t, k: copy(t, k).start())
    _for_tokens(MOVE_CHUNK, lambda t, k: copy(t, k).wait())


def _dispatch(dest, hn, n_rows):
    T = hn.shape[0] // SLAB
    tm = MOVE_CHUNK
    return pl.pallas_call(
        _dispatch_kernel,
        grid_spec=pltpu.PrefetchScalarGridSpec(
            num_scalar_prefetch=1,
            grid=(T // tm,),
            in_specs=[pl.BlockSpec((tm * SLAB, LANES), lambda i, d: (i, 0)), pl.BlockSpec(memory_space=pl.ANY)],
            out_specs=pl.BlockSpec(memory_space=pl.ANY),
            scratch_shapes=[pltpu.SemaphoreType.DMA(())],
        ),
        out_shape=jax.ShapeDtypeStruct((n_rows * SLAB, LANES), F32),
        input_output_aliases={2: 0},
        compiler_params=_params("arbitrary"),
        name="moe_dispatch",
    )(dest, hn, jnp.zeros((n_rows * SLAB, LANES), F32))


def _expert_kernel(blk_e_ref, n_used_ref, xs_ref, wg_ref, wu_ref, wd_ref, ys_ref, wg_s, wu_s, wd_s):
    b = pl.program_id(0)

    @pl.when((b == 0) | (blk_e_ref[b] != blk_e_ref[jnp.maximum(b - 1, 0)]))
    def _():
        wg_s[...] = wg_ref[0].astype(BF16)
        wu_s[...] = wu_ref[0].astype(BF16)
        wd_s[...] = wd_ref[0].astype(BF16)

    @pl.when(b < n_used_ref[0])
    def _():
        x = _slab_load(xs_ref, MOE_ROW_BLOCK).astype(BF16)
        gate = _dot(x, wg_s[...])
        hid = (gate * jax.nn.sigmoid(gate) * _dot(x, wu_s[...])).astype(BF16)
        _slab_store(ys_ref, _dot(hid, wd_s[...]))

    @pl.when(b >= n_used_ref[0])
    def _():
        ys_ref[...] = jnp.zeros_like(ys_ref)


def _experts(blk_expert, n_used, xs, w_gate, w_up, w_down):
    R = xs.shape[0] // SLAB
    rb = MOE_ROW_BLOCK
    D = w_gate.shape[1]
    up_spec = pl.BlockSpec((1, D, MOE_HIDDEN), lambda b, be, nu: (be[b], 0, 0))
    return pl.pallas_call(
        _expert_kernel,
        grid_spec=pltpu.PrefetchScalarGridSpec(
            num_scalar_prefetch=2,
            grid=(R // rb,),
            in_specs=[pl.BlockSpec((rb * SLAB, LANES), lambda b, be, nu: (jnp.minimum(b, nu[0] - 1), 0)),
                      up_spec, up_spec, pl.BlockSpec((1, MOE_HIDDEN, D), lambda b, be, nu: (be[b], 0, 0))],
            out_specs=pl.BlockSpec((rb * SLAB, LANES), lambda b, be, nu: (b, 0)),
            scratch_shapes=[pltpu.VMEM((D, MOE_HIDDEN), BF16), pltpu.VMEM((D, MOE_HIDDEN), BF16),
                            pltpu.VMEM((MOE_HIDDEN, D), BF16)],
        ),
        out_shape=jax.ShapeDtypeStruct((R * SLAB, LANES), F32),
        compiler_params=_params("arbitrary"),
        name="moe_experts",
    )(blk_expert, n_used, xs, w_gate, w_up, w_down)


def _combine_kernel(dest_ref, x_ref, gate_ref, ys_ref, out_ref, buf, sem):
    tm = x_ref.shape[0]
    step = pl.program_id(0)

    def copy(s, t, k):
        a = 2 * t + k
        slot = s % 2
        return pltpu.make_async_copy(ys_ref.at[_slab(dest_ref[s * (2 * tm) + a])], buf.at[slot, _slab(a * SLAB)],
                                     sem.at[slot])

    @pl.when(step == 0)
    def _():
        _for_tokens(tm, lambda t, k: copy(step, t, k).start())

    @pl.when(step + 1 < pl.num_programs(0))
    def _():
        _for_tokens(tm, lambda t, k: copy(step + 1, t, k).start())

    _for_tokens(tm, lambda t, k: copy(step, t, k).wait())
    mine = buf.at[step % 2]
    y0 = _slab_load(mine, tm, 0, 2 * SLAB)
    y1 = _slab_load(mine, tm, SLAB, 2 * SLAB)
    out_ref[...] = x_ref[...] + gate_ref[:, 0:1] * y0 + gate_ref[:, 1:2] * y1


def _combine(dest, x2d, gates_col, ys, tm=MOVE_CHUNK):
    T, D = x2d.shape
    return pl.pallas_call(
        _combine_kernel,
        grid_spec=pltpu.PrefetchScalarGridSpec(
            num_scalar_prefetch=1,
            grid=(T // tm,),
            in_specs=[pl.BlockSpec((tm, D), lambda i, d: (i, 0)), pl.BlockSpec((tm, 2), lambda i, d: (i, 0)),
                      pl.BlockSpec(memory_space=pl.ANY)],
            out_specs=pl.BlockSpec((tm, D), lambda i, d: (i, 0)),
            scratch_shapes=[pltpu.VMEM((2, tm * 2 * SLAB, LANES), F32), pltpu.SemaphoreType.DMA((2,))],
        ),
        out_shape=jax.ShapeDtypeStruct((T, D), F32),
        compiler_params=_params("arbitrary"),
        name="moe_combine",
    )(dest, x2d, gates_col, ys)


def _moe(x, gain, router_group, router_group_b, router_expert, router_expert_b, w_gate, w_up, w_down):
    B, S, D = x.shape
    T = B * S
    x2d = x.reshape(T, D)
    hn, oi, of, cnt = _route(x2d, gain, router_group, router_group_b, router_expert, router_expert_b)
    rb = MOE_ROW_BLOCK
    counts = cnt[:, 0]
    cap = (counts + rb - 1) // rb * rb
    cap_end = jnp.cumsum(cap)
    start = cap_end - cap
    eid = oi[0:2].T
    first = jnp.sum(jnp.where(eid[:, :, None] == jnp.arange(MOE_EXPERTS), start, 0), axis=-1)
    dest = ((first + oi[2:4].T) * SLAB).reshape(2 * T).astype(I32)
    n_rows = 2 * T + MOE_EXPERTS * rb
    n_blk = n_rows // rb
    blk_first_row = jnp.arange(n_blk, dtype=I32) * rb
    blk_expert = jnp.minimum(jnp.sum(cap_end[None, :] <= blk_first_row[:, None], axis=1), MOE_EXPERTS - 1)
    n_used = (cap_end[-1:] // rb).astype(I32)
    xs = _dispatch(dest, hn, n_rows)
    ys = _experts(blk_expert.astype(I32), n_used, xs, w_gate, w_up, w_down)
    out = _combine(dest, x2d, of[0:2].T, ys)
    return out.reshape(B, S, D)


NSA_TQ = 256
SLC_TQ = 256
NSA_PAIRS = NSA_HPG // 2
N_SEL = 32
NSA_Q_SCALE = NSA_DK ** -0.5 * LOG2E
LO_ONE = LANES - 1
HI_ONE = 0
SLC_TK = 512
WIN_SPAN = WINDOW + SLC_TQ
AUG_POS = N_SEL
MASK_BIG = 1e30


def _pair_norm(t, gain2, lane_lo):
    t2 = t * t
    s_lo = jnp.sum(jnp.where(lane_lo, t2, 0.0), axis=-1, keepdims=True)
    s_hi = jnp.sum(jnp.where(lane_lo, 0.0, t2), axis=-1, keepdims=True)
    inv = jnp.where(lane_lo, lax.rsqrt(s_lo * (1.0 / NSA_DK) + 1e-6), lax.rsqrt(s_hi * (1.0 / NSA_DK) + 1e-6))
    return t * inv * gain2


def _inproj1_kernel(x_ref, prel_ref, blocks_ref, g_ref, win_ref, wvt_ref, qg_ref, ksg_ref, kwg_ref,
                    q_out, kc_out, vc_out, ks_out, vs_out, kw_out, vw_out, gate_out):
    h = _rms(x_ref[...]) * g_ref[...]
    hb = h.astype(BF16)
    proj = _dot(hb, win_ref[...])
    lane = lax.broadcasted_iota(I32, (1, LANES), 1)
    lane_lo = lane < NSA_DK
    nq = NSA_HEADS * NSA_DK
    for p in range(nq // LANES):
        sl = slice(p * LANES, (p + 1) * LANES)
        q_out[:, sl] = (_pair_norm(proj[:, sl], qg_ref[...], lane_lo) * NSA_Q_SCALE).astype(BF16)
    part = lambda i: proj[:, nq + i * LANES:nq + (i + 1) * LANES]
    kc_out[...] = part(0)
    vc_out[...] = part(1)

    def lo_hi(t, lo_pad, hi_pad):
        r = pltpu.roll(t, NSA_DK, 1)
        return jnp.concatenate([jnp.where(lane_lo, t, lo_pad), jnp.where(lane_lo, hi_pad, r),
                                jnp.where(lane_lo, r, lo_pad), jnp.where(lane_lo, hi_pad, t)], axis=1).astype(BF16)

    prel = prel_ref[...]
    byte = lambda k: ((prel >> (8 * k)) & 255).astype(F32)
    half_lane = (lane & (NSA_DK - 1)) - AUG_POS
    kaug = blocks_ref[...] + jnp.where((half_lane == 0) | (half_lane == 3), byte(2),
                                       jnp.where((half_lane == 1) | (half_lane == 4), byte(1),
                                                 jnp.where((half_lane == 2) | (half_lane == 5), byte(0), 0.0)))
    ks_out[...] = lo_hi(_pair_norm(part(2), ksg_ref[...], lane_lo), kaug, kaug)
    kw_out[...] = lo_hi(_pair_norm(part(3), kwg_ref[...], lane_lo), kaug, kaug)
    gate_out[...] = jax.nn.sigmoid(part(4))

    vt = _dot_nt(wvt_ref[...], hb)
    tm = vt.shape[1]
    sub = lax.broadcasted_iota(I32, (NSA_DK, 1), 0)
    pad_lo = jnp.broadcast_to(jnp.where(sub == LO_ONE - NSA_DK, 1.0, 0.0), (NSA_DK, tm))
    pad_hi = jnp.broadcast_to(jnp.where(sub == HI_ONE, 1.0, 0.0), (NSA_DK, tm))
    for branch, out in enumerate((vs_out, vw_out)):
        pieces = []
        for grp in range(NSA_G):
            v = vt[(2 * branch + grp) * NSA_DK:(2 * branch + grp + 1) * NSA_DK, :]
            pieces += [v, pad_lo, pad_hi, v]
        out[...] = jnp.concatenate(pieces, axis=0).astype(BF16)


def _inproj1(x2d, positions, gain, w_in, q_norm, k_norm_slc, k_norm_win, tm=256):
    T, D = x2d.shape
    S = positions.shape[1]
    prel = (positions - positions[:, :1]).reshape(T, 1)
    own_block = np.arange(S)[:, None] // SLC_BLOCK == (np.arange(LANES)[None, :] % NSA_DK)
    blocks = jnp.asarray(own_block * MASK_BIG, F32)
    nq_, kvw = NSA_HEADS * NSA_DK, NSA_KVW
    keep = [w_in[:, :nq_ + 3 * kvw], w_in[:, nq_ + 4 * kvw:nq_ + 5 * kvw], w_in[:, nq_ + 6 * kvw:]]
    w_in_p = jnp.concatenate(keep, axis=1)
    n_in = w_in_p.shape[1]
    n_pad = -n_in % LANES
    w_in_p = jnp.pad(w_in_p, ((0, 0), (0, n_pad))).astype(BF16)
    two = lambda g: jnp.tile(g, 2).reshape(1, LANES)
    row = lambda n: pl.BlockSpec((tm, n), lambda i: (i, 0))
    nq = NSA_HEADS * NSA_DK
    vcols = lambda i: w_in[:, nq + i * NSA_KVW:nq + (i + 1) * NSA_KVW]
    w_vt = jnp.concatenate([vcols(3), vcols(5)], axis=1).T.astype(BF16)
    bf = lambda n: jax.ShapeDtypeStruct((T, n), BF16)
    f32 = lambda n: jax.ShapeDtypeStruct((T, n), F32)
    col = pl.BlockSpec((4 * LANES, tm), lambda i: (0, i))
    vt_shape = jax.ShapeDtypeStruct((4 * LANES, T), BF16)
    return pl.pallas_call(
        _inproj1_kernel,
        grid=(T // tm,),
        in_specs=[row(D), row(1), pl.BlockSpec((tm, LANES), lambda i: (i % (S // tm), 0)), _full((1, D)),
                  _full((D, n_in + n_pad)), _full((2 * NSA_KVW, D)), _full((1, LANES)), _full((1, LANES)),
                  _full((1, LANES))],
        out_specs=[row(nq), row(LANES), row(LANES), row(4 * LANES), col, row(4 * LANES), col, row(LANES)],
        out_shape=[bf(nq), f32(LANES), f32(LANES), bf(4 * LANES), vt_shape, bf(4 * LANES), vt_shape, f32(LANES)],
        compiler_params=_params("parallel"),
        name="inproj1",
    )(x2d, prel, blocks, gain.reshape(1, D), w_in_p, w_vt, two(q_norm), two(k_norm_slc), two(k_norm_win))


def _stack_pairs(q_ref, rows):
    return jnp.concatenate([q_ref[0, rows, p * LANES:(p + 1) * LANES] for p in range(NSA_PAIRS)], axis=0)


def _cmp_kernel(q_ref, kch_ref, vch_ref, pos3_ref, pend_ref, gate_ref, slope_ref,
                pek_ref, w1k_ref, w2k_ref, kcg_ref, pev_ref, w1v_ref, w2v_ref, ovl_ref, eye_ref,
                o_ref, sel_ref, kc_s, vc_s):
    tq = NSA_TQ
    S = q_ref.shape[1]

    def compress(ch_ref, pe_ref, w1_ref, w2_ref):
        a = ch_ref[0, 0]
        h_lo = _dot((a + pe_ref[0:1, :]).astype(BF16), w1_ref[0])
        h_hi = _dot((a + pe_ref[1:2, :]).astype(BF16), w1_ref[1])
        n = h_hi.shape[0]
        hid = jax.nn.gelu(h_lo + pltpu.roll(h_hi, n - 1, 0)).astype(BF16)
        return _dot(hid, w2_ref[0]), _dot(hid, w2_ref[1])

    k_lo, k_hi = compress(kch_ref, pek_ref, w1k_ref, w2k_ref)
    for e, kk in enumerate((k_lo, k_hi)):
        kk = kk * lax.rsqrt(jnp.sum(kk * kk, axis=-1, keepdims=True) * (1.0 / NSA_DK) + 1e-6) * kcg_ref[e:e + 1, :]
        kc_s[e] = kk.astype(BF16)
    v_lo, v_hi = compress(vch_ref, pev_ref, w1v_ref, w2v_ref)
    vc_s[0] = v_lo.T.astype(BF16)
    vc_s[1] = v_hi.T.astype(BF16)

    n_cmp = (S - CMP_BLOCK) // CMP_STRIDE + 1
    n_lanes = NSA_PAIRS * tq
    lane = lax.broadcasted_iota(I32, (1, LANES), 1)
    sub_lo = lax.broadcasted_iota(I32, (LANES, 1), 0) < NSA_DK
    blk_row = lax.broadcasted_iota(I32, (N_SEL, 1), 0)
    cmp_row = lax.broadcasted_iota(I32, (LANES, 1), 0)
    in_range = cmp_row < n_cmp

    def q_block(qi, carry):
        t0 = pl.multiple_of(qi * tq, tq)
        rows = pl.ds(t0, tq)
        qs = _stack_pairs(q_ref, rows)
        pos_q0 = pos3_ref[0, pl.ds(qi * (tq // LANES), 1), :][:, 0:1]
        posrel = (pend_ref[0] - pos_q0).astype(F32)
        t_lane = t0 + (lax.broadcasted_iota(I32, (1, n_lanes), 1) & (tq - 1))
        valid = (t_lane >= CMP_STRIDE * cmp_row + (CMP_BLOCK - 1)) & in_range
        gt = gate_ref[0, 0, :, rows]
        psum = jnp.zeros((LANES, tq), F32)
        outs = []
        for e in range(2):
            slopes = jnp.concatenate([jnp.broadcast_to(slope_ref[0, NSA_PAIRS * e + p:NSA_PAIRS * e + p + 1, 0:1], (1, tq))
                                      for p in range(NSA_PAIRS)], axis=1)
            s = _dot_nt(kc_s[e], qs) + posrel * slopes
            s = jnp.where(valid, s, NEG_INF)
            m = jnp.max(s, axis=0, keepdims=True)
            p = jnp.where(valid, jnp.exp2(s - m), 0.0)
            p = p / jnp.maximum(jnp.sum(p, axis=0, keepdims=True), 1e-20)
            psum = psum + functools.reduce(jnp.add, [p[:, i * tq:(i + 1) * tq] for i in range(NSA_PAIRS)])
            gate = jnp.concatenate([gt[2 * i + e:2 * i + e + 1, :] for i in range(NSA_PAIRS)], axis=1)
            outs.append(_dot(vc_s[e], p.astype(BF16)) * gate)
        z = jnp.where(sub_lo, outs[0], outs[1])
        imp = jnp.dot(ovl_ref[...], psum, preferred_element_type=F32, precision=lax.Precision.HIGHEST)
        cur = (t0 + lax.broadcasted_iota(I32, (1, tq), 1)) // SLC_BLOCK
        forced = (blk_row == 0) | (blk_row == cur) | (blk_row == cur - 1)
        imp = jnp.where(forced, FORCE_SCORE, jnp.where(blk_row <= cur, imp, -1.0))
        rank = jnp.zeros((N_SEL, tq), I32)
        for i in range(N_SEL):
            ri = imp[i:i + 1, :]
            rank = rank + jnp.where((ri > imp) | ((ri == imp) & (blk_row > i)), 1, 0)
        sel_t = jnp.where(rank < SLC_TOP_N, 1.0, 0.0).astype(BF16)
        gap = jnp.zeros((NSA_DK - N_SEL, tq), BF16)
        sel_t = jnp.concatenate([sel_t, gap, sel_t, gap], axis=0)
        sel = _dot_nt(eye_ref[...], sel_t)
        sel_ref[0, 0, rows, :] = jnp.where((lane & (NSA_DK - 1)) < N_SEL, sel - 1.0, 0.0).astype(BF16)
        for p in range(NSA_PAIRS):
            o_ref[0, rows, p * LANES:(p + 1) * LANES] = z[:, p * tq:(p + 1) * tq].T.astype(BF16)
        return carry

    lax.fori_loop(0, S // tq, q_block, 0)


def _lo_hi_cols(w):
    z = jnp.zeros_like(w)
    return jnp.stack([jnp.concatenate([w, z], axis=1), jnp.concatenate([z, w], axis=1)])


def _nsa_cmp(q, kch, vch, pos3, pend, gates_g, slopes_t, pe_k, w1_k, w2_k, k_norm_cmp, pe_v, w1_v, w2_v):
    B, S, _ = q.shape
    G = NSA_G
    nch = S // CMP_STRIDE
    half = CMP_STRIDE * NSA_DK
    pe2 = lambda pe: pe.reshape(2, half)
    w1_2 = lambda w: w.reshape(2, half, CMP_HIDDEN).astype(BF16)
    n_sel = S // SLC_BLOCK
    n_cmp = (S - CMP_BLOCK) // CMP_STRIDE + 1
    cmp_start = np.arange(LANES) * CMP_STRIDE
    slc_start = np.arange(n_sel) * SLC_BLOCK
    overlap = ((cmp_start[None, :] < slc_start[:, None] + SLC_BLOCK) & (cmp_start[None, :] + CMP_BLOCK > slc_start[:, None])
               & (np.arange(LANES)[None, :] < n_cmp))
    ovl = jnp.asarray(overlap, F32)
    eye = jnp.eye(NSA_TQ, dtype=BF16)
    qspec = pl.BlockSpec((1, S, NSA_PAIRS * LANES), lambda b, g: (b, 0, g))
    chspec = pl.BlockSpec((1, 1, nch, half), lambda b, g: (b, g, 0, 0))
    return pl.pallas_call(
        _cmp_kernel,
        grid=(B, G),
        in_specs=[qspec, chspec, chspec,
                  pl.BlockSpec((1, S // LANES, LANES), lambda b, g: (b, 0, 0)),
                  pl.BlockSpec((1, LANES, 1), lambda b, g: (b, 0, 0)),
                  pl.BlockSpec((1, 1, 3 * NSA_HPG, S), lambda b, g: (b, g, 0, 0)),
                  pl.BlockSpec((1, NSA_HPG, LANES), lambda b, g: (g, 0, 0)),
                  _full((2, half)), _full((2, half, CMP_HIDDEN)), _full((2, CMP_HIDDEN, LANES)), _full((2, LANES)),
                  _full((2, half)), _full((2, half, CMP_HIDDEN)), _full((2, CMP_HIDDEN, LANES)),
                  _full((n_sel, LANES)), _full((NSA_TQ, NSA_TQ))],
        out_specs=[qspec, pl.BlockSpec((1, 1, S, LANES), lambda b, g: (b, g, 0, 0))],
        out_shape=[jax.ShapeDtypeStruct((B, S, NSA_HEADS * NSA_DK), BF16),
                   jax.ShapeDtypeStruct((B, G, S, LANES), BF16)],
        scratch_shapes=[pltpu.VMEM((2, nch, LANES), BF16), pltpu.VMEM((2, nch, LANES), BF16)],
        compiler_params=_params("parallel", "parallel"),
        name="nsa_cmp",
    )(q, kch, vch, pos3, pend, gates_g, slopes_t, pe2(pe_k), w1_2(w1_k), _lo_hi_cols(w2_k).astype(BF16),
      _lo_hi_cols(k_norm_cmp.reshape(1, -1)).reshape(2, LANES), pe2(pe_v), w1_2(w1_v),
      _lo_hi_cols(w2_v).astype(BF16), ovl, eye)


def _slc_win_kernel(q_ref, ks_ref, vs_ref, kw_ref, vw_ref, sel_ref, gate_ref, slopeq_ref, ocmp_ref, o_ref):
    tq = SLC_TQ
    S = q_ref.shape[1]
    lane = lax.broadcasted_iota(I32, (1, LANES), 1)
    lane_lo = lane < NSA_DK
    sub_lo = lax.broadcasted_iota(I32, (LANES, 1), 0) < NSA_DK
    ones_row = (LO_ONE, HI_ONE)

    def q_block(qi, carry):
        t0 = pl.multiple_of(qi * tq, tq)
        rows = pl.ds(t0, tq)
        qs = _stack_pairs(q_ref, rows)
        selm1 = sel_ref[0, 0, rows, :]

        def q_aug(e, with_sel):
            feats = [slopeq_ref[0, NSA_PAIRS * e + p:NSA_PAIRS * e + p + 1, :] for p in range(NSA_PAIRS)]
            if with_sel:
                extra = jnp.concatenate([selm1 + f for f in feats], axis=0)
            else:
                extra = jnp.concatenate([jnp.broadcast_to(f, (tq, LANES)) for f in feats], axis=0)
            return jnp.where(lane_lo, qs, extra) if e == 0 else jnp.where(lane_lo, extra, qs)

        def k_aug(k_ref, e, r0, n):
            return k_ref[0, pl.ds(r0, n), e * LANES:(e + 1) * LANES]

        gt = gate_ref[0, 0, :, rows]
        tq_lane = t0 + (lax.broadcasted_iota(I32, (1, NSA_PAIRS * tq), 1) & (tq - 1))

        def gated(accs, branch):
            outs = []
            for e, acc in enumerate(accs):
                gate = jnp.concatenate([gt[branch * NSA_HPG + 2 * p + e:branch * NSA_HPG + 2 * p + e + 1, :]
                                        for p in range(NSA_PAIRS)], axis=1)
                outs.append(acc * (gate / acc[ones_row[e]:ones_row[e] + 1, :]))
            return jnp.where(sub_lo, outs[0], outs[1])

        qa = [q_aug(0, True), q_aug(1, True)]
        krow = lax.broadcasted_iota(I32, (SLC_TK, 1), 0)

        def slc_step(j, state, last):
            r0 = pl.multiple_of(j * SLC_TK, SLC_TK)
            ss = [_dot_nt(k_aug(ks_ref, e, r0, SLC_TK), qa[e]) for e in range(2)]
            out = []
            for e in range(2):
                m, acc = state[e]
                s = jnp.where(r0 + krow <= tq_lane, ss[e], NEG_INF) if last else ss[e]
                m_new = jnp.maximum(m, jnp.max(s, axis=0, keepdims=True))
                p = jnp.exp2(s - m_new).astype(BF16)
                pv = _dot(vs_ref[e * LANES:(e + 1) * LANES, pl.ds(r0, SLC_TK)], p)
                out.append((m_new, jnp.exp2(m - m_new) * acc + pv))
            return tuple(out)

        init = (jnp.full((1, NSA_PAIRS * tq), NEG_INF, F32), jnp.zeros((LANES, NSA_PAIRS * tq), F32))
        n_full = qi // (SLC_TK // tq)
        state = lax.fori_loop(0, n_full, functools.partial(slc_step, last=False), (init, init))
        state = slc_step(n_full, state, True)
        z = gated([state[0][1], state[1][1]], 1)

        w0 = pl.multiple_of(jnp.maximum(t0 - WINDOW, 0), tq)
        rel = tq_lane - (w0 + lax.broadcasted_iota(I32, (WIN_SPAN, 1), 0))
        wmask = (rel >= 0) & (rel < WINDOW)
        ss = [_dot_nt(k_aug(kw_ref, e, w0, WIN_SPAN), q_aug(e, False)) for e in range(2)]
        accs = []
        for e in range(2):
            s = jnp.where(wmask, ss[e], NEG_INF)
            p = jnp.exp2(s - jnp.max(s, axis=0, keepdims=True)).astype(BF16)
            accs.append(_dot(vw_ref[e * LANES:(e + 1) * LANES, pl.ds(w0, WIN_SPAN)], p))
        z = z + gated(accs, 2)

        for p in range(NSA_PAIRS):
            sl = slice(p * LANES, (p + 1) * LANES)
            o_ref[0, rows, sl] = (ocmp_ref[0, rows, sl].astype(F32) + z[:, p * tq:(p + 1) * tq].T).astype(BF16)
        return carry

    lax.fori_loop(0, S // tq, q_block, 0)


def _slope_aug(slopes_eo):
    hi = slopes_eo.astype(BF16).astype(F32)
    lo = (slopes_eo - hi).astype(BF16).astype(F32)
    w = jnp.asarray([65536.0, 256.0, 1.0], F32)
    feats = jnp.concatenate([hi[..., None] * w, lo[..., None] * w], axis=-1)
    half = jnp.pad(feats, ((0, 0), (0, 0), (AUG_POS, NSA_DK - AUG_POS - 6)))
    return jnp.concatenate([half, half], axis=-1).astype(BF16)


def _nsa_slc_win(q, ks4, vst4, kw4, vwt4, sel, gates_t, slopeq, o_cmp):
    B, S, _ = q.shape
    qspec = pl.BlockSpec((1, S, NSA_PAIRS * LANES), lambda b, g: (b, 0, g))
    kvspec = pl.BlockSpec((1, S, 2 * LANES), lambda b, g: (b, 0, g))
    vtspec = pl.BlockSpec((2 * LANES, S), lambda b, g: (g, b))
    return pl.pallas_call(
        _slc_win_kernel,
        grid=(B, NSA_G),
        in_specs=[qspec, kvspec, vtspec, kvspec, vtspec,
                  pl.BlockSpec((1, 1, S, LANES), lambda b, g: (b, g, 0, 0)),
                  pl.BlockSpec((1, 1, 3 * NSA_HPG, S), lambda b, g: (b, g, 0, 0)),
                  pl.BlockSpec((1, NSA_HPG, LANES), lambda b, g: (g, 0, 0)),
                  qspec],
        out_specs=qspec,
        out_shape=jax.ShapeDtypeStruct((B, S, NSA_HEADS * NSA_DK), BF16),
        compiler_params=_params("parallel", "parallel"),
        name="nsa_slc_win",
    )(q, ks4, vst4, kw4, vwt4, sel, gates_t, slopeq, o_cmp)


def _outproj_kernel(o_ref, x_ref, w_ref, out_ref):
    out_ref[...] = x_ref[...] + _dot(o_ref[...], w_ref[...])


def _outproj(o2d, x2d, w_out, tm=512):
    T, D = x2d.shape
    K = o2d.shape[1]
    return pl.pallas_call(
        _outproj_kernel,
        grid=(T // tm,),
        in_specs=[pl.BlockSpec((tm, K), lambda i: (i, 0)), pl.BlockSpec((tm, D), lambda i: (i, 0)), _full((K, D))],
        out_specs=pl.BlockSpec((tm, D), lambda i: (i, 0)),
        out_shape=jax.ShapeDtypeStruct((T, D), F32),
        compiler_params=_params("parallel"),
        name="outproj1",
    )(o2d, x2d, w_out.astype(BF16))


def _mixer1(x, positions, gain, w_in, q_norm, k_norm_cmp, k_norm_slc, k_norm_win,
            pe_k, w1_k, w2_k, pe_v, w1_v, w2_v, w_out):
    B, S, D = x.shape
    T = B * S
    G, HPG, DK = NSA_G, NSA_HPG, NSA_DK
    assert S // SLC_BLOCK == N_SEL and S % NSA_TQ == 0 and NSA_TQ % LANES == 0
    x2d = x.reshape(T, D)
    q, kc, vc, ks4, vs4, kw4, vw4, gates = _inproj1(x2d, positions, gain, w_in, q_norm, k_norm_slc,
                                                    k_norm_win)
    b3 = lambda t: t.reshape(B, S, -1)
    chunks = lambda t: t.reshape(B, S, G, DK).transpose(0, 2, 1, 3).reshape(B, G, S // CMP_STRIDE, CMP_STRIDE * DK)
    gates_t = gates[:, :3 * NSA_HEADS].reshape(B, S, 3, G, HPG).transpose(0, 3, 2, 4, 1).reshape(B, G, 3 * HPG, S)
    n_cmp = (S - CMP_BLOCK) // CMP_STRIDE + 1
    pend = jnp.pad(positions[:, CMP_BLOCK - 1::CMP_STRIDE][:, :n_cmp], ((0, 0), (0, LANES - n_cmp))).reshape(B, LANES, 1)
    pos3 = positions.reshape(B, S // LANES, LANES)
    slopes = jnp.asarray(2.0 ** (-8.0 * np.arange(1, NSA_HEADS + 1) / NSA_HEADS), F32) * LOG2E
    slopes_eo = slopes.reshape(G, NSA_PAIRS, 2).transpose(0, 2, 1).reshape(G, HPG)
    slopes_t = jnp.broadcast_to(slopes_eo[:, :, None], (G, HPG, LANES))
    o_cmp, sel = _nsa_cmp(b3(q), chunks(kc), chunks(vc), pos3, pend, gates_t, slopes_t,
                          pe_k, w1_k, w2_k, k_norm_cmp, pe_v, w1_v, w2_v)
    o = _nsa_slc_win(b3(q), b3(ks4), vs4, b3(kw4), vw4, sel, gates_t, _slope_aug(slopes_eo), o_cmp)
    return _outproj(o.reshape(T, -1), x2d, w_out).reshape(B, S, D)


def kernel(x, positions, norm_mix, norm_ffn, mix0_w_in, mla_q_a_norm, mla_w_uq, mla_kv_a_norm, mla_w_ukv, mla_q_norm, mla_k_norm, conv_dw_w, conv_dw_b, conv_ln_g, conv_ln_b, mix0_w_out, nsa_w_in, nsa_q_norm, nsa_k_norm_cmp, nsa_k_norm_slc, nsa_k_norm_win, nsa_cmp_pe_k, nsa_cmp_w1_k, nsa_cmp_w2_k, nsa_cmp_pe_v, nsa_cmp_w1_v, nsa_cmp_w2_v, nsa_w_out, moe_router_group, moe_router_group_b, moe_router_expert, moe_router_expert_b, moe_w_gate, moe_w_up, moe_w_down):
    def moe(x, layer):
        return _moe(x, norm_ffn[layer], moe_router_group[layer], moe_router_group_b[layer], moe_router_expert[layer],
                    moe_router_expert_b[layer], moe_w_gate[layer], moe_w_up[layer], moe_w_down[layer])

    x = _mixer0(x, positions, norm_mix[0], mix0_w_in[0], mla_q_a_norm[0], mla_w_uq[0], mla_kv_a_norm[0], mla_w_ukv[0],
                mla_q_norm[0], mla_k_norm[0], conv_dw_w[0], conv_dw_b[0], conv_ln_g[0], conv_ln_b[0], mix0_w_out[0])
    x = moe(x, 0)
    x = _mixer1(x, positions, norm_mix[1], nsa_w_in[0], nsa_q_norm[0], nsa_k_norm_cmp[0], nsa_k_norm_slc[0],
                nsa_k_norm_win[0], nsa_cmp_pe_k[0], nsa_cmp_w1_k[0], nsa_cmp_w2_k[0], nsa_cmp_pe_v[0],
                nsa_cmp_w1_v[0], nsa_cmp_w2_v[0], nsa_w_out[0])
    return moe(x, 1)
```

```python
import functools

import numpy as np
import jax
import jax.numpy as jnp
from jax import lax
from jax.experimental import pallas as pl
from jax.experimental.pallas import tpu as pltpu

F32 = jnp.float32
BF16 = jnp.bfloat16
I32 = jnp.int32

LANES = 128
NEG_INF = -1e30
LOG2E = 1.4426950408889634
Q_BLOCK = 128
MLA_HEADS = 8
MLA_Q_LORA = 256
MLA_KV_LORA = 128
MLA_NOPE = 64
MLA_ROPE = 32
MLA_V = 64
MLA_QK = MLA_NOPE + MLA_ROPE
ROPE_THETA = 10000.0
CONV_CH = 512
CONV_W = 31
NSA_HEADS = 16
NSA_G = 2
NSA_HPG = NSA_HEADS // NSA_G
NSA_DK = 64
NSA_KVW = NSA_G * NSA_DK
CMP_BLOCK = 32
CMP_STRIDE = 16
CMP_HIDDEN = 128
SLC_BLOCK = 64
SLC_TOP_N = 8
WINDOW = 256
FORCE_SCORE = 1e4
MOE_GROUPS = 4
MOE_EPG = 8
MOE_EXPERTS = MOE_GROUPS * MOE_EPG
MOE_HIDDEN = 256
MOE_ROW_BLOCK = 256

VMEM_LIMIT = 56 * 1024 * 1024


def _params(*sem):
    return pltpu.CompilerParams(dimension_semantics=sem, vmem_limit_bytes=VMEM_LIMIT)


def _full(shape):
    n = len(shape)
    return pl.BlockSpec(shape, lambda *_: (0,) * n)


def _row_mean(x):
    n = x.shape[-1]
    folded = functools.reduce(jnp.add, [x[:, c:c + LANES] for c in range(0, n, LANES)])
    return jnp.sum(folded, axis=-1, keepdims=True) * (1.0 / n)


def _rms(x, eps=1e-6):
    return x * lax.rsqrt(_row_mean(x * x) + eps)


def _dot(a, b):
    return jnp.dot(a, b, preferred_element_type=F32)


def _group_matrix(groups):
    width = LANES // groups
    same = (lax.broadcasted_iota(I32, (LANES, LANES), 0) // width) == (lax.broadcasted_iota(I32, (LANES, LANES), 1) // width)
    return jnp.where(same, 1.0, 0.0).astype(BF16)


def _group_sums(t, group_matrix):
    hi = t.astype(BF16)
    lo = (t - hi.astype(F32)).astype(BF16)
    return _dot(hi, group_matrix) + _dot(lo, group_matrix)


def _dot_nt(a, b, **kw):
    return lax.dot_general(a, b, (((1,), (1,)), ((), ())), preferred_element_type=F32, **kw)


def _inproj0_kernel(x_ref, pos_ref, gmix_ref, win_ref, qan_ref, wuq_ref, kvan_ref, wuk_ref, wuv_ref,
                    qg_ref, kg_ref, invf_ref, sgn_ref, q_out, k_out, v_out, u_out):
    h = _rms(x_ref[...]) * gmix_ref[...]
    proj = _dot(h.astype(BF16), win_ref[...])
    c_q = proj[:, :MLA_Q_LORA]
    c_kv = proj[:, MLA_Q_LORA:MLA_Q_LORA + MLA_KV_LORA]
    k_rope = proj[:, 384:512]
    a = proj[:, 512:512 + CONV_CH]
    g = proj[:, 512 + CONV_CH:]
    u_out[...] = a * jax.nn.sigmoid(g)
    cqn = (_rms(c_q) * qan_ref[...]).astype(BF16)
    ckvn = (_rms(c_kv) * kvan_ref[...]).astype(BF16)
    q = _dot(cqn, wuq_ref[...])
    kn = _dot(ckvn, wuk_ref[...])
    slot_row = lax.broadcasted_iota(I32, (MLA_HEADS * LANES, 1), 0) & (LANES - 1)
    v_out[...] = (_dot_nt(wuv_ref[...], ckvn) + jnp.where(slot_row == MLA_V, 1.0, 0.0)).astype(BF16)
    ang = pos_ref[...].astype(F32) * invf_ref[...]
    cos = jnp.cos(ang)
    sin = jnp.sin(ang) * sgn_ref[...]
    lane = lax.broadcasted_iota(I32, (1, LANES), 1)
    first_half = (lane >= MLA_NOPE) & (lane < MLA_NOPE + MLA_ROPE // 2)
    scale = MLA_QK ** -0.5 * LOG2E

    all_lanes = _group_matrix(1)

    def norm_rope(t, gain):
        t = t * lax.rsqrt(_group_sums(t * t, all_lanes) * (1.0 / MLA_QK) + 1e-6) * gain
        partner = jnp.where(first_half, pltpu.roll(t, LANES - MLA_ROPE // 2, 1), pltpu.roll(t, MLA_ROPE // 2, 1))
        return t * cos + partner * sin

    for hd in range(MLA_HEADS):
        sl = slice(hd * LANES, (hd + 1) * LANES)
        q_out[:, sl] = (norm_rope(q[:, sl], qg_ref[...]) * scale).astype(BF16)
        k_out[:, sl] = norm_rope(kn[:, sl] + k_rope, kg_ref[...]).astype(BF16)


def _head_slots(w, n_heads, width, offset=0):
    k = w.shape[0]
    w = w.reshape(k, n_heads, width)
    w = jnp.pad(w, ((0, 0), (0, 0), (offset, LANES - width - offset)))
    return w.reshape(k, n_heads * LANES)


def _inproj0(x2d, pos_col, gmix, w_in, q_a_norm, w_uq, kv_a_norm, w_ukv, q_norm, k_norm, tm=512):
    T, D = x2d.shape
    H = MLA_HEADS
    w_krope = jnp.pad(w_in[:, 384:416], ((0, 0), (MLA_NOPE, LANES - MLA_NOPE - MLA_ROPE)))
    w_in_p = jnp.concatenate([w_in[:, :384], w_krope, w_in[:, 416:]], axis=1).astype(BF16)
    w_uq_p = _head_slots(w_uq, H, MLA_QK).astype(BF16)
    w_ukv3 = w_ukv.reshape(MLA_KV_LORA, H, MLA_NOPE + MLA_V)
    w_uk_p = _head_slots(w_ukv3[:, :, :MLA_NOPE].reshape(MLA_KV_LORA, H * MLA_NOPE), H, MLA_NOPE).astype(BF16)
    w_uv = _head_slots(w_ukv3[:, :, MLA_NOPE:].reshape(MLA_KV_LORA, H * MLA_V), H, MLA_V).T.astype(BF16)
    pad = LANES - MLA_QK
    qg = jnp.pad(q_norm, (0, pad)).reshape(1, LANES)
    kg = jnp.pad(k_norm, (0, pad)).reshape(1, LANES)
    half = MLA_ROPE // 2
    inv_freq = ROPE_THETA ** (-jnp.arange(half, dtype=F32) / half)
    invf = jnp.zeros((LANES,), F32).at[MLA_NOPE:MLA_NOPE + half].set(inv_freq).at[MLA_NOPE + half:MLA_QK].set(inv_freq)
    sgn = jnp.zeros((LANES,), F32).at[MLA_NOPE:MLA_NOPE + half].set(-1.0).at[MLA_NOPE + half:MLA_QK].set(1.0)
    row = lambda n: pl.BlockSpec((tm, n), lambda i: (i, 0))
    n_in = w_in_p.shape[1]
    return pl.pallas_call(
        _inproj0_kernel,
        grid=(T // tm,),
        in_specs=[row(D), row(1), _full((1, D)), _full((D, n_in)), _full((1, MLA_Q_LORA)),
                  _full((MLA_Q_LORA, H * LANES)), _full((1, MLA_KV_LORA)), _full((MLA_KV_LORA, H * LANES)),
                  _full((H * LANES, MLA_KV_LORA)), _full((1, LANES)), _full((1, LANES)), _full((1, LANES)),
                  _full((1, LANES))],
        out_specs=[row(H * LANES), row(H * LANES), pl.BlockSpec((H * LANES, tm), lambda i: (0, i)), row(CONV_CH)],
        out_shape=[jax.ShapeDtypeStruct((T, H * LANES), BF16), jax.ShapeDtypeStruct((T, H * LANES), BF16),
                   jax.ShapeDtypeStruct((H * LANES, T), BF16), jax.ShapeDtypeStruct((T, CONV_CH), F32)],
        compiler_params=_params("parallel"),
        name="inproj0",
    )(x2d, pos_col, gmix.reshape(1, D), w_in_p, q_a_norm.reshape(1, -1), w_uq_p, kv_a_norm.reshape(1, -1),
      w_uk_p, w_uv, qg, kg, invf.reshape(1, LANES), sgn.reshape(1, LANES))


MLA_TQ = 512
MLA_TK = 512
MLA_ONE = MLA_V


def _mla_attn_kernel(q_ref, k_ref, vt_ref, o_ref):
    S = q_ref.shape[1]
    tq, tk = MLA_TQ, MLA_TK
    krow = lax.broadcasted_iota(I32, (tk, 1), 0)
    qcol = lax.broadcasted_iota(I32, (1, tq), 1)

    def q_block(qi, carry):
        q0 = pl.multiple_of(qi * tq, tq)
        qs = [q_ref[0, pl.ds(q0, tq), hh * LANES:(hh + 1) * LANES] for hh in range(2)]

        def kv_step(j, state, masked):
            k0 = pl.multiple_of(j * tk, tk)
            hs = [slice(hh * LANES, (hh + 1) * LANES) for hh in range(2)]
            ss = [_dot_nt(k_ref[0, pl.ds(k0, tk), hs[hh]], qs[hh]) for hh in range(2)]
            out = []
            for hh in range(2):
                m, acc = state[hh]
                s = jnp.where(k0 + krow <= q0 + qcol, ss[hh], NEG_INF) if masked else ss[hh]
                m_new = jnp.maximum(m, jnp.max(s, axis=0, keepdims=True))
                p = jnp.exp2(s - m_new).astype(BF16)
                out.append((m_new, jnp.exp2(m - m_new) * acc + _dot(vt_ref[hs[hh], pl.ds(k0, tk)], p)))
            return tuple(out)

        init = (jnp.full((1, tq), NEG_INF, F32), jnp.zeros((LANES, tq), F32))
        n_full = (qi * tq) // tk
        state = lax.fori_loop(0, n_full, functools.partial(kv_step, masked=False), (init, init))
        state = kv_step(n_full, state, True)
        outs = []
        for hh in range(2):
            acc = state[hh][1]
            outs.append((acc * (1.0 / acc[MLA_ONE:MLA_ONE + 1, :]))[:MLA_V, :])
        o_ref[0, pl.ds(q0, tq), :] = jnp.concatenate(outs, axis=0).T.astype(BF16)
        return carry

    lax.fori_loop(0, S // tq, q_block, 0)


def _mla_attn(q, k, vt):
    B, S, _ = q.shape
    spec = pl.BlockSpec((1, S, 2 * LANES), lambda b, h: (b, 0, h))
    return pl.pallas_call(
        _mla_attn_kernel,
        grid=(B, MLA_HEADS // 2),
        in_specs=[spec, spec, pl.BlockSpec((2 * LANES, S), lambda b, h: (h, b))],
        out_specs=pl.BlockSpec((1, S, 2 * MLA_V), lambda b, h: (b, 0, h)),
        out_shape=jax.ShapeDtypeStruct((B, S, MLA_HEADS * MLA_V), BF16),
        compiler_params=_params("parallel", "parallel"),
        name="mla_attn",
    )(q, k, vt)


CONV_TILE = 64
CONV_PAD = 32


CONV_ROWS = 512


def _conv_out0_kernel(u_ref, o_ref, x_ref, dww_ref, dwb_ref, lng_ref, lnb_ref, wo_ref, out_ref, upad, act, shifted):
    step = pl.program_id(1)

    @pl.when(step == 0)
    def _():
        upad[0:CONV_PAD, :] = jnp.zeros((CONV_PAD, CONV_CH), F32)
        upad[CONV_PAD:, :] = u_ref[0]

    lead = CONV_PAD - (CONV_W - 1)
    base = step * CONV_ROWS

    def tile(i, carry):
        t0 = pl.multiple_of(i * CONV_TILE, CONV_TILE)
        win = upad[pl.ds(pl.multiple_of(base + t0, CONV_TILE), CONV_TILE + CONV_PAD), :]
        acc = jnp.zeros((CONV_TILE, CONV_CH), F32) + dwb_ref[...]
        span = CONV_TILE + CONV_PAD - 8
        for r in range(1, 8):
            shifted[r - 1] = win[r:r + span, :]
        for j in range(CONV_W):
            r, k = (lead + j) % 8, (lead + j) // 8 * 8
            tap = win[k:k + CONV_TILE, :] if r == 0 else shifted[r - 1, k:k + CONV_TILE, :]
            acc = acc + tap * dww_ref[j:j + 1, :]
        xc = acc - _row_mean(acc)
        var = _row_mean(xc * xc)
        y = xc * lax.rsqrt(var + 1e-5) * lng_ref[...] + lnb_ref[...]
        act[pl.ds(t0, CONV_TILE), :] = (y * jax.nn.sigmoid(y)).astype(BF16)
        return carry

    lax.fori_loop(0, CONV_ROWS // CONV_TILE, tile, 0)
    n_o = o_ref.shape[2]
    y = _dot(o_ref[0], wo_ref[:n_o, :]) + _dot(act[...], wo_ref[n_o:, :])
    out_ref[0] = x_ref[0] + y


def _conv_out0(u, o_mla, x, dw_w, dw_b, ln_g, ln_b, w_out):
    B, S, D = x.shape
    n_o = o_mla.shape[2]
    w_out_p = w_out.astype(BF16)
    tspec = lambda n: pl.BlockSpec((1, CONV_ROWS, n), lambda b, t: (b, t, 0))
    return pl.pallas_call(
        _conv_out0_kernel,
        grid=(B, S // CONV_ROWS),
        in_specs=[pl.BlockSpec((1, S, CONV_CH), lambda b, t: (b, 0, 0)), tspec(n_o), tspec(D),
                  _full((CONV_W, CONV_CH)), _full((1, CONV_CH)), _full((1, CONV_CH)), _full((1, CONV_CH)),
                  _full((n_o + CONV_CH, D))],
        out_specs=tspec(D),
        out_shape=jax.ShapeDtypeStruct((B, S, D), F32),
        scratch_shapes=[pltpu.VMEM((S + CONV_PAD, CONV_CH), F32), pltpu.VMEM((CONV_ROWS, CONV_CH), BF16),
                        pltpu.VMEM((7, CONV_TILE + CONV_PAD - 8, CONV_CH), F32)],
        compiler_params=_params("parallel", "arbitrary"),
        name="conv_out0",
    )(u, o_mla, x, dw_w, dw_b.reshape(1, -1), ln_g.reshape(1, -1), ln_b.reshape(1, -1), w_out_p)


def _mixer0(x, positions, gmix, w_in, q_a_norm, w_uq, kv_a_norm, w_ukv, q_norm, k_norm,
            dw_w, dw_b, ln_g, ln_b, w_out):
    B, S, D = x.shape
    T = B * S
    q, k, v, u = _inproj0(x.reshape(T, D), positions.reshape(T, 1), gmix, w_in, q_a_norm, w_uq, kv_a_norm, w_ukv,
                          q_norm, k_norm)
    o = _mla_attn(q.reshape(B, S, -1), k.reshape(B, S, -1), v)
    return _conv_out0(u.reshape(B, S, -1), o, x, dw_w, dw_b, ln_g, ln_b, w_out)


ROUTE_TILE = 512
ROUTE_ROWS = 40


def _route_kernel(x_ref, g_ref, wr_ref, br_ref, tri_ref, hn_ref, oi_ref, of_ref, cnt_ref, carry):
    @pl.when(pl.program_id(0) == 0)
    def _():
        carry[...] = jnp.zeros_like(carry)

    hn = _rms(x_ref[...]) * g_ref[...]
    _slab_store(hn_ref, hn)
    tm = hn.shape[0]
    logits = _dot_nt(wr_ref[...], hn, precision=lax.Precision.HIGHEST) + br_ref[...]
    gl = logits[MOE_EXPERTS:MOE_EXPERTS + MOE_GROUPS]
    rid_g = lax.broadcasted_iota(I32, (MOE_GROUPS, tm), 0)
    gmax = jnp.max(gl, axis=0, keepdims=True)
    grp = jnp.min(jnp.where(gl == gmax, rid_g, MOE_GROUPS), axis=0, keepdims=True)
    g_w = 1.0 / jnp.sum(jnp.exp(gl - gmax), axis=0, keepdims=True)
    e_in = jnp.zeros((MOE_EPG, tm), F32)
    for gi in range(MOE_GROUPS):
        e_in = jnp.where(grp == gi, logits[gi * MOE_EPG:(gi + 1) * MOE_EPG], e_in)
    rid_e = lax.broadcasted_iota(I32, (MOE_EPG, tm), 0)
    v1 = jnp.max(e_in, axis=0, keepdims=True)
    i1 = jnp.min(jnp.where(e_in == v1, rid_e, MOE_EPG), axis=0, keepdims=True)
    rest = jnp.where(rid_e == i1, -jnp.inf, e_in)
    v2 = jnp.max(rest, axis=0, keepdims=True)
    i2 = jnp.min(jnp.where(rest == v2, rid_e, MOE_EPG), axis=0, keepdims=True)
    ex = jnp.exp(v2 - v1)
    den = 1.0 + ex
    e1 = grp * MOE_EPG + i1
    e2 = grp * MOE_EPG + i2
    rid = lax.broadcasted_iota(I32, (MOE_EXPERTS, tm), 0)
    hit1 = rid == e1
    hit2 = rid == e2
    member = jnp.where(hit1 | hit2, 1.0, 0.0)
    before = _dot(member.astype(BF16), tri_ref[...]) + carry[...]
    r1 = jnp.sum(jnp.where(hit1, before, 0.0), axis=0, keepdims=True)
    r2 = jnp.sum(jnp.where(hit2, before, 0.0), axis=0, keepdims=True)
    carry[...] = carry[...] + jnp.sum(member, axis=1, keepdims=True)
    oi_ref[...] = jnp.zeros_like(oi_ref)
    oi_ref[0:1, :] = e1
    oi_ref[1:2, :] = e2
    oi_ref[2:3, :] = r1.astype(I32)
    oi_ref[3:4, :] = r2.astype(I32)
    of_ref[...] = jnp.zeros_like(of_ref)
    of_ref[0:1, :] = g_w / den
    of_ref[1:2, :] = g_w * ex / den
    cnt_ref[...] = jnp.broadcast_to(carry[...], cnt_ref.shape).astype(I32)


def _route(x2d, gain, router_group, router_group_b, router_expert, router_expert_b):
    T, D = x2d.shape
    tm = ROUTE_TILE
    pad = ROUTE_ROWS - MOE_EXPERTS - MOE_GROUPS
    wr = jnp.concatenate([router_expert.T, router_group.T, jnp.zeros((pad, D), F32)], axis=0)
    br = jnp.concatenate([router_expert_b, router_group_b, jnp.zeros((pad,), F32)]).reshape(ROUTE_ROWS, 1)
    tri = (jnp.arange(tm)[:, None] < jnp.arange(tm)[None, :]).astype(BF16)
    return pl.pallas_call(
        _route_kernel,
        grid=(T // tm,),
        in_specs=[pl.BlockSpec((tm, D), lambda i: (i, 0)), _full((1, D)), _full((ROUTE_ROWS, D)),
                  _full((ROUTE_ROWS, 1)), _full((tm, tm))],
        out_specs=[pl.BlockSpec((tm * SLAB, LANES), lambda i: (i, 0)), pl.BlockSpec((8, tm), lambda i: (0, i)),
                   pl.BlockSpec((8, tm), lambda i: (0, i)), _full((MOE_EXPERTS, LANES))],
        out_shape=[jax.ShapeDtypeStruct((T * SLAB, LANES), F32), jax.ShapeDtypeStruct((8, T), I32),
                   jax.ShapeDtypeStruct((8, T), F32), jax.ShapeDtypeStruct((MOE_EXPERTS, LANES), I32)],
        scratch_shapes=[pltpu.VMEM((MOE_EXPERTS, 1), F32)],
        compiler_params=_params("arbitrary"),
        name="moe_route",
    )(x2d, gain.reshape(1, D), wr, br, tri)


MOVE_CHUNK = 512
MOVE_UNROLL = 8


SLAB = 8


def _slab_load(ref, n, first=0, stride=SLAB):
    return jnp.concatenate([ref[pl.ds(first + j, n, stride=stride), :] for j in range(SLAB)], axis=1)


def _slab_store(ref, value):
    n = value.shape[0]
    for j in range(SLAB):
        ref[pl.ds(j, n, stride=SLAB), :] = value[:, j * LANES:(j + 1) * LANES]


def _slab(row):
    return pl.ds(pl.multiple_of(row, SLAB), SLAB)


def _for_tokens(n, fn):
    def body(t, carry):
        fn(t, 0)
        fn(t, 1)
        return carry
    lax.fori_loop(0, n, body, 0, unroll=MOVE_UNROLL)


def _dispatch_kernel(dest_ref, hn_ref, xs_init_ref, xs_ref, sem):
    del xs_init_ref
    base = pl.program_id(0) * (2 * MOVE_CHUNK)

    def copy(t, k):
        return pltpu.make_async_copy(hn_ref.at[_slab(t * SLAB)], xs_ref.at[_slab(dest_ref[base + 2 * t + k])], sem)

    _for_tokens(MOVE_CHUNK, lambda t, k: copy(t, k).start())
    _for_tokens(MOVE_CHUNK, lambda t, k: copy(t, k).wait())


def _dispatch(dest, hn, n_rows):
    T = hn.shape[0] // SLAB
    tm = MOVE_CHUNK
    return pl.pallas_call(
        _dispatch_kernel,
        grid_spec=pltpu.PrefetchScalarGridSpec(
            num_scalar_prefetch=1,
            grid=(T // tm,),
            in_specs=[pl.BlockSpec((tm * SLAB, LANES), lambda i, d: (i, 0)), pl.BlockSpec(memory_space=pl.ANY)],
            out_specs=pl.BlockSpec(memory_space=pl.ANY),
            scratch_shapes=[pltpu.SemaphoreType.DMA(())],
        ),
        out_shape=jax.ShapeDtypeStruct((n_rows * SLAB, LANES), F32),
        input_output_aliases={2: 0},
        compiler_params=_params("arbitrary"),
        name="moe_dispatch",
    )(dest, hn, jnp.zeros((n_rows * SLAB, LANES), F32))


def _expert_kernel(blk_e_ref, n_used_ref, row_src_ref, hn_ref, wg_ref, wu_ref, wd_ref, ys_ref,
                   xbuf, sem, wg_s, wu_s, wd_s):
    b = pl.program_id(0)
    rb = MOE_ROW_BLOCK
    n_used = n_used_ref[0]

    def copy(blk, i):
        slot = blk % 2
        return pltpu.make_async_copy(hn_ref.at[_slab(row_src_ref[blk * rb + i])], xbuf.at[slot, _slab(i * SLAB)],
                                     sem.at[slot])

    def for_rows(fn):
        def body(i, carry):
            fn(i)
            return carry
        lax.fori_loop(0, rb, body, 0, unroll=MOVE_UNROLL)

    @pl.when(b == 0)
    def _():
        for_rows(lambda i: copy(b, i).start())

    @pl.when(b + 1 < n_used)
    def _():
        for_rows(lambda i: copy(b + 1, i).start())

    @pl.when((b == 0) | (blk_e_ref[b] != blk_e_ref[jnp.maximum(b - 1, 0)]))
    def _():
        wg_s[...] = wg_ref[0].astype(BF16)
        wu_s[...] = wu_ref[0].astype(BF16)
        wd_s[...] = wd_ref[0].astype(BF16)

    @pl.when(b < n_used)
    def _():
        for_rows(lambda i: copy(b, i).wait())
        x = _slab_load(xbuf.at[b % 2], rb).astype(BF16)
        gate = _dot(x, wg_s[...])
        hid = (gate * jax.nn.sigmoid(gate) * _dot(x, wu_s[...])).astype(BF16)
        _slab_store(ys_ref, _dot(hid, wd_s[...]))

    @pl.when(b >= n_used)
    def _():
        ys_ref[...] = jnp.zeros_like(ys_ref)


def _experts(blk_expert, n_used, row_src, hn, w_gate, w_up, w_down):
    R = row_src.shape[0]
    rb = MOE_ROW_BLOCK
    D = w_gate.shape[1]
    up_spec = pl.BlockSpec((1, D, MOE_HIDDEN), lambda b, be, nu, rs: (be[b], 0, 0))
    return pl.pallas_call(
        _expert_kernel,
        grid_spec=pltpu.PrefetchScalarGridSpec(
            num_scalar_prefetch=3,
            grid=(R // rb,),
            in_specs=[pl.BlockSpec(memory_space=pl.ANY), up_spec, up_spec,
                      pl.BlockSpec((1, MOE_HIDDEN, D), lambda b, be, nu, rs: (be[b], 0, 0))],
            out_specs=pl.BlockSpec((rb * SLAB, LANES), lambda b, be, nu, rs: (b, 0)),
            scratch_shapes=[pltpu.VMEM((2, rb * SLAB, LANES), F32), pltpu.SemaphoreType.DMA((2,)),
                            pltpu.VMEM((D, MOE_HIDDEN), BF16), pltpu.VMEM((D, MOE_HIDDEN), BF16),
                            pltpu.VMEM((MOE_HIDDEN, D), BF16)],
        ),
        out_shape=jax.ShapeDtypeStruct((R * SLAB, LANES), F32),
        compiler_params=_params("arbitrary"),
        name="moe_experts",
    )(blk_expert, n_used, row_src, hn, w_gate, w_up, w_down)


def _combine_kernel(dest_ref, x_ref, gate_ref, ys_ref, out_ref, buf, sem):
    tm = x_ref.shape[0]
    step = pl.program_id(0)

    def copy(s, t, k):
        a = 2 * t + k
        slot = s % 2
        return pltpu.make_async_copy(ys_ref.at[_slab(dest_ref[s * (2 * tm) + a])], buf.at[slot, _slab(a * SLAB)],
                                     sem.at[slot])

    @pl.when(step == 0)
    def _():
        _for_tokens(tm, lambda t, k: copy(step, t, k).start())

    @pl.when(step + 1 < pl.num_programs(0))
    def _():
        _for_tokens(tm, lambda t, k: copy(step + 1, t, k).start())

    _for_tokens(tm, lambda t, k: copy(step, t, k).wait())
    mine = buf.at[step % 2]
    y0 = _slab_load(mine, tm, 0, 2 * SLAB)
    y1 = _slab_load(mine, tm, SLAB, 2 * SLAB)
    out_ref[...] = x_ref[...] + gate_ref[:, 0:1] * y0 + gate_ref[:, 1:2] * y1


def _combine(dest, x2d, gates_col, ys, tm=MOVE_CHUNK):
    T, D = x2d.shape
    return pl.pallas_call(
        _combine_kernel,
        grid_spec=pltpu.PrefetchScalarGridSpec(
            num_scalar_prefetch=1,
            grid=(T // tm,),
            in_specs=[pl.BlockSpec((tm, D), lambda i, d: (i, 0)), pl.BlockSpec((tm, 2), lambda i, d: (i, 0)),
                      pl.BlockSpec(memory_space=pl.ANY)],
            out_specs=pl.BlockSpec((tm, D), lambda i, d: (i, 0)),
            scratch_shapes=[pltpu.VMEM((2, tm * 2 * SLAB, LANES), F32), pltpu.SemaphoreType.DMA((2,))],
        ),
        out_shape=jax.ShapeDtypeStruct((T, D), F32),
        compiler_params=_params("arbitrary"),
        name="moe_combine",
    )(dest, x2d, gates_col, ys)


def _moe(x, gain, router_group, router_group_b, router_expert, router_expert_b, w_gate, w_up, w_down):
    B, S, D = x.shape
    T = B * S
    x2d = x.reshape(T, D)
    hn, oi, of, cnt = _route(x2d, gain, router_group, router_group_b, router_expert, router_expert_b)
    rb = MOE_ROW_BLOCK
    counts = cnt[:, 0]
    cap = (counts + rb - 1) // rb * rb
    cap_end = jnp.cumsum(cap)
    start = cap_end - cap
    eid = oi[0:2].T
    first = jnp.sum(jnp.where(eid[:, :, None] == jnp.arange(MOE_EXPERTS), start, 0), axis=-1)
    dest_row = (first + oi[2:4].T).reshape(2 * T).astype(I32)
    dest = dest_row * SLAB
    n_rows = 2 * T + MOE_EXPERTS * rb
    tok_slab = jnp.repeat(jnp.arange(T, dtype=I32) * SLAB, 2)
    row_src = jnp.zeros((n_rows,), I32).at[dest_row].set(tok_slab, unique_indices=True)
    n_blk = n_rows // rb
    blk_first_row = jnp.arange(n_blk, dtype=I32) * rb
    blk_expert = jnp.minimum(jnp.sum(cap_end[None, :] <= blk_first_row[:, None], axis=1), MOE_EXPERTS - 1)
    n_used = (cap_end[-1:] // rb).astype(I32)
    ys = _experts(blk_expert.astype(I32), n_used, row_src, hn, w_gate, w_up, w_down)
    out = _combine(dest, x2d, of[0:2].T, ys)
    return out.reshape(B, S, D)


NSA_TQ = 256
SLC_TQ = 256
NSA_PAIRS = NSA_HPG // 2
N_SEL = 32
NSA_Q_SCALE = NSA_DK ** -0.5 * LOG2E
LO_ONE = LANES - 1
HI_ONE = 0
SLC_TK = 512
WIN_SPAN = WINDOW + SLC_TQ
AUG_POS = N_SEL
MASK_BIG = 1e30


def _pair_norm(t, gain2, lane_lo):
    t2 = t * t
    s_lo = jnp.sum(jnp.where(lane_lo, t2, 0.0), axis=-1, keepdims=True)
    s_hi = jnp.sum(jnp.where(lane_lo, 0.0, t2), axis=-1, keepdims=True)
    inv = jnp.where(lane_lo, lax.rsqrt(s_lo * (1.0 / NSA_DK) + 1e-6), lax.rsqrt(s_hi * (1.0 / NSA_DK) + 1e-6))
    return t * inv * gain2


def _inproj1_kernel(x_ref, prel_ref, blocks_ref, g_ref, win_ref, wvt_ref, qg_ref, ksg_ref, kwg_ref,
                    q_out, kc_out, vc_out, ks_out, vs_out, kw_out, vw_out, gate_out):
    h = _rms(x_ref[...]) * g_ref[...]
    hb = h.astype(BF16)
    proj = _dot(hb, win_ref[...])
    lane = lax.broadcasted_iota(I32, (1, LANES), 1)
    lane_lo = lane < NSA_DK
    nq = NSA_HEADS * NSA_DK
    for p in range(nq // LANES):
        sl = slice(p * LANES, (p + 1) * LANES)
        q_out[:, sl] = (_pair_norm(proj[:, sl], qg_ref[...], lane_lo) * NSA_Q_SCALE).astype(BF16)
    part = lambda i: proj[:, nq + i * LANES:nq + (i + 1) * LANES]
    kc_out[...] = part(0)
    vc_out[...] = part(1)

    def lo_hi(t, lo_pad, hi_pad):
        r = pltpu.roll(t, NSA_DK, 1)
        return jnp.concatenate([jnp.where(lane_lo, t, lo_pad), jnp.where(lane_lo, hi_pad, r),
                                jnp.where(lane_lo, r, lo_pad), jnp.where(lane_lo, hi_pad, t)], axis=1).astype(BF16)

    prel = prel_ref[...]
    byte = lambda k: ((prel >> (8 * k)) & 255).astype(F32)
    half_lane = (lane & (NSA_DK - 1)) - AUG_POS
    kaug = blocks_ref[...] + jnp.where((half_lane == 0) | (half_lane == 3), byte(2),
                                       jnp.where((half_lane == 1) | (half_lane == 4), byte(1),
                                                 jnp.where((half_lane == 2) | (half_lane == 5), byte(0), 0.0)))
    ks_out[...] = lo_hi(_pair_norm(part(2), ksg_ref[...], lane_lo), kaug, kaug)
    kw_out[...] = lo_hi(_pair_norm(part(3), kwg_ref[...], lane_lo), kaug, kaug)
    gate_out[...] = jax.nn.sigmoid(part(4))

    vt = _dot_nt(wvt_ref[...], hb)
    tm = vt.shape[1]
    sub = lax.broadcasted_iota(I32, (NSA_DK, 1), 0)
    pad_lo = jnp.broadcast_to(jnp.where(sub == LO_ONE - NSA_DK, 1.0, 0.0), (NSA_DK, tm))
    pad_hi = jnp.broadcast_to(jnp.where(sub == HI_ONE, 1.0, 0.0), (NSA_DK, tm))
    for branch, out in enumerate((vs_out, vw_out)):
        pieces = []
        for grp in range(NSA_G):
            v = vt[(2 * branch + grp) * NSA_DK:(2 * branch + grp + 1) * NSA_DK, :]
            pieces += [v, pad_lo, pad_hi, v]
        out[...] = jnp.concatenate(pieces, axis=0).astype(BF16)


def _inproj1(x2d, positions, gain, w_in, q_norm, k_norm_slc, k_norm_win, tm=256):
    T, D = x2d.shape
    S = positions.shape[1]
    prel = (positions - positions[:, :1]).reshape(T, 1)
    own_block = np.arange(S)[:, None] // SLC_BLOCK == (np.arange(LANES)[None, :] % NSA_DK)
    blocks = jnp.asarray(own_block * MASK_BIG, F32)
    nq_, kvw = NSA_HEADS * NSA_DK, NSA_KVW
    keep = [w_in[:, :nq_ + 3 * kvw], w_in[:, nq_ + 4 * kvw:nq_ + 5 * kvw], w_in[:, nq_ + 6 * kvw:]]
    w_in_p = jnp.concatenate(keep, axis=1)
    n_in = w_in_p.shape[1]
    n_pad = -n_in % LANES
    w_in_p = jnp.pad(w_in_p, ((0, 0), (0, n_pad))).astype(BF16)
    two = lambda g: jnp.tile(g, 2).reshape(1, LANES)
    row = lambda n: pl.BlockSpec((tm, n), lambda i: (i, 0))
    nq = NSA_HEADS * NSA_DK
    vcols = lambda i: w_in[:, nq + i * NSA_KVW:nq + (i + 1) * NSA_KVW]
    w_vt = jnp.concatenate([vcols(3), vcols(5)], axis=1).T.astype(BF16)
    bf = lambda n: jax.ShapeDtypeStruct((T, n), BF16)
    f32 = lambda n: jax.ShapeDtypeStruct((T, n), F32)
    col = pl.BlockSpec((4 * LANES, tm), lambda i: (0, i))
    vt_shape = jax.ShapeDtypeStruct((4 * LANES, T), BF16)
    return pl.pallas_call(
        _inproj1_kernel,
        grid=(T // tm,),
        in_specs=[row(D), row(1), pl.BlockSpec((tm, LANES), lambda i: (i % (S // tm), 0)), _full((1, D)),
                  _full((D, n_in + n_pad)), _full((2 * NSA_KVW, D)), _full((1, LANES)), _full((1, LANES)),
                  _full((1, LANES))],
        out_specs=[row(nq), row(LANES), row(LANES), row(4 * LANES), col, row(4 * LANES), col, row(LANES)],
        out_shape=[bf(nq), f32(LANES), f32(LANES), bf(4 * LANES), vt_shape, bf(4 * LANES), vt_shape, f32(LANES)],
        compiler_params=_params("parallel"),
        name="inproj1",
    )(x2d, prel, blocks, gain.reshape(1, D), w_in_p, w_vt, two(q_norm), two(k_norm_slc), two(k_norm_win))


def _stack_pairs(q_ref, rows):
    return jnp.concatenate([q_ref[0, rows, p * LANES:(p + 1) * LANES] for p in range(NSA_PAIRS)], axis=0)


def _cmp_kernel(q_ref, kch_ref, vch_ref, pos3_ref, pend_ref, gate_ref, slope_ref,
                pek_ref, w1k_ref, w2k_ref, kcg_ref, pev_ref, w1v_ref, w2v_ref, ovl_ref, eye_ref,
                o_ref, sel_ref, kc_s, vc_s):
    tq = NSA_TQ
    S = q_ref.shape[1]

    def compress(ch_ref, pe_ref, w1_ref, w2_ref):
        a = ch_ref[0, 0]
        h_lo = _dot((a + pe_ref[0:1, :]).astype(BF16), w1_ref[0])
        h_hi = _dot((a + pe_ref[1:2, :]).astype(BF16), w1_ref[1])
        n = h_hi.shape[0]
        hid = jax.nn.gelu(h_lo + pltpu.roll(h_hi, n - 1, 0)).astype(BF16)
        return _dot(hid, w2_ref[0]), _dot(hid, w2_ref[1])

    k_lo, k_hi = compress(kch_ref, pek_ref, w1k_ref, w2k_ref)
    for e, kk in enumerate((k_lo, k_hi)):
        kk = kk * lax.rsqrt(jnp.sum(kk * kk, axis=-1, keepdims=True) * (1.0 / NSA_DK) + 1e-6) * kcg_ref[e:e + 1, :]
        kc_s[e] = kk.astype(BF16)
    v_lo, v_hi = compress(vch_ref, pev_ref, w1v_ref, w2v_ref)
    vc_s[0] = v_lo.T.astype(BF16)
    vc_s[1] = v_hi.T.astype(BF16)

    n_cmp = (S - CMP_BLOCK) // CMP_STRIDE + 1
    n_lanes = NSA_PAIRS * tq
    lane = lax.broadcasted_iota(I32, (1, LANES), 1)
    sub_lo = lax.broadcasted_iota(I32, (LANES, 1), 0) < NSA_DK
    blk_row = lax.broadcasted_iota(I32, (N_SEL, 1), 0)
    cmp_row = lax.broadcasted_iota(I32, (LANES, 1), 0)
    in_range = cmp_row < n_cmp

    def q_block(qi, carry):
        t0 = pl.multiple_of(qi * tq, tq)
        rows = pl.ds(t0, tq)
        qs = _stack_pairs(q_ref, rows)
        pos_q0 = pos3_ref[0, pl.ds(qi * (tq // LANES), 1), :][:, 0:1]
        posrel = (pend_ref[0] - pos_q0).astype(F32)
        t_lane = t0 + (lax.broadcasted_iota(I32, (1, n_lanes), 1) & (tq - 1))
        valid = (t_lane >= CMP_STRIDE * cmp_row + (CMP_BLOCK - 1)) & in_range
        gt = gate_ref[0, 0, :, rows]
        psum = jnp.zeros((LANES, tq), F32)
        outs = []
        for e in range(2):
            slopes = jnp.concatenate([jnp.broadcast_to(slope_ref[0, NSA_PAIRS * e + p:NSA_PAIRS * e + p + 1, 0:1], (1, tq))
                                      for p in range(NSA_PAIRS)], axis=1)
            s = _dot_nt(kc_s[e], qs) + posrel * slopes
            s = jnp.where(valid, s, NEG_INF)
            m = jnp.max(s, axis=0, keepdims=True)
            p = jnp.where(valid, jnp.exp2(s - m), 0.0)
            p = p / jnp.maximum(jnp.sum(p, axis=0, keepdims=True), 1e-20)
            psum = psum + functools.reduce(jnp.add, [p[:, i * tq:(i + 1) * tq] for i in range(NSA_PAIRS)])
            gate = jnp.concatenate([gt[2 * i + e:2 * i + e + 1, :] for i in range(NSA_PAIRS)], axis=1)
            outs.append(_dot(vc_s[e], p.astype(BF16)) * gate)
        z = jnp.where(sub_lo, outs[0], outs[1])
        imp = jnp.dot(ovl_ref[...], psum, preferred_element_type=F32, precision=lax.Precision.HIGHEST)
        cur = (t0 + lax.broadcasted_iota(I32, (1, tq), 1)) // SLC_BLOCK
        forced = (blk_row == 0) | (blk_row == cur) | (blk_row == cur - 1)
        imp = jnp.where(forced, FORCE_SCORE, jnp.where(blk_row <= cur, imp, -1.0))
        rank = jnp.zeros((N_SEL, tq), I32)
        for i in range(N_SEL):
            ri = imp[i:i + 1, :]
            rank = rank + jnp.where((ri > imp) | ((ri == imp) & (blk_row > i)), 1, 0)
        sel_t = jnp.where(rank < SLC_TOP_N, 1.0, 0.0).astype(BF16)
        gap = jnp.zeros((NSA_DK - N_SEL, tq), BF16)
        sel_t = jnp.concatenate([sel_t, gap, sel_t, gap], axis=0)
        sel = _dot_nt(eye_ref[...], sel_t)
        sel_ref[0, 0, rows, :] = jnp.where((lane & (NSA_DK - 1)) < N_SEL, sel - 1.0, 0.0).astype(BF16)
        for p in range(NSA_PAIRS):
            o_ref[0, rows, p * LANES:(p + 1) * LANES] = z[:, p * tq:(p + 1) * tq].T.astype(BF16)
        return carry

    lax.fori_loop(0, S // tq, q_block, 0)


def _lo_hi_cols(w):
    z = jnp.zeros_like(w)
    return jnp.stack([jnp.concatenate([w, z], axis=1), jnp.concatenate([z, w], axis=1)])


def _nsa_cmp(q, kch, vch, pos3, pend, gates_g, slopes_t, pe_k, w1_k, w2_k, k_norm_cmp, pe_v, w1_v, w2_v):
    B, S, _ = q.shape
    G = NSA_G
    nch = S // CMP_STRIDE
    half = CMP_STRIDE * NSA_DK
    pe2 = lambda pe: pe.reshape(2, half)
    w1_2 = lambda w: w.reshape(2, half, CMP_HIDDEN).astype(BF16)
    n_sel = S // SLC_BLOCK
    n_cmp = (S - CMP_BLOCK) // CMP_STRIDE + 1
    cmp_start = np.arange(LANES) * CMP_STRIDE
    slc_start = np.arange(n_sel) * SLC_BLOCK
    overlap = ((cmp_start[None, :] < slc_start[:, None] + SLC_BLOCK) & (cmp_start[None, :] + CMP_BLOCK > slc_start[:, None])
               & (np.arange(LANES)[None, :] < n_cmp))
    ovl = jnp.asarray(overlap, F32)
    eye = jnp.eye(NSA_TQ, dtype=BF16)
    qspec = pl.BlockSpec((1, S, NSA_PAIRS * LANES), lambda b, g: (b, 0, g))
    chspec = pl.BlockSpec((1, 1, nch, half), lambda b, g: (b, g, 0, 0))
    return pl.pallas_call(
        _cmp_kernel,
        grid=(B, G),
        in_specs=[qspec, chspec, chspec,
                  pl.BlockSpec((1, S // LANES, LANES), lambda b, g: (b, 0, 0)),
                  pl.BlockSpec((1, LANES, 1), lambda b, g: (b, 0, 0)),
                  pl.BlockSpec((1, 1, 3 * NSA_HPG, S), lambda b, g: (b, g, 0, 0)),
                  pl.BlockSpec((1, NSA_HPG, LANES), lambda b, g: (g, 0, 0)),
                  _full((2, half)), _full((2, half, CMP_HIDDEN)), _full((2, CMP_HIDDEN, LANES)), _full((2, LANES)),
                  _full((2, half)), _full((2, half, CMP_HIDDEN)), _full((2, CMP_HIDDEN, LANES)),
                  _full((n_sel, LANES)), _full((NSA_TQ, NSA_TQ))],
        out_specs=[qspec, pl.BlockSpec((1, 1, S, LANES), lambda b, g: (b, g, 0, 0))],
        out_shape=[jax.ShapeDtypeStruct((B, S, NSA_HEADS * NSA_DK), BF16),
                   jax.ShapeDtypeStruct((B, G, S, LANES), BF16)],
        scratch_shapes=[pltpu.VMEM((2, nch, LANES), BF16), pltpu.VMEM((2, nch, LANES), BF16)],
        compiler_params=_params("parallel", "parallel"),
        name="nsa_cmp",
    )(q, kch, vch, pos3, pend, gates_g, slopes_t, pe2(pe_k), w1_2(w1_k), _lo_hi_cols(w2_k).astype(BF16),
      _lo_hi_cols(k_norm_cmp.reshape(1, -1)).reshape(2, LANES), pe2(pe_v), w1_2(w1_v),
      _lo_hi_cols(w2_v).astype(BF16), ovl, eye)


def _slc_win_kernel(q_ref, ks_ref, vs_ref, kw_ref, vw_ref, sel_ref, gate_ref, slopeq_ref, ocmp_ref, o_ref):
    tq = SLC_TQ
    S = q_ref.shape[1]
    lane = lax.broadcasted_iota(I32, (1, LANES), 1)
    lane_lo = lane < NSA_DK
    sub_lo = lax.broadcasted_iota(I32, (LANES, 1), 0) < NSA_DK
    ones_row = (LO_ONE, HI_ONE)

    def q_block(qi, carry):
        t0 = pl.multiple_of(qi * tq, tq)
        rows = pl.ds(t0, tq)
        qs = _stack_pairs(q_ref, rows)
        selm1 = sel_ref[0, 0, rows, :]

        def q_aug(e, with_sel):
            feats = [slopeq_ref[0, NSA_PAIRS * e + p:NSA_PAIRS * e + p + 1, :] for p in range(NSA_PAIRS)]
            if with_sel:
                extra = jnp.concatenate([selm1 + f for f in feats], axis=0)
            else:
                extra = jnp.concatenate([jnp.broadcast_to(f, (tq, LANES)) for f in feats], axis=0)
            return jnp.where(lane_lo, qs, extra) if e == 0 else jnp.where(lane_lo, extra, qs)

        def k_aug(k_ref, e, r0, n):
            return k_ref[0, pl.ds(r0, n), e * LANES:(e + 1) * LANES]

        gt = gate_ref[0, 0, :, rows]
        tq_lane = t0 + (lax.broadcasted_iota(I32, (1, NSA_PAIRS * tq), 1) & (tq - 1))

        def gated(accs, branch):
            outs = []
            for e, acc in enumerate(accs):
                gate = jnp.concatenate([gt[branch * NSA_HPG + 2 * p + e:branch * NSA_HPG + 2 * p + e + 1, :]
                                        for p in range(NSA_PAIRS)], axis=1)
                outs.append(acc * (gate / acc[ones_row[e]:ones_row[e] + 1, :]))
            return jnp.where(sub_lo, outs[0], outs[1])

        qa = [q_aug(0, True), q_aug(1, True)]
        krow = lax.broadcasted_iota(I32, (SLC_TK, 1), 0)

        def slc_step(j, state, last):
            r0 = pl.multiple_of(j * SLC_TK, SLC_TK)
            ss = [_dot_nt(k_aug(ks_ref, e, r0, SLC_TK), qa[e]) for e in range(2)]
            out = []
            for e in range(2):
                m, acc = state[e]
                s = jnp.where(r0 + krow <= tq_lane, ss[e], NEG_INF) if last else ss[e]
                m_new = jnp.maximum(m, jnp.max(s, axis=0, keepdims=True))
                p = jnp.exp2(s - m_new).astype(BF16)
                pv = _dot(vs_ref[e * LANES:(e + 1) * LANES, pl.ds(r0, SLC_TK)], p)
                out.append((m_new, jnp.exp2(m - m_new) * acc + pv))
            return tuple(out)

        init = (jnp.full((1, NSA_PAIRS * tq), NEG_INF, F32), jnp.zeros((LANES, NSA_PAIRS * tq), F32))
        n_full = qi // (SLC_TK // tq)
        state = lax.fori_loop(0, n_full, functools.partial(slc_step, last=False), (init, init))
        state = slc_step(n_full, state, True)
        z = gated([state[0][1], state[1][1]], 1)

        w0 = pl.multiple_of(jnp.maximum(t0 - WINDOW, 0), tq)
        rel = tq_lane - (w0 + lax.broadcasted_iota(I32, (WIN_SPAN, 1), 0))
        wmask = (rel >= 0) & (rel < WINDOW)
        ss = [_dot_nt(k_aug(kw_ref, e, w0, WIN_SPAN), q_aug(e, False)) for e in range(2)]
        accs = []
        for e in range(2):
            s = jnp.where(wmask, ss[e], NEG_INF)
            p = jnp.exp2(s - jnp.max(s, axis=0, keepdims=True)).astype(BF16)
            accs.append(_dot(vw_ref[e * LANES:(e + 1) * LANES, pl.ds(w0, WIN_SPAN)], p))
        z = z + gated(accs, 2)

        for p in range(NSA_PAIRS):
            sl = slice(p * LANES, (p + 1) * LANES)
            o_ref[0, rows, sl] = (ocmp_ref[0, rows, sl].astype(F32) + z[:, p * tq:(p + 1) * tq].T).astype(BF16)
        return carry

    lax.fori_loop(0, S // tq, q_block, 0)


def _slope_aug(slopes_eo):
    hi = slopes_eo.astype(BF16).astype(F32)
    lo = (slopes_eo - hi).astype(BF16).astype(F32)
    w = jnp.asarray([65536.0, 256.0, 1.0], F32)
    feats = jnp.concatenate([hi[..., None] * w, lo[..., None] * w], axis=-1)
    half = jnp.pad(feats, ((0, 0), (0, 0), (AUG_POS, NSA_DK - AUG_POS - 6)))
    return jnp.concatenate([half, half], axis=-1).astype(BF16)


def _nsa_slc_win(q, ks4, vst4, kw4, vwt4, sel, gates_t, slopeq, o_cmp):
    B, S, _ = q.shape
    qspec = pl.BlockSpec((1, S, NSA_PAIRS * LANES), lambda b, g: (b, 0, g))
    kvspec = pl.BlockSpec((1, S, 2 * LANES), lambda b, g: (b, 0, g))
    vtspec = pl.BlockSpec((2 * LANES, S), lambda b, g: (g, b))
    return pl.pallas_call(
        _slc_win_kernel,
        grid=(B, NSA_G),
        in_specs=[qspec, kvspec, vtspec, kvspec, vtspec,
                  pl.BlockSpec((1, 1, S, LANES), lambda b, g: (b, g, 0, 0)),
                  pl.BlockSpec((1, 1, 3 * NSA_HPG, S), lambda b, g: (b, g, 0, 0)),
                  pl.BlockSpec((1, NSA_HPG, LANES), lambda b, g: (g, 0, 0)),
                  qspec],
        out_specs=qspec,
        out_shape=jax.ShapeDtypeStruct((B, S, NSA_HEADS * NSA_DK), BF16),
        compiler_params=_params("parallel", "parallel"),
        name="nsa_slc_win",
    )(q, ks4, vst4, kw4, vwt4, sel, gates_t, slopeq, o_cmp)


def _outproj_kernel(o_ref, x_ref, w_ref, out_ref):
    out_ref[...] = x_ref[...] + _dot(o_ref[...], w_ref[...])


def _outproj(o2d, x2d, w_out, tm=512):
    T, D = x2d.shape
    K = o2d.shape[1]
    return pl.pallas_call(
        _outproj_kernel,
        grid=(T // tm,),
        in_specs=[pl.BlockSpec((tm, K), lambda i: (i, 0)), pl.BlockSpec((tm, D), lambda i: (i, 0)), _full((K, D))],
        out_specs=pl.BlockSpec((tm, D), lambda i: (i, 0)),
        out_shape=jax.ShapeDtypeStruct((T, D), F32),
        compiler_params=_params("parallel"),
        name="outproj1",
    )(o2d, x2d, w_out.astype(BF16))


def _mixer1(x, positions, gain, w_in, q_norm, k_norm_cmp, k_norm_slc, k_norm_win,
            pe_k, w1_k, w2_k, pe_v, w1_v, w2_v, w_out):
    B, S, D = x.shape
    T = B * S
    G, HPG, DK = NSA_G, NSA_HPG, NSA_DK
    assert S // SLC_BLOCK == N_SEL and S % NSA_TQ == 0 and NSA_TQ % LANES == 0
    x2d = x.reshape(T, D)
    q, kc, vc, ks4, vs4, kw4, vw4, gates = _inproj1(x2d, positions, gain, w_in, q_norm, k_norm_slc,
                                                    k_norm_win)
    b3 = lambda t: t.reshape(B, S, -1)
    chunks = lambda t: t.reshape(B, S, G, DK).transpose(0, 2, 1, 3).reshape(B, G, S // CMP_STRIDE, CMP_STRIDE * DK)
    gates_t = gates[:, :3 * NSA_HEADS].reshape(B, S, 3, G, HPG).transpose(0, 3, 2, 4, 1).reshape(B, G, 3 * HPG, S)
    n_cmp = (S - CMP_BLOCK) // CMP_STRIDE + 1
    pend = jnp.pad(positions[:, CMP_BLOCK - 1::CMP_STRIDE][:, :n_cmp], ((0, 0), (0, LANES - n_cmp))).reshape(B, LANES, 1)
    pos3 = positions.reshape(B, S // LANES, LANES)
    slopes = jnp.asarray(2.0 ** (-8.0 * np.arange(1, NSA_HEADS + 1) / NSA_HEADS), F32) * LOG2E
    slopes_eo = slopes.reshape(G, NSA_PAIRS, 2).transpose(0, 2, 1).reshape(G, HPG)
    slopes_t = jnp.broadcast_to(slopes_eo[:, :, None], (G, HPG, LANES))
    o_cmp, sel = _nsa_cmp(b3(q), chunks(kc), chunks(vc), pos3, pend, gates_t, slopes_t,
                          pe_k, w1_k, w2_k, k_norm_cmp, pe_v, w1_v, w2_v)
    o = _nsa_slc_win(b3(q), b3(ks4), vs4, b3(kw4), vw4, sel, gates_t, _slope_aug(slopes_eo), o_cmp)
    return _outproj(o.reshape(T, -1), x2d, w_out).reshape(B, S, D)


def kernel(x, positions, norm_mix, norm_ffn, mix0_w_in, mla_q_a_norm, mla_w_uq, mla_kv_a_norm, mla_w_ukv, mla_q_norm, mla_k_norm, conv_dw_w, conv_dw_b, conv_ln_g, conv_ln_b, mix0_w_out, nsa_w_in, nsa_q_norm, nsa_k_norm_cmp, nsa_k_norm_slc, nsa_k_norm_win, nsa_cmp_pe_k, nsa_cmp_w1_k, nsa_cmp_w2_k, nsa_cmp_pe_v, nsa_cmp_w1_v, nsa_cmp_w2_v, nsa_w_out, moe_router_group, moe_router_group_b, moe_router_expert, moe_router_expert_b, moe_w_gate, moe_w_up, moe_w_down):
    def moe(x, layer):
        return _moe(x, norm_ffn[layer], moe_router_group[layer], moe_router_group_b[layer], moe_router_expert[layer],
                    moe_router_expert_b[layer], moe_w_gate[layer], moe_w_up[layer], moe_w_down[layer])

    x = _mixer0(x, positions, norm_mix[0], mix0_w_in[0], mla_q_a_norm[0], mla_w_uq[0], mla_kv_a_norm[0], mla_w_ukv[0],
                mla_q_norm[0], mla_k_norm[0], conv_dw_w[0], conv_dw_b[0], conv_ln_g[0], conv_ln_b[0], mix0_w_out[0])
    x = moe(x, 0)
    x = _mixer1(x, positions, norm_mix[1], nsa_w_in[0], nsa_q_norm[0], nsa_k_norm_cmp[0], nsa_k_norm_slc[0],
                nsa_k_norm_win[0], nsa_cmp_pe_k[0], nsa_cmp_w1_k[0], nsa_cmp_w2_k[0], nsa_cmp_pe_v[0],
                nsa_cmp_w1_v[0], nsa_cmp_w2_v[0], nsa_w_out[0])
    return moe(x, 1)
```

```python
import functools

import numpy as np
import jax
import jax.numpy as jnp
from jax import lax
from jax.experimental import pallas as pl
from jax.experimental.pallas import tpu as pltpu

F32 = jnp.float32
BF16 = jnp.bfloat16
I32 = jnp.int32

LANES = 128
NEG_INF = -1e30
LOG2E = 1.4426950408889634
Q_BLOCK = 128
MLA_HEADS = 8
MLA_Q_LORA = 256
MLA_KV_LORA = 128
MLA_NOPE = 64
MLA_ROPE = 32
MLA_V = 64
MLA_QK = MLA_NOPE + MLA_ROPE
ROPE_THETA = 10000.0
CONV_CH = 512
CONV_W = 31
NSA_HEADS = 16
NSA_G = 2
NSA_HPG = NSA_HEADS // NSA_G
NSA_DK = 64
NSA_KVW = NSA_G * NSA_DK
CMP_BLOCK = 32
CMP_STRIDE = 16
CMP_HIDDEN = 128
SLC_BLOCK = 64
SLC_TOP_N = 8
WINDOW = 256
FORCE_SCORE = 1e4
MOE_GROUPS = 4
MOE_EPG = 8
MOE_EXPERTS = MOE_GROUPS * MOE_EPG
MOE_HIDDEN = 256
MOE_ROW_BLOCK = 256

VMEM_LIMIT = 56 * 1024 * 1024


def _params(*sem):
    return pltpu.CompilerParams(dimension_semantics=sem, vmem_limit_bytes=VMEM_LIMIT)


def _full(shape):
    n = len(shape)
    return pl.BlockSpec(shape, lambda *_: (0,) * n)


def _row_mean(x):
    n = x.shape[-1]
    folded = functools.reduce(jnp.add, [x[:, c:c + LANES] for c in range(0, n, LANES)])
    return jnp.sum(folded, axis=-1, keepdims=True) * (1.0 / n)


def _rms(x, eps=1e-6):
    return x * lax.rsqrt(_row_mean(x * x) + eps)


def _dot(a, b):
    return jnp.dot(a, b, preferred_element_type=F32)


def _group_matrix(groups):
    width = LANES // groups
    same = (lax.broadcasted_iota(I32, (LANES, LANES), 0) // width) == (lax.broadcasted_iota(I32, (LANES, LANES), 1) // width)
    return jnp.where(same, 1.0, 0.0).astype(BF16)


def _group_sums(t, group_matrix):
    return _dot(t.astype(BF16), group_matrix)


def _dot_nt(a, b, **kw):
    return lax.dot_general(a, b, (((1,), (1,)), ((), ())), preferred_element_type=F32, **kw)


def _inproj0_kernel(x_ref, pos_ref, gmix_ref, win_ref, qan_ref, wuq_ref, kvan_ref, wuk_ref, wuv_ref,
                    qg_ref, kg_ref, invf_ref, q_out, k_out, v_out, u_out):
    h = _rms(x_ref[...]) * gmix_ref[...]
    proj = _dot(h.astype(BF16), win_ref[...])
    c_q = proj[:, :MLA_Q_LORA]
    c_kv = proj[:, MLA_Q_LORA:MLA_Q_LORA + MLA_KV_LORA]
    k_rope = proj[:, 384:512]
    a = proj[:, 512:512 + CONV_CH]
    g = proj[:, 512 + CONV_CH:]
    u_out[...] = a * jax.nn.sigmoid(g)
    cqn = (_rms(c_q) * qan_ref[...]).astype(BF16)
    ckvn = (_rms(c_kv) * kvan_ref[...]).astype(BF16)
    q = _dot(cqn, wuq_ref[...])
    kn = _dot(ckvn, wuk_ref[...])
    slot_row = lax.broadcasted_iota(I32, (MLA_HEADS * LANES, 1), 0) & (LANES - 1)
    v_out[...] = (_dot_nt(wuv_ref[...], ckvn) + jnp.where(slot_row == MLA_V, 1.0, 0.0)).astype(BF16)
    ang = invf_ref[...] * pos_ref[...].astype(F32)
    cos_t, sin_t = jnp.cos(ang), jnp.sin(ang)
    tm = ang.shape[1]
    tail = jnp.zeros((LANES - MLA_QK, tm), F32)
    cos = jnp.concatenate([jnp.ones((MLA_NOPE, tm), F32), cos_t, cos_t, tail], axis=0).T
    sin = jnp.concatenate([jnp.zeros((MLA_NOPE, tm), F32), -sin_t, sin_t, tail], axis=0).T
    lane = lax.broadcasted_iota(I32, (1, LANES), 1)
    first_half = (lane >= MLA_NOPE) & (lane < MLA_NOPE + MLA_ROPE // 2)
    scale = MLA_QK ** -0.5 * LOG2E

    all_lanes = _group_matrix(1)

    def norm_rope(t, gain):
        t = t * lax.rsqrt(_group_sums(t * t, all_lanes) * (1.0 / MLA_QK) + 1e-6) * gain
        partner = jnp.where(first_half, pltpu.roll(t, LANES - MLA_ROPE // 2, 1), pltpu.roll(t, MLA_ROPE // 2, 1))
        return t * cos + partner * sin

    for hd in range(MLA_HEADS):
        sl = slice(hd * LANES, (hd + 1) * LANES)
        q_out[:, sl] = (norm_rope(q[:, sl], qg_ref[...]) * scale).astype(BF16)
        k_out[:, sl] = norm_rope(kn[:, sl] + k_rope, kg_ref[...]).astype(BF16)


def _head_slots(w, n_heads, width, offset=0):
    k = w.shape[0]
    w = w.reshape(k, n_heads, width)
    w = jnp.pad(w, ((0, 0), (0, 0), (offset, LANES - width - offset)))
    return w.reshape(k, n_heads * LANES)


def _inproj0(x2d, pos_row, gmix, w_in, q_a_norm, w_uq, kv_a_norm, w_ukv, q_norm, k_norm, tm=512):
    T, D = x2d.shape
    H = MLA_HEADS
    w_krope = jnp.pad(w_in[:, 384:416], ((0, 0), (MLA_NOPE, LANES - MLA_NOPE - MLA_ROPE)))
    w_in_p = jnp.concatenate([w_in[:, :384], w_krope, w_in[:, 416:]], axis=1).astype(BF16)
    w_uq_p = _head_slots(w_uq, H, MLA_QK).astype(BF16)
    w_ukv3 = w_ukv.reshape(MLA_KV_LORA, H, MLA_NOPE + MLA_V)
    w_uk_p = _head_slots(w_ukv3[:, :, :MLA_NOPE].reshape(MLA_KV_LORA, H * MLA_NOPE), H, MLA_NOPE).astype(BF16)
    w_uv = _head_slots(w_ukv3[:, :, MLA_NOPE:].reshape(MLA_KV_LORA, H * MLA_V), H, MLA_V).T.astype(BF16)
    pad = LANES - MLA_QK
    qg = jnp.pad(q_norm, (0, pad)).reshape(1, LANES)
    kg = jnp.pad(k_norm, (0, pad)).reshape(1, LANES)
    half = MLA_ROPE // 2
    inv_freq = (ROPE_THETA ** (-jnp.arange(half, dtype=F32) / half)).reshape(half, 1)
    row = lambda n: pl.BlockSpec((tm, n), lambda i: (i, 0))
    n_in = w_in_p.shape[1]
    return pl.pallas_call(
        _inproj0_kernel,
        grid=(T // tm,),
        in_specs=[row(D), pl.BlockSpec((1, tm), lambda i: (0, i)), _full((1, D)), _full((D, n_in)),
                  _full((1, MLA_Q_LORA)), _full((MLA_Q_LORA, H * LANES)), _full((1, MLA_KV_LORA)),
                  _full((MLA_KV_LORA, H * LANES)), _full((H * LANES, MLA_KV_LORA)), _full((1, LANES)),
                  _full((1, LANES)), _full((half, 1))],
        out_specs=[row(H * LANES), row(H * LANES), pl.BlockSpec((H * LANES, tm), lambda i: (0, i)), row(CONV_CH)],
        out_shape=[jax.ShapeDtypeStruct((T, H * LANES), BF16), jax.ShapeDtypeStruct((T, H * LANES), BF16),
                   jax.ShapeDtypeStruct((H * LANES, T), BF16), jax.ShapeDtypeStruct((T, CONV_CH), F32)],
        compiler_params=_params("parallel"),
        name="inproj0",
    )(x2d, pos_row, gmix.reshape(1, D), w_in_p, q_a_norm.reshape(1, -1), w_uq_p, kv_a_norm.reshape(1, -1),
      w_uk_p, w_uv, qg, kg, inv_freq)


MLA_TQ = 512
MLA_TK = 512
MLA_ONE = MLA_V


def _mla_attn_kernel(q_ref, k_ref, vt_ref, o_ref):
    S = q_ref.shape[1]
    tq, tk = MLA_TQ, MLA_TK
    krow = lax.broadcasted_iota(I32, (tk, 1), 0)
    qcol = lax.broadcasted_iota(I32, (1, tq), 1)

    def q_block(qi, carry):
        q0 = pl.multiple_of(qi * tq, tq)
        qs = [q_ref[0, pl.ds(q0, tq), hh * LANES:(hh + 1) * LANES] for hh in range(2)]

        def kv_step(j, state, masked):
            k0 = pl.multiple_of(j * tk, tk)
            hs = [slice(hh * LANES, (hh + 1) * LANES) for hh in range(2)]
            ss = [_dot_nt(k_ref[0, pl.ds(k0, tk), hs[hh]], qs[hh]) for hh in range(2)]
            out = []
            for hh in range(2):
                m, acc = state[hh]
                s = jnp.where(k0 + krow <= q0 + qcol, ss[hh], NEG_INF) if masked else ss[hh]
                m_new = jnp.maximum(m, jnp.max(s, axis=0, keepdims=True))
                p = jnp.exp2(s - m_new).astype(BF16)
                out.append((m_new, jnp.exp2(m - m_new) * acc + _dot(vt_ref[hs[hh], pl.ds(k0, tk)], p)))
            return tuple(out)

        init = (jnp.full((1, tq), NEG_INF, F32), jnp.zeros((LANES, tq), F32))
        n_full = (qi * tq) // tk
        state = lax.fori_loop(0, n_full, functools.partial(kv_step, masked=False), (init, init))
        state = kv_step(n_full, state, True)
        outs = []
        for hh in range(2):
            acc = state[hh][1]
            outs.append((acc * (1.0 / acc[MLA_ONE:MLA_ONE + 1, :]))[:MLA_V, :])
        o_ref[0, pl.ds(q0, tq), :] = jnp.concatenate(outs, axis=0).T.astype(BF16)
        return carry

    lax.fori_loop(0, S // tq, q_block, 0)


def _mla_attn(q, k, vt):
    B, S, _ = q.shape
    spec = pl.BlockSpec((1, S, 2 * LANES), lambda b, h: (b, 0, h))
    return pl.pallas_call(
        _mla_attn_kernel,
        grid=(B, MLA_HEADS // 2),
        in_specs=[spec, spec, pl.BlockSpec((2 * LANES, S), lambda b, h: (h, b))],
        out_specs=pl.BlockSpec((1, S, 2 * MLA_V), lambda b, h: (b, 0, h)),
        out_shape=jax.ShapeDtypeStruct((B, S, MLA_HEADS * MLA_V), BF16),
        compiler_params=_params("parallel", "parallel"),
        name="mla_attn",
    )(q, k, vt)


CONV_TILE = 64
CONV_PAD = 32


CONV_ROWS = 512


def _conv_out0_kernel(u_ref, o_ref, x_ref, dww_ref, dwb_ref, lng_ref, lnb_ref, wo_ref, out_ref, upad, act, shifted):
    step = pl.program_id(1)

    @pl.when(step == 0)
    def _():
        upad[0:CONV_PAD, :] = jnp.zeros((CONV_PAD, CONV_CH), F32)
        upad[CONV_PAD:, :] = u_ref[0]

    lead = CONV_PAD - (CONV_W - 1)
    base = step * CONV_ROWS

    def tile(i, carry):
        t0 = pl.multiple_of(i * CONV_TILE, CONV_TILE)
        win = upad[pl.ds(pl.multiple_of(base + t0, CONV_TILE), CONV_TILE + CONV_PAD), :]
        acc = jnp.zeros((CONV_TILE, CONV_CH), F32) + dwb_ref[...]
        span = CONV_TILE + CONV_PAD - 8
        for r in range(1, 8):
            shifted[r - 1] = win[r:r + span, :]
        for j in range(CONV_W):
            r, k = (lead + j) % 8, (lead + j) // 8 * 8
            tap = win[k:k + CONV_TILE, :] if r == 0 else shifted[r - 1, k:k + CONV_TILE, :]
            acc = acc + tap * dww_ref[j:j + 1, :]
        xc = acc - _row_mean(acc)
        var = _row_mean(xc * xc)
        y = xc * lax.rsqrt(var + 1e-5) * lng_ref[...] + lnb_ref[...]
        act[pl.ds(t0, CONV_TILE), :] = (y * jax.nn.sigmoid(y)).astype(BF16)
        return carry

    lax.fori_loop(0, CONV_ROWS // CONV_TILE, tile, 0)
    n_o = o_ref.shape[2]
    y = _dot(o_ref[0], wo_ref[:n_o, :]) + _dot(act[...], wo_ref[n_o:, :])
    out_ref[0] = x_ref[0] + y


def _conv_out0(u, o_mla, x, dw_w, dw_b, ln_g, ln_b, w_out):
    B, S, D = x.shape
    n_o = o_mla.shape[2]
    w_out_p = w_out.astype(BF16)
    tspec = lambda n: pl.BlockSpec((1, CONV_ROWS, n), lambda b, t: (b, t, 0))
    return pl.pallas_call(
        _conv_out0_kernel,
        grid=(B, S // CONV_ROWS),
        in_specs=[pl.BlockSpec((1, S, CONV_CH), lambda b, t: (b, 0, 0)), tspec(n_o), tspec(D),
                  _full((CONV_W, CONV_CH)), _full((1, CONV_CH)), _full((1, CONV_CH)), _full((1, CONV_CH)),
                  _full((n_o + CONV_CH, D))],
        out_specs=tspec(D),
        out_shape=jax.ShapeDtypeStruct((B, S, D), F32),
        scratch_shapes=[pltpu.VMEM((S + CONV_PAD, CONV_CH), F32), pltpu.VMEM((CONV_ROWS, CONV_CH), BF16),
                        pltpu.VMEM((7, CONV_TILE + CONV_PAD - 8, CONV_CH), F32)],
        compiler_params=_params("parallel", "arbitrary"),
        name="conv_out0",
    )(u, o_mla, x, dw_w, dw_b.reshape(1, -1), ln_g.reshape(1, -1), ln_b.reshape(1, -1), w_out_p)


def _mixer0(x, positions, gmix, w_in, q_a_norm, w_uq, kv_a_norm, w_ukv, q_norm, k_norm,
            dw_w, dw_b, ln_g, ln_b, w_out):
    B, S, D = x.shape
    T = B * S
    q, k, v, u = _inproj0(x.reshape(T, D), positions.reshape(1, T), gmix, w_in, q_a_norm, w_uq, kv_a_norm, w_ukv,
                          q_norm, k_norm)
    o = _mla_attn(q.reshape(B, S, -1), k.reshape(B, S, -1), v)
    return _conv_out0(u.reshape(B, S, -1), o, x, dw_w, dw_b, ln_g, ln_b, w_out)


ROUTE_TILE = 512
ROUTE_ROWS = 40


def _route_kernel(x_ref, g_ref, wr_ref, br_ref, tri_ref, hn_ref, oi_ref, of_ref, cnt_ref, carry):
    @pl.when(pl.program_id(0) == 0)
    def _():
        carry[...] = jnp.zeros_like(carry)

    hn = _rms(x_ref[...]) * g_ref[...]
    _slab_store(hn_ref, hn)
    tm = hn.shape[0]
    logits = _dot_nt(wr_ref[...], hn, precision=lax.Precision.HIGHEST) + br_ref[...]
    gl = logits[MOE_EXPERTS:MOE_EXPERTS + MOE_GROUPS]
    rid_g = lax.broadcasted_iota(I32, (MOE_GROUPS, tm), 0)
    gmax = jnp.max(gl, axis=0, keepdims=True)
    grp = jnp.min(jnp.where(gl == gmax, rid_g, MOE_GROUPS), axis=0, keepdims=True)
    g_w = 1.0 / jnp.sum(jnp.exp(gl - gmax), axis=0, keepdims=True)
    e_in = jnp.zeros((MOE_EPG, tm), F32)
    for gi in range(MOE_GROUPS):
        e_in = jnp.where(grp == gi, logits[gi * MOE_EPG:(gi + 1) * MOE_EPG], e_in)
    rid_e = lax.broadcasted_iota(I32, (MOE_EPG, tm), 0)
    v1 = jnp.max(e_in, axis=0, keepdims=True)
    i1 = jnp.min(jnp.where(e_in == v1, rid_e, MOE_EPG), axis=0, keepdims=True)
    rest = jnp.where(rid_e == i1, -jnp.inf, e_in)
    v2 = jnp.max(rest, axis=0, keepdims=True)
    i2 = jnp.min(jnp.where(rest == v2, rid_e, MOE_EPG), axis=0, keepdims=True)
    ex = jnp.exp(v2 - v1)
    den = 1.0 + ex
    e1 = grp * MOE_EPG + i1
    e2 = grp * MOE_EPG + i2
    rid = lax.broadcasted_iota(I32, (MOE_EXPERTS, tm), 0)
    hit1 = rid == e1
    hit2 = rid == e2
    member = jnp.where(hit1 | hit2, 1.0, 0.0)
    before = _dot(member.astype(BF16), tri_ref[...]) + carry[...]
    r1 = jnp.sum(jnp.where(hit1, before, 0.0), axis=0, keepdims=True)
    r2 = jnp.sum(jnp.where(hit2, before, 0.0), axis=0, keepdims=True)
    carry[...] = carry[...] + jnp.sum(member, axis=1, keepdims=True)
    oi_ref[...] = jnp.zeros_like(oi_ref)
    oi_ref[0:1, :] = e1
    oi_ref[1:2, :] = e2
    oi_ref[2:3, :] = r1.astype(I32)
    oi_ref[3:4, :] = r2.astype(I32)
    of_ref[...] = jnp.zeros_like(of_ref)
    of_ref[0:1, :] = g_w / den
    of_ref[1:2, :] = g_w * ex / den
    cnt_ref[...] = jnp.broadcast_to(carry[...], cnt_ref.shape).astype(I32)


def _route(x2d, gain, router_group, router_group_b, router_expert, router_expert_b):
    T, D = x2d.shape
    tm = ROUTE_TILE
    pad = ROUTE_ROWS - MOE_EXPERTS - MOE_GROUPS
    wr = jnp.concatenate([router_expert.T, router_group.T, jnp.zeros((pad, D), F32)], axis=0)
    br = jnp.concatenate([router_expert_b, router_group_b, jnp.zeros((pad,), F32)]).reshape(ROUTE_ROWS, 1)
    tri = (jnp.arange(tm)[:, None] < jnp.arange(tm)[None, :]).astype(BF16)
    return pl.pallas_call(
        _route_kernel,
        grid=(T // tm,),
        in_specs=[pl.BlockSpec((tm, D), lambda i: (i, 0)), _full((1, D)), _full((ROUTE_ROWS, D)),
                  _full((ROUTE_ROWS, 1)), _full((tm, tm))],
        out_specs=[pl.BlockSpec((tm * SLAB, LANES), lambda i: (i, 0)), pl.BlockSpec((8, tm), lambda i: (0, i)),
                   pl.BlockSpec((8, tm), lambda i: (0, i)), _full((MOE_EXPERTS, LANES))],
        out_shape=[jax.ShapeDtypeStruct((T * SLAB, LANES), F32), jax.ShapeDtypeStruct((8, T), I32),
                   jax.ShapeDtypeStruct((8, T), F32), jax.ShapeDtypeStruct((MOE_EXPERTS, LANES), I32)],
        scratch_shapes=[pltpu.VMEM((MOE_EXPERTS, 1), F32)],
        compiler_params=_params("arbitrary"),
        name="moe_route",
    )(x2d, gain.reshape(1, D), wr, br, tri)


MOVE_CHUNK = 512
MOVE_UNROLL = 8


SLAB = 8


def _slab_load(ref, n, first=0, stride=SLAB):
    return jnp.concatenate([ref[pl.ds(first + j, n, stride=stride), :] for j in range(SLAB)], axis=1)


def _slab_store(ref, value):
    n = value.shape[0]
    for j in range(SLAB):
        ref[pl.ds(j, n, stride=SLAB), :] = value[:, j * LANES:(j + 1) * LANES]


def _slab(row):
    return pl.ds(pl.multiple_of(row, SLAB), SLAB)


def _for_tokens(n, fn):
    def body(t, carry):
        fn(t, 0)
        fn(t, 1)
        return carry
    lax.fori_loop(0, n, body, 0, unroll=MOVE_UNROLL)


def _dispatch_kernel(dest_ref, hn_ref, xs_init_ref, xs_ref, sem):
    del xs_init_ref
    base = pl.program_id(0) * (2 * MOVE_CHUNK)

    def copy(t, k):
        return pltpu.make_async_copy(hn_ref.at[_slab(t * SLAB)], xs_ref.at[_slab(dest_ref[base + 2 * t + k])], sem)

    _for_tokens(MOVE_CHUNK, lambda t, k: copy(t, k).start())
    _for_tokens(MOVE_CHUNK, lambda t, k: copy(t, k).wait())


def _dispatch(dest, hn, n_rows):
    T = hn.shape[0] // SLAB
    tm = MOVE_CHUNK
    return pl.pallas_call(
        _dispatch_kernel,
        grid_spec=pltpu.PrefetchScalarGridSpec(
            num_scalar_prefetch=1,
            grid=(T // tm,),
            in_specs=[pl.BlockSpec((tm * SLAB, LANES), lambda i, d: (i, 0)), pl.BlockSpec(memory_space=pl.ANY)],
            out_specs=pl.BlockSpec(memory_space=pl.ANY),
            scratch_shapes=[pltpu.SemaphoreType.DMA(())],
        ),
        out_shape=jax.ShapeDtypeStruct((n_rows * SLAB, LANES), F32),
        input_output_aliases={2: 0},
        compiler_params=_params("arbitrary"),
        name="moe_dispatch",
    )(dest, hn, jnp.zeros((n_rows * SLAB, LANES), F32))


def _expert_kernel(blk_e_ref, n_used_ref, xs_ref, wg_ref, wu_ref, wd_ref, ys_ref, wg_s, wu_s, wd_s):
    b = pl.program_id(0)

    @pl.when((b == 0) | (blk_e_ref[b] != blk_e_ref[jnp.maximum(b - 1, 0)]))
    def _():
        wg_s[...] = wg_ref[0].astype(BF16)
        wu_s[...] = wu_ref[0].astype(BF16)
        wd_s[...] = wd_ref[0].astype(BF16)

    @pl.when(b < n_used_ref[0])
    def _():
        x = _slab_load(xs_ref, MOE_ROW_BLOCK).astype(BF16)
        gate = _dot(x, wg_s[...])
        hid = (gate * jax.nn.sigmoid(gate) * _dot(x, wu_s[...])).astype(BF16)
        _slab_store(ys_ref, _dot(hid, wd_s[...]))

    @pl.when(b >= n_used_ref[0])
    def _():
        ys_ref[...] = jnp.zeros_like(ys_ref)


def _experts(blk_expert, n_used, xs, w_gate, w_up, w_down, layer):
    R = xs.shape[0] // SLAB
    rb = MOE_ROW_BLOCK
    D = w_gate.shape[2]
    up_spec = pl.BlockSpec((None, 1, D, MOE_HIDDEN), lambda b, be, nu: (layer, be[b], 0, 0))
    return pl.pallas_call(
        _expert_kernel,
        grid_spec=pltpu.PrefetchScalarGridSpec(
            num_scalar_prefetch=2,
            grid=(R // rb,),
            in_specs=[pl.BlockSpec((rb * SLAB, LANES), lambda b, be, nu: (jnp.minimum(b, nu[0] - 1), 0)),
                      up_spec, up_spec,
                      pl.BlockSpec((None, 1, MOE_HIDDEN, D), lambda b, be, nu: (layer, be[b], 0, 0))],
            out_specs=pl.BlockSpec((rb * SLAB, LANES), lambda b, be, nu: (b, 0)),
            scratch_shapes=[pltpu.VMEM((D, MOE_HIDDEN), BF16), pltpu.VMEM((D, MOE_HIDDEN), BF16),
                            pltpu.VMEM((MOE_HIDDEN, D), BF16)],
        ),
        out_shape=jax.ShapeDtypeStruct((R * SLAB, LANES), F32),
        compiler_params=_params("arbitrary"),
        name="moe_experts",
    )(blk_expert, n_used, xs, w_gate, w_up, w_down)


def _combine_kernel(dest_ref, x_ref, gate_ref, ys_ref, out_ref, buf, sem):
    tm = x_ref.shape[0]
    step = pl.program_id(0)

    def copy(s, t, k):
        a = 2 * t + k
        slot = s % 2
        return pltpu.make_async_copy(ys_ref.at[_slab(dest_ref[s * (2 * tm) + a])], buf.at[slot, _slab(a * SLAB)],
                                     sem.at[slot])

    @pl.when(step == 0)
    def _():
        _for_tokens(tm, lambda t, k: copy(step, t, k).start())

    @pl.when(step + 1 < pl.num_programs(0))
    def _():
        _for_tokens(tm, lambda t, k: copy(step + 1, t, k).start())

    _for_tokens(tm, lambda t, k: copy(step, t, k).wait())
    mine = buf.at[step % 2]
    y0 = _slab_load(mine, tm, 0, 2 * SLAB)
    y1 = _slab_load(mine, tm, SLAB, 2 * SLAB)
    out_ref[...] = x_ref[...] + gate_ref[:, 0:1] * y0 + gate_ref[:, 1:2] * y1


def _combine(dest, x2d, gates_col, ys, tm=MOVE_CHUNK):
    T, D = x2d.shape
    return pl.pallas_call(
        _combine_kernel,
        grid_spec=pltpu.PrefetchScalarGridSpec(
            num_scalar_prefetch=1,
            grid=(T // tm,),
            in_specs=[pl.BlockSpec((tm, D), lambda i, d: (i, 0)), pl.BlockSpec((tm, 2), lambda i, d: (i, 0)),
                      pl.BlockSpec(memory_space=pl.ANY)],
            out_specs=pl.BlockSpec((tm, D), lambda i, d: (i, 0)),
            scratch_shapes=[pltpu.VMEM((2, tm * 2 * SLAB, LANES), F32), pltpu.SemaphoreType.DMA((2,))],
        ),
        out_shape=jax.ShapeDtypeStruct((T, D), F32),
        compiler_params=_params("arbitrary"),
        name="moe_combine",
    )(dest, x2d, gates_col, ys)


def _moe(x, gain, router_group, router_group_b, router_expert, router_expert_b, w_gate, w_up, w_down, layer):
    B, S, D = x.shape
    T = B * S
    x2d = x.reshape(T, D)
    hn, oi, of, cnt = _route(x2d, gain, router_group, router_group_b, router_expert, router_expert_b)
    rb = MOE_ROW_BLOCK
    counts = cnt[:, 0]
    cap = (counts + rb - 1) // rb * rb
    cap_end = jnp.cumsum(cap)
    start = cap_end - cap
    eid = oi[0:2].T
    first = jnp.sum(jnp.where(eid[:, :, None] == jnp.arange(MOE_EXPERTS), start, 0), axis=-1)
    dest = ((first + oi[2:4].T) * SLAB).reshape(2 * T).astype(I32)
    n_rows = 2 * T + MOE_EXPERTS * rb
    n_blk = n_rows // rb
    blk_first_row = jnp.arange(n_blk, dtype=I32) * rb
    blk_expert = jnp.minimum(jnp.sum(cap_end[None, :] <= blk_first_row[:, None], axis=1), MOE_EXPERTS - 1)
    n_used = (cap_end[-1:] // rb).astype(I32)
    xs = _dispatch(dest, hn, n_rows)
    ys = _experts(blk_expert.astype(I32), n_used, xs, w_gate, w_up, w_down, layer)
    out = _combine(dest, x2d, of[0:2].T, ys)
    return out.reshape(B, S, D)


NSA_TQ = 256
SLC_TQ = 256
NSA_PAIRS = NSA_HPG // 2
N_SEL = 32
NSA_Q_SCALE = NSA_DK ** -0.5 * LOG2E
LO_ONE = LANES - 1
HI_ONE = 0
SLC_TK = 512
WIN_SPAN = WINDOW + SLC_TQ
AUG_POS = N_SEL
MASK_BIG = 1e30


def _pair_norm(t, gain2, lane_lo):
    t2 = t * t
    s_lo = jnp.sum(jnp.where(lane_lo, t2, 0.0), axis=-1, keepdims=True)
    s_hi = jnp.sum(jnp.where(lane_lo, 0.0, t2), axis=-1, keepdims=True)
    inv = jnp.where(lane_lo, lax.rsqrt(s_lo * (1.0 / NSA_DK) + 1e-6), lax.rsqrt(s_hi * (1.0 / NSA_DK) + 1e-6))
    return t * inv * gain2


def _inproj1_kernel(x_ref, prel_ref, blocks_ref, g_ref, win_ref, wvt_ref, qg_ref, ksg_ref, kwg_ref,
                    q_out, kc_out, vc_out, ks_out, vs_out, kw_out, vw_out, gate_out):
    h = _rms(x_ref[...]) * g_ref[...]
    hb = h.astype(BF16)
    proj = _dot(hb, win_ref[...])
    lane = lax.broadcasted_iota(I32, (1, LANES), 1)
    lane_lo = lane < NSA_DK
    nq = NSA_HEADS * NSA_DK
    for p in range(nq // LANES):
        sl = slice(p * LANES, (p + 1) * LANES)
        q_out[:, sl] = (_pair_norm(proj[:, sl], qg_ref[...], lane_lo) * NSA_Q_SCALE).astype(BF16)
    part = lambda i: proj[:, nq + i * LANES:nq + (i + 1) * LANES]
    kc_out[...] = part(0)
    vc_out[...] = part(1)

    def lo_hi(t, lo_pad, hi_pad):
        r = pltpu.roll(t, NSA_DK, 1)
        return jnp.concatenate([jnp.where(lane_lo, t, lo_pad), jnp.where(lane_lo, hi_pad, r),
                                jnp.where(lane_lo, r, lo_pad), jnp.where(lane_lo, hi_pad, t)], axis=1).astype(BF16)

    prel = prel_ref[...]
    byte = lambda k: ((prel >> (8 * k)) & 255).astype(F32)
    half_lane = (lane & (NSA_DK - 1)) - AUG_POS
    kaug = blocks_ref[...] + jnp.where((half_lane == 0) | (half_lane == 3), byte(2),
                                       jnp.where((half_lane == 1) | (half_lane == 4), byte(1),
                                                 jnp.where((half_lane == 2) | (half_lane == 5), byte(0), 0.0)))
    ks_out[...] = lo_hi(_pair_norm(part(2), ksg_ref[...], lane_lo), kaug, kaug)
    kw_out[...] = lo_hi(_pair_norm(part(3), kwg_ref[...], lane_lo), kaug, kaug)
    gate_out[...] = jax.nn.sigmoid(part(4))

    vt = _dot_nt(wvt_ref[...], hb)
    tm = vt.shape[1]
    sub = lax.broadcasted_iota(I32, (NSA_DK, 1), 0)
    pad_lo = jnp.broadcast_to(jnp.where(sub == LO_ONE - NSA_DK, 1.0, 0.0), (NSA_DK, tm))
    pad_hi = jnp.broadcast_to(jnp.where(sub == HI_ONE, 1.0, 0.0), (NSA_DK, tm))
    for branch, out in enumerate((vs_out, vw_out)):
        pieces = []
        for grp in range(NSA_G):
            v = vt[(2 * branch + grp) * NSA_DK:(2 * branch + grp + 1) * NSA_DK, :]
            pieces += [v, pad_lo, pad_hi, v]
        out[...] = jnp.concatenate(pieces, axis=0).astype(BF16)


def _inproj1(x2d, positions, gain, w_in, q_norm, k_norm_slc, k_norm_win, tm=256):
    T, D = x2d.shape
    S = positions.shape[1]
    prel = (positions - positions[:, :1]).reshape(T, 1)
    own_block = np.arange(S)[:, None] // SLC_BLOCK == (np.arange(LANES)[None, :] % NSA_DK)
    blocks = jnp.asarray(own_block * MASK_BIG, F32)
    nq_, kvw = NSA_HEADS * NSA_DK, NSA_KVW
    keep = [w_in[:, :nq_ + 3 * kvw], w_in[:, nq_ + 4 * kvw:nq_ + 5 * kvw], w_in[:, nq_ + 6 * kvw:]]
    w_in_p = jnp.concatenate(keep, axis=1)
    n_in = w_in_p.shape[1]
    n_pad = -n_in % LANES
    w_in_p = jnp.pad(w_in_p, ((0, 0), (0, n_pad))).astype(BF16)
    two = lambda g: jnp.tile(g, 2).reshape(1, LANES)
    row = lambda n: pl.BlockSpec((tm, n), lambda i: (i, 0))
    nq = NSA_HEADS * NSA_DK
    vcols = lambda i: w_in[:, nq + i * NSA_KVW:nq + (i + 1) * NSA_KVW]
    w_vt = jnp.concatenate([vcols(3), vcols(5)], axis=1).T.astype(BF16)
    bf = lambda n: jax.ShapeDtypeStruct((T, n), BF16)
    f32 = lambda n: jax.ShapeDtypeStruct((T, n), F32)
    col = pl.BlockSpec((4 * LANES, tm), lambda i: (0, i))
    vt_shape = jax.ShapeDtypeStruct((4 * LANES, T), BF16)
    return pl.pallas_call(
        _inproj1_kernel,
        grid=(T // tm,),
        in_specs=[row(D), row(1), pl.BlockSpec((tm, LANES), lambda i: (i % (S // tm), 0)), _full((1, D)),
                  _full((D, n_in + n_pad)), _full((2 * NSA_KVW, D)), _full((1, LANES)), _full((1, LANES)),
                  _full((1, LANES))],
        out_specs=[row(nq), row(LANES), row(LANES), row(4 * LANES), col, row(4 * LANES), col, row(LANES)],
        out_shape=[bf(nq), f32(LANES), f32(LANES), bf(4 * LANES), vt_shape, bf(4 * LANES), vt_shape, f32(LANES)],
        compiler_params=_params("parallel"),
        name="inproj1",
    )(x2d, prel, blocks, gain.reshape(1, D), w_in_p, w_vt, two(q_norm), two(k_norm_slc), two(k_norm_win))


def _stack_pairs(q_ref, rows):
    return jnp.concatenate([q_ref[0, rows, p * LANES:(p + 1) * LANES] for p in range(NSA_PAIRS)], axis=0)


def _cmp_kernel(q_ref, kch_ref, vch_ref, pos3_ref, pend_ref, gate_ref, slope_ref,
                pek_ref, w1k_ref, w2k_ref, kcg_ref, pev_ref, w1v_ref, w2v_ref, ovl_ref, eye_ref,
                o_ref, sel_ref, kc_s, vc_s):
    tq = NSA_TQ
    S = q_ref.shape[1]

    def compress(ch_ref, pe_ref, w1_ref, w2_ref):
        a = ch_ref[0, 0]
        h_lo = _dot((a + pe_ref[0:1, :]).astype(BF16), w1_ref[0])
        h_hi = _dot((a + pe_ref[1:2, :]).astype(BF16), w1_ref[1])
        n = h_hi.shape[0]
        hid = jax.nn.gelu(h_lo + pltpu.roll(h_hi, n - 1, 0)).astype(BF16)
        return _dot(hid, w2_ref[0]), _dot(hid, w2_ref[1])

    k_lo, k_hi = compress(kch_ref, pek_ref, w1k_ref, w2k_ref)
    for e, kk in enumerate((k_lo, k_hi)):
        kk = kk * lax.rsqrt(jnp.sum(kk * kk, axis=-1, keepdims=True) * (1.0 / NSA_DK) + 1e-6) * kcg_ref[e:e + 1, :]
        kc_s[e] = kk.astype(BF16)
    v_lo, v_hi = compress(vch_ref, pev_ref, w1v_ref, w2v_ref)
    vc_s[0] = v_lo.T.astype(BF16)
    vc_s[1] = v_hi.T.astype(BF16)

    n_cmp = (S - CMP_BLOCK) // CMP_STRIDE + 1
    n_lanes = NSA_PAIRS * tq
    lane = lax.broadcasted_iota(I32, (1, LANES), 1)
    sub_lo = lax.broadcasted_iota(I32, (LANES, 1), 0) < NSA_DK
    blk_row = lax.broadcasted_iota(I32, (N_SEL, 1), 0)
    cmp_row = lax.broadcasted_iota(I32, (LANES, 1), 0)
    in_range = cmp_row < n_cmp

    def q_block(qi, carry):
        t0 = pl.multiple_of(qi * tq, tq)
        rows = pl.ds(t0, tq)
        qs = _stack_pairs(q_ref, rows)
        pos_q0 = pos3_ref[0, pl.ds(qi * (tq // LANES), 1), :][:, 0:1]
        posrel = (pend_ref[0] - pos_q0).astype(F32)
        t_lane = t0 + (lax.broadcasted_iota(I32, (1, n_lanes), 1) & (tq - 1))
        valid = (t_lane >= CMP_STRIDE * cmp_row + (CMP_BLOCK - 1)) & in_range
        gt = gate_ref[0, 0, :, rows]
        psum = jnp.zeros((LANES, tq), F32)
        outs = []
        for e in range(2):
            slopes = jnp.concatenate([jnp.broadcast_to(slope_ref[0, NSA_PAIRS * e + p:NSA_PAIRS * e + p + 1, 0:1], (1, tq))
                                      for p in range(NSA_PAIRS)], axis=1)
            s = _dot_nt(kc_s[e], qs) + posrel * slopes
            s = jnp.where(valid, s, NEG_INF)
            m = jnp.max(s, axis=0, keepdims=True)
            p = jnp.where(valid, jnp.exp2(s - m), 0.0)
            p = p / jnp.maximum(jnp.sum(p, axis=0, keepdims=True), 1e-20)
            psum = psum + functools.reduce(jnp.add, [p[:, i * tq:(i + 1) * tq] for i in range(NSA_PAIRS)])
            gate = jnp.concatenate([gt[2 * i + e:2 * i + e + 1, :] for i in range(NSA_PAIRS)], axis=1)
            outs.append(_dot(vc_s[e], p.astype(BF16)) * gate)
        z = jnp.where(sub_lo, outs[0], outs[1])
        imp = jnp.dot(ovl_ref[...], psum, preferred_element_type=F32, precision=lax.Precision.HIGHEST)
        cur = (t0 + lax.broadcasted_iota(I32, (1, tq), 1)) // SLC_BLOCK
        forced = (blk_row == 0) | (blk_row == cur) | (blk_row == cur - 1)
        imp = jnp.where(forced, FORCE_SCORE, jnp.where(blk_row <= cur, imp, -1.0))
        rank = jnp.zeros((N_SEL, tq), I32)
        for i in range(N_SEL):
            ri = imp[i:i + 1, :]
            rank = rank + jnp.where((ri > imp) | ((ri == imp) & (blk_row > i)), 1, 0)
        sel_t = jnp.where(rank < SLC_TOP_N, 1.0, 0.0).astype(BF16)
        gap = jnp.zeros((NSA_DK - N_SEL, tq), BF16)
        sel_t = jnp.concatenate([sel_t, gap, sel_t, gap], axis=0)
        sel = _dot_nt(eye_ref[...], sel_t)
        sel_ref[0, 0, rows, :] = jnp.where((lane & (NSA_DK - 1)) < N_SEL, sel - 1.0, 0.0).astype(BF16)
        for p in range(NSA_PAIRS):
            o_ref[0, rows, p * LANES:(p + 1) * LANES] = z[:, p * tq:(p + 1) * tq].T.astype(BF16)
        return carry

    lax.fori_loop(0, S // tq, q_block, 0)


def _lo_hi_cols(w):
    z = jnp.zeros_like(w)
    return jnp.stack([jnp.concatenate([w, z], axis=1), jnp.concatenate([z, w], axis=1)])


def _nsa_cmp(q, kch, vch, pos3, pend, gates_g, slopes_t, pe_k, w1_k, w2_k, k_norm_cmp, pe_v, w1_v, w2_v):
    B, S, _ = q.shape
    G = NSA_G
    nch = S // CMP_STRIDE
    half = CMP_STRIDE * NSA_DK
    pe2 = lambda pe: pe.reshape(2, half)
    w1_2 = lambda w: w.reshape(2, half, CMP_HIDDEN).astype(BF16)
    n_sel = S // SLC_BLOCK
    n_cmp = (S - CMP_BLOCK) // CMP_STRIDE + 1
    cmp_start = np.arange(LANES) * CMP_STRIDE
    slc_start = np.arange(n_sel) * SLC_BLOCK
    overlap = ((cmp_start[None, :] < slc_start[:, None] + SLC_BLOCK) & (cmp_start[None, :] + CMP_BLOCK > slc_start[:, None])
               & (np.arange(LANES)[None, :] < n_cmp))
    ovl = jnp.asarray(overlap, F32)
    eye = jnp.eye(NSA_TQ, dtype=BF16)
    qspec = pl.BlockSpec((1, S, NSA_PAIRS * LANES), lambda b, g: (b, 0, g))
    chspec = pl.BlockSpec((1, 1, nch, half), lambda b, g: (b, g, 0, 0))
    return pl.pallas_call(
        _cmp_kernel,
        grid=(B, G),
        in_specs=[qspec, chspec, chspec,
                  pl.BlockSpec((1, S // LANES, LANES), lambda b, g: (b, 0, 0)),
                  pl.BlockSpec((1, LANES, 1), lambda b, g: (b, 0, 0)),
                  pl.BlockSpec((1, 1, 3 * NSA_HPG, S), lambda b, g: (b, g, 0, 0)),
                  pl.BlockSpec((1, NSA_HPG, LANES), lambda b, g: (g, 0, 0)),
                  _full((2, half)), _full((2, half, CMP_HIDDEN)), _full((2, CMP_HIDDEN, LANES)), _full((2, LANES)),
                  _full((2, half)), _full((2, half, CMP_HIDDEN)), _full((2, CMP_HIDDEN, LANES)),
                  _full((n_sel, LANES)), _full((NSA_TQ, NSA_TQ))],
        out_specs=[qspec, pl.BlockSpec((1, 1, S, LANES), lambda b, g: (b, g, 0, 0))],
        out_shape=[jax.ShapeDtypeStruct((B, S, NSA_HEADS * NSA_DK), BF16),
                   jax.ShapeDtypeStruct((B, G, S, LANES), BF16)],
        scratch_shapes=[pltpu.VMEM((2, nch, LANES), BF16), pltpu.VMEM((2, nch, LANES), BF16)],
        compiler_params=_params("parallel", "parallel"),
        name="nsa_cmp",
    )(q, kch, vch, pos3, pend, gates_g, slopes_t, pe2(pe_k), w1_2(w1_k), _lo_hi_cols(w2_k).astype(BF16),
      _lo_hi_cols(k_norm_cmp.reshape(1, -1)).reshape(2, LANES), pe2(pe_v), w1_2(w1_v),
      _lo_hi_cols(w2_v).astype(BF16), ovl, eye)


def _slc_win_kernel(q_ref, ks_ref, vs_ref, kw_ref, vw_ref, sel_ref, gate_ref, slopeq_ref, ocmp_ref, o_ref):
    tq = SLC_TQ
    S = q_ref.shape[1]
    lane = lax.broadcasted_iota(I32, (1, LANES), 1)
    lane_lo = lane < NSA_DK
    sub_lo = lax.broadcasted_iota(I32, (LANES, 1), 0) < NSA_DK
    ones_row = (LO_ONE, HI_ONE)

    def q_block(qi, carry):
        t0 = pl.multiple_of(qi * tq, tq)
        rows = pl.ds(t0, tq)
        qs = _stack_pairs(q_ref, rows)
        selm1 = sel_ref[0, 0, rows, :]

        def q_aug(e, with_sel):
            feats = [slopeq_ref[0, NSA_PAIRS * e + p:NSA_PAIRS * e + p + 1, :] for p in range(NSA_PAIRS)]
            if with_sel:
                extra = jnp.concatenate([selm1 + f for f in feats], axis=0)
            else:
                extra = jnp.concatenate([jnp.broadcast_to(f, (tq, LANES)) for f in feats], axis=0)
            return jnp.where(lane_lo, qs, extra) if e == 0 else jnp.where(lane_lo, extra, qs)

        def k_aug(k_ref, e, r0, n):
            return k_ref[0, pl.ds(r0, n), e * LANES:(e + 1) * LANES]

        gt = gate_ref[0, 0, :, rows]
        tq_lane = t0 + (lax.broadcasted_iota(I32, (1, NSA_PAIRS * tq), 1) & (tq - 1))

        def gated(accs, branch):
            outs = []
            for e, acc in enumerate(accs):
                gate = jnp.concatenate([gt[branch * NSA_HPG + 2 * p + e:branch * NSA_HPG + 2 * p + e + 1, :]
                                        for p in range(NSA_PAIRS)], axis=1)
                outs.append(acc * (gate / acc[ones_row[e]:ones_row[e] + 1, :]))
            return jnp.where(sub_lo, outs[0], outs[1])

        qa = [q_aug(0, True), q_aug(1, True)]
        krow = lax.broadcasted_iota(I32, (SLC_TK, 1), 0)

        def slc_scores(j):
            r0 = pl.multiple_of(j * SLC_TK, SLC_TK)
            return [_dot_nt(k_aug(ks_ref, e, r0, SLC_TK), qa[e]) for e in range(2)]

        def slc_softmax(j, ss, state, last):
            r0 = pl.multiple_of(j * SLC_TK, SLC_TK)
            out = []
            for e in range(2):
                m, acc = state[e]
                s = jnp.where(r0 + krow <= tq_lane, ss[e], NEG_INF) if last else ss[e]
                m_new = jnp.maximum(m, jnp.max(s, axis=0, keepdims=True))
                p = jnp.exp2(s - m_new).astype(BF16)
                pv = _dot(vs_ref[e * LANES:(e + 1) * LANES, pl.ds(r0, SLC_TK)], p)
                out.append((m_new, jnp.exp2(m - m_new) * acc + pv))
            return tuple(out)

        init = (jnp.full((1, NSA_PAIRS * tq), NEG_INF, F32), jnp.zeros((LANES, NSA_PAIRS * tq), F32))
        n_full = qi // (SLC_TK // tq)
        state = lax.fori_loop(0, n_full, lambda j, st: slc_softmax(j, slc_scores(j), st, False), (init, init))

        w0 = pl.multiple_of(jnp.maximum(t0 - WINDOW, 0), tq)
        ss_last = slc_scores(n_full)
        ss_win = [_dot_nt(k_aug(kw_ref, e, w0, WIN_SPAN), q_aug(e, False)) for e in range(2)]
        state = slc_softmax(n_full, ss_last, state, True)
        z = gated([state[0][1], state[1][1]], 1)
        rel = tq_lane - (w0 + lax.broadcasted_iota(I32, (WIN_SPAN, 1), 0))
        wmask = (rel >= 0) & (rel < WINDOW)
        accs = []
        for e in range(2):
            s = jnp.where(wmask, ss_win[e], NEG_INF)
            p = jnp.exp2(s - jnp.max(s, axis=0, keepdims=True)).astype(BF16)
            accs.append(_dot(vw_ref[e * LANES:(e + 1) * LANES, pl.ds(w0, WIN_SPAN)], p))
        z = z + gated(accs, 2)

        for p in range(NSA_PAIRS):
            sl = slice(p * LANES, (p + 1) * LANES)
            o_ref[0, rows, sl] = (ocmp_ref[0, rows, sl].astype(F32) + z[:, p * tq:(p + 1) * tq].T).astype(BF16)
        return carry

    lax.fori_loop(0, S // tq, q_block, 0)


def _slope_aug(slopes_eo):
    hi = slopes_eo.astype(BF16).astype(F32)
    lo = (slopes_eo - hi).astype(BF16).astype(F32)
    w = jnp.asarray([65536.0, 256.0, 1.0], F32)
    feats = jnp.concatenate([hi[..., None] * w, lo[..., None] * w], axis=-1)
    half = jnp.pad(feats, ((0, 0), (0, 0), (AUG_POS, NSA_DK - AUG_POS - 6)))
    return jnp.concatenate([half, half], axis=-1).astype(BF16)


def _nsa_slc_win(q, ks4, vst4, kw4, vwt4, sel, gates_t, slopeq, o_cmp):
    B, S, _ = q.shape
    qspec = pl.BlockSpec((1, S, NSA_PAIRS * LANES), lambda b, g: (b, 0, g))
    kvspec = pl.BlockSpec((1, S, 2 * LANES), lambda b, g: (b, 0, g))
    vtspec = pl.BlockSpec((2 * LANES, S), lambda b, g: (g, b))
    return pl.pallas_call(
        _slc_win_kernel,
        grid=(B, NSA_G),
        in_specs=[qspec, kvspec, vtspec, kvspec, vtspec,
                  pl.BlockSpec((1, 1, S, LANES), lambda b, g: (b, g, 0, 0)),
                  pl.BlockSpec((1, 1, 3 * NSA_HPG, S), lambda b, g: (b, g, 0, 0)),
                  pl.BlockSpec((1, NSA_HPG, LANES), lambda b, g: (g, 0, 0)),
                  qspec],
        out_specs=qspec,
        out_shape=jax.ShapeDtypeStruct((B, S, NSA_HEADS * NSA_DK), BF16),
        compiler_params=_params("parallel", "parallel"),
        name="nsa_slc_win",
    )(q, ks4, vst4, kw4, vwt4, sel, gates_t, slopeq, o_cmp)


def _outproj_kernel(o_ref, x_ref, w_ref, out_ref):
    out_ref[...] = x_ref[...] + _dot(o_ref[...], w_ref[...])


def _outproj(o2d, x2d, w_out, tm=512):
    T, D = x2d.shape
    K = o2d.shape[1]
    return pl.pallas_call(
        _outproj_kernel,
        grid=(T // tm,),
        in_specs=[pl.BlockSpec((tm, K), lambda i: (i, 0)), pl.BlockSpec((tm, D), lambda i: (i, 0)), _full((K, D))],
        out_specs=pl.BlockSpec((tm, D), lambda i: (i, 0)),
        out_shape=jax.ShapeDtypeStruct((T, D), F32),
        compiler_params=_params("parallel"),
        name="outproj1",
    )(o2d, x2d, w_out.astype(BF16))


def _mixer1(x, positions, gain, w_in, q_norm, k_norm_cmp, k_norm_slc, k_norm_win,
            pe_k, w1_k, w2_k, pe_v, w1_v, w2_v, w_out):
    B, S, D = x.shape
    T = B * S
    G, HPG, DK = NSA_G, NSA_HPG, NSA_DK
    assert S // SLC_BLOCK == N_SEL and S % NSA_TQ == 0 and NSA_TQ % LANES == 0
    x2d = x.reshape(T, D)
    q, kc, vc, ks4, vs4, kw4, vw4, gates = _inproj1(x2d, positions, gain, w_in, q_norm, k_norm_slc,
                                                    k_norm_win)
    b3 = lambda t: t.reshape(B, S, -1)
    chunks = lambda t: t.reshape(B, S, G, DK).transpose(0, 2, 1, 3).reshape(B, G, S // CMP_STRIDE, CMP_STRIDE * DK)
    gates_t = gates[:, :3 * NSA_HEADS].reshape(B, S, 3, G, HPG).transpose(0, 3, 2, 4, 1).reshape(B, G, 3 * HPG, S)
    n_cmp = (S - CMP_BLOCK) // CMP_STRIDE + 1
    pend = jnp.pad(positions[:, CMP_BLOCK - 1::CMP_STRIDE][:, :n_cmp], ((0, 0), (0, LANES - n_cmp))).reshape(B, LANES, 1)
    pos3 = positions.reshape(B, S // LANES, LANES)
    slopes = jnp.asarray(2.0 ** (-8.0 * np.arange(1, NSA_HEADS + 1) / NSA_HEADS), F32) * LOG2E
    slopes_eo = slopes.reshape(G, NSA_PAIRS, 2).transpose(0, 2, 1).reshape(G, HPG)
    slopes_t = jnp.broadcast_to(slopes_eo[:, :, None], (G, HPG, LANES))
    o_cmp, sel = _nsa_cmp(b3(q), chunks(kc), chunks(vc), pos3, pend, gates_t, slopes_t,
                          pe_k, w1_k, w2_k, k_norm_cmp, pe_v, w1_v, w2_v)
    o = _nsa_slc_win(b3(q), b3(ks4), vs4, b3(kw4), vw4, sel, gates_t, _slope_aug(slopes_eo), o_cmp)
    return _outproj(o.reshape(T, -1), x2d, w_out).reshape(B, S, D)


def kernel(x, positions, norm_mix, norm_ffn, mix0_w_in, mla_q_a_norm, mla_w_uq, mla_kv_a_norm, mla_w_ukv, mla_q_norm, mla_k_norm, conv_dw_w, conv_dw_b, conv_ln_g, conv_ln_b, mix0_w_out, nsa_w_in, nsa_q_norm, nsa_k_norm_cmp, nsa_k_norm_slc, nsa_k_norm_win, nsa_cmp_pe_k, nsa_cmp_w1_k, nsa_cmp_w2_k, nsa_cmp_pe_v, nsa_cmp_w1_v, nsa_cmp_w2_v, nsa_w_out, moe_router_group, moe_router_group_b, moe_router_expert, moe_router_expert_b, moe_w_gate, moe_w_up, moe_w_down):
    def moe(x, layer):
        return _moe(x, norm_ffn[layer], moe_router_group[layer], moe_router_group_b[layer], moe_router_expert[layer],
                    moe_router_expert_b[layer], moe_w_gate, moe_w_up, moe_w_down, layer)

    x = _mixer0(x, positions, norm_mix[0], mix0_w_in[0], mla_q_a_norm[0], mla_w_uq[0], mla_kv_a_norm[0], mla_w_ukv[0],
                mla_q_norm[0], mla_k_norm[0], conv_dw_w[0], conv_dw_b[0], conv_ln_g[0], conv_ln_b[0], mix0_w_out[0])
    x = moe(x, 0)
    x = _mixer1(x, positions, norm_mix[1], nsa_w_in[0], nsa_q_norm[0], nsa_k_norm_cmp[0], nsa_k_norm_slc[0],
                nsa_k_norm_win[0], nsa_cmp_pe_k[0], nsa_cmp_w1_k[0], nsa_cmp_w2_k[0], nsa_cmp_pe_v[0],
                nsa_cmp_w1_v[0], nsa_cmp_w2_v[0], nsa_w_out[0])
    return moe(x, 1)
```

```python
import functools

import numpy as np
import jax
import jax.numpy as jnp
from jax import lax
from jax.experimental import pallas as pl
from jax.experimental.pallas import tpu as pltpu

F32 = jnp.float32
BF16 = jnp.bfloat16
I32 = jnp.int32

LANES = 128
NEG_INF = -1e30
LOG2E = 1.4426950408889634
Q_BLOCK = 128
MLA_HEADS = 8
MLA_Q_LORA = 256
MLA_KV_LORA = 128
MLA_NOPE = 64
MLA_ROPE = 32
MLA_V = 64
MLA_QK = MLA_NOPE + MLA_ROPE
ROPE_THETA = 10000.0
CONV_CH = 512
CONV_W = 31
NSA_HEADS = 16
NSA_G = 2
NSA_HPG = NSA_HEADS // NSA_G
NSA_DK = 64
NSA_KVW = NSA_G * NSA_DK
CMP_BLOCK = 32
CMP_STRIDE = 16
CMP_HIDDEN = 128
SLC_BLOCK = 64
SLC_TOP_N = 8
WINDOW = 256
FORCE_SCORE = 1e4
MOE_GROUPS = 4
MOE_EPG = 8
MOE_EXPERTS = MOE_GROUPS * MOE_EPG
MOE_HIDDEN = 256
MOE_ROW_BLOCK = 256

VMEM_LIMIT = 56 * 1024 * 1024


def _params(*sem):
    return pltpu.CompilerParams(dimension_semantics=sem, vmem_limit_bytes=VMEM_LIMIT)


def _full(shape):
    n = len(shape)
    return pl.BlockSpec(shape, lambda *_: (0,) * n)


def _row_mean(x):
    n = x.shape[-1]
    folded = functools.reduce(jnp.add, [x[:, c:c + LANES] for c in range(0, n, LANES)])
    return jnp.sum(folded, axis=-1, keepdims=True) * (1.0 / n)


def _rms(x, eps=1e-6):
    return x * lax.rsqrt(_row_mean(x * x) + eps)


def _dot(a, b):
    return jnp.dot(a, b, preferred_element_type=F32)


def _group_matrix(groups):
    width = LANES // groups
    same = (lax.broadcasted_iota(I32, (LANES, LANES), 0) // width) == (lax.broadcasted_iota(I32, (LANES, LANES), 1) // width)
    return jnp.where(same, 1.0, 0.0).astype(BF16)


def _group_sums(t, group_matrix):
    return _dot(t.astype(BF16), group_matrix)


def _dot_nt(a, b, **kw):
    return lax.dot_general(a, b, (((1,), (1,)), ((), ())), preferred_element_type=F32, **kw)


def _inproj0_kernel(x_ref, pos_ref, gmix_ref, win_ref, qan_ref, wuq_ref, kvan_ref, wuk_ref, wuv_ref,
                    qg_ref, kg_ref, invf_ref, q_out, k_out, v_out, u_out):
    h = _rms(x_ref[...]) * gmix_ref[...]
    proj = _dot(h.astype(BF16), win_ref[...])
    c_q = proj[:, :MLA_Q_LORA]
    c_kv = proj[:, MLA_Q_LORA:MLA_Q_LORA + MLA_KV_LORA]
    k_rope = proj[:, 384:512]
    a = proj[:, 512:512 + CONV_CH]
    g = proj[:, 512 + CONV_CH:]
    u_out[...] = a * jax.nn.sigmoid(g)
    cqn = (_rms(c_q) * qan_ref[...]).astype(BF16)
    ckvn = (_rms(c_kv) * kvan_ref[...]).astype(BF16)
    q = _dot(cqn, wuq_ref[...])
    kn = _dot(ckvn, wuk_ref[...])
    slot_row = lax.broadcasted_iota(I32, (MLA_HEADS * LANES, 1), 0) & (LANES - 1)
    v_out[...] = (_dot_nt(wuv_ref[...], ckvn) + jnp.where(slot_row == MLA_V, 1.0, 0.0)).astype(BF16)
    ang = invf_ref[...] * pos_ref[...].astype(F32)
    cos_t, sin_t = jnp.cos(ang), jnp.sin(ang)
    tm = ang.shape[1]
    tail = jnp.zeros((LANES - MLA_QK, tm), F32)
    cos = jnp.concatenate([jnp.ones((MLA_NOPE, tm), F32), cos_t, cos_t, tail], axis=0).T
    sin = jnp.concatenate([jnp.zeros((MLA_NOPE, tm), F32), -sin_t, sin_t, tail], axis=0).T
    lane = lax.broadcasted_iota(I32, (1, LANES), 1)
    first_half = (lane >= MLA_NOPE) & (lane < MLA_NOPE + MLA_ROPE // 2)
    scale = MLA_QK ** -0.5 * LOG2E

    all_lanes = _group_matrix(1)

    def norm_rope(t, gain):
        t = t * lax.rsqrt(_group_sums(t * t, all_lanes) * (1.0 / MLA_QK) + 1e-6) * gain
        partner = jnp.where(first_half, pltpu.roll(t, LANES - MLA_ROPE // 2, 1), pltpu.roll(t, MLA_ROPE // 2, 1))
        return t * cos + partner * sin

    for hd in range(MLA_HEADS):
        sl = slice(hd * LANES, (hd + 1) * LANES)
        q_out[:, sl] = (norm_rope(q[:, sl], qg_ref[...]) * scale).astype(BF16)
        k_out[:, sl] = norm_rope(kn[:, sl] + k_rope, kg_ref[...]).astype(BF16)


def _head_slots(w, n_heads, width, offset=0):
    k = w.shape[0]
    w = w.reshape(k, n_heads, width)
    w = jnp.pad(w, ((0, 0), (0, 0), (offset, LANES - width - offset)))
    return w.reshape(k, n_heads * LANES)


def _inproj0(x2d, pos_row, gmix, w_in, q_a_norm, w_uq, kv_a_norm, w_ukv, q_norm, k_norm, tm=512):
    T, D = x2d.shape
    H = MLA_HEADS
    w_krope = jnp.pad(w_in[:, 384:416], ((0, 0), (MLA_NOPE, LANES - MLA_NOPE - MLA_ROPE)))
    w_in_p = jnp.concatenate([w_in[:, :384], w_krope, w_in[:, 416:]], axis=1).astype(BF16)
    w_uq_p = _head_slots(w_uq, H, MLA_QK).astype(BF16)
    w_ukv3 = w_ukv.reshape(MLA_KV_LORA, H, MLA_NOPE + MLA_V)
    w_uk_p = _head_slots(w_ukv3[:, :, :MLA_NOPE].reshape(MLA_KV_LORA, H * MLA_NOPE), H, MLA_NOPE).astype(BF16)
    w_uv = _head_slots(w_ukv3[:, :, MLA_NOPE:].reshape(MLA_KV_LORA, H * MLA_V), H, MLA_V).T.astype(BF16)
    pad = LANES - MLA_QK
    qg = jnp.pad(q_norm, (0, pad)).reshape(1, LANES)
    kg = jnp.pad(k_norm, (0, pad)).reshape(1, LANES)
    half = MLA_ROPE // 2
    inv_freq = (ROPE_THETA ** (-jnp.arange(half, dtype=F32) / half)).reshape(half, 1)
    row = lambda n: pl.BlockSpec((tm, n), lambda i: (i, 0))
    n_in = w_in_p.shape[1]
    return pl.pallas_call(
        _inproj0_kernel,
        grid=(T // tm,),
        in_specs=[row(D), pl.BlockSpec((1, tm), lambda i: (0, i)), _full((1, D)), _full((D, n_in)),
                  _full((1, MLA_Q_LORA)), _full((MLA_Q_LORA, H * LANES)), _full((1, MLA_KV_LORA)),
                  _full((MLA_KV_LORA, H * LANES)), _full((H * LANES, MLA_KV_LORA)), _full((1, LANES)),
                  _full((1, LANES)), _full((half, 1))],
        out_specs=[row(H * LANES), row(H * LANES), pl.BlockSpec((H * LANES, tm), lambda i: (0, i)), row(CONV_CH)],
        out_shape=[jax.ShapeDtypeStruct((T, H * LANES), BF16), jax.ShapeDtypeStruct((T, H * LANES), BF16),
                   jax.ShapeDtypeStruct((H * LANES, T), BF16), jax.ShapeDtypeStruct((T, CONV_CH), F32)],
        compiler_params=_params("parallel"),
        name="inproj0",
    )(x2d, pos_row, gmix.reshape(1, D), w_in_p, q_a_norm.reshape(1, -1), w_uq_p, kv_a_norm.reshape(1, -1),
      w_uk_p, w_uv, qg, kg, inv_freq)


MLA_TQ = 512
MLA_TK = 512
MLA_ONE = MLA_V


def _mla_attn_kernel(q_ref, k_ref, vt_ref, o_ref):
    S = q_ref.shape[1]
    tq, tk = MLA_TQ, MLA_TK
    krow = lax.broadcasted_iota(I32, (tk, 1), 0)
    qcol = lax.broadcasted_iota(I32, (1, tq), 1)

    def q_block(qi, carry):
        q0 = pl.multiple_of(qi * tq, tq)
        qs = [q_ref[0, pl.ds(q0, tq), hh * LANES:(hh + 1) * LANES] for hh in range(2)]

        def kv_step(j, state, masked):
            k0 = pl.multiple_of(j * tk, tk)
            hs = [slice(hh * LANES, (hh + 1) * LANES) for hh in range(2)]
            ss = [_dot_nt(k_ref[0, pl.ds(k0, tk), hs[hh]], qs[hh]) for hh in range(2)]
            out = []
            for hh in range(2):
                m, acc = state[hh]
                s = jnp.where(k0 + krow <= q0 + qcol, ss[hh], NEG_INF) if masked else ss[hh]
                m_new = jnp.maximum(m, jnp.max(s, axis=0, keepdims=True))
                p = jnp.exp2(s - m_new).astype(BF16)
                out.append((m_new, jnp.exp2(m - m_new) * acc + _dot(vt_ref[hs[hh], pl.ds(k0, tk)], p)))
            return tuple(out)

        init = (jnp.full((1, tq), NEG_INF, F32), jnp.zeros((LANES, tq), F32))
        n_full = (qi * tq) // tk
        state = lax.fori_loop(0, n_full, functools.partial(kv_step, masked=False), (init, init))
        state = kv_step(n_full, state, True)
        outs = []
        for hh in range(2):
            acc = state[hh][1]
            outs.append((acc * (1.0 / acc[MLA_ONE:MLA_ONE + 1, :]))[:MLA_V, :])
        o_ref[0, pl.ds(q0, tq), :] = jnp.concatenate(outs, axis=0).T.astype(BF16)
        return carry

    lax.fori_loop(0, S // tq, q_block, 0)


def _mla_attn(q, k, vt):
    B, S, _ = q.shape
    spec = pl.BlockSpec((1, S, 2 * LANES), lambda b, h: (b, 0, h))
    return pl.pallas_call(
        _mla_attn_kernel,
        grid=(B, MLA_HEADS // 2),
        in_specs=[spec, spec, pl.BlockSpec((2 * LANES, S), lambda b, h: (h, b))],
        out_specs=pl.BlockSpec((1, S, 2 * MLA_V), lambda b, h: (b, 0, h)),
        out_shape=jax.ShapeDtypeStruct((B, S, MLA_HEADS * MLA_V), BF16),
        compiler_params=_params("parallel", "parallel"),
        name="mla_attn",
    )(q, k, vt)


CONV_TILE = 64
CONV_PAD = 32


CONV_ROWS = 512


def _conv_out0_kernel(u_ref, o_ref, x_ref, dww_ref, dwb_ref, lng_ref, lnb_ref, wo_ref, out_ref, upad, act, shifted):
    step = pl.program_id(1)

    @pl.when(step == 0)
    def _():
        upad[0:CONV_PAD, :] = jnp.zeros((CONV_PAD, CONV_CH), F32)
        upad[CONV_PAD:, :] = u_ref[0]

    lead = CONV_PAD - (CONV_W - 1)
    base = step * CONV_ROWS

    def tile(i, carry):
        t0 = pl.multiple_of(i * CONV_TILE, CONV_TILE)
        win = upad[pl.ds(pl.multiple_of(base + t0, CONV_TILE), CONV_TILE + CONV_PAD), :]
        acc = jnp.zeros((CONV_TILE, CONV_CH), F32) + dwb_ref[...]
        span = CONV_TILE + CONV_PAD - 8
        for r in range(1, 8):
            shifted[r - 1] = win[r:r + span, :]
        for j in range(CONV_W):
            r, k = (lead + j) % 8, (lead + j) // 8 * 8
            tap = win[k:k + CONV_TILE, :] if r == 0 else shifted[r - 1, k:k + CONV_TILE, :]
            acc = acc + tap * dww_ref[j:j + 1, :]
        xc = acc - _row_mean(acc)
        var = _row_mean(xc * xc)
        y = xc * lax.rsqrt(var + 1e-5) * lng_ref[...] + lnb_ref[...]
        act[pl.ds(t0, CONV_TILE), :] = (y * jax.nn.sigmoid(y)).astype(BF16)
        return carry

    lax.fori_loop(0, CONV_ROWS // CONV_TILE, tile, 0)
    n_o = o_ref.shape[2]
    y = _dot(o_ref[0], wo_ref[:n_o, :]) + _dot(act[...], wo_ref[n_o:, :])
    out_ref[0] = x_ref[0] + y


def _conv_out0(u, o_mla, x, dw_w, dw_b, ln_g, ln_b, w_out):
    B, S, D = x.shape
    n_o = o_mla.shape[2]
    w_out_p = w_out.astype(BF16)
    tspec = lambda n: pl.BlockSpec((1, CONV_ROWS, n), lambda b, t: (b, t, 0))
    return pl.pallas_call(
        _conv_out0_kernel,
        grid=(B, S // CONV_ROWS),
        in_specs=[pl.BlockSpec((1, S, CONV_CH), lambda b, t: (b, 0, 0)), tspec(n_o), tspec(D),
                  _full((CONV_W, CONV_CH)), _full((1, CONV_CH)), _full((1, CONV_CH)), _full((1, CONV_CH)),
                  _full((n_o + CONV_CH, D))],
        out_specs=tspec(D),
        out_shape=jax.ShapeDtypeStruct((B, S, D), F32),
        scratch_shapes=[pltpu.VMEM((S + CONV_PAD, CONV_CH), F32), pltpu.VMEM((CONV_ROWS, CONV_CH), BF16),
                        pltpu.VMEM((7, CONV_TILE + CONV_PAD - 8, CONV_CH), F32)],
        compiler_params=_params("parallel", "arbitrary"),
        name="conv_out0",
    )(u, o_mla, x, dw_w, dw_b.reshape(1, -1), ln_g.reshape(1, -1), ln_b.reshape(1, -1), w_out_p)


def _mixer0(x, positions, gmix, w_in, q_a_norm, w_uq, kv_a_norm, w_ukv, q_norm, k_norm,
            dw_w, dw_b, ln_g, ln_b, w_out):
    B, S, D = x.shape
    T = B * S
    q, k, v, u = _inproj0(x.reshape(T, D), positions.reshape(1, T), gmix, w_in, q_a_norm, w_uq, kv_a_norm, w_ukv,
                          q_norm, k_norm)
    o = _mla_attn(q.reshape(B, S, -1), k.reshape(B, S, -1), v)
    return _conv_out0(u.reshape(B, S, -1), o, x, dw_w, dw_b, ln_g, ln_b, w_out)


ROUTE_TILE = 512
ROUTE_ROWS = 40


def _route_kernel(x_ref, g_ref, wr_ref, br_ref, tri_ref, hn_ref, oi_ref, of_ref, cnt_ref, carry):
    @pl.when(pl.program_id(0) == 0)
    def _():
        carry[...] = jnp.zeros_like(carry)

    hn = _rms(x_ref[...]) * g_ref[...]
    _slab_store(hn_ref, hn)
    tm = hn.shape[0]
    logits = _dot_nt(wr_ref[...], hn, precision=lax.Precision.HIGHEST) + br_ref[...]
    gl = logits[MOE_EXPERTS:MOE_EXPERTS + MOE_GROUPS]
    rid_g = lax.broadcasted_iota(I32, (MOE_GROUPS, tm), 0)
    gmax = jnp.max(gl, axis=0, keepdims=True)
    grp = jnp.min(jnp.where(gl == gmax, rid_g, MOE_GROUPS), axis=0, keepdims=True)
    g_w = 1.0 / jnp.sum(jnp.exp(gl - gmax), axis=0, keepdims=True)
    e_in = jnp.zeros((MOE_EPG, tm), F32)
    for gi in range(MOE_GROUPS):
        e_in = jnp.where(grp == gi, logits[gi * MOE_EPG:(gi + 1) * MOE_EPG], e_in)
    rid_e = lax.broadcasted_iota(I32, (MOE_EPG, tm), 0)
    v1 = jnp.max(e_in, axis=0, keepdims=True)
    i1 = jnp.min(jnp.where(e_in == v1, rid_e, MOE_EPG), axis=0, keepdims=True)
    rest = jnp.where(rid_e == i1, -jnp.inf, e_in)
    v2 = jnp.max(rest, axis=0, keepdims=True)
    i2 = jnp.min(jnp.where(rest == v2, rid_e, MOE_EPG), axis=0, keepdims=True)
    ex = jnp.exp(v2 - v1)
    den = 1.0 + ex
    e1 = grp * MOE_EPG + i1
    e2 = grp * MOE_EPG + i2
    rid = lax.broadcasted_iota(I32, (MOE_EXPERTS, tm), 0)
    hit1 = rid == e1
    hit2 = rid == e2
    member = jnp.where(hit1 | hit2, 1.0, 0.0)
    before = _dot(member.astype(BF16), tri_ref[...]) + carry[...]
    r1 = jnp.sum(jnp.where(hit1, before, 0.0), axis=0, keepdims=True)
    r2 = jnp.sum(jnp.where(hit2, before, 0.0), axis=0, keepdims=True)
    carry[...] = carry[...] + jnp.sum(member, axis=1, keepdims=True)
    oi_ref[...] = jnp.zeros_like(oi_ref)
    oi_ref[0:1, :] = e1
    oi_ref[1:2, :] = e2
    oi_ref[2:3, :] = r1.astype(I32)
    oi_ref[3:4, :] = r2.astype(I32)
    of_ref[...] = jnp.zeros_like(of_ref)
    of_ref[0:1, :] = g_w / den
    of_ref[1:2, :] = g_w * ex / den
    cnt_ref[...] = jnp.broadcast_to(carry[...], cnt_ref.shape).astype(I32)


def _route(x2d, gain, router_group, router_group_b, router_expert, router_expert_b):
    T, D = x2d.shape
    tm = ROUTE_TILE
    pad = ROUTE_ROWS - MOE_EXPERTS - MOE_GROUPS
    wr = jnp.concatenate([router_expert.T, router_group.T, jnp.zeros((pad, D), F32)], axis=0)
    br = jnp.concatenate([router_expert_b, router_group_b, jnp.zeros((pad,), F32)]).reshape(ROUTE_ROWS, 1)
    tri = (jnp.arange(tm)[:, None] < jnp.arange(tm)[None, :]).astype(BF16)
    return pl.pallas_call(
        _route_kernel,
        grid=(T // tm,),
        in_specs=[pl.BlockSpec((tm, D), lambda i: (i, 0)), _full((1, D)), _full((ROUTE_ROWS, D)),
                  _full((ROUTE_ROWS, 1)), _full((tm, tm))],
        out_specs=[pl.BlockSpec((tm * SLAB, LANES), lambda i: (i, 0)), pl.BlockSpec((8, tm), lambda i: (0, i)),
                   pl.BlockSpec((8, tm), lambda i: (0, i)), _full((MOE_EXPERTS, LANES))],
        out_shape=[jax.ShapeDtypeStruct((T * SLAB, LANES), F32), jax.ShapeDtypeStruct((8, T), I32),
                   jax.ShapeDtypeStruct((8, T), F32), jax.ShapeDtypeStruct((MOE_EXPERTS, LANES), I32)],
        scratch_shapes=[pltpu.VMEM((MOE_EXPERTS, 1), F32)],
        compiler_params=_params("arbitrary"),
        name="moe_route",
    )(x2d, gain.reshape(1, D), wr, br, tri)


MOVE_CHUNK = 512
MOVE_UNROLL = 8


SLAB = 8


def _slab_load(ref, n, first=0, stride=SLAB):
    return jnp.concatenate([ref[pl.ds(first + j, n, stride=stride), :] for j in range(SLAB)], axis=1)


def _slab_store(ref, value):
    n = value.shape[0]
    for j in range(SLAB):
        ref[pl.ds(j, n, stride=SLAB), :] = value[:, j * LANES:(j + 1) * LANES]


def _slab(row):
    return pl.ds(pl.multiple_of(row, SLAB), SLAB)


def _for_tokens(n, fn):
    def body(t, carry):
        fn(t, 0)
        fn(t, 1)
        return carry
    lax.fori_loop(0, n, body, 0, unroll=MOVE_UNROLL)


def _dispatch_kernel(dest_ref, gaps_ref, hn_ref, xs_ref, zero, sem, zsem):
    base = pl.program_id(0) * (2 * MOVE_CHUNK)

    @pl.when(pl.program_id(0) == 0)
    def _():
        zero[...] = jnp.zeros_like(zero)

        def fill(r):
            return pltpu.make_async_copy(zero, xs_ref.at[_slab(r * SLAB)], zsem)

        def each_gap(action):
            def gap(g, carry):
                def row(r, c):
                    action(r)
                    return c
                lax.fori_loop(gaps_ref[2 * g], gaps_ref[2 * g + 1], row, 0)
                return carry
            lax.fori_loop(0, gaps_ref.shape[0] // 2, gap, 0)

        each_gap(lambda r: fill(r).start())
        each_gap(lambda r: fill(r).wait())

    def copy(t, k):
        return pltpu.make_async_copy(hn_ref.at[_slab(t * SLAB)], xs_ref.at[_slab(dest_ref[base + 2 * t + k])], sem)

    _for_tokens(MOVE_CHUNK, lambda t, k: copy(t, k).start())
    _for_tokens(MOVE_CHUNK, lambda t, k: copy(t, k).wait())


def _dispatch(dest, gaps, hn, n_rows):
    T = hn.shape[0] // SLAB
    tm = MOVE_CHUNK
    return pl.pallas_call(
        _dispatch_kernel,
        grid_spec=pltpu.PrefetchScalarGridSpec(
            num_scalar_prefetch=2,
            grid=(T // tm,),
            in_specs=[pl.BlockSpec((tm * SLAB, LANES), lambda i, d, g: (i, 0))],
            out_specs=pl.BlockSpec(memory_space=pl.ANY),
            scratch_shapes=[pltpu.VMEM((SLAB, LANES), F32), pltpu.SemaphoreType.DMA(()),
                            pltpu.SemaphoreType.DMA(())],
        ),
        out_shape=jax.ShapeDtypeStruct((n_rows * SLAB, LANES), F32),
        compiler_params=_params("arbitrary"),
        name="moe_dispatch",
    )(dest, gaps, hn)


def _expert_kernel(blk_e_ref, n_used_ref, xs_ref, wg_ref, wu_ref, wd_ref, ys_ref, wg_s, wu_s, wd_s):
    b = pl.program_id(0)

    @pl.when((b == 0) | (blk_e_ref[b] != blk_e_ref[jnp.maximum(b - 1, 0)]))
    def _():
        wg_s[...] = wg_ref[0].astype(BF16)
        wu_s[...] = wu_ref[0].astype(BF16)
        wd_s[...] = wd_ref[0].astype(BF16)

    @pl.when(b < n_used_ref[0])
    def _():
        x = _slab_load(xs_ref, MOE_ROW_BLOCK).astype(BF16)
        gate = _dot(x, wg_s[...])
        hid = (gate * jax.nn.sigmoid(gate) * _dot(x, wu_s[...])).astype(BF16)
        _slab_store(ys_ref, _dot(hid, wd_s[...]))

    @pl.when(b >= n_used_ref[0])
    def _():
        ys_ref[...] = jnp.zeros_like(ys_ref)


def _experts(blk_expert, n_used, xs, w_gate, w_up, w_down, layer):
    R = xs.shape[0] // SLAB
    rb = MOE_ROW_BLOCK
    D = w_gate.shape[2]
    up_spec = pl.BlockSpec((None, 1, D, MOE_HIDDEN), lambda b, be, nu: (layer, be[b], 0, 0))
    return pl.pallas_call(
        _expert_kernel,
        grid_spec=pltpu.PrefetchScalarGridSpec(
            num_scalar_prefetch=2,
            grid=(R // rb,),
            in_specs=[pl.BlockSpec((rb * SLAB, LANES), lambda b, be, nu: (jnp.minimum(b, nu[0] - 1), 0)),
                      up_spec, up_spec,
                      pl.BlockSpec((None, 1, MOE_HIDDEN, D), lambda b, be, nu: (layer, be[b], 0, 0))],
            out_specs=pl.BlockSpec((rb * SLAB, LANES), lambda b, be, nu: (b, 0)),
            scratch_shapes=[pltpu.VMEM((D, MOE_HIDDEN), BF16), pltpu.VMEM((D, MOE_HIDDEN), BF16),
                            pltpu.VMEM((MOE_HIDDEN, D), BF16)],
        ),
        out_shape=jax.ShapeDtypeStruct((R * SLAB, LANES), F32),
        compiler_params=_params("arbitrary"),
        name="moe_experts",
    )(blk_expert, n_used, xs, w_gate, w_up, w_down)


def _combine_kernel(dest_ref, x_ref, gate_ref, ys_ref, out_ref, buf, sem):
    tm = x_ref.shape[0]
    step = pl.program_id(0)

    def copy(s, t, k):
        a = 2 * t + k
        slot = s % 2
        return pltpu.make_async_copy(ys_ref.at[_slab(dest_ref[s * (2 * tm) + a])], buf.at[slot, _slab(a * SLAB)],
                                     sem.at[slot])

    @pl.when(step == 0)
    def _():
        _for_tokens(tm, lambda t, k: copy(step, t, k).start())

    @pl.when(step + 1 < pl.num_programs(0))
    def _():
        _for_tokens(tm, lambda t, k: copy(step + 1, t, k).start())

    _for_tokens(tm, lambda t, k: copy(step, t, k).wait())
    mine = buf.at[step % 2]
    y0 = _slab_load(mine, tm, 0, 2 * SLAB)
    y1 = _slab_load(mine, tm, SLAB, 2 * SLAB)
    out_ref[...] = x_ref[...] + gate_ref[:, 0:1] * y0 + gate_ref[:, 1:2] * y1


def _combine(dest, x2d, gates_col, ys, tm=MOVE_CHUNK):
    T, D = x2d.shape
    return pl.pallas_call(
        _combine_kernel,
        grid_spec=pltpu.PrefetchScalarGridSpec(
            num_scalar_prefetch=1,
            grid=(T // tm,),
            in_specs=[pl.BlockSpec((tm, D), lambda i, d: (i, 0)), pl.BlockSpec((tm, 2), lambda i, d: (i, 0)),
                      pl.BlockSpec(memory_space=pl.ANY)],
            out_specs=pl.BlockSpec((tm, D), lambda i, d: (i, 0)),
            scratch_shapes=[pltpu.VMEM((2, tm * 2 * SLAB, LANES), F32), pltpu.SemaphoreType.DMA((2,))],
        ),
        out_shape=jax.ShapeDtypeStruct((T, D), F32),
        compiler_params=_params("arbitrary"),
        name="moe_combine",
    )(dest, x2d, gates_col, ys)


def _moe(x, gain, router_group, router_group_b, router_expert, router_expert_b, w_gate, w_up, w_down, layer):
    B, S, D = x.shape
    T = B * S
    x2d = x.reshape(T, D)
    hn, oi, of, cnt = _route(x2d, gain, router_group, router_group_b, router_expert, router_expert_b)
    rb = MOE_ROW_BLOCK
    counts = cnt[:, 0]
    cap = (counts + rb - 1) // rb * rb
    cap_end = jnp.cumsum(cap)
    start = cap_end - cap
    eid = oi[0:2].T
    first = jnp.sum(jnp.where(eid[:, :, None] == jnp.arange(MOE_EXPERTS), start, 0), axis=-1)
    dest = ((first + oi[2:4].T) * SLAB).reshape(2 * T).astype(I32)
    n_rows = 2 * T + MOE_EXPERTS * rb
    n_blk = n_rows // rb
    blk_first_row = jnp.arange(n_blk, dtype=I32) * rb
    blk_expert = jnp.minimum(jnp.sum(cap_end[None, :] <= blk_first_row[:, None], axis=1), MOE_EXPERTS - 1)
    n_used = (cap_end[-1:] // rb).astype(I32)
    gap_lo = jnp.concatenate([start + counts, cap_end[-1:]])
    gap_hi = jnp.concatenate([cap_end, jnp.full((1,), n_rows, cap_end.dtype)])
    gaps = jnp.stack([gap_lo, gap_hi], axis=1).reshape(-1).astype(I32)
    xs = _dispatch(dest, gaps, hn, n_rows)
    ys = _experts(blk_expert.astype(I32), n_used, xs, w_gate, w_up, w_down, layer)
    out = _combine(dest, x2d, of[0:2].T, ys)
    return out.reshape(B, S, D)


NSA_TQ = 256
SLC_TQ = 256
NSA_PAIRS = NSA_HPG // 2
N_SEL = 32
NSA_Q_SCALE = NSA_DK ** -0.5 * LOG2E
LO_ONE = LANES - 1
HI_ONE = 0
SLC_TK = 512
WIN_SPAN = WINDOW + SLC_TQ
AUG_POS = N_SEL
MASK_BIG = 1e30


def _pair_norm(t, gain2, lane_lo):
    t2 = t * t
    s_lo = jnp.sum(jnp.where(lane_lo, t2, 0.0), axis=-1, keepdims=True)
    s_hi = jnp.sum(jnp.where(lane_lo, 0.0, t2), axis=-1, keepdims=True)
    inv = jnp.where(lane_lo, lax.rsqrt(s_lo * (1.0 / NSA_DK) + 1e-6), lax.rsqrt(s_hi * (1.0 / NSA_DK) + 1e-6))
    return t * inv * gain2


def _inproj1_kernel(x_ref, prel_ref, blocks_ref, g_ref, win_ref, wvt_ref, qg_ref, ksg_ref, kwg_ref,
                    q_out, kc_out, vc_out, ks_out, vs_out, kw_out, vw_out, gate_out):
    h = _rms(x_ref[...]) * g_ref[...]
    hb = h.astype(BF16)
    proj = _dot(hb, win_ref[...])
    lane = lax.broadcasted_iota(I32, (1, LANES), 1)
    lane_lo = lane < NSA_DK
    nq = NSA_HEADS * NSA_DK
    for p in range(nq // LANES):
        sl = slice(p * LANES, (p + 1) * LANES)
        q_out[:, sl] = (_pair_norm(proj[:, sl], qg_ref[...], lane_lo) * NSA_Q_SCALE).astype(BF16)
    part = lambda i: proj[:, nq + i * LANES:nq + (i + 1) * LANES]
    kc_out[...] = part(0)
    vc_out[...] = part(1)

    def lo_hi(t, lo_pad, hi_pad):
        r = pltpu.roll(t, NSA_DK, 1)
        return jnp.concatenate([jnp.where(lane_lo, t, lo_pad), jnp.where(lane_lo, hi_pad, r),
                                jnp.where(lane_lo, r, lo_pad), jnp.where(lane_lo, hi_pad, t)], axis=1).astype(BF16)

    prel = prel_ref[...]
    byte = lambda k: ((prel >> (8 * k)) & 255).astype(F32)
    half_lane = (lane & (NSA_DK - 1)) - AUG_POS
    kaug = blocks_ref[...] + jnp.where((half_lane == 0) | (half_lane == 3), byte(2),
                                       jnp.where((half_lane == 1) | (half_lane == 4), byte(1),
                                                 jnp.where((half_lane == 2) | (half_lane == 5), byte(0), 0.0)))
    ks_out[...] = lo_hi(_pair_norm(part(2), ksg_ref[...], lane_lo), kaug, kaug)
    kw_out[...] = lo_hi(_pair_norm(part(3), kwg_ref[...], lane_lo), kaug, kaug)
    gate_out[...] = jax.nn.sigmoid(part(4))

    vt = _dot_nt(wvt_ref[...], hb)
    tm = vt.shape[1]
    sub = lax.broadcasted_iota(I32, (NSA_DK, 1), 0)
    pad_lo = jnp.broadcast_to(jnp.where(sub == LO_ONE - NSA_DK, 1.0, 0.0), (NSA_DK, tm))
    pad_hi = jnp.broadcast_to(jnp.where(sub == HI_ONE, 1.0, 0.0), (NSA_DK, tm))
    for branch, out in enumerate((vs_out, vw_out)):
        pieces = []
        for grp in range(NSA_G):
            v = vt[(2 * branch + grp) * NSA_DK:(2 * branch + grp + 1) * NSA_DK, :]
            pieces += [v, pad_lo, pad_hi, v]
        out[...] = jnp.concatenate(pieces, axis=0).astype(BF16)


def _inproj1(x2d, positions, gain, w_in, q_norm, k_norm_slc, k_norm_win, tm=256):
    T, D = x2d.shape
    S = positions.shape[1]
    prel = (positions - positions[:, :1]).reshape(T, 1)
    own_block = np.arange(S)[:, None] // SLC_BLOCK == (np.arange(LANES)[None, :] % NSA_DK)
    blocks = jnp.asarray(own_block * MASK_BIG, F32)
    nq_, kvw = NSA_HEADS * NSA_DK, NSA_KVW
    keep = [w_in[:, :nq_ + 3 * kvw], w_in[:, nq_ + 4 * kvw:nq_ + 5 * kvw], w_in[:, nq_ + 6 * kvw:]]
    w_in_p = jnp.concatenate(keep, axis=1)
    n_in = w_in_p.shape[1]
    n_pad = -n_in % LANES
    w_in_p = jnp.pad(w_in_p, ((0, 0), (0, n_pad))).astype(BF16)
    two = lambda g: jnp.tile(g, 2).reshape(1, LANES)
    row = lambda n: pl.BlockSpec((tm, n), lambda i: (i, 0))
    nq = NSA_HEADS * NSA_DK
    vcols = lambda i: w_in[:, nq + i * NSA_KVW:nq + (i + 1) * NSA_KVW]
    w_vt = jnp.concatenate([vcols(3), vcols(5)], axis=1).T.astype(BF16)
    bf = lambda n: jax.ShapeDtypeStruct((T, n), BF16)
    f32 = lambda n: jax.ShapeDtypeStruct((T, n), F32)
    col = pl.BlockSpec((4 * LANES, tm), lambda i: (0, i))
    vt_shape = jax.ShapeDtypeStruct((4 * LANES, T), BF16)
    return pl.pallas_call(
        _inproj1_kernel,
        grid=(T // tm,),
        in_specs=[row(D), row(1), pl.BlockSpec((tm, LANES), lambda i: (i % (S // tm), 0)), _full((1, D)),
                  _full((D, n_in + n_pad)), _full((2 * NSA_KVW, D)), _full((1, LANES)), _full((1, LANES)),
                  _full((1, LANES))],
        out_specs=[row(nq), row(LANES), row(LANES), row(4 * LANES), col, row(4 * LANES), col, row(LANES)],
        out_shape=[bf(nq), f32(LANES), f32(LANES), bf(4 * LANES), vt_shape, bf(4 * LANES), vt_shape, f32(LANES)],
        compiler_params=_params("parallel"),
        name="inproj1",
    )(x2d, prel, blocks, gain.reshape(1, D), w_in_p, w_vt, two(q_norm), two(k_norm_slc), two(k_norm_win))


def _stack_pairs(q_ref, rows):
    return jnp.concatenate([q_ref[0, rows, p * LANES:(p + 1) * LANES] for p in range(NSA_PAIRS)], axis=0)


def _cmp_kernel(q_ref, kch_ref, vch_ref, pos3_ref, pend_ref, gate_ref, slope_ref,
                pek_ref, w1k_ref, w2k_ref, kcg_ref, pev_ref, w1v_ref, w2v_ref, ovl_ref, eye_ref,
                o_ref, sel_ref, kc_s, vc_s):
    tq = NSA_TQ
    S = q_ref.shape[1]

    def compress(ch_ref, pe_ref, w1_ref, w2_ref):
        a = ch_ref[0, 0]
        h_lo = _dot((a + pe_ref[0:1, :]).astype(BF16), w1_ref[0])
        h_hi = _dot((a + pe_ref[1:2, :]).astype(BF16), w1_ref[1])
        n = h_hi.shape[0]
        hid = jax.nn.gelu(h_lo + pltpu.roll(h_hi, n - 1, 0)).astype(BF16)
        return _dot(hid, w2_ref[0]), _dot(hid, w2_ref[1])

    k_lo, k_hi = compress(kch_ref, pek_ref, w1k_ref, w2k_ref)
    for e, kk in enumerate((k_lo, k_hi)):
        kk = kk * lax.rsqrt(jnp.sum(kk * kk, axis=-1, keepdims=True) * (1.0 / NSA_DK) + 1e-6) * kcg_ref[e:e + 1, :]
        kc_s[e] = kk.astype(BF16)
    v_lo, v_hi = compress(vch_ref, pev_ref, w1v_ref, w2v_ref)
    vc_s[0] = v_lo.T.astype(BF16)
    vc_s[1] = v_hi.T.astype(BF16)

    n_cmp = (S - CMP_BLOCK) // CMP_STRIDE + 1
    n_lanes = NSA_PAIRS * tq
    lane = lax.broadcasted_iota(I32, (1, LANES), 1)
    sub_lo = lax.broadcasted_iota(I32, (LANES, 1), 0) < NSA_DK
    blk_row = lax.broadcasted_iota(I32, (N_SEL, 1), 0)
    cmp_row = lax.broadcasted_iota(I32, (LANES, 1), 0)
    in_range = cmp_row < n_cmp

    def q_block(qi, carry):
        t0 = pl.multiple_of(qi * tq, tq)
        rows = pl.ds(t0, tq)
        qs = _stack_pairs(q_ref, rows)
        pos_q0 = pos3_ref[0, pl.ds(qi * (tq // LANES), 1), :][:, 0:1]
        posrel = (pend_ref[0] - pos_q0).astype(F32)
        t_lane = t0 + (lax.broadcasted_iota(I32, (1, n_lanes), 1) & (tq - 1))
        valid = (t_lane >= CMP_STRIDE * cmp_row + (CMP_BLOCK - 1)) & in_range
        gt = gate_ref[0, 0, :, rows]
        psum = jnp.zeros((LANES, tq), F32)
        outs = []
        for e in range(2):
            slopes = jnp.concatenate([jnp.broadcast_to(slope_ref[0, NSA_PAIRS * e + p:NSA_PAIRS * e + p + 1, 0:1], (1, tq))
                                      for p in range(NSA_PAIRS)], axis=1)
            s = _dot_nt(kc_s[e], qs) + posrel * slopes
            s = jnp.where(valid, s, NEG_INF)
            m = jnp.max(s, axis=0, keepdims=True)
            p = jnp.where(valid, jnp.exp2(s - m), 0.0)
            p = p / jnp.maximum(jnp.sum(p, axis=0, keepdims=True), 1e-20)
            psum = psum + functools.reduce(jnp.add, [p[:, i * tq:(i + 1) * tq] for i in range(NSA_PAIRS)])
            gate = jnp.concatenate([gt[2 * i + e:2 * i + e + 1, :] for i in range(NSA_PAIRS)], axis=1)
            outs.append(_dot(vc_s[e], p.astype(BF16)) * gate)
        z = jnp.where(sub_lo, outs[0], outs[1])
        imp = jnp.dot(ovl_ref[...], psum, preferred_element_type=F32, precision=lax.Precision.HIGHEST)
        cur = (t0 + lax.broadcasted_iota(I32, (1, tq), 1)) // SLC_BLOCK
        forced = (blk_row == 0) | (blk_row == cur) | (blk_row == cur - 1)
        imp = jnp.where(forced, FORCE_SCORE, jnp.where(blk_row <= cur, imp, -1.0))
        rank = jnp.zeros((N_SEL, tq), I32)
        for i in range(N_SEL):
            ri = imp[i:i + 1, :]
            rank = rank + jnp.where((ri > imp) | ((ri == imp) & (blk_row > i)), 1, 0)
        sel_t = jnp.where(rank < SLC_TOP_N, 1.0, 0.0).astype(BF16)
        gap = jnp.zeros((NSA_DK - N_SEL, tq), BF16)
        sel_t = jnp.concatenate([sel_t, gap, sel_t, gap], axis=0)
        sel = _dot_nt(eye_ref[...], sel_t)
        sel_ref[0, 0, rows, :] = jnp.where((lane & (NSA_DK - 1)) < N_SEL, sel - 1.0, 0.0).astype(BF16)
        for p in range(NSA_PAIRS):
            o_ref[0, rows, p * LANES:(p + 1) * LANES] = z[:, p * tq:(p + 1) * tq].T.astype(BF16)
        return carry

    lax.fori_loop(0, S // tq, q_block, 0)


def _lo_hi_cols(w):
    z = jnp.zeros_like(w)
    return jnp.stack([jnp.concatenate([w, z], axis=1), jnp.concatenate([z, w], axis=1)])


def _nsa_cmp(q, kch, vch, pos3, pend, gates_g, slopes_t, pe_k, w1_k, w2_k, k_norm_cmp, pe_v, w1_v, w2_v):
    B, S, _ = q.shape
    G = NSA_G
    nch = S // CMP_STRIDE
    half = CMP_STRIDE * NSA_DK
    pe2 = lambda pe: pe.reshape(2, half)
    w1_2 = lambda w: w.reshape(2, half, CMP_HIDDEN).astype(BF16)
    n_sel = S // SLC_BLOCK
    n_cmp = (S - CMP_BLOCK) // CMP_STRIDE + 1
    cmp_start = np.arange(LANES) * CMP_STRIDE
    slc_start = np.arange(n_sel) * SLC_BLOCK
    overlap = ((cmp_start[None, :] < slc_start[:, None] + SLC_BLOCK) & (cmp_start[None, :] + CMP_BLOCK > slc_start[:, None])
               & (np.arange(LANES)[None, :] < n_cmp))
    ovl = jnp.asarray(overlap, F32)
    eye = jnp.eye(NSA_TQ, dtype=BF16)
    qspec = pl.BlockSpec((1, S, NSA_PAIRS * LANES), lambda b, g: (b, 0, g))
    chspec = pl.BlockSpec((1, 1, nch, half), lambda b, g: (b, g, 0, 0))
    return pl.pallas_call(
        _cmp_kernel,
        grid=(B, G),
        in_specs=[qspec, chspec, chspec,
                  pl.BlockSpec((1, S // LANES, LANES), lambda b, g: (b, 0, 0)),
                  pl.BlockSpec((1, LANES, 1), lambda b, g: (b, 0, 0)),
                  pl.BlockSpec((1, 1, 3 * NSA_HPG, S), lambda b, g: (b, g, 0, 0)),
                  pl.BlockSpec((1, NSA_HPG, LANES), lambda b, g: (g, 0, 0)),
                  _full((2, half)), _full((2, half, CMP_HIDDEN)), _full((2, CMP_HIDDEN, LANES)), _full((2, LANES)),
                  _full((2, half)), _full((2, half, CMP_HIDDEN)), _full((2, CMP_HIDDEN, LANES)),
                  _full((n_sel, LANES)), _full((NSA_TQ, NSA_TQ))],
        out_specs=[qspec, pl.BlockSpec((1, 1, S, LANES), lambda b, g: (b, g, 0, 0))],
        out_shape=[jax.ShapeDtypeStruct((B, S, NSA_HEADS * NSA_DK), BF16),
                   jax.ShapeDtypeStruct((B, G, S, LANES), BF16)],
        scratch_shapes=[pltpu.VMEM((2, nch, LANES), BF16), pltpu.VMEM((2, nch, LANES), BF16)],
        compiler_params=_params("parallel", "parallel"),
        name="nsa_cmp",
    )(q, kch, vch, pos3, pend, gates_g, slopes_t, pe2(pe_k), w1_2(w1_k), _lo_hi_cols(w2_k).astype(BF16),
      _lo_hi_cols(k_norm_cmp.reshape(1, -1)).reshape(2, LANES), pe2(pe_v), w1_2(w1_v),
      _lo_hi_cols(w2_v).astype(BF16), ovl, eye)


def _slc_win_kernel(q_ref, ks_ref, vs_ref, kw_ref, vw_ref, sel_ref, gate_ref, slopeq_ref, ocmp_ref, o_ref):
    tq = SLC_TQ
    S = q_ref.shape[1]
    lane = lax.broadcasted_iota(I32, (1, LANES), 1)
    lane_lo = lane < NSA_DK
    sub_lo = lax.broadcasted_iota(I32, (LANES, 1), 0) < NSA_DK
    ones_row = (LO_ONE, HI_ONE)

    def q_block(qi, carry):
        t0 = pl.multiple_of(qi * tq, tq)
        rows = pl.ds(t0, tq)
        qs = _stack_pairs(q_ref, rows)
        selm1 = sel_ref[0, 0, rows, :]

        def q_aug(e, with_sel):
            feats = [slopeq_ref[0, NSA_PAIRS * e + p:NSA_PAIRS * e + p + 1, :] for p in range(NSA_PAIRS)]
            if with_sel:
                extra = jnp.concatenate([selm1 + f for f in feats], axis=0)
            else:
                extra = jnp.concatenate([jnp.broadcast_to(f, (tq, LANES)) for f in feats], axis=0)
            return jnp.where(lane_lo, qs, extra) if e == 0 else jnp.where(lane_lo, extra, qs)

        def k_aug(k_ref, e, r0, n):
            return k_ref[0, pl.ds(r0, n), e * LANES:(e + 1) * LANES]

        gt = gate_ref[0, 0, :, rows]
        tq_lane = t0 + (lax.broadcasted_iota(I32, (1, NSA_PAIRS * tq), 1) & (tq - 1))

        def gated(accs, branch):
            outs = []
            for e, acc in enumerate(accs):
                gate = jnp.concatenate([gt[branch * NSA_HPG + 2 * p + e:branch * NSA_HPG + 2 * p + e + 1, :]
                                        for p in range(NSA_PAIRS)], axis=1)
                outs.append(acc * (gate / acc[ones_row[e]:ones_row[e] + 1, :]))
            return jnp.where(sub_lo, outs[0], outs[1])

        qa = [q_aug(0, True), q_aug(1, True)]
        krow = lax.broadcasted_iota(I32, (SLC_TK, 1), 0)

        def slc_scores(j):
            r0 = pl.multiple_of(j * SLC_TK, SLC_TK)
            return [_dot_nt(k_aug(ks_ref, e, r0, SLC_TK), qa[e]) for e in range(2)]

        def slc_softmax(j, ss, state, last):
            r0 = pl.multiple_of(j * SLC_TK, SLC_TK)
            out = []
            for e in range(2):
                m, acc = state[e]
                s = jnp.where(r0 + krow <= tq_lane, ss[e], NEG_INF) if last else ss[e]
                m_new = jnp.maximum(m, jnp.max(s, axis=0, keepdims=True))
                p = jnp.exp2(s - m_new).astype(BF16)
                pv = _dot(vs_ref[e * LANES:(e + 1) * LANES, pl.ds(r0, SLC_TK)], p)
                out.append((m_new, jnp.exp2(m - m_new) * acc + pv))
            return tuple(out)

        init = (jnp.full((1, NSA_PAIRS * tq), NEG_INF, F32), jnp.zeros((LANES, NSA_PAIRS * tq), F32))
        n_full = qi // (SLC_TK // tq)
        state = lax.fori_loop(0, n_full, lambda j, st: slc_softmax(j, slc_scores(j), st, False), (init, init))

        w0 = pl.multiple_of(jnp.maximum(t0 - WINDOW, 0), tq)
        ss_last = slc_scores(n_full)
        ss_win = [_dot_nt(k_aug(kw_ref, e, w0, WIN_SPAN), q_aug(e, False)) for e in range(2)]
        state = slc_softmax(n_full, ss_last, state, True)
        z = gated([state[0][1], state[1][1]], 1)
        rel = tq_lane - (w0 + lax.broadcasted_iota(I32, (WIN_SPAN, 1), 0))
        wmask = (rel >= 0) & (rel < WINDOW)
        accs = []
        for e in range(2):
            s = jnp.where(wmask, ss_win[e], NEG_INF)
            p = jnp.exp2(s - jnp.max(s, axis=0, keepdims=True)).astype(BF16)
            accs.append(_dot(vw_ref[e * LANES:(e + 1) * LANES, pl.ds(w0, WIN_SPAN)], p))
        z = z + gated(accs, 2)

        for p in range(NSA_PAIRS):
            sl = slice(p * LANES, (p + 1) * LANES)
            o_ref[0, rows, sl] = (ocmp_ref[0, rows, sl].astype(F32) + z[:, p * tq:(p + 1) * tq].T).astype(BF16)
        return carry

    lax.fori_loop(0, S // tq, q_block, 0)


def _slope_aug(slopes_eo):
    hi = slopes_eo.astype(BF16).astype(F32)
    lo = (slopes_eo - hi).astype(BF16).astype(F32)
    w = jnp.asarray([65536.0, 256.0, 1.0], F32)
    feats = jnp.concatenate([hi[..., None] * w, lo[..., None] * w], axis=-1)
    half = jnp.pad(feats, ((0, 0), (0, 0), (AUG_POS, NSA_DK - AUG_POS - 6)))
    return jnp.concatenate([half, half], axis=-1).astype(BF16)


def _nsa_slc_win(q, ks4, vst4, kw4, vwt4, sel, gates_t, slopeq, o_cmp):
    B, S, _ = q.shape
    qspec = pl.BlockSpec((1, S, NSA_PAIRS * LANES), lambda b, g: (b, 0, g))
    kvspec = pl.BlockSpec((1, S, 2 * LANES), lambda b, g: (b, 0, g))
    vtspec = pl.BlockSpec((2 * LANES, S), lambda b, g: (g, b))
    return pl.pallas_call(
        _slc_win_kernel,
        grid=(B, NSA_G),
        in_specs=[qspec, kvspec, vtspec, kvspec, vtspec,
                  pl.BlockSpec((1, 1, S, LANES), lambda b, g: (b, g, 0, 0)),
                  pl.BlockSpec((1, 1, 3 * NSA_HPG, S), lambda b, g: (b, g, 0, 0)),
                  pl.BlockSpec((1, NSA_HPG, LANES), lambda b, g: (g, 0, 0)),
                  qspec],
        out_specs=qspec,
        out_shape=jax.ShapeDtypeStruct((B, S, NSA_HEADS * NSA_DK), BF16),
        compiler_params=_params("parallel", "parallel"),
        name="nsa_slc_win",
    )(q, ks4, vst4, kw4, vwt4, sel, gates_t, slopeq, o_cmp)


def _outproj_kernel(o_ref, x_ref, w_ref, out_ref):
    out_ref[...] = x_ref[...] + _dot(o_ref[...], w_ref[...])


def _outproj(o2d, x2d, w_out, tm=512):
    T, D = x2d.shape
    K = o2d.shape[1]
    return pl.pallas_call(
        _outproj_kernel,
        grid=(T // tm,),
        in_specs=[pl.BlockSpec((tm, K), lambda i: (i, 0)), pl.BlockSpec((tm, D), lambda i: (i, 0)), _full((K, D))],
        out_specs=pl.BlockSpec((tm, D), lambda i: (i, 0)),
        out_shape=jax.ShapeDtypeStruct((T, D), F32),
        compiler_params=_params("parallel"),
        name="outproj1",
    )(o2d, x2d, w_out.astype(BF16))


def _mixer1(x, positions, gain, w_in, q_norm, k_norm_cmp, k_norm_slc, k_norm_win,
            pe_k, w1_k, w2_k, pe_v, w1_v, w2_v, w_out):
    B, S, D = x.shape
    T = B * S
    G, HPG, DK = NSA_G, NSA_HPG, NSA_DK
    assert S // SLC_BLOCK == N_SEL and S % NSA_TQ == 0 and NSA_TQ % LANES == 0
    x2d = x.reshape(T, D)
    q, kc, vc, ks4, vs4, kw4, vw4, gates = _inproj1(x2d, positions, gain, w_in, q_norm, k_norm_slc,
                                                    k_norm_win)
    b3 = lambda t: t.reshape(B, S, -1)
    chunks = lambda t: t.reshape(B, S, G, DK).transpose(0, 2, 1, 3).reshape(B, G, S // CMP_STRIDE, CMP_STRIDE * DK)
    gates_t = gates[:, :3 * NSA_HEADS].reshape(B, S, 3, G, HPG).transpose(0, 3, 2, 4, 1).reshape(B, G, 3 * HPG, S)
    n_cmp = (S - CMP_BLOCK) // CMP_STRIDE + 1
    pend = jnp.pad(positions[:, CMP_BLOCK - 1::CMP_STRIDE][:, :n_cmp], ((0, 0), (0, LANES - n_cmp))).reshape(B, LANES, 1)
    pos3 = positions.reshape(B, S // LANES, LANES)
    slopes = jnp.asarray(2.0 ** (-8.0 * np.arange(1, NSA_HEADS + 1) / NSA_HEADS), F32) * LOG2E
    slopes_eo = slopes.reshape(G, NSA_PAIRS, 2).transpose(0, 2, 1).reshape(G, HPG)
    slopes_t = jnp.broadcast_to(slopes_eo[:, :, None], (G, HPG, LANES))
    o_cmp, sel = _nsa_cmp(b3(q), chunks(kc), chunks(vc), pos3, pend, gates_t, slopes_t,
                          pe_k, w1_k, w2_k, k_norm_cmp, pe_v, w1_v, w2_v)
    o = _nsa_slc_win(b3(q), b3(ks4), vs4, b3(kw4), vw4, sel, gates_t, _slope_aug(slopes_eo), o_cmp)
    return _outproj(o.reshape(T, -1), x2d, w_out).reshape(B, S, D)


def kernel(x, positions, norm_mix, norm_ffn, mix0_w_in, mla_q_a_norm, mla_w_uq, mla_kv_a_norm, mla_w_ukv, mla_q_norm, mla_k_norm, conv_dw_w, conv_dw_b, conv_ln_g, conv_ln_b, mix0_w_out, nsa_w_in, nsa_q_norm, nsa_k_norm_cmp, nsa_k_norm_slc, nsa_k_norm_win, nsa_cmp_pe_k, nsa_cmp_w1_k, nsa_cmp_w2_k, nsa_cmp_pe_v, nsa_cmp_w1_v, nsa_cmp_w2_v, nsa_w_out, moe_router_group, moe_router_group_b, moe_router_expert, moe_router_expert_b, moe_w_gate, moe_w_up, moe_w_down):
    def moe(x, layer):
        return _moe(x, norm_ffn[layer], moe_router_group[layer], moe_router_group_b[layer], moe_router_expert[layer],
                    moe_router_expert_b[layer], moe_w_gate, moe_w_up, moe_w_down, layer)

    x = _mixer0(x, positions, norm_mix[0], mix0_w_in[0], mla_q_a_norm[0], mla_w_uq[0], mla_kv_a_norm[0], mla_w_ukv[0],
                mla_q_norm[0], mla_k_norm[0], conv_dw_w[0], conv_dw_b[0], conv_ln_g[0], conv_ln_b[0], mix0_w_out[0])
    x = moe(x, 0)
    x = _mixer1(x, positions, norm_mix[1], nsa_w_in[0], nsa_q_norm[0], nsa_k_norm_cmp[0], nsa_k_norm_slc[0],
                nsa_k_norm_win[0], nsa_cmp_pe_k[0], nsa_cmp_w1_k[0], nsa_cmp_w2_k[0], nsa_cmp_pe_v[0],
                nsa_cmp_w1_v[0], nsa_cmp_w2_v[0], nsa_w_out[0])
    return moe(x, 1)
```

```python
import functools

import numpy as np
import jax
import jax.numpy as jnp
from jax import lax
from jax.experimental import pallas as pl
from jax.experimental.pallas import tpu as pltpu

F32 = jnp.float32
BF16 = jnp.bfloat16
I32 = jnp.int32

LANES = 128
NEG_INF = -1e30
LOG2E = 1.4426950408889634
Q_BLOCK = 128
MLA_HEADS = 8
MLA_Q_LORA = 256
MLA_KV_LORA = 128
MLA_NOPE = 64
MLA_ROPE = 32
MLA_V = 64
MLA_QK = MLA_NOPE + MLA_ROPE
ROPE_THETA = 10000.0
CONV_CH = 512
CONV_W = 31
NSA_HEADS = 16
NSA_G = 2
NSA_HPG = NSA_HEADS // NSA_G
NSA_DK = 64
NSA_KVW = NSA_G * NSA_DK
CMP_BLOCK = 32
CMP_STRIDE = 16
CMP_HIDDEN = 128
SLC_BLOCK = 64
SLC_TOP_N = 8
WINDOW = 256
FORCE_SCORE = 1e4
MOE_GROUPS = 4
MOE_EPG = 8
MOE_EXPERTS = MOE_GROUPS * MOE_EPG
MOE_HIDDEN = 256
MOE_ROW_BLOCK = 512

VMEM_LIMIT = 56 * 1024 * 1024


def _params(*sem):
    return pltpu.CompilerParams(dimension_semantics=sem, vmem_limit_bytes=VMEM_LIMIT)


def _full(shape):
    n = len(shape)
    return pl.BlockSpec(shape, lambda *_: (0,) * n)


def _row_mean(x):
    n = x.shape[-1]
    folded = functools.reduce(jnp.add, [x[:, c:c + LANES] for c in range(0, n, LANES)])
    return jnp.sum(folded, axis=-1, keepdims=True) * (1.0 / n)


def _rms(x, eps=1e-6):
    return x * lax.rsqrt(_row_mean(x * x) + eps)


def _dot(a, b):
    return jnp.dot(a, b, preferred_element_type=F32)


def _group_matrix(groups):
    width = LANES // groups
    same = (lax.broadcasted_iota(I32, (LANES, LANES), 0) // width) == (lax.broadcasted_iota(I32, (LANES, LANES), 1) // width)
    return jnp.where(same, 1.0, 0.0).astype(BF16)


def _group_sums(t, group_matrix):
    return _dot(t.astype(BF16), group_matrix)


def _dot_nt(a, b, **kw):
    return lax.dot_general(a, b, (((1,), (1,)), ((), ())), preferred_element_type=F32, **kw)


def _inproj0_kernel(x_ref, pos_ref, gmix_ref, win_ref, qan_ref, wuq_ref, kvan_ref, wuk_ref, wuv_ref,
                    qg_ref, kg_ref, invf_ref, q_out, k_out, v_out, u_out):
    h = _rms(x_ref[...]) * gmix_ref[...]
    proj = _dot(h.astype(BF16), win_ref[...])
    c_q = proj[:, :MLA_Q_LORA]
    c_kv = proj[:, MLA_Q_LORA:MLA_Q_LORA + MLA_KV_LORA]
    k_rope = proj[:, 384:512]
    a = proj[:, 512:512 + CONV_CH]
    g = proj[:, 512 + CONV_CH:]
    u_out[...] = a * jax.nn.sigmoid(g)
    cqn = (_rms(c_q) * qan_ref[...]).astype(BF16)
    ckvn = (_rms(c_kv) * kvan_ref[...]).astype(BF16)
    q = _dot(cqn, wuq_ref[...])
    kn = _dot(ckvn, wuk_ref[...])
    slot_row = lax.broadcasted_iota(I32, (MLA_HEADS * LANES, 1), 0) & (LANES - 1)
    v_out[...] = (_dot_nt(wuv_ref[...], ckvn) + jnp.where(slot_row == MLA_V, 1.0, 0.0)).astype(BF16)
    ang = invf_ref[...] * pos_ref[...].astype(F32)
    cos_t, sin_t = jnp.cos(ang), jnp.sin(ang)
    tm = ang.shape[1]
    tail = jnp.zeros((LANES - MLA_QK, tm), F32)
    cos = jnp.concatenate([jnp.ones((MLA_NOPE, tm), F32), cos_t, cos_t, tail], axis=0).T
    sin = jnp.concatenate([jnp.zeros((MLA_NOPE, tm), F32), -sin_t, sin_t, tail], axis=0).T
    lane = lax.broadcasted_iota(I32, (1, LANES), 1)
    first_half = (lane >= MLA_NOPE) & (lane < MLA_NOPE + MLA_ROPE // 2)
    scale = MLA_QK ** -0.5 * LOG2E

    all_lanes = _group_matrix(1)

    def norm_rope(t, gain):
        t = t * lax.rsqrt(_group_sums(t * t, all_lanes) * (1.0 / MLA_QK) + 1e-6) * gain
        partner = jnp.where(first_half, pltpu.roll(t, LANES - MLA_ROPE // 2, 1), pltpu.roll(t, MLA_ROPE // 2, 1))
        return t * cos + partner * sin

    for hd in range(MLA_HEADS):
        sl = slice(hd * LANES, (hd + 1) * LANES)
        q_out[:, sl] = (norm_rope(q[:, sl], qg_ref[...]) * scale).astype(BF16)
        k_out[:, sl] = norm_rope(kn[:, sl] + k_rope, kg_ref[...]).astype(BF16)


def _head_slots(w, n_heads, width, offset=0):
    k = w.shape[0]
    w = w.reshape(k, n_heads, width)
    w = jnp.pad(w, ((0, 0), (0, 0), (offset, LANES - width - offset)))
    return w.reshape(k, n_heads * LANES)


def _inproj0(x2d, pos_row, gmix, w_in, q_a_norm, w_uq, kv_a_norm, w_ukv, q_norm, k_norm, tm=512):
    T, D = x2d.shape
    H = MLA_HEADS
    w_krope = jnp.pad(w_in[:, 384:416], ((0, 0), (MLA_NOPE, LANES - MLA_NOPE - MLA_ROPE)))
    w_in_p = jnp.concatenate([w_in[:, :384], w_krope, w_in[:, 416:]], axis=1).astype(BF16)
    w_uq_p = _head_slots(w_uq, H, MLA_QK).astype(BF16)
    w_ukv3 = w_ukv.reshape(MLA_KV_LORA, H, MLA_NOPE + MLA_V)
    w_uk_p = _head_slots(w_ukv3[:, :, :MLA_NOPE].reshape(MLA_KV_LORA, H * MLA_NOPE), H, MLA_NOPE).astype(BF16)
    w_uv = _head_slots(w_ukv3[:, :, MLA_NOPE:].reshape(MLA_KV_LORA, H * MLA_V), H, MLA_V).T.astype(BF16)
    pad = LANES - MLA_QK
    qg = jnp.pad(q_norm, (0, pad)).reshape(1, LANES)
    kg = jnp.pad(k_norm, (0, pad)).reshape(1, LANES)
    half = MLA_ROPE // 2
    inv_freq = (ROPE_THETA ** (-jnp.arange(half, dtype=F32) / half)).reshape(half, 1)
    row = lambda n: pl.BlockSpec((tm, n), lambda i: (i, 0))
    n_in = w_in_p.shape[1]
    return pl.pallas_call(
        _inproj0_kernel,
        grid=(T // tm,),
        in_specs=[row(D), pl.BlockSpec((1, tm), lambda i: (0, i)), _full((1, D)), _full((D, n_in)),
                  _full((1, MLA_Q_LORA)), _full((MLA_Q_LORA, H * LANES)), _full((1, MLA_KV_LORA)),
                  _full((MLA_KV_LORA, H * LANES)), _full((H * LANES, MLA_KV_LORA)), _full((1, LANES)),
                  _full((1, LANES)), _full((half, 1))],
        out_specs=[row(H * LANES), row(H * LANES), pl.BlockSpec((H * LANES, tm), lambda i: (0, i)), row(CONV_CH)],
        out_shape=[jax.ShapeDtypeStruct((T, H * LANES), BF16), jax.ShapeDtypeStruct((T, H * LANES), BF16),
                   jax.ShapeDtypeStruct((H * LANES, T), BF16), jax.ShapeDtypeStruct((T, CONV_CH), F32)],
        compiler_params=_params("parallel"),
        name="inproj0",
    )(x2d, pos_row, gmix.reshape(1, D), w_in_p, q_a_norm.reshape(1, -1), w_uq_p, kv_a_norm.reshape(1, -1),
      w_uk_p, w_uv, qg, kg, inv_freq)


MLA_TQ = 512
MLA_TK = 512
MLA_ONE = MLA_V


def _mla_attn_kernel(q_ref, k_ref, vt_ref, o_ref):
    S = q_ref.shape[1]
    tq, tk = MLA_TQ, MLA_TK
    krow = lax.broadcasted_iota(I32, (tk, 1), 0)
    qcol = lax.broadcasted_iota(I32, (1, tq), 1)

    def q_block(qi, carry):
        q0 = pl.multiple_of(qi * tq, tq)
        qs = [q_ref[0, pl.ds(q0, tq), hh * LANES:(hh + 1) * LANES] for hh in range(2)]

        def kv_step(j, state, masked):
            k0 = pl.multiple_of(j * tk, tk)
            hs = [slice(hh * LANES, (hh + 1) * LANES) for hh in range(2)]
            ss = [_dot_nt(k_ref[0, pl.ds(k0, tk), hs[hh]], qs[hh]) for hh in range(2)]
            out = []
            for hh in range(2):
                m, acc = state[hh]
                s = jnp.where(k0 + krow <= q0 + qcol, ss[hh], NEG_INF) if masked else ss[hh]
                m_new = jnp.maximum(m, jnp.max(s, axis=0, keepdims=True))
                p = jnp.exp2(s - m_new).astype(BF16)
                out.append((m_new, jnp.exp2(m - m_new) * acc + _dot(vt_ref[hs[hh], pl.ds(k0, tk)], p)))
            return tuple(out)

        init = (jnp.full((1, tq), NEG_INF, F32), jnp.zeros((LANES, tq), F32))
        n_full = (qi * tq) // tk
        state = lax.fori_loop(0, n_full, functools.partial(kv_step, masked=False), (init, init))
        state = kv_step(n_full, state, True)
        outs = []
        for hh in range(2):
            acc = state[hh][1]
            outs.append((acc * (1.0 / acc[MLA_ONE:MLA_ONE + 1, :]))[:MLA_V, :])
        o_ref[0, pl.ds(q0, tq), :] = jnp.concatenate(outs, axis=0).T.astype(BF16)
        return carry

    lax.fori_loop(0, S // tq, q_block, 0)


def _mla_attn(q, k, vt):
    B, S, _ = q.shape
    spec = pl.BlockSpec((1, S, 2 * LANES), lambda b, h: (b, 0, h))
    return pl.pallas_call(
        _mla_attn_kernel,
        grid=(B, MLA_HEADS // 2),
        in_specs=[spec, spec, pl.BlockSpec((2 * LANES, S), lambda b, h: (h, b))],
        out_specs=pl.BlockSpec((1, S, 2 * MLA_V), lambda b, h: (b, 0, h)),
        out_shape=jax.ShapeDtypeStruct((B, S, MLA_HEADS * MLA_V), BF16),
        compiler_params=_params("parallel", "parallel"),
        name="mla_attn",
    )(q, k, vt)


CONV_TILE = 64
CONV_PAD = 32


CONV_ROWS = 512


def _conv_out0_kernel(u_ref, o_ref, x_ref, dww_ref, dwb_ref, lng_ref, lnb_ref, wo_ref, out_ref, upad, act, shifted):
    step = pl.program_id(1)

    @pl.when(step == 0)
    def _():
        upad[0:CONV_PAD, :] = jnp.zeros((CONV_PAD, CONV_CH), F32)
        upad[CONV_PAD:, :] = u_ref[0]

    lead = CONV_PAD - (CONV_W - 1)
    base = step * CONV_ROWS

    def tile(i, carry):
        t0 = pl.multiple_of(i * CONV_TILE, CONV_TILE)
        win = upad[pl.ds(pl.multiple_of(base + t0, CONV_TILE), CONV_TILE + CONV_PAD), :]
        acc = jnp.zeros((CONV_TILE, CONV_CH), F32) + dwb_ref[...]
        span = CONV_TILE + CONV_PAD - 8
        for r in range(1, 8):
            shifted[r - 1] = win[r:r + span, :]
        for j in range(CONV_W):
            r, k = (lead + j) % 8, (lead + j) // 8 * 8
            tap = win[k:k + CONV_TILE, :] if r == 0 else shifted[r - 1, k:k + CONV_TILE, :]
            acc = acc + tap * dww_ref[j:j + 1, :]
        xc = acc - _row_mean(acc)
        var = _row_mean(xc * xc)
        y = xc * lax.rsqrt(var + 1e-5) * lng_ref[...] + lnb_ref[...]
        act[pl.ds(t0, CONV_TILE), :] = (y * jax.nn.sigmoid(y)).astype(BF16)
        return carry

    lax.fori_loop(0, CONV_ROWS // CONV_TILE, tile, 0)
    n_o = o_ref.shape[2]
    y = _dot(o_ref[0], wo_ref[:n_o, :]) + _dot(act[...], wo_ref[n_o:, :])
    out_ref[0] = x_ref[0] + y


def _conv_out0(u, o_mla, x, dw_w, dw_b, ln_g, ln_b, w_out):
    B, S, D = x.shape
    n_o = o_mla.shape[2]
    w_out_p = w_out.astype(BF16)
    tspec = lambda n: pl.BlockSpec((1, CONV_ROWS, n), lambda b, t: (b, t, 0))
    return pl.pallas_call(
        _conv_out0_kernel,
        grid=(B, S // CONV_ROWS),
        in_specs=[pl.BlockSpec((1, S, CONV_CH), lambda b, t: (b, 0, 0)), tspec(n_o), tspec(D),
                  _full((CONV_W, CONV_CH)), _full((1, CONV_CH)), _full((1, CONV_CH)), _full((1, CONV_CH)),
                  _full((n_o + CONV_CH, D))],
        out_specs=tspec(D),
        out_shape=jax.ShapeDtypeStruct((B, S, D), F32),
        scratch_shapes=[pltpu.VMEM((S + CONV_PAD, CONV_CH), F32), pltpu.VMEM((CONV_ROWS, CONV_CH), BF16),
                        pltpu.VMEM((7, CONV_TILE + CONV_PAD - 8, CONV_CH), F32)],
        compiler_params=_params("parallel", "arbitrary"),
        name="conv_out0",
    )(u, o_mla, x, dw_w, dw_b.reshape(1, -1), ln_g.reshape(1, -1), ln_b.reshape(1, -1), w_out_p)


def _mixer0(x, positions, gmix, w_in, q_a_norm, w_uq, kv_a_norm, w_ukv, q_norm, k_norm,
            dw_w, dw_b, ln_g, ln_b, w_out):
    B, S, D = x.shape
    T = B * S
    q, k, v, u = _inproj0(x.reshape(T, D), positions.reshape(1, T), gmix, w_in, q_a_norm, w_uq, kv_a_norm, w_ukv,
                          q_norm, k_norm)
    o = _mla_attn(q.reshape(B, S, -1), k.reshape(B, S, -1), v)
    return _conv_out0(u.reshape(B, S, -1), o, x, dw_w, dw_b, ln_g, ln_b, w_out)


ROUTE_TILE = 512
ROUTE_ROWS = 40


def _route_kernel(x_ref, g_ref, wr_ref, br_ref, tri_ref, hn_ref, oi_ref, of_ref, cnt_ref, carry):
    @pl.when(pl.program_id(0) == 0)
    def _():
        carry[...] = jnp.zeros_like(carry)

    hn = _rms(x_ref[...]) * g_ref[...]
    _slab_store(hn_ref, hn)
    tm = hn.shape[0]
    logits = _dot_nt(wr_ref[...], hn, precision=lax.Precision.HIGHEST) + br_ref[...]
    gl = logits[MOE_EXPERTS:MOE_EXPERTS + MOE_GROUPS]
    rid_g = lax.broadcasted_iota(I32, (MOE_GROUPS, tm), 0)
    gmax = jnp.max(gl, axis=0, keepdims=True)
    grp = jnp.min(jnp.where(gl == gmax, rid_g, MOE_GROUPS), axis=0, keepdims=True)
    g_w = 1.0 / jnp.sum(jnp.exp(gl - gmax), axis=0, keepdims=True)
    e_in = jnp.zeros((MOE_EPG, tm), F32)
    for gi in range(MOE_GROUPS):
        e_in = jnp.where(grp == gi, logits[gi * MOE_EPG:(gi + 1) * MOE_EPG], e_in)
    rid_e = lax.broadcasted_iota(I32, (MOE_EPG, tm), 0)
    v1 = jnp.max(e_in, axis=0, keepdims=True)
    i1 = jnp.min(jnp.where(e_in == v1, rid_e, MOE_EPG), axis=0, keepdims=True)
    rest = jnp.where(rid_e == i1, -jnp.inf, e_in)
    v2 = jnp.max(rest, axis=0, keepdims=True)
    i2 = jnp.min(jnp.where(rest == v2, rid_e, MOE_EPG), axis=0, keepdims=True)
    ex = jnp.exp(v2 - v1)
    den = 1.0 + ex
    e1 = grp * MOE_EPG + i1
    e2 = grp * MOE_EPG + i2
    rid = lax.broadcasted_iota(I32, (MOE_EXPERTS, tm), 0)
    hit1 = rid == e1
    hit2 = rid == e2
    member = jnp.where(hit1 | hit2, 1.0, 0.0)
    before = _dot(member.astype(BF16), tri_ref[...]) + carry[...]
    r1 = jnp.sum(jnp.where(hit1, before, 0.0), axis=0, keepdims=True)
    r2 = jnp.sum(jnp.where(hit2, before, 0.0), axis=0, keepdims=True)
    carry[...] = carry[...] + jnp.sum(member, axis=1, keepdims=True)
    oi_ref[...] = jnp.zeros_like(oi_ref)
    oi_ref[0:1, :] = e1
    oi_ref[1:2, :] = e2
    oi_ref[2:3, :] = r1.astype(I32)
    oi_ref[3:4, :] = r2.astype(I32)
    of_ref[...] = jnp.zeros_like(of_ref)
    of_ref[0:1, :] = g_w / den
    of_ref[1:2, :] = g_w * ex / den
    cnt_ref[...] = jnp.broadcast_to(carry[...], cnt_ref.shape).astype(I32)


def _route(x2d, gain, router_group, router_group_b, router_expert, router_expert_b):
    T, D = x2d.shape
    tm = ROUTE_TILE
    pad = ROUTE_ROWS - MOE_EXPERTS - MOE_GROUPS
    wr = jnp.concatenate([router_expert.T, router_group.T, jnp.zeros((pad, D), F32)], axis=0)
    br = jnp.concatenate([router_expert_b, router_group_b, jnp.zeros((pad,), F32)]).reshape(ROUTE_ROWS, 1)
    tri = (jnp.arange(tm)[:, None] < jnp.arange(tm)[None, :]).astype(BF16)
    return pl.pallas_call(
        _route_kernel,
        grid=(T // tm,),
        in_specs=[pl.BlockSpec((tm, D), lambda i: (i, 0)), _full((1, D)), _full((ROUTE_ROWS, D)),
                  _full((ROUTE_ROWS, 1)), _full((tm, tm))],
        out_specs=[pl.BlockSpec((tm * SLAB, LANES), lambda i: (i, 0)), pl.BlockSpec((8, tm), lambda i: (0, i)),
                   pl.BlockSpec((8, tm), lambda i: (0, i)), _full((MOE_EXPERTS, LANES))],
        out_shape=[jax.ShapeDtypeStruct((T * SLAB, LANES), F32), jax.ShapeDtypeStruct((8, T), I32),
                   jax.ShapeDtypeStruct((8, T), F32), jax.ShapeDtypeStruct((MOE_EXPERTS, LANES), I32)],
        scratch_shapes=[pltpu.VMEM((MOE_EXPERTS, 1), F32)],
        compiler_params=_params("arbitrary"),
        name="moe_route",
    )(x2d, gain.reshape(1, D), wr, br, tri)


MOVE_CHUNK = 512
MOVE_UNROLL = 8


SLAB = 8


def _slab_load(ref, n, first=0, stride=SLAB):
    return jnp.concatenate([ref[pl.ds(first + j, n, stride=stride), :] for j in range(SLAB)], axis=1)


def _slab_store(ref, value):
    n = value.shape[0]
    for j in range(SLAB):
        ref[pl.ds(j, n, stride=SLAB), :] = value[:, j * LANES:(j + 1) * LANES]


def _slab(row):
    return pl.ds(pl.multiple_of(row, SLAB), SLAB)


def _for_tokens(n, fn):
    def body(t, carry):
        fn(t, 0)
        fn(t, 1)
        return carry
    lax.fori_loop(0, n, body, 0, unroll=MOVE_UNROLL)


def _dispatch_kernel(dest_ref, hn_ref, xs_init_ref, xs_ref, sem):
    del xs_init_ref
    base = pl.program_id(0) * (2 * MOVE_CHUNK)

    def copy(t, k):
        return pltpu.make_async_copy(hn_ref.at[_slab(t * SLAB)], xs_ref.at[_slab(dest_ref[base + 2 * t + k])], sem)

    _for_tokens(MOVE_CHUNK, lambda t, k: copy(t, k).start())
    _for_tokens(MOVE_CHUNK, lambda t, k: copy(t, k).wait())


def _dispatch(dest, hn, n_rows):
    T = hn.shape[0] // SLAB
    tm = MOVE_CHUNK
    return pl.pallas_call(
        _dispatch_kernel,
        grid_spec=pltpu.PrefetchScalarGridSpec(
            num_scalar_prefetch=1,
            grid=(T // tm,),
            in_specs=[pl.BlockSpec((tm * SLAB, LANES), lambda i, d: (i, 0)), pl.BlockSpec(memory_space=pl.ANY)],
            out_specs=pl.BlockSpec(memory_space=pl.ANY),
            scratch_shapes=[pltpu.SemaphoreType.DMA(())],
        ),
        out_shape=jax.ShapeDtypeStruct((n_rows * SLAB, LANES), F32),
        input_output_aliases={2: 0},
        compiler_params=_params("arbitrary"),
        name="moe_dispatch",
    )(dest, hn, jnp.zeros((n_rows * SLAB, LANES), F32))


def _expert_kernel(blk_e_ref, n_used_ref, xs_ref, wg_ref, wu_ref, wd_ref, ys_ref, wg_s, wu_s, wd_s):
    b = pl.program_id(0)

    @pl.when((b == 0) | (blk_e_ref[b] != blk_e_ref[jnp.maximum(b - 1, 0)]))
    def _():
        wg_s[...] = wg_ref[0].astype(BF16)
        wu_s[...] = wu_ref[0].astype(BF16)
        wd_s[...] = wd_ref[0].astype(BF16)

    @pl.when(b < n_used_ref[0])
    def _():
        x = _slab_load(xs_ref, MOE_ROW_BLOCK).astype(BF16)
        gate = _dot(x, wg_s[...])
        hid = (gate * jax.nn.sigmoid(gate) * _dot(x, wu_s[...])).astype(BF16)
        _slab_store(ys_ref, _dot(hid, wd_s[...]))

    @pl.when(b >= n_used_ref[0])
    def _():
        ys_ref[...] = jnp.zeros_like(ys_ref)


def _experts(blk_expert, n_used, xs, w_gate, w_up, w_down, layer):
    R = xs.shape[0] // SLAB
    rb = MOE_ROW_BLOCK
    D = w_gate.shape[2]
    up_spec = pl.BlockSpec((None, 1, D, MOE_HIDDEN), lambda b, be, nu: (layer, be[b], 0, 0))
    return pl.pallas_call(
        _expert_kernel,
        grid_spec=pltpu.PrefetchScalarGridSpec(
            num_scalar_prefetch=2,
            grid=(R // rb,),
            in_specs=[pl.BlockSpec((rb * SLAB, LANES), lambda b, be, nu: (jnp.minimum(b, nu[0] - 1), 0)),
                      up_spec, up_spec,
                      pl.BlockSpec((None, 1, MOE_HIDDEN, D), lambda b, be, nu: (layer, be[b], 0, 0))],
            out_specs=pl.BlockSpec((rb * SLAB, LANES), lambda b, be, nu: (b, 0)),
            scratch_shapes=[pltpu.VMEM((D, MOE_HIDDEN), BF16), pltpu.VMEM((D, MOE_HIDDEN), BF16),
                            pltpu.VMEM((MOE_HIDDEN, D), BF16)],
        ),
        out_shape=jax.ShapeDtypeStruct((R * SLAB, LANES), F32),
        compiler_params=_params("arbitrary"),
        name="moe_experts",
    )(blk_expert, n_used, xs, w_gate, w_up, w_down)


def _combine_kernel(dest_ref, x_ref, gate_ref, ys_ref, out_ref, buf, sem):
    tm = x_ref.shape[0]
    step = pl.program_id(0)

    def copy(s, t, k):
        a = 2 * t + k
        slot = s % 2
        return pltpu.make_async_copy(ys_ref.at[_slab(dest_ref[s * (2 * tm) + a])], buf.at[slot, _slab(a * SLAB)],
                                     sem.at[slot])

    @pl.when(step == 0)
    def _():
        _for_tokens(tm, lambda t, k: copy(step, t, k).start())

    @pl.when(step + 1 < pl.num_programs(0))
    def _():
        _for_tokens(tm, lambda t, k: copy(step + 1, t, k).start())

    _for_tokens(tm, lambda t, k: copy(step, t, k).wait())
    mine = buf.at[step % 2]
    y0 = _slab_load(mine, tm, 0, 2 * SLAB)
    y1 = _slab_load(mine, tm, SLAB, 2 * SLAB)
    out_ref[...] = x_ref[...] + gate_ref[:, 0:1] * y0 + gate_ref[:, 1:2] * y1


def _combine(dest, x2d, gates_col, ys, tm=MOVE_CHUNK):
    T, D = x2d.shape
    return pl.pallas_call(
        _combine_kernel,
        grid_spec=pltpu.PrefetchScalarGridSpec(
            num_scalar_prefetch=1,
            grid=(T // tm,),
            in_specs=[pl.BlockSpec((tm, D), lambda i, d: (i, 0)), pl.BlockSpec((tm, 2), lambda i, d: (i, 0)),
                      pl.BlockSpec(memory_space=pl.ANY)],
            out_specs=pl.BlockSpec((tm, D), lambda i, d: (i, 0)),
            scratch_shapes=[pltpu.VMEM((2, tm * 2 * SLAB, LANES), F32), pltpu.SemaphoreType.DMA((2,))],
        ),
        out_shape=jax.ShapeDtypeStruct((T, D), F32),
        compiler_params=_params("arbitrary"),
        name="moe_combine",
    )(dest, x2d, gates_col, ys)


def _moe(x, gain, router_group, router_group_b, router_expert, router_expert_b, w_gate, w_up, w_down, layer):
    B, S, D = x.shape
    T = B * S
    x2d = x.reshape(T, D)
    hn, oi, of, cnt = _route(x2d, gain, router_group, router_group_b, router_expert, router_expert_b)
    rb = MOE_ROW_BLOCK
    counts = cnt[:, 0]
    cap = (counts + rb - 1) // rb * rb
    cap_end = jnp.cumsum(cap)
    start = cap_end - cap
    eid = oi[0:2].T
    first = jnp.sum(jnp.where(eid[:, :, None] == jnp.arange(MOE_EXPERTS), start, 0), axis=-1)
    dest = ((first + oi[2:4].T) * SLAB).reshape(2 * T).astype(I32)
    n_rows = 2 * T + MOE_EXPERTS * rb
    n_blk = n_rows // rb
    blk_first_row = jnp.arange(n_blk, dtype=I32) * rb
    blk_expert = jnp.minimum(jnp.sum(cap_end[None, :] <= blk_first_row[:, None], axis=1), MOE_EXPERTS - 1)
    n_used = (cap_end[-1:] // rb).astype(I32)
    xs = _dispatch(dest, hn, n_rows)
    ys = _experts(blk_expert.astype(I32), n_used, xs, w_gate, w_up, w_down, layer)
    out = _combine(dest, x2d, of[0:2].T, ys)
    return out.reshape(B, S, D)


NSA_TQ = 256
SLC_TQ = 256
NSA_PAIRS = NSA_HPG // 2
N_SEL = 32
NSA_Q_SCALE = NSA_DK ** -0.5 * LOG2E
LO_ONE = LANES - 1
HI_ONE = 0
SLC_TK = 512
WIN_SPAN = WINDOW + SLC_TQ
AUG_POS = N_SEL
MASK_BIG = 1e30


def _pair_norm(t, gain2, lane_lo):
    t2 = t * t
    s_lo = jnp.sum(jnp.where(lane_lo, t2, 0.0), axis=-1, keepdims=True)
    s_hi = jnp.sum(jnp.where(lane_lo, 0.0, t2), axis=-1, keepdims=True)
    inv = jnp.where(lane_lo, lax.rsqrt(s_lo * (1.0 / NSA_DK) + 1e-6), lax.rsqrt(s_hi * (1.0 / NSA_DK) + 1e-6))
    return t * inv * gain2


def _inproj1_kernel(x_ref, prel_ref, blocks_ref, g_ref, win_ref, wvt_ref, qg_ref, ksg_ref, kwg_ref,
                    q_out, kc_out, vc_out, ks_out, vs_out, kw_out, vw_out, gate_out):
    h = _rms(x_ref[...]) * g_ref[...]
    hb = h.astype(BF16)
    proj = _dot(hb, win_ref[...])
    lane = lax.broadcasted_iota(I32, (1, LANES), 1)
    lane_lo = lane < NSA_DK
    nq = NSA_HEADS * NSA_DK
    for p in range(nq // LANES):
        sl = slice(p * LANES, (p + 1) * LANES)
        q_out[:, sl] = (_pair_norm(proj[:, sl], qg_ref[...], lane_lo) * NSA_Q_SCALE).astype(BF16)
    part = lambda i: proj[:, nq + i * LANES:nq + (i + 1) * LANES]
    kc_out[...] = part(0)
    vc_out[...] = part(1)

    def lo_hi(t, lo_pad, hi_pad):
        r = pltpu.roll(t, NSA_DK, 1)
        return jnp.concatenate([jnp.where(lane_lo, t, lo_pad), jnp.where(lane_lo, hi_pad, r),
                                jnp.where(lane_lo, r, lo_pad), jnp.where(lane_lo, hi_pad, t)], axis=1).astype(BF16)

    prel = prel_ref[...]
    byte = lambda k: ((prel >> (8 * k)) & 255).astype(F32)
    half_lane = (lane & (NSA_DK - 1)) - AUG_POS
    kaug = blocks_ref[...] + jnp.where((half_lane == 0) | (half_lane == 3), byte(2),
                                       jnp.where((half_lane == 1) | (half_lane == 4), byte(1),
                                                 jnp.where((half_lane == 2) | (half_lane == 5), byte(0), 0.0)))
    ks_out[...] = lo_hi(_pair_norm(part(2), ksg_ref[...], lane_lo), kaug, kaug)
    kw_out[...] = lo_hi(_pair_norm(part(3), kwg_ref[...], lane_lo), kaug, kaug)
    gate_out[...] = jax.nn.sigmoid(part(4))

    vt = _dot_nt(wvt_ref[...], hb)
    tm = vt.shape[1]
    sub = lax.broadcasted_iota(I32, (NSA_DK, 1), 0)
    pad_lo = jnp.broadcast_to(jnp.where(sub == LO_ONE - NSA_DK, 1.0, 0.0), (NSA_DK, tm))
    pad_hi = jnp.broadcast_to(jnp.where(sub == HI_ONE, 1.0, 0.0), (NSA_DK, tm))
    for branch, out in enumerate((vs_out, vw_out)):
        pieces = []
        for grp in range(NSA_G):
            v = vt[(2 * branch + grp) * NSA_DK:(2 * branch + grp + 1) * NSA_DK, :]
            pieces += [v, pad_lo, pad_hi, v]
        out[...] = jnp.concatenate(pieces, axis=0).astype(BF16)


def _inproj1(x2d, positions, gain, w_in, q_norm, k_norm_slc, k_norm_win, tm=256):
    T, D = x2d.shape
    S = positions.shape[1]
    prel = (positions - positions[:, :1]).reshape(T, 1)
    own_block = np.arange(S)[:, None] // SLC_BLOCK == (np.arange(LANES)[None, :] % NSA_DK)
    blocks = jnp.asarray(own_block * MASK_BIG, F32)
    nq_, kvw = NSA_HEADS * NSA_DK, NSA_KVW
    keep = [w_in[:, :nq_ + 3 * kvw], w_in[:, nq_ + 4 * kvw:nq_ + 5 * kvw], w_in[:, nq_ + 6 * kvw:]]
    w_in_p = jnp.concatenate(keep, axis=1)
    n_in = w_in_p.shape[1]
    n_pad = -n_in % LANES
    w_in_p = jnp.pad(w_in_p, ((0, 0), (0, n_pad))).astype(BF16)
    two = lambda g: jnp.tile(g, 2).reshape(1, LANES)
    row = lambda n: pl.BlockSpec((tm, n), lambda i: (i, 0))
    nq = NSA_HEADS * NSA_DK
    vcols = lambda i: w_in[:, nq + i * NSA_KVW:nq + (i + 1) * NSA_KVW]
    w_vt = jnp.concatenate([vcols(3), vcols(5)], axis=1).T.astype(BF16)
    bf = lambda n: jax.ShapeDtypeStruct((T, n), BF16)
    f32 = lambda n: jax.ShapeDtypeStruct((T, n), F32)
    col = pl.BlockSpec((4 * LANES, tm), lambda i: (0, i))
    vt_shape = jax.ShapeDtypeStruct((4 * LANES, T), BF16)
    return pl.pallas_call(
        _inproj1_kernel,
        grid=(T // tm,),
        in_specs=[row(D), row(1), pl.BlockSpec((tm, LANES), lambda i: (i % (S // tm), 0)), _full((1, D)),
                  _full((D, n_in + n_pad)), _full((2 * NSA_KVW, D)), _full((1, LANES)), _full((1, LANES)),
                  _full((1, LANES))],
        out_specs=[row(nq), row(LANES), row(LANES), row(4 * LANES), col, row(4 * LANES), col, row(LANES)],
        out_shape=[bf(nq), f32(LANES), f32(LANES), bf(4 * LANES), vt_shape, bf(4 * LANES), vt_shape, f32(LANES)],
        compiler_params=_params("parallel"),
        name="inproj1",
    )(x2d, prel, blocks, gain.reshape(1, D), w_in_p, w_vt, two(q_norm), two(k_norm_slc), two(k_norm_win))


def _stack_pairs(q_ref, rows):
    return jnp.concatenate([q_ref[0, rows, p * LANES:(p + 1) * LANES] for p in range(NSA_PAIRS)], axis=0)


def _cmp_kernel(q_ref, kch_ref, vch_ref, pos3_ref, pend_ref, gate_ref, slope_ref,
                pek_ref, w1k_ref, w2k_ref, kcg_ref, pev_ref, w1v_ref, w2v_ref, ovl_ref, eye_ref,
                o_ref, sel_ref, kc_s, vc_s):
    tq = NSA_TQ
    S = q_ref.shape[1]

    def compress(ch_ref, pe_ref, w1_ref, w2_ref):
        a = ch_ref[0, 0]
        h_lo = _dot((a + pe_ref[0:1, :]).astype(BF16), w1_ref[0])
        h_hi = _dot((a + pe_ref[1:2, :]).astype(BF16), w1_ref[1])
        n = h_hi.shape[0]
        hid = jax.nn.gelu(h_lo + pltpu.roll(h_hi, n - 1, 0)).astype(BF16)
        return _dot(hid, w2_ref[0]), _dot(hid, w2_ref[1])

    k_lo, k_hi = compress(kch_ref, pek_ref, w1k_ref, w2k_ref)
    for e, kk in enumerate((k_lo, k_hi)):
        kk = kk * lax.rsqrt(jnp.sum(kk * kk, axis=-1, keepdims=True) * (1.0 / NSA_DK) + 1e-6) * kcg_ref[e:e + 1, :]
        kc_s[e] = kk.astype(BF16)
    v_lo, v_hi = compress(vch_ref, pev_ref, w1v_ref, w2v_ref)
    vc_s[0] = v_lo.T.astype(BF16)
    vc_s[1] = v_hi.T.astype(BF16)

    n_cmp = (S - CMP_BLOCK) // CMP_STRIDE + 1
    n_lanes = NSA_PAIRS * tq
    lane = lax.broadcasted_iota(I32, (1, LANES), 1)
    sub_lo = lax.broadcasted_iota(I32, (LANES, 1), 0) < NSA_DK
    blk_row = lax.broadcasted_iota(I32, (N_SEL, 1), 0)
    cmp_row = lax.broadcasted_iota(I32, (LANES, 1), 0)
    in_range = cmp_row < n_cmp

    def q_block(qi, carry):
        t0 = pl.multiple_of(qi * tq, tq)
        rows = pl.ds(t0, tq)
        qs = _stack_pairs(q_ref, rows)
        pos_q0 = pos3_ref[0, pl.ds(qi * (tq // LANES), 1), :][:, 0:1]
        posrel = (pend_ref[0] - pos_q0).astype(F32)
        t_lane = t0 + (lax.broadcasted_iota(I32, (1, n_lanes), 1) & (tq - 1))
        valid = (t_lane >= CMP_STRIDE * cmp_row + (CMP_BLOCK - 1)) & in_range
        gt = gate_ref[0, 0, :, rows]
        psum = jnp.zeros((LANES, tq), F32)
        outs = []
        for e in range(2):
            slopes = jnp.concatenate([jnp.broadcast_to(slope_ref[0, NSA_PAIRS * e + p:NSA_PAIRS * e + p + 1, 0:1], (1, tq))
                                      for p in range(NSA_PAIRS)], axis=1)
            s = _dot_nt(kc_s[e], qs) + posrel * slopes
            s = jnp.where(valid, s, NEG_INF)
            m = jnp.max(s, axis=0, keepdims=True)
            p = jnp.where(valid, jnp.exp2(s - m), 0.0)
            p = p / jnp.maximum(jnp.sum(p, axis=0, keepdims=True), 1e-20)
            psum = psum + functools.reduce(jnp.add, [p[:, i * tq:(i + 1) * tq] for i in range(NSA_PAIRS)])
            gate = jnp.concatenate([gt[2 * i + e:2 * i + e + 1, :] for i in range(NSA_PAIRS)], axis=1)
            outs.append(_dot(vc_s[e], p.astype(BF16)) * gate)
        z = jnp.where(sub_lo, outs[0], outs[1])
        imp = jnp.dot(ovl_ref[...], psum, preferred_element_type=F32, precision=lax.Precision.HIGHEST)
        cur = (t0 + lax.broadcasted_iota(I32, (1, tq), 1)) // SLC_BLOCK
        forced = (blk_row == 0) | (blk_row == cur) | (blk_row == cur - 1)
        imp = jnp.where(forced, FORCE_SCORE, jnp.where(blk_row <= cur, imp, -1.0))
        rank = jnp.zeros((N_SEL, tq), I32)
        for i in range(N_SEL):
            ri = imp[i:i + 1, :]
            rank = rank + jnp.where((ri > imp) | ((ri == imp) & (blk_row > i)), 1, 0)
        sel_t = jnp.where(rank < SLC_TOP_N, 1.0, 0.0).astype(BF16)
        gap = jnp.zeros((NSA_DK - N_SEL, tq), BF16)
        sel_t = jnp.concatenate([sel_t, gap, sel_t, gap], axis=0)
        sel = _dot_nt(eye_ref[...], sel_t)
        sel_ref[0, 0, rows, :] = jnp.where((lane & (NSA_DK - 1)) < N_SEL, sel - 1.0, 0.0).astype(BF16)
        for p in range(NSA_PAIRS):
            o_ref[0, rows, p * LANES:(p + 1) * LANES] = z[:, p * tq:(p + 1) * tq].T.astype(BF16)
        return carry

    lax.fori_loop(0, S // tq, q_block, 0)


def _lo_hi_cols(w):
    z = jnp.zeros_like(w)
    return jnp.stack([jnp.concatenate([w, z], axis=1), jnp.concatenate([z, w], axis=1)])


def _nsa_cmp(q, kch, vch, pos3, pend, gates_g, slopes_t, pe_k, w1_k, w2_k, k_norm_cmp, pe_v, w1_v, w2_v):
    B, S, _ = q.shape
    G = NSA_G
    nch = S // CMP_STRIDE
    half = CMP_STRIDE * NSA_DK
    pe2 = lambda pe: pe.reshape(2, half)
    w1_2 = lambda w: w.reshape(2, half, CMP_HIDDEN).astype(BF16)
    n_sel = S // SLC_BLOCK
    n_cmp = (S - CMP_BLOCK) // CMP_STRIDE + 1
    cmp_start = np.arange(LANES) * CMP_STRIDE
    slc_start = np.arange(n_sel) * SLC_BLOCK
    overlap = ((cmp_start[None, :] < slc_start[:, None] + SLC_BLOCK) & (cmp_start[None, :] + CMP_BLOCK > slc_start[:, None])
               & (np.arange(LANES)[None, :] < n_cmp))
    ovl = jnp.asarray(overlap, F32)
    eye = jnp.eye(NSA_TQ, dtype=BF16)
    qspec = pl.BlockSpec((1, S, NSA_PAIRS * LANES), lambda b, g: (b, 0, g))
    chspec = pl.BlockSpec((1, 1, nch, half), lambda b, g: (b, g, 0, 0))
    return pl.pallas_call(
        _cmp_kernel,
        grid=(B, G),
        in_specs=[qspec, chspec, chspec,
                  pl.BlockSpec((1, S // LANES, LANES), lambda b, g: (b, 0, 0)),
                  pl.BlockSpec((1, LANES, 1), lambda b, g: (b, 0, 0)),
                  pl.BlockSpec((1, 1, 3 * NSA_HPG, S), lambda b, g: (b, g, 0, 0)),
                  pl.BlockSpec((1, NSA_HPG, LANES), lambda b, g: (g, 0, 0)),
                  _full((2, half)), _full((2, half, CMP_HIDDEN)), _full((2, CMP_HIDDEN, LANES)), _full((2, LANES)),
                  _full((2, half)), _full((2, half, CMP_HIDDEN)), _full((2, CMP_HIDDEN, LANES)),
                  _full((n_sel, LANES)), _full((NSA_TQ, NSA_TQ))],
        out_specs=[qspec, pl.BlockSpec((1, 1, S, LANES), lambda b, g: (b, g, 0, 0))],
        out_shape=[jax.ShapeDtypeStruct((B, S, NSA_HEADS * NSA_DK), BF16),
                   jax.ShapeDtypeStruct((B, G, S, LANES), BF16)],
        scratch_shapes=[pltpu.VMEM((2, nch, LANES), BF16), pltpu.VMEM((2, nch, LANES), BF16)],
        compiler_params=_params("parallel", "parallel"),
        name="nsa_cmp",
    )(q, kch, vch, pos3, pend, gates_g, slopes_t, pe2(pe_k), w1_2(w1_k), _lo_hi_cols(w2_k).astype(BF16),
      _lo_hi_cols(k_norm_cmp.reshape(1, -1)).reshape(2, LANES), pe2(pe_v), w1_2(w1_v),
      _lo_hi_cols(w2_v).astype(BF16), ovl, eye)


def _slc_win_kernel(q_ref, ks_ref, vs_ref, kw_ref, vw_ref, sel_ref, gate_ref, slopeq_ref, ocmp_ref, o_ref):
    tq = SLC_TQ
    S = q_ref.shape[1]
    lane = lax.broadcasted_iota(I32, (1, LANES), 1)
    lane_lo = lane < NSA_DK
    sub_lo = lax.broadcasted_iota(I32, (LANES, 1), 0) < NSA_DK
    ones_row = (LO_ONE, HI_ONE)

    def q_block(qi, carry):
        t0 = pl.multiple_of(qi * tq, tq)
        rows = pl.ds(t0, tq)
        qs = _stack_pairs(q_ref, rows)
        selm1 = sel_ref[0, 0, rows, :]

        def q_aug(e, with_sel):
            feats = [slopeq_ref[0, NSA_PAIRS * e + p:NSA_PAIRS * e + p + 1, :] for p in range(NSA_PAIRS)]
            if with_sel:
                extra = jnp.concatenate([selm1 + f for f in feats], axis=0)
            else:
                extra = jnp.concatenate([jnp.broadcast_to(f, (tq, LANES)) for f in feats], axis=0)
            return jnp.where(lane_lo, qs, extra) if e == 0 else jnp.where(lane_lo, extra, qs)

        def k_aug(k_ref, e, r0, n):
            return k_ref[0, pl.ds(r0, n), e * LANES:(e + 1) * LANES]

        gt = gate_ref[0, 0, :, rows]
        tq_lane = t0 + (lax.broadcasted_iota(I32, (1, NSA_PAIRS * tq), 1) & (tq - 1))

        def gated(accs, branch):
            outs = []
            for e, acc in enumerate(accs):
                gate = jnp.concatenate([gt[branch * NSA_HPG + 2 * p + e:branch * NSA_HPG + 2 * p + e + 1, :]
                                        for p in range(NSA_PAIRS)], axis=1)
                outs.append(acc * (gate / acc[ones_row[e]:ones_row[e] + 1, :]))
            return jnp.where(sub_lo, outs[0], outs[1])

        qa = [q_aug(0, True), q_aug(1, True)]
        krow = lax.broadcasted_iota(I32, (SLC_TK, 1), 0)

        def slc_scores(j):
            r0 = pl.multiple_of(j * SLC_TK, SLC_TK)
            return [_dot_nt(k_aug(ks_ref, e, r0, SLC_TK), qa[e]) for e in range(2)]

        def slc_softmax(j, ss, state, last):
            r0 = pl.multiple_of(j * SLC_TK, SLC_TK)
            out = []
            for e in range(2):
                m, acc = state[e]
                s = jnp.where(r0 + krow <= tq_lane, ss[e], NEG_INF) if last else ss[e]
                m_new = jnp.maximum(m, jnp.max(s, axis=0, keepdims=True))
                p = jnp.exp2(s - m_new).astype(BF16)
                pv = _dot(vs_ref[e * LANES:(e + 1) * LANES, pl.ds(r0, SLC_TK)], p)
                out.append((m_new, jnp.exp2(m - m_new) * acc + pv))
            return tuple(out)

        init = (jnp.full((1, NSA_PAIRS * tq), NEG_INF, F32), jnp.zeros((LANES, NSA_PAIRS * tq), F32))
        n_full = qi // (SLC_TK // tq)
        state = lax.fori_loop(0, n_full, lambda j, st: slc_softmax(j, slc_scores(j), st, False), (init, init))

        w0 = pl.multiple_of(jnp.maximum(t0 - WINDOW, 0), tq)
        ss_last = slc_scores(n_full)
        ss_win = [_dot_nt(k_aug(kw_ref, e, w0, WIN_SPAN), q_aug(e, False)) for e in range(2)]
        state = slc_softmax(n_full, ss_last, state, True)
        z = gated([state[0][1], state[1][1]], 1)
        rel = tq_lane - (w0 + lax.broadcasted_iota(I32, (WIN_SPAN, 1), 0))
        wmask = (rel >= 0) & (rel < WINDOW)
        accs = []
        for e in range(2):
            s = jnp.where(wmask, ss_win[e], NEG_INF)
            p = jnp.exp2(s - jnp.max(s, axis=0, keepdims=True)).astype(BF16)
            accs.append(_dot(vw_ref[e * LANES:(e + 1) * LANES, pl.ds(w0, WIN_SPAN)], p))
        z = z + gated(accs, 2)

        for p in range(NSA_PAIRS):
            sl = slice(p * LANES, (p + 1) * LANES)
            o_ref[0, rows, sl] = (ocmp_ref[0, rows, sl].astype(F32) + z[:, p * tq:(p + 1) * tq].T).astype(BF16)
        return carry

    lax.fori_loop(0, S // tq, q_block, 0)


def _slope_aug(slopes_eo):
    hi = slopes_eo.astype(BF16).astype(F32)
    lo = (slopes_eo - hi).astype(BF16).astype(F32)
    w = jnp.asarray([65536.0, 256.0, 1.0], F32)
    feats = jnp.concatenate([hi[..., None] * w, lo[..., None] * w], axis=-1)
    half = jnp.pad(feats, ((0, 0), (0, 0), (AUG_POS, NSA_DK - AUG_POS - 6)))
    return jnp.concatenate([half, half], axis=-1).astype(BF16)


def _nsa_slc_win(q, ks4, vst4, kw4, vwt4, sel, gates_t, slopeq, o_cmp):
    B, S, _ = q.shape
    qspec = pl.BlockSpec((1, S, NSA_PAIRS * LANES), lambda b, g: (b, 0, g))
    kvspec = pl.BlockSpec((1, S, 2 * LANES), lambda b, g: (b, 0, g))
    vtspec = pl.BlockSpec((2 * LANES, S), lambda b, g: (g, b))
    return pl.pallas_call(
        _slc_win_kernel,
        grid=(B, NSA_G),
        in_specs=[qspec, kvspec, vtspec, kvspec, vtspec,
                  pl.BlockSpec((1, 1, S, LANES), lambda b, g: (b, g, 0, 0)),
                  pl.BlockSpec((1, 1, 3 * NSA_HPG, S), lambda b, g: (b, g, 0, 0)),
                  pl.BlockSpec((1, NSA_HPG, LANES), lambda b, g: (g, 0, 0)),
                  qspec],
        out_specs=qspec,
        out_shape=jax.ShapeDtypeStruct((B, S, NSA_HEADS * NSA_DK), BF16),
        compiler_params=_params("parallel", "parallel"),
        name="nsa_slc_win",
    )(q, ks4, vst4, kw4, vwt4, sel, gates_t, slopeq, o_cmp)


def _outproj_kernel(o_ref, x_ref, w_ref, out_ref):
    out_ref[...] = x_ref[...] + _dot(o_ref[...], w_ref[...])


def _outproj(o2d, x2d, w_out, tm=512):
    T, D = x2d.shape
    K = o2d.shape[1]
    return pl.pallas_call(
        _outproj_kernel,
        grid=(T // tm,),
        in_specs=[pl.BlockSpec((tm, K), lambda i: (i, 0)), pl.BlockSpec((tm, D), lambda i: (i, 0)), _full((K, D))],
        out_specs=pl.BlockSpec((tm, D), lambda i: (i, 0)),
        out_shape=jax.ShapeDtypeStruct((T, D), F32),
        compiler_params=_params("parallel"),
        name="outproj1",
    )(o2d, x2d, w_out.astype(BF16))


def _mixer1(x, positions, gain, w_in, q_norm, k_norm_cmp, k_norm_slc, k_norm_win,
            pe_k, w1_k, w2_k, pe_v, w1_v, w2_v, w_out):
    B, S, D = x.shape
    T = B * S
    G, HPG, DK = NSA_G, NSA_HPG, NSA_DK
    assert S // SLC_BLOCK == N_SEL and S % NSA_TQ == 0 and NSA_TQ % LANES == 0
    x2d = x.reshape(T, D)
    q, kc, vc, ks4, vs4, kw4, vw4, gates = _inproj1(x2d, positions, gain, w_in, q_norm, k_norm_slc,
                                                    k_norm_win)
    b3 = lambda t: t.reshape(B, S, -1)
    chunks = lambda t: t.reshape(B, S, G, DK).transpose(0, 2, 1, 3).reshape(B, G, S // CMP_STRIDE, CMP_STRIDE * DK)
    gates_t = gates[:, :3 * NSA_HEADS].reshape(B, S, 3, G, HPG).transpose(0, 3, 2, 4, 1).reshape(B, G, 3 * HPG, S)
    n_cmp = (S - CMP_BLOCK) // CMP_STRIDE + 1
    pend = jnp.pad(positions[:, CMP_BLOCK - 1::CMP_STRIDE][:, :n_cmp], ((0, 0), (0, LANES - n_cmp))).reshape(B, LANES, 1)
    pos3 = positions.reshape(B, S // LANES, LANES)
    slopes = jnp.asarray(2.0 ** (-8.0 * np.arange(1, NSA_HEADS + 1) / NSA_HEADS), F32) * LOG2E
    slopes_eo = slopes.reshape(G, NSA_PAIRS, 2).transpose(0, 2, 1).reshape(G, HPG)
    slopes_t = jnp.broadcast_to(slopes_eo[:, :, None], (G, HPG, LANES))
    o_cmp, sel = _nsa_cmp(b3(q), chunks(kc), chunks(vc), pos3, pend, gates_t, slopes_t,
                          pe_k, w1_k, w2_k, k_norm_cmp, pe_v, w1_v, w2_v)
    o = _nsa_slc_win(b3(q), b3(ks4), vs4, b3(kw4), vw4, sel, gates_t, _slope_aug(slopes_eo), o_cmp)
    return _outproj(o.reshape(T, -1), x2d, w_out).reshape(B, S, D)


def kernel(x, positions, norm_mix, norm_ffn, mix0_w_in, mla_q_a_norm, mla_w_uq, mla_kv_a_norm, mla_w_ukv, mla_q_norm, mla_k_norm, conv_dw_w, conv_dw_b, conv_ln_g, conv_ln_b, mix0_w_out, nsa_w_in, nsa_q_norm, nsa_k_norm_cmp, nsa_k_norm_slc, nsa_k_norm_win, nsa_cmp_pe_k, nsa_cmp_w1_k, nsa_cmp_w2_k, nsa_cmp_pe_v, nsa_cmp_w1_v, nsa_cmp_w2_v, nsa_w_out, moe_router_group, moe_router_group_b, moe_router_expert, moe_router_expert_b, moe_w_gate, moe_w_up, moe_w_down):
    def moe(x, layer):
        return _moe(x, norm_ffn[layer], moe_router_group[layer], moe_router_group_b[layer], moe_router_expert[layer],
                    moe_router_expert_b[layer], moe_w_gate, moe_w_up, moe_w_down, layer)

    x = _mixer0(x, positions, norm_mix[0], mix0_w_in[0], mla_q_a_norm[0], mla_w_uq[0], mla_kv_a_norm[0], mla_w_ukv[0],
                mla_q_norm[0], mla_k_norm[0], conv_dw_w[0], conv_dw_b[0], conv_ln_g[0], conv_ln_b[0], mix0_w_out[0])
    x = moe(x, 0)
    x = _mixer1(x, positions, norm_mix[1], nsa_w_in[0], nsa_q_norm[0], nsa_k_norm_cmp[0], nsa_k_norm_slc[0],
                nsa_k_norm_win[0], nsa_cmp_pe_k[0], nsa_cmp_w1_k[0], nsa_cmp_w2_k[0], nsa_cmp_pe_v[0],
                nsa_cmp_w1_v[0], nsa_cmp_w2_v[0], nsa_w_out[0])
    return moe(x, 1)
```

```python
import functools

import numpy as np
import jax
import jax.numpy as jnp
from jax import lax
from jax.experimental import pallas as pl
from jax.experimental.pallas import tpu as pltpu

F32 = jnp.float32
BF16 = jnp.bfloat16
I32 = jnp.int32

LANES = 128
NEG_INF = -1e30
LOG2E = 1.4426950408889634
Q_BLOCK = 128
MLA_HEADS = 8
MLA_Q_LORA = 256
MLA_KV_LORA = 128
MLA_NOPE = 64
MLA_ROPE = 32
MLA_V = 64
MLA_QK = MLA_NOPE + MLA_ROPE
ROPE_THETA = 10000.0
CONV_CH = 512
CONV_W = 31
NSA_HEADS = 16
NSA_G = 2
NSA_HPG = NSA_HEADS // NSA_G
NSA_DK = 64
NSA_KVW = NSA_G * NSA_DK
CMP_BLOCK = 32
CMP_STRIDE = 16
CMP_HIDDEN = 128
SLC_BLOCK = 64
SLC_TOP_N = 8
WINDOW = 256
FORCE_SCORE = 1e4
MOE_GROUPS = 4
MOE_EPG = 8
MOE_EXPERTS = MOE_GROUPS * MOE_EPG
MOE_HIDDEN = 256
MOE_ROW_BLOCK = 1024

VMEM_LIMIT = 56 * 1024 * 1024


def _params(*sem):
    return pltpu.CompilerParams(dimension_semantics=sem, vmem_limit_bytes=VMEM_LIMIT)


def _full(shape):
    n = len(shape)
    return pl.BlockSpec(shape, lambda *_: (0,) * n)


def _row_mean(x):
    n = x.shape[-1]
    folded = functools.reduce(jnp.add, [x[:, c:c + LANES] for c in range(0, n, LANES)])
    return jnp.sum(folded, axis=-1, keepdims=True) * (1.0 / n)


def _rms(x, eps=1e-6):
    return x * lax.rsqrt(_row_mean(x * x) + eps)


def _dot(a, b):
    return jnp.dot(a, b, preferred_element_type=F32)


def _group_matrix(groups):
    width = LANES // groups
    same = (lax.broadcasted_iota(I32, (LANES, LANES), 0) // width) == (lax.broadcasted_iota(I32, (LANES, LANES), 1) // width)
    return jnp.where(same, 1.0, 0.0).astype(BF16)


def _group_sums(t, group_matrix):
    return _dot(t.astype(BF16), group_matrix)


def _dot_nt(a, b, **kw):
    return lax.dot_general(a, b, (((1,), (1,)), ((), ())), preferred_element_type=F32, **kw)


def _inproj0_kernel(x_ref, pos_ref, gmix_ref, win_ref, qan_ref, wuq_ref, kvan_ref, wuk_ref, wuv_ref,
                    qg_ref, kg_ref, invf_ref, q_out, k_out, v_out, u_out):
    h = _rms(x_ref[...]) * gmix_ref[...]
    proj = _dot(h.astype(BF16), win_ref[...])
    c_q = proj[:, :MLA_Q_LORA]
    c_kv = proj[:, MLA_Q_LORA:MLA_Q_LORA + MLA_KV_LORA]
    k_rope = proj[:, 384:512]
    a = proj[:, 512:512 + CONV_CH]
    g = proj[:, 512 + CONV_CH:]
    u_out[...] = a * jax.nn.sigmoid(g)
    cqn = (_rms(c_q) * qan_ref[...]).astype(BF16)
    ckvn = (_rms(c_kv) * kvan_ref[...]).astype(BF16)
    q = _dot(cqn, wuq_ref[...])
    kn = _dot(ckvn, wuk_ref[...])
    slot_row = lax.broadcasted_iota(I32, (MLA_HEADS * LANES, 1), 0) & (LANES - 1)
    v_out[...] = (_dot_nt(wuv_ref[...], ckvn) + jnp.where(slot_row == MLA_V, 1.0, 0.0)).astype(BF16)
    ang = invf_ref[...] * pos_ref[...].astype(F32)
    cos_t, sin_t = jnp.cos(ang), jnp.sin(ang)
    tm = ang.shape[1]
    tail = jnp.zeros((LANES - MLA_QK, tm), F32)
    cos = jnp.concatenate([jnp.ones((MLA_NOPE, tm), F32), cos_t, cos_t, tail], axis=0).T
    sin = jnp.concatenate([jnp.zeros((MLA_NOPE, tm), F32), -sin_t, sin_t, tail], axis=0).T
    lane = lax.broadcasted_iota(I32, (1, LANES), 1)
    first_half = (lane >= MLA_NOPE) & (lane < MLA_NOPE + MLA_ROPE // 2)
    scale = MLA_QK ** -0.5 * LOG2E

    all_lanes = _group_matrix(1)

    def norm_rope(t, gain):
        t = t * lax.rsqrt(_group_sums(t * t, all_lanes) * (1.0 / MLA_QK) + 1e-6) * gain
        partner = jnp.where(first_half, pltpu.roll(t, LANES - MLA_ROPE // 2, 1), pltpu.roll(t, MLA_ROPE // 2, 1))
        return t * cos + partner * sin

    for hd in range(MLA_HEADS):
        sl = slice(hd * LANES, (hd + 1) * LANES)
        q_out[:, sl] = (norm_rope(q[:, sl], qg_ref[...]) * scale).astype(BF16)
        k_out[:, sl] = norm_rope(kn[:, sl] + k_rope, kg_ref[...]).astype(BF16)


def _head_slots(w, n_heads, width, offset=0):
    k = w.shape[0]
    w = w.reshape(k, n_heads, width)
    w = jnp.pad(w, ((0, 0), (0, 0), (offset, LANES - width - offset)))
    return w.reshape(k, n_heads * LANES)


def _inproj0(x2d, pos_row, gmix, w_in, q_a_norm, w_uq, kv_a_norm, w_ukv, q_norm, k_norm, tm=512):
    T, D = x2d.shape
    H = MLA_HEADS
    w_krope = jnp.pad(w_in[:, 384:416], ((0, 0), (MLA_NOPE, LANES - MLA_NOPE - MLA_ROPE)))
    w_in_p = jnp.concatenate([w_in[:, :384], w_krope, w_in[:, 416:]], axis=1).astype(BF16)
    w_uq_p = _head_slots(w_uq, H, MLA_QK).astype(BF16)
    w_ukv3 = w_ukv.reshape(MLA_KV_LORA, H, MLA_NOPE + MLA_V)
    w_uk_p = _head_slots(w_ukv3[:, :, :MLA_NOPE].reshape(MLA_KV_LORA, H * MLA_NOPE), H, MLA_NOPE).astype(BF16)
    w_uv = _head_slots(w_ukv3[:, :, MLA_NOPE:].reshape(MLA_KV_LORA, H * MLA_V), H, MLA_V).T.astype(BF16)
    pad = LANES - MLA_QK
    qg = jnp.pad(q_norm, (0, pad)).reshape(1, LANES)
    kg = jnp.pad(k_norm, (0, pad)).reshape(1, LANES)
    half = MLA_ROPE // 2
    inv_freq = (ROPE_THETA ** (-jnp.arange(half, dtype=F32) / half)).reshape(half, 1)
    row = lambda n: pl.BlockSpec((tm, n), lambda i: (i, 0))
    n_in = w_in_p.shape[1]
    return pl.pallas_call(
        _inproj0_kernel,
        grid=(T // tm,),
        in_specs=[row(D), pl.BlockSpec((1, tm), lambda i: (0, i)), _full((1, D)), _full((D, n_in)),
                  _full((1, MLA_Q_LORA)), _full((MLA_Q_LORA, H * LANES)), _full((1, MLA_KV_LORA)),
                  _full((MLA_KV_LORA, H * LANES)), _full((H * LANES, MLA_KV_LORA)), _full((1, LANES)),
                  _full((1, LANES)), _full((half, 1))],
        out_specs=[row(H * LANES), row(H * LANES), pl.BlockSpec((H * LANES, tm), lambda i: (0, i)), row(CONV_CH)],
        out_shape=[jax.ShapeDtypeStruct((T, H * LANES), BF16), jax.ShapeDtypeStruct((T, H * LANES), BF16),
                   jax.ShapeDtypeStruct((H * LANES, T), BF16), jax.ShapeDtypeStruct((T, CONV_CH), F32)],
        compiler_params=_params("parallel"),
        name="inproj0",
    )(x2d, pos_row, gmix.reshape(1, D), w_in_p, q_a_norm.reshape(1, -1), w_uq_p, kv_a_norm.reshape(1, -1),
      w_uk_p, w_uv, qg, kg, inv_freq)


MLA_TQ = 512
MLA_TK = 512
MLA_ONE = MLA_V


def _mla_attn_kernel(q_ref, k_ref, vt_ref, o_ref):
    S = q_ref.shape[1]
    tq, tk = MLA_TQ, MLA_TK
    krow = lax.broadcasted_iota(I32, (tk, 1), 0)
    qcol = lax.broadcasted_iota(I32, (1, tq), 1)

    def q_block(qi, carry):
        q0 = pl.multiple_of(qi * tq, tq)
        qs = [q_ref[0, pl.ds(q0, tq), hh * LANES:(hh + 1) * LANES] for hh in range(2)]

        def kv_step(j, state, masked):
            k0 = pl.multiple_of(j * tk, tk)
            hs = [slice(hh * LANES, (hh + 1) * LANES) for hh in range(2)]
            ss = [_dot_nt(k_ref[0, pl.ds(k0, tk), hs[hh]], qs[hh]) for hh in range(2)]
            out = []
            for hh in range(2):
                m, acc = state[hh]
                s = jnp.where(k0 + krow <= q0 + qcol, ss[hh], NEG_INF) if masked else ss[hh]
                m_new = jnp.maximum(m, jnp.max(s, axis=0, keepdims=True))
                p = jnp.exp2(s - m_new).astype(BF16)
                out.append((m_new, jnp.exp2(m - m_new) * acc + _dot(vt_ref[hs[hh], pl.ds(k0, tk)], p)))
            return tuple(out)

        init = (jnp.full((1, tq), NEG_INF, F32), jnp.zeros((LANES, tq), F32))
        n_full = (qi * tq) // tk
        state = lax.fori_loop(0, n_full, functools.partial(kv_step, masked=False), (init, init))
        state = kv_step(n_full, state, True)
        outs = []
        for hh in range(2):
            acc = state[hh][1]
            outs.append((acc * (1.0 / acc[MLA_ONE:MLA_ONE + 1, :]))[:MLA_V, :])
        o_ref[0, pl.ds(q0, tq), :] = jnp.concatenate(outs, axis=0).T.astype(BF16)
        return carry

    lax.fori_loop(0, S // tq, q_block, 0)


def _mla_attn(q, k, vt):
    B, S, _ = q.shape
    spec = pl.BlockSpec((1, S, 2 * LANES), lambda b, h: (b, 0, h))
    return pl.pallas_call(
        _mla_attn_kernel,
        grid=(B, MLA_HEADS // 2),
        in_specs=[spec, spec, pl.BlockSpec((2 * LANES, S), lambda b, h: (h, b))],
        out_specs=pl.BlockSpec((1, S, 2 * MLA_V), lambda b, h: (b, 0, h)),
        out_shape=jax.ShapeDtypeStruct((B, S, MLA_HEADS * MLA_V), BF16),
        compiler_params=_params("parallel", "parallel"),
        name="mla_attn",
    )(q, k, vt)


CONV_TILE = 64
CONV_PAD = 32


CONV_ROWS = 512


def _conv_out0_kernel(u_ref, o_ref, x_ref, dww_ref, dwb_ref, lng_ref, lnb_ref, wo_ref, out_ref, upad, act, shifted):
    step = pl.program_id(1)

    @pl.when(step == 0)
    def _():
        upad[0:CONV_PAD, :] = jnp.zeros((CONV_PAD, CONV_CH), F32)
        upad[CONV_PAD:, :] = u_ref[0]

    lead = CONV_PAD - (CONV_W - 1)
    base = step * CONV_ROWS

    def tile(i, carry):
        t0 = pl.multiple_of(i * CONV_TILE, CONV_TILE)
        win = upad[pl.ds(pl.multiple_of(base + t0, CONV_TILE), CONV_TILE + CONV_PAD), :]
        acc = jnp.zeros((CONV_TILE, CONV_CH), F32) + dwb_ref[...]
        span = CONV_TILE + CONV_PAD - 8
        for r in range(1, 8):
            shifted[r - 1] = win[r:r + span, :]
        for j in range(CONV_W):
            r, k = (lead + j) % 8, (lead + j) // 8 * 8
            tap = win[k:k + CONV_TILE, :] if r == 0 else shifted[r - 1, k:k + CONV_TILE, :]
            acc = acc + tap * dww_ref[j:j + 1, :]
        xc = acc - _row_mean(acc)
        var = _row_mean(xc * xc)
        y = xc * lax.rsqrt(var + 1e-5) * lng_ref[...] + lnb_ref[...]
        act[pl.ds(t0, CONV_TILE), :] = (y * jax.nn.sigmoid(y)).astype(BF16)
        return carry

    lax.fori_loop(0, CONV_ROWS // CONV_TILE, tile, 0)
    n_o = o_ref.shape[2]
    y = _dot(o_ref[0], wo_ref[:n_o, :]) + _dot(act[...], wo_ref[n_o:, :])
    out_ref[0] = x_ref[0] + y


def _conv_out0(u, o_mla, x, dw_w, dw_b, ln_g, ln_b, w_out):
    B, S, D = x.shape
    n_o = o_mla.shape[2]
    w_out_p = w_out.astype(BF16)
    tspec = lambda n: pl.BlockSpec((1, CONV_ROWS, n), lambda b, t: (b, t, 0))
    return pl.pallas_call(
        _conv_out0_kernel,
        grid=(B, S // CONV_ROWS),
        in_specs=[pl.BlockSpec((1, S, CONV_CH), lambda b, t: (b, 0, 0)), tspec(n_o), tspec(D),
                  _full((CONV_W, CONV_CH)), _full((1, CONV_CH)), _full((1, CONV_CH)), _full((1, CONV_CH)),
                  _full((n_o + CONV_CH, D))],
        out_specs=tspec(D),
        out_shape=jax.ShapeDtypeStruct((B, S, D), F32),
        scratch_shapes=[pltpu.VMEM((S + CONV_PAD, CONV_CH), F32), pltpu.VMEM((CONV_ROWS, CONV_CH), BF16),
                        pltpu.VMEM((7, CONV_TILE + CONV_PAD - 8, CONV_CH), F32)],
        compiler_params=_params("parallel", "arbitrary"),
        name="conv_out0",
    )(u, o_mla, x, dw_w, dw_b.reshape(1, -1), ln_g.reshape(1, -1), ln_b.reshape(1, -1), w_out_p)


def _mixer0(x, positions, gmix, w_in, q_a_norm, w_uq, kv_a_norm, w_ukv, q_norm, k_norm,
            dw_w, dw_b, ln_g, ln_b, w_out):
    B, S, D = x.shape
    T = B * S
    q, k, v, u = _inproj0(x.reshape(T, D), positions.reshape(1, T), gmix, w_in, q_a_norm, w_uq, kv_a_norm, w_ukv,
                          q_norm, k_norm)
    o = _mla_attn(q.reshape(B, S, -1), k.reshape(B, S, -1), v)
    return _conv_out0(u.reshape(B, S, -1), o, x, dw_w, dw_b, ln_g, ln_b, w_out)


ROUTE_TILE = 1024
ROUTE_ROWS = 40


def _route_kernel(x_ref, g_ref, wr_ref, br_ref, tri_ref, hn_ref, oi_ref, of_ref, cnt_ref, carry):
    @pl.when(pl.program_id(0) == 0)
    def _():
        carry[...] = jnp.zeros_like(carry)

    hn = _rms(x_ref[...]) * g_ref[...]
    _slab_store(hn_ref, hn)
    tm = hn.shape[0]
    logits = _dot_nt(wr_ref[...], hn, precision=lax.Precision.HIGHEST) + br_ref[...]
    gl = logits[MOE_EXPERTS:MOE_EXPERTS + MOE_GROUPS]
    rid_g = lax.broadcasted_iota(I32, (MOE_GROUPS, tm), 0)
    gmax = jnp.max(gl, axis=0, keepdims=True)
    grp = jnp.min(jnp.where(gl == gmax, rid_g, MOE_GROUPS), axis=0, keepdims=True)
    g_w = 1.0 / jnp.sum(jnp.exp(gl - gmax), axis=0, keepdims=True)
    e_in = jnp.zeros((MOE_EPG, tm), F32)
    for gi in range(MOE_GROUPS):
        e_in = jnp.where(grp == gi, logits[gi * MOE_EPG:(gi + 1) * MOE_EPG], e_in)
    rid_e = lax.broadcasted_iota(I32, (MOE_EPG, tm), 0)
    v1 = jnp.max(e_in, axis=0, keepdims=True)
    i1 = jnp.min(jnp.where(e_in == v1, rid_e, MOE_EPG), axis=0, keepdims=True)
    rest = jnp.where(rid_e == i1, -jnp.inf, e_in)
    v2 = jnp.max(rest, axis=0, keepdims=True)
    i2 = jnp.min(jnp.where(rest == v2, rid_e, MOE_EPG), axis=0, keepdims=True)
    ex = jnp.exp(v2 - v1)
    den = 1.0 + ex
    e1 = grp * MOE_EPG + i1
    e2 = grp * MOE_EPG + i2
    rid = lax.broadcasted_iota(I32, (MOE_EXPERTS, tm), 0)
    hit1 = rid == e1
    hit2 = rid == e2
    member = jnp.where(hit1 | hit2, 1.0, 0.0)
    before = _dot(member.astype(BF16), tri_ref[...]) + carry[...]
    r1 = jnp.sum(jnp.where(hit1, before, 0.0), axis=0, keepdims=True)
    r2 = jnp.sum(jnp.where(hit2, before, 0.0), axis=0, keepdims=True)
    carry[...] = carry[...] + jnp.sum(member, axis=1, keepdims=True)
    oi_ref[...] = jnp.zeros_like(oi_ref)
    oi_ref[0:1, :] = e1
    oi_ref[1:2, :] = e2
    oi_ref[2:3, :] = r1.astype(I32)
    oi_ref[3:4, :] = r2.astype(I32)
    of_ref[...] = jnp.zeros_like(of_ref)
    of_ref[0:1, :] = g_w / den
    of_ref[1:2, :] = g_w * ex / den
    cnt_ref[...] = jnp.broadcast_to(carry[...], cnt_ref.shape).astype(I32)


def _route(x2d, gain, router_group, router_group_b, router_expert, router_expert_b):
    T, D = x2d.shape
    tm = ROUTE_TILE
    pad = ROUTE_ROWS - MOE_EXPERTS - MOE_GROUPS
    wr = jnp.concatenate([router_expert.T, router_group.T, jnp.zeros((pad, D), F32)], axis=0)
    br = jnp.concatenate([router_expert_b, router_group_b, jnp.zeros((pad,), F32)]).reshape(ROUTE_ROWS, 1)
    tri = (jnp.arange(tm)[:, None] < jnp.arange(tm)[None, :]).astype(BF16)
    return pl.pallas_call(
        _route_kernel,
        grid=(T // tm,),
        in_specs=[pl.BlockSpec((tm, D), lambda i: (i, 0)), _full((1, D)), _full((ROUTE_ROWS, D)),
                  _full((ROUTE_ROWS, 1)), _full((tm, tm))],
        out_specs=[pl.BlockSpec((tm * SLAB, LANES), lambda i: (i, 0)), pl.BlockSpec((8, tm), lambda i: (0, i)),
                   pl.BlockSpec((8, tm), lambda i: (0, i)), _full((MOE_EXPERTS, LANES))],
        out_shape=[jax.ShapeDtypeStruct((T * SLAB, LANES), F32), jax.ShapeDtypeStruct((8, T), I32),
                   jax.ShapeDtypeStruct((8, T), F32), jax.ShapeDtypeStruct((MOE_EXPERTS, LANES), I32)],
        scratch_shapes=[pltpu.VMEM((MOE_EXPERTS, 1), F32)],
        compiler_params=_params("arbitrary"),
        name="moe_route",
    )(x2d, gain.reshape(1, D), wr, br, tri)


MOVE_CHUNK = 512
MOVE_UNROLL = 8


SLAB = 8


def _slab_load(ref, n, first=0, stride=SLAB):
    return jnp.concatenate([ref[pl.ds(first + j, n, stride=stride), :] for j in range(SLAB)], axis=1)


def _slab_store(ref, value):
    n = value.shape[0]
    for j in range(SLAB):
        ref[pl.ds(j, n, stride=SLAB), :] = value[:, j * LANES:(j + 1) * LANES]


def _slab(row):
    return pl.ds(pl.multiple_of(row, SLAB), SLAB)


def _for_tokens(n, fn):
    def body(t, carry):
        fn(t, 0)
        fn(t, 1)
        return carry
    lax.fori_loop(0, n, body, 0, unroll=MOVE_UNROLL)


def _dispatch_kernel(dest_ref, hn_ref, xs_init_ref, xs_ref, sem):
    del xs_init_ref
    base = pl.program_id(0) * (2 * MOVE_CHUNK)

    def copy(t, k):
        return pltpu.make_async_copy(hn_ref.at[_slab(t * SLAB)], xs_ref.at[_slab(dest_ref[base + 2 * t + k])], sem)

    _for_tokens(MOVE_CHUNK, lambda t, k: copy(t, k).start())
    _for_tokens(MOVE_CHUNK, lambda t, k: copy(t, k).wait())


def _dispatch(dest, hn, n_rows):
    T = hn.shape[0] // SLAB
    tm = MOVE_CHUNK
    return pl.pallas_call(
        _dispatch_kernel,
        grid_spec=pltpu.PrefetchScalarGridSpec(
            num_scalar_prefetch=1,
            grid=(T // tm,),
            in_specs=[pl.BlockSpec((tm * SLAB, LANES), lambda i, d: (i, 0)), pl.BlockSpec(memory_space=pl.ANY)],
            out_specs=pl.BlockSpec(memory_space=pl.ANY),
            scratch_shapes=[pltpu.SemaphoreType.DMA(())],
        ),
        out_shape=jax.ShapeDtypeStruct((n_rows * SLAB, LANES), F32),
        input_output_aliases={2: 0},
        compiler_params=_params("arbitrary"),
        name="moe_dispatch",
    )(dest, hn, jnp.zeros((n_rows * SLAB, LANES), F32))


def _expert_kernel(blk_e_ref, n_used_ref, xs_ref, wg_ref, wu_ref, wd_ref, ys_ref, wg_s, wu_s, wd_s):
    b = pl.program_id(0)

    @pl.when((b == 0) | (blk_e_ref[b] != blk_e_ref[jnp.maximum(b - 1, 0)]))
    def _():
        wg_s[...] = wg_ref[0].astype(BF16)
        wu_s[...] = wu_ref[0].astype(BF16)
        wd_s[...] = wd_ref[0].astype(BF16)

    @pl.when(b < n_used_ref[0])
    def _():
        x = _slab_load(xs_ref, MOE_ROW_BLOCK).astype(BF16)
        gate = _dot(x, wg_s[...])
        hid = (gate * jax.nn.sigmoid(gate) * _dot(x, wu_s[...])).astype(BF16)
        _slab_store(ys_ref, _dot(hid, wd_s[...]))

    @pl.when(b >= n_used_ref[0])
    def _():
        ys_ref[...] = jnp.zeros_like(ys_ref)


def _experts(blk_expert, n_used, xs, w_gate, w_up, w_down, layer):
    R = xs.shape[0] // SLAB
    rb = MOE_ROW_BLOCK
    D = w_gate.shape[2]
    up_spec = pl.BlockSpec((None, 1, D, MOE_HIDDEN), lambda b, be, nu: (layer, be[b], 0, 0))
    return pl.pallas_call(
        _expert_kernel,
        grid_spec=pltpu.PrefetchScalarGridSpec(
            num_scalar_prefetch=2,
            grid=(R // rb,),
            in_specs=[pl.BlockSpec((rb * SLAB, LANES), lambda b, be, nu: (jnp.minimum(b, nu[0] - 1), 0)),
                      up_spec, up_spec,
                      pl.BlockSpec((None, 1, MOE_HIDDEN, D), lambda b, be, nu: (layer, be[b], 0, 0))],
            out_specs=pl.BlockSpec((rb * SLAB, LANES), lambda b, be, nu: (b, 0)),
            scratch_shapes=[pltpu.VMEM((D, MOE_HIDDEN), BF16), pltpu.VMEM((D, MOE_HIDDEN), BF16),
                            pltpu.VMEM((MOE_HIDDEN, D), BF16)],
        ),
        out_shape=jax.ShapeDtypeStruct((R * SLAB, LANES), F32),
        compiler_params=_params("arbitrary"),
        name="moe_experts",
    )(blk_expert, n_used, xs, w_gate, w_up, w_down)


def _combine_kernel(dest_ref, x_ref, gate_ref, ys_ref, out_ref, buf, sem):
    tm = x_ref.shape[0]
    step = pl.program_id(0)

    def copy(s, t, k):
        a = 2 * t + k
        slot = s % 2
        return pltpu.make_async_copy(ys_ref.at[_slab(dest_ref[s * (2 * tm) + a])], buf.at[slot, _slab(a * SLAB)],
                                     sem.at[slot])

    @pl.when(step == 0)
    def _():
        _for_tokens(tm, lambda t, k: copy(step, t, k).start())

    @pl.when(step + 1 < pl.num_programs(0))
    def _():
        _for_tokens(tm, lambda t, k: copy(step + 1, t, k).start())

    _for_tokens(tm, lambda t, k: copy(step, t, k).wait())
    mine = buf.at[step % 2]
    y0 = _slab_load(mine, tm, 0, 2 * SLAB)
    y1 = _slab_load(mine, tm, SLAB, 2 * SLAB)
    out_ref[...] = x_ref[...] + gate_ref[:, 0:1] * y0 + gate_ref[:, 1:2] * y1


def _combine(dest, x2d, gates_col, ys, tm=MOVE_CHUNK):
    T, D = x2d.shape
    return pl.pallas_call(
        _combine_kernel,
        grid_spec=pltpu.PrefetchScalarGridSpec(
            num_scalar_prefetch=1,
            grid=(T // tm,),
            in_specs=[pl.BlockSpec((tm, D), lambda i, d: (i, 0)), pl.BlockSpec((tm, 2), lambda i, d: (i, 0)),
                      pl.BlockSpec(memory_space=pl.ANY)],
            out_specs=pl.BlockSpec((tm, D), lambda i, d: (i, 0)),
            scratch_shapes=[pltpu.VMEM((2, tm * 2 * SLAB, LANES), F32), pltpu.SemaphoreType.DMA((2,))],
        ),
        out_shape=jax.ShapeDtypeStruct((T, D), F32),
        compiler_params=_params("arbitrary"),
        name="moe_combine",
    )(dest, x2d, gates_col, ys)


def _moe(x, gain, router_group, router_group_b, router_expert, router_expert_b, w_gate, w_up, w_down, layer):
    B, S, D = x.shape
    T = B * S
    x2d = x.reshape(T, D)
    hn, oi, of, cnt = _route(x2d, gain, router_group, router_group_b, router_expert, router_expert_b)
    rb = MOE_ROW_BLOCK
    counts = cnt[:, 0]
    cap = (counts + rb - 1) // rb * rb
    cap_end = jnp.cumsum(cap)
    start = cap_end - cap
    eid = oi[0:2].T
    first = jnp.sum(jnp.where(eid[:, :, None] == jnp.arange(MOE_EXPERTS), start, 0), axis=-1)
    dest = ((first + oi[2:4].T) * SLAB).reshape(2 * T).astype(I32)
    n_rows = 2 * T + MOE_EXPERTS * rb
    n_blk = n_rows // rb
    blk_first_row = jnp.arange(n_blk, dtype=I32) * rb
    blk_expert = jnp.minimum(jnp.sum(cap_end[None, :] <= blk_first_row[:, None], axis=1), MOE_EXPERTS - 1)
    n_used = (cap_end[-1:] // rb).astype(I32)
    xs = _dispatch(dest, hn, n_rows)
    ys = _experts(blk_expert.astype(I32), n_used, xs, w_gate, w_up, w_down, layer)
    out = _combine(dest, x2d, of[0:2].T, ys)
    return out.reshape(B, S, D)


NSA_TQ = 256
SLC_TQ = 256
NSA_PAIRS = NSA_HPG // 2
N_SEL = 32
NSA_Q_SCALE = NSA_DK ** -0.5 * LOG2E
LO_ONE = LANES - 1
HI_ONE = 0
SLC_TK = 512
WIN_SPAN = WINDOW + SLC_TQ
AUG_POS = N_SEL
MASK_BIG = 1e30


def _pair_norm(t, gain2, lane_lo):
    t2 = t * t
    s_lo = jnp.sum(jnp.where(lane_lo, t2, 0.0), axis=-1, keepdims=True)
    s_hi = jnp.sum(jnp.where(lane_lo, 0.0, t2), axis=-1, keepdims=True)
    inv = jnp.where(lane_lo, lax.rsqrt(s_lo * (1.0 / NSA_DK) + 1e-6), lax.rsqrt(s_hi * (1.0 / NSA_DK) + 1e-6))
    return t * inv * gain2


def _inproj1_kernel(x_ref, prel_ref, blocks_ref, g_ref, win_ref, wvt_ref, qg_ref, ksg_ref, kwg_ref,
                    q_out, kc_out, vc_out, ks_out, vs_out, kw_out, vw_out, gate_out):
    h = _rms(x_ref[...]) * g_ref[...]
    hb = h.astype(BF16)
    proj = _dot(hb, win_ref[...])
    lane = lax.broadcasted_iota(I32, (1, LANES), 1)
    lane_lo = lane < NSA_DK
    nq = NSA_HEADS * NSA_DK
    for p in range(nq // LANES):
        sl = slice(p * LANES, (p + 1) * LANES)
        q_out[:, sl] = (_pair_norm(proj[:, sl], qg_ref[...], lane_lo) * NSA_Q_SCALE).astype(BF16)
    part = lambda i: proj[:, nq + i * LANES:nq + (i + 1) * LANES]
    kc_out[...] = part(0)
    vc_out[...] = part(1)

    def lo_hi(t, lo_pad, hi_pad):
        r = pltpu.roll(t, NSA_DK, 1)
        return jnp.concatenate([jnp.where(lane_lo, t, lo_pad), jnp.where(lane_lo, hi_pad, r),
                                jnp.where(lane_lo, r, lo_pad), jnp.where(lane_lo, hi_pad, t)], axis=1).astype(BF16)

    prel = prel_ref[...]
    byte = lambda k: ((prel >> (8 * k)) & 255).astype(F32)
    half_lane = (lane & (NSA_DK - 1)) - AUG_POS
    kaug = blocks_ref[...] + jnp.where((half_lane == 0) | (half_lane == 3), byte(2),
                                       jnp.where((half_lane == 1) | (half_lane == 4), byte(1),
                                                 jnp.where((half_lane == 2) | (half_lane == 5), byte(0), 0.0)))
    ks_out[...] = lo_hi(_pair_norm(part(2), ksg_ref[...], lane_lo), kaug, kaug)
    kw_out[...] = lo_hi(_pair_norm(part(3), kwg_ref[...], lane_lo), kaug, kaug)
    gate_out[...] = jax.nn.sigmoid(part(4))

    vt = _dot_nt(wvt_ref[...], hb)
    tm = vt.shape[1]
    sub = lax.broadcasted_iota(I32, (NSA_DK, 1), 0)
    pad_lo = jnp.broadcast_to(jnp.where(sub == LO_ONE - NSA_DK, 1.0, 0.0), (NSA_DK, tm))
    pad_hi = jnp.broadcast_to(jnp.where(sub == HI_ONE, 1.0, 0.0), (NSA_DK, tm))
    for branch, out in enumerate((vs_out, vw_out)):
        pieces = []
        for grp in range(NSA_G):
            v = vt[(2 * branch + grp) * NSA_DK:(2 * branch + grp + 1) * NSA_DK, :]
            pieces += [v, pad_lo, pad_hi, v]
        out[...] = jnp.concatenate(pieces, axis=0).astype(BF16)


def _inproj1(x2d, positions, gain, w_in, q_norm, k_norm_slc, k_norm_win, tm=256):
    T, D = x2d.shape
    S = positions.shape[1]
    prel = (positions - positions[:, :1]).reshape(T, 1)
    own_block = np.arange(S)[:, None] // SLC_BLOCK == (np.arange(LANES)[None, :] % NSA_DK)
    blocks = jnp.asarray(own_block * MASK_BIG, F32)
    nq_, kvw = NSA_HEADS * NSA_DK, NSA_KVW
    keep = [w_in[:, :nq_ + 3 * kvw], w_in[:, nq_ + 4 * kvw:nq_ + 5 * kvw], w_in[:, nq_ + 6 * kvw:]]
    w_in_p = jnp.concatenate(keep, axis=1)
    n_in = w_in_p.shape[1]
    n_pad = -n_in % LANES
    w_in_p = jnp.pad(w_in_p, ((0, 0), (0, n_pad))).astype(BF16)
    two = lambda g: jnp.tile(g, 2).reshape(1, LANES)
    row = lambda n: pl.BlockSpec((tm, n), lambda i: (i, 0))
    nq = NSA_HEADS * NSA_DK
    vcols = lambda i: w_in[:, nq + i * NSA_KVW:nq + (i + 1) * NSA_KVW]
    w_vt = jnp.concatenate([vcols(3), vcols(5)], axis=1).T.astype(BF16)
    bf = lambda n: jax.ShapeDtypeStruct((T, n), BF16)
    f32 = lambda n: jax.ShapeDtypeStruct((T, n), F32)
    col = pl.BlockSpec((4 * LANES, tm), lambda i: (0, i))
    vt_shape = jax.ShapeDtypeStruct((4 * LANES, T), BF16)
    return pl.pallas_call(
        _inproj1_kernel,
        grid=(T // tm,),
        in_specs=[row(D), row(1), pl.BlockSpec((tm, LANES), lambda i: (i % (S // tm), 0)), _full((1, D)),
                  _full((D, n_in + n_pad)), _full((2 * NSA_KVW, D)), _full((1, LANES)), _full((1, LANES)),
                  _full((1, LANES))],
        out_specs=[row(nq), row(LANES), row(LANES), row(4 * LANES), col, row(4 * LANES), col, row(LANES)],
        out_shape=[bf(nq), f32(LANES), f32(LANES), bf(4 * LANES), vt_shape, bf(4 * LANES), vt_shape, f32(LANES)],
        compiler_params=_params("parallel"),
        name="inproj1",
    )(x2d, prel, blocks, gain.reshape(1, D), w_in_p, w_vt, two(q_norm), two(k_norm_slc), two(k_norm_win))


def _stack_pairs(q_ref, rows):
    return jnp.concatenate([q_ref[0, rows, p * LANES:(p + 1) * LANES] for p in range(NSA_PAIRS)], axis=0)


def _cmp_kernel(q_ref, kch_ref, vch_ref, pos3_ref, pend_ref, gate_ref, slope_ref,
                pek_ref, w1k_ref, w2k_ref, kcg_ref, pev_ref, w1v_ref, w2v_ref, ovl_ref, eye_ref,
                o_ref, sel_ref, kc_s, vc_s):
    tq = NSA_TQ
    S = q_ref.shape[1]

    def compress(ch_ref, pe_ref, w1_ref, w2_ref):
        a = ch_ref[0, 0]
        h_lo = _dot((a + pe_ref[0:1, :]).astype(BF16), w1_ref[0])
        h_hi = _dot((a + pe_ref[1:2, :]).astype(BF16), w1_ref[1])
        n = h_hi.shape[0]
        hid = jax.nn.gelu(h_lo + pltpu.roll(h_hi, n - 1, 0)).astype(BF16)
        return _dot(hid, w2_ref[0]), _dot(hid, w2_ref[1])

    k_lo, k_hi = compress(kch_ref, pek_ref, w1k_ref, w2k_ref)
    for e, kk in enumerate((k_lo, k_hi)):
        kk = kk * lax.rsqrt(jnp.sum(kk * kk, axis=-1, keepdims=True) * (1.0 / NSA_DK) + 1e-6) * kcg_ref[e:e + 1, :]
        kc_s[e] = kk.astype(BF16)
    v_lo, v_hi = compress(vch_ref, pev_ref, w1v_ref, w2v_ref)
    vc_s[0] = v_lo.T.astype(BF16)
    vc_s[1] = v_hi.T.astype(BF16)

    n_cmp = (S - CMP_BLOCK) // CMP_STRIDE + 1
    n_lanes = NSA_PAIRS * tq
    lane = lax.broadcasted_iota(I32, (1, LANES), 1)
    sub_lo = lax.broadcasted_iota(I32, (LANES, 1), 0) < NSA_DK
    blk_row = lax.broadcasted_iota(I32, (N_SEL, 1), 0)
    cmp_row = lax.broadcasted_iota(I32, (LANES, 1), 0)
    in_range = cmp_row < n_cmp

    def q_block(qi, carry):
        t0 = pl.multiple_of(qi * tq, tq)
        rows = pl.ds(t0, tq)
        qs = _stack_pairs(q_ref, rows)
        pos_q0 = pos3_ref[0, pl.ds(qi * (tq // LANES), 1), :][:, 0:1]
        posrel = (pend_ref[0] - pos_q0).astype(F32)
        t_lane = t0 + (lax.broadcasted_iota(I32, (1, n_lanes), 1) & (tq - 1))
        valid = (t_lane >= CMP_STRIDE * cmp_row + (CMP_BLOCK - 1)) & in_range
        gt = gate_ref[0, 0, :, rows]
        psum = jnp.zeros((LANES, tq), F32)
        outs = []
        for e in range(2):
            slopes = jnp.concatenate([jnp.broadcast_to(slope_ref[0, NSA_PAIRS * e + p:NSA_PAIRS * e + p + 1, 0:1], (1, tq))
                                      for p in range(NSA_PAIRS)], axis=1)
            s = _dot_nt(kc_s[e], qs) + posrel * slopes
            s = jnp.where(valid, s, NEG_INF)
            m = jnp.max(s, axis=0, keepdims=True)
            p = jnp.where(valid, jnp.exp2(s - m), 0.0)
            p = p / jnp.maximum(jnp.sum(p, axis=0, keepdims=True), 1e-20)
            psum = psum + functools.reduce(jnp.add, [p[:, i * tq:(i + 1) * tq] for i in range(NSA_PAIRS)])
            gate = jnp.concatenate([gt[2 * i + e:2 * i + e + 1, :] for i in range(NSA_PAIRS)], axis=1)
            outs.append(_dot(vc_s[e], p.astype(BF16)) * gate)
        z = jnp.where(sub_lo, outs[0], outs[1])
        imp = jnp.dot(ovl_ref[...], psum, preferred_element_type=F32, precision=lax.Precision.HIGHEST)
        cur = (t0 + lax.broadcasted_iota(I32, (1, tq), 1)) // SLC_BLOCK
        forced = (blk_row == 0) | (blk_row == cur) | (blk_row == cur - 1)
        imp = jnp.where(forced, FORCE_SCORE, jnp.where(blk_row <= cur, imp, -1.0))
        rank = jnp.zeros((N_SEL, tq), I32)
        for i in range(N_SEL):
            ri = imp[i:i + 1, :]
            rank = rank + jnp.where((ri > imp) | ((ri == imp) & (blk_row > i)), 1, 0)
        sel_t = jnp.where(rank < SLC_TOP_N, 1.0, 0.0).astype(BF16)
        gap = jnp.zeros((NSA_DK - N_SEL, tq), BF16)
        sel_t = jnp.concatenate([sel_t, gap, sel_t, gap], axis=0)
        sel = _dot_nt(eye_ref[...], sel_t)
        sel_ref[0, 0, rows, :] = jnp.where((lane & (NSA_DK - 1)) < N_SEL, sel - 1.0, 0.0).astype(BF16)
        for p in range(NSA_PAIRS):
            o_ref[0, rows, p * LANES:(p + 1) * LANES] = z[:, p * tq:(p + 1) * tq].T.astype(BF16)
        return carry

    lax.fori_loop(0, S // tq, q_block, 0)


def _lo_hi_cols(w):
    z = jnp.zeros_like(w)
    return jnp.stack([jnp.concatenate([w, z], axis=1), jnp.concatenate([z, w], axis=1)])


def _nsa_cmp(q, kch, vch, pos3, pend, gates_g, slopes_t, pe_k, w1_k, w2_k, k_norm_cmp, pe_v, w1_v, w2_v):
    B, S, _ = q.shape
    G = NSA_G
    nch = S // CMP_STRIDE
    half = CMP_STRIDE * NSA_DK
    pe2 = lambda pe: pe.reshape(2, half)
    w1_2 = lambda w: w.reshape(2, half, CMP_HIDDEN).astype(BF16)
    n_sel = S // SLC_BLOCK
    n_cmp = (S - CMP_BLOCK) // CMP_STRIDE + 1
    cmp_start = np.arange(LANES) * CMP_STRIDE
    slc_start = np.arange(n_sel) * SLC_BLOCK
    overlap = ((cmp_start[None, :] < slc_start[:, None] + SLC_BLOCK) & (cmp_start[None, :] + CMP_BLOCK > slc_start[:, None])
               & (np.arange(LANES)[None, :] < n_cmp))
    ovl = jnp.asarray(overlap, F32)
    eye = jnp.eye(NSA_TQ, dtype=BF16)
    qspec = pl.BlockSpec((1, S, NSA_PAIRS * LANES), lambda b, g: (b, 0, g))
    chspec = pl.BlockSpec((1, 1, nch, half), lambda b, g: (b, g, 0, 0))
    return pl.pallas_call(
        _cmp_kernel,
        grid=(B, G),
        in_specs=[qspec, chspec, chspec,
                  pl.BlockSpec((1, S // LANES, LANES), lambda b, g: (b, 0, 0)),
                  pl.BlockSpec((1, LANES, 1), lambda b, g: (b, 0, 0)),
                  pl.BlockSpec((1, 1, 3 * NSA_HPG, S), lambda b, g: (b, g, 0, 0)),
                  pl.BlockSpec((1, NSA_HPG, LANES), lambda b, g: (g, 0, 0)),
                  _full((2, half)), _full((2, half, CMP_HIDDEN)), _full((2, CMP_HIDDEN, LANES)), _full((2, LANES)),
                  _full((2, half)), _full((2, half, CMP_HIDDEN)), _full((2, CMP_HIDDEN, LANES)),
                  _full((n_sel, LANES)), _full((NSA_TQ, NSA_TQ))],
        out_specs=[qspec, pl.BlockSpec((1, 1, S, LANES), lambda b, g: (b, g, 0, 0))],
        out_shape=[jax.ShapeDtypeStruct((B, S, NSA_HEADS * NSA_DK), BF16),
                   jax.ShapeDtypeStruct((B, G, S, LANES), BF16)],
        scratch_shapes=[pltpu.VMEM((2, nch, LANES), BF16), pltpu.VMEM((2, nch, LANES), BF16)],
        compiler_params=_params("parallel", "parallel"),
        name="nsa_cmp",
    )(q, kch, vch, pos3, pend, gates_g, slopes_t, pe2(pe_k), w1_2(w1_k), _lo_hi_cols(w2_k).astype(BF16),
      _lo_hi_cols(k_norm_cmp.reshape(1, -1)).reshape(2, LANES), pe2(pe_v), w1_2(w1_v),
      _lo_hi_cols(w2_v).astype(BF16), ovl, eye)


def _slc_win_kernel(q_ref, ks_ref, vs_ref, kw_ref, vw_ref, sel_ref, gate_ref, slopeq_ref, ocmp_ref, o_ref):
    tq = SLC_TQ
    S = q_ref.shape[1]
    lane = lax.broadcasted_iota(I32, (1, LANES), 1)
    lane_lo = lane < NSA_DK
    sub_lo = lax.broadcasted_iota(I32, (LANES, 1), 0) < NSA_DK
    ones_row = (LO_ONE, HI_ONE)

    def q_block(qi, carry):
        t0 = pl.multiple_of(qi * tq, tq)
        rows = pl.ds(t0, tq)
        qs = _stack_pairs(q_ref, rows)
        selm1 = sel_ref[0, 0, rows, :]

        def q_aug(e, with_sel):
            feats = [slopeq_ref[0, NSA_PAIRS * e + p:NSA_PAIRS * e + p + 1, :] for p in range(NSA_PAIRS)]
            if with_sel:
                extra = jnp.concatenate([selm1 + f for f in feats], axis=0)
            else:
                extra = jnp.concatenate([jnp.broadcast_to(f, (tq, LANES)) for f in feats], axis=0)
            return jnp.where(lane_lo, qs, extra) if e == 0 else jnp.where(lane_lo, extra, qs)

        def k_aug(k_ref, e, r0, n):
            return k_ref[0, pl.ds(r0, n), e * LANES:(e + 1) * LANES]

        gt = gate_ref[0, 0, :, rows]
        tq_lane = t0 + (lax.broadcasted_iota(I32, (1, NSA_PAIRS * tq), 1) & (tq - 1))

        def gated(accs, branch):
            outs = []
            for e, acc in enumerate(accs):
                gate = jnp.concatenate([gt[branch * NSA_HPG + 2 * p + e:branch * NSA_HPG + 2 * p + e + 1, :]
                                        for p in range(NSA_PAIRS)], axis=1)
                outs.append(acc * (gate / acc[ones_row[e]:ones_row[e] + 1, :]))
            return jnp.where(sub_lo, outs[0], outs[1])

        qa = [q_aug(0, True), q_aug(1, True)]
        krow = lax.broadcasted_iota(I32, (SLC_TK, 1), 0)

        def slc_scores(j):
            r0 = pl.multiple_of(j * SLC_TK, SLC_TK)
            return [_dot_nt(k_aug(ks_ref, e, r0, SLC_TK), qa[e]) for e in range(2)]

        def slc_softmax(j, ss, state, last):
            r0 = pl.multiple_of(j * SLC_TK, SLC_TK)
            out = []
            for e in range(2):
                m, acc = state[e]
                s = jnp.where(r0 + krow <= tq_lane, ss[e], NEG_INF) if last else ss[e]
                m_new = jnp.maximum(m, jnp.max(s, axis=0, keepdims=True))
                p = jnp.exp2(s - m_new).astype(BF16)
                pv = _dot(vs_ref[e * LANES:(e + 1) * LANES, pl.ds(r0, SLC_TK)], p)
                out.append((m_new, jnp.exp2(m - m_new) * acc + pv))
            return tuple(out)

        init = (jnp.full((1, NSA_PAIRS * tq), NEG_INF, F32), jnp.zeros((LANES, NSA_PAIRS * tq), F32))
        n_full = qi // (SLC_TK // tq)
        state = lax.fori_loop(0, n_full, lambda j, st: slc_softmax(j, slc_scores(j), st, False), (init, init))

        w0 = pl.multiple_of(jnp.maximum(t0 - WINDOW, 0), tq)
        ss_last = slc_scores(n_full)
        ss_win = [_dot_nt(k_aug(kw_ref, e, w0, WIN_SPAN), q_aug(e, False)) for e in range(2)]
        state = slc_softmax(n_full, ss_last, state, True)
        z = gated([state[0][1], state[1][1]], 1)
        rel = tq_lane - (w0 + lax.broadcasted_iota(I32, (WIN_SPAN, 1), 0))
        wmask = (rel >= 0) & (rel < WINDOW)
        accs = []
        for e in range(2):
            s = jnp.where(wmask, ss_win[e], NEG_INF)
            p = jnp.exp2(s - jnp.max(s, axis=0, keepdims=True)).astype(BF16)
            accs.append(_dot(vw_ref[e * LANES:(e + 1) * LANES, pl.ds(w0, WIN_SPAN)], p))
        z = z + gated(accs, 2)

        for p in range(NSA_PAIRS):
            sl = slice(p * LANES, (p + 1) * LANES)
            o_ref[0, rows, sl] = (ocmp_ref[0, rows, sl].astype(F32) + z[:, p * tq:(p + 1) * tq].T).astype(BF16)
        return carry

    lax.fori_loop(0, S // tq, q_block, 0)


def _slope_aug(slopes_eo):
    hi = slopes_eo.astype(BF16).astype(F32)
    lo = (slopes_eo - hi).astype(BF16).astype(F32)
    w = jnp.asarray([65536.0, 256.0, 1.0], F32)
    feats = jnp.concatenate([hi[..., None] * w, lo[..., None] * w], axis=-1)
    half = jnp.pad(feats, ((0, 0), (0, 0), (AUG_POS, NSA_DK - AUG_POS - 6)))
    return jnp.concatenate([half, half], axis=-1).astype(BF16)


def _nsa_slc_win(q, ks4, vst4, kw4, vwt4, sel, gates_t, slopeq, o_cmp):
    B, S, _ = q.shape
    qspec = pl.BlockSpec((1, S, NSA_PAIRS * LANES), lambda b, g: (b, 0, g))
    kvspec = pl.BlockSpec((1, S, 2 * LANES), lambda b, g: (b, 0, g))
    vtspec = pl.BlockSpec((2 * LANES, S), lambda b, g: (g, b))
    return pl.pallas_call(
        _slc_win_kernel,
        grid=(B, NSA_G),
        in_specs=[qspec, kvspec, vtspec, kvspec, vtspec,
                  pl.BlockSpec((1, 1, S, LANES), lambda b, g: (b, g, 0, 0)),
                  pl.BlockSpec((1, 1, 3 * NSA_HPG, S), lambda b, g: (b, g, 0, 0)),
                  pl.BlockSpec((1, NSA_HPG, LANES), lambda b, g: (g, 0, 0)),
                  qspec],
        out_specs=qspec,
        out_shape=jax.ShapeDtypeStruct((B, S, NSA_HEADS * NSA_DK), BF16),
        compiler_params=_params("parallel", "parallel"),
        name="nsa_slc_win",
    )(q, ks4, vst4, kw4, vwt4, sel, gates_t, slopeq, o_cmp)


def _outproj_kernel(o_ref, x_ref, w_ref, out_ref):
    out_ref[...] = x_ref[...] + _dot(o_ref[...], w_ref[...])


def _outproj(o2d, x2d, w_out, tm=512):
    T, D = x2d.shape
    K = o2d.shape[1]
    return pl.pallas_call(
        _outproj_kernel,
        grid=(T // tm,),
        in_specs=[pl.BlockSpec((tm, K), lambda i: (i, 0)), pl.BlockSpec((tm, D), lambda i: (i, 0)), _full((K, D))],
        out_specs=pl.BlockSpec((tm, D), lambda i: (i, 0)),
        out_shape=jax.ShapeDtypeStruct((T, D), F32),
        compiler_params=_params("parallel"),
        name="outproj1",
    )(o2d, x2d, w_out.astype(BF16))


def _mixer1(x, positions, gain, w_in, q_norm, k_norm_cmp, k_norm_slc, k_norm_win,
            pe_k, w1_k, w2_k, pe_v, w1_v, w2_v, w_out):
    B, S, D = x.shape
    T = B * S
    G, HPG, DK = NSA_G, NSA_HPG, NSA_DK
    assert S // SLC_BLOCK == N_SEL and S % NSA_TQ == 0 and NSA_TQ % LANES == 0
    x2d = x.reshape(T, D)
    q, kc, vc, ks4, vs4, kw4, vw4, gates = _inproj1(x2d, positions, gain, w_in, q_norm, k_norm_slc,
                                                    k_norm_win)
    b3 = lambda t: t.reshape(B, S, -1)
    chunks = lambda t: t.reshape(B, S, G, DK).transpose(0, 2, 1, 3).reshape(B, G, S // CMP_STRIDE, CMP_STRIDE * DK)
    gates_t = gates[:, :3 * NSA_HEADS].reshape(B, S, 3, G, HPG).transpose(0, 3, 2, 4, 1).reshape(B, G, 3 * HPG, S)
    n_cmp = (S - CMP_BLOCK) // CMP_STRIDE + 1
    pend = jnp.pad(positions[:, CMP_BLOCK - 1::CMP_STRIDE][:, :n_cmp], ((0, 0), (0, LANES - n_cmp))).reshape(B, LANES, 1)
    pos3 = positions.reshape(B, S // LANES, LANES)
    slopes = jnp.asarray(2.0 ** (-8.0 * np.arange(1, NSA_HEADS + 1) / NSA_HEADS), F32) * LOG2E
    slopes_eo = slopes.reshape(G, NSA_PAIRS, 2).transpose(0, 2, 1).reshape(G, HPG)
    slopes_t = jnp.broadcast_to(slopes_eo[:, :, None], (G, HPG, LANES))
    o_cmp, sel = _nsa_cmp(b3(q), chunks(kc), chunks(vc), pos3, pend, gates_t, slopes_t,
                          pe_k, w1_k, w2_k, k_norm_cmp, pe_v, w1_v, w2_v)
    o = _nsa_slc_win(b3(q), b3(ks4), vs4, b3(kw4), vw4, sel, gates_t, _slope_aug(slopes_eo), o_cmp)
    return _outproj(o.reshape(T, -1), x2d, w_out).reshape(B, S, D)


def kernel(x, positions, norm_mix, norm_ffn, mix0_w_in, mla_q_a_norm, mla_w_uq, mla_kv_a_norm, mla_w_ukv, mla_q_norm, mla_k_norm, conv_dw_w, conv_dw_b, conv_ln_g, conv_ln_b, mix0_w_out, nsa_w_in, nsa_q_norm, nsa_k_norm_cmp, nsa_k_norm_slc, nsa_k_norm_win, nsa_cmp_pe_k, nsa_cmp_w1_k, nsa_cmp_w2_k, nsa_cmp_pe_v, nsa_cmp_w1_v, nsa_cmp_w2_v, nsa_w_out, moe_router_group, moe_router_group_b, moe_router_expert, moe_router_expert_b, moe_w_gate, moe_w_up, moe_w_down):
    def moe(x, layer):
        return _moe(x, norm_ffn[layer], moe_router_group[layer], moe_router_group_b[layer], moe_router_expert[layer],
                    moe_router_expert_b[layer], moe_w_gate, moe_w_up, moe_w_down, layer)

    x = _mixer0(x, positions, norm_mix[0], mix0_w_in[0], mla_q_a_norm[0], mla_w_uq[0], mla_kv_a_norm[0], mla_w_ukv[0],
                mla_q_norm[0], mla_k_norm[0], conv_dw_w[0], conv_dw_b[0], conv_ln_g[0], conv_ln_b[0], mix0_w_out[0])
    x = moe(x, 0)
    x = _mixer1(x, positions, norm_mix[1], nsa_w_in[0], nsa_q_norm[0], nsa_k_norm_cmp[0], nsa_k_norm_slc[0],
                nsa_k_norm_win[0], nsa_cmp_pe_k[0], nsa_cmp_w1_k[0], nsa_cmp_w2_k[0], nsa_cmp_pe_v[0],
                nsa_cmp_w1_v[0], nsa_cmp_w2_v[0], nsa_w_out[0])
    return moe(x, 1)
```

```python
import functools

import numpy as np
import jax
import jax.numpy as jnp
from jax import lax
from jax.experimental import pallas as pl
from jax.experimental.pallas import tpu as pltpu

F32 = jnp.float32
BF16 = jnp.bfloat16
I32 = jnp.int32

LANES = 128
NEG_INF = -1e30
LOG2E = 1.4426950408889634
Q_BLOCK = 128
MLA_HEADS = 8
MLA_Q_LORA = 256
MLA_KV_LORA = 128
MLA_NOPE = 64
MLA_ROPE = 32
MLA_V = 64
MLA_QK = MLA_NOPE + MLA_ROPE
ROPE_THETA = 10000.0
CONV_CH = 512
CONV_W = 31
NSA_HEADS = 16
NSA_G = 2
NSA_HPG = NSA_HEADS // NSA_G
NSA_DK = 64
NSA_KVW = NSA_G * NSA_DK
CMP_BLOCK = 32
CMP_STRIDE = 16
CMP_HIDDEN = 128
SLC_BLOCK = 64
SLC_TOP_N = 8
WINDOW = 256
FORCE_SCORE = 1e4
MOE_GROUPS = 4
MOE_EPG = 8
MOE_EXPERTS = MOE_GROUPS * MOE_EPG
MOE_HIDDEN = 256
MOE_ROW_BLOCK = 1024

VMEM_LIMIT = 56 * 1024 * 1024


def _params(*sem):
    return pltpu.CompilerParams(dimension_semantics=sem, vmem_limit_bytes=VMEM_LIMIT)


def _full(shape):
    n = len(shape)
    return pl.BlockSpec(shape, lambda *_: (0,) * n)


def _row_mean(x):
    n = x.shape[-1]
    folded = functools.reduce(jnp.add, [x[:, c:c + LANES] for c in range(0, n, LANES)])
    return jnp.sum(folded, axis=-1, keepdims=True) * (1.0 / n)


def _rms(x, eps=1e-6):
    return x * lax.rsqrt(_row_mean(x * x) + eps)


def _dot(a, b):
    return jnp.dot(a, b, preferred_element_type=F32)


def _group_matrix(groups):
    width = LANES // groups
    same = (lax.broadcasted_iota(I32, (LANES, LANES), 0) // width) == (lax.broadcasted_iota(I32, (LANES, LANES), 1) // width)
    return jnp.where(same, 1.0, 0.0).astype(BF16)


def _group_sums(t, group_matrix):
    return _dot(t.astype(BF16), group_matrix)


def _dot_nt(a, b, **kw):
    return lax.dot_general(a, b, (((1,), (1,)), ((), ())), preferred_element_type=F32, **kw)


def _inproj0_kernel(x_ref, pos_ref, gmix_ref, win_ref, qan_ref, wuq_ref, kvan_ref, wuk_ref, wuv_ref,
                    qg_ref, kg_ref, invf_ref, q_out, k_out, v_out, u_out):
    h = _rms(x_ref[...]) * gmix_ref[...]
    proj = _dot(h.astype(BF16), win_ref[...])
    c_q = proj[:, :MLA_Q_LORA]
    c_kv = proj[:, MLA_Q_LORA:MLA_Q_LORA + MLA_KV_LORA]
    k_rope = proj[:, 384:512]
    a = proj[:, 512:512 + CONV_CH]
    g = proj[:, 512 + CONV_CH:]
    u_out[...] = a * jax.nn.sigmoid(g)
    cqn = (_rms(c_q) * qan_ref[...]).astype(BF16)
    ckvn = (_rms(c_kv) * kvan_ref[...]).astype(BF16)
    q = _dot(cqn, wuq_ref[...])
    kn = _dot(ckvn, wuk_ref[...])
    slot_row = lax.broadcasted_iota(I32, (MLA_HEADS * LANES, 1), 0) & (LANES - 1)
    v_out[...] = (_dot_nt(wuv_ref[...], ckvn) + jnp.where(slot_row == MLA_V, 1.0, 0.0)).astype(BF16)
    ang = invf_ref[...] * pos_ref[...].astype(F32)
    cos_t, sin_t = jnp.cos(ang), jnp.sin(ang)
    tm = ang.shape[1]
    tail = jnp.zeros((LANES - MLA_QK, tm), F32)
    cos = jnp.concatenate([jnp.ones((MLA_NOPE, tm), F32), cos_t, cos_t, tail], axis=0).T
    sin = jnp.concatenate([jnp.zeros((MLA_NOPE, tm), F32), -sin_t, sin_t, tail], axis=0).T
    lane = lax.broadcasted_iota(I32, (1, LANES), 1)
    first_half = (lane >= MLA_NOPE) & (lane < MLA_NOPE + MLA_ROPE // 2)
    scale = MLA_QK ** -0.5 * LOG2E

    all_lanes = _group_matrix(1)

    def norm_rope(t, gain):
        t = t * lax.rsqrt(_group_sums(t * t, all_lanes) * (1.0 / MLA_QK) + 1e-6) * gain
        partner = jnp.where(first_half, pltpu.roll(t, LANES - MLA_ROPE // 2, 1), pltpu.roll(t, MLA_ROPE // 2, 1))
        return t * cos + partner * sin

    for hd in range(MLA_HEADS):
        sl = slice(hd * LANES, (hd + 1) * LANES)
        q_out[:, sl] = (norm_rope(q[:, sl], qg_ref[...]) * scale).astype(BF16)
        k_out[:, sl] = norm_rope(kn[:, sl] + k_rope, kg_ref[...]).astype(BF16)


def _head_slots(w, n_heads, width, offset=0):
    k = w.shape[0]
    w = w.reshape(k, n_heads, width)
    w = jnp.pad(w, ((0, 0), (0, 0), (offset, LANES - width - offset)))
    return w.reshape(k, n_heads * LANES)


def _inproj0(x2d, pos_row, gmix, w_in, q_a_norm, w_uq, kv_a_norm, w_ukv, q_norm, k_norm, tm=512):
    T, D = x2d.shape
    H = MLA_HEADS
    w_krope = jnp.pad(w_in[:, 384:416], ((0, 0), (MLA_NOPE, LANES - MLA_NOPE - MLA_ROPE)))
    w_in_p = jnp.concatenate([w_in[:, :384], w_krope, w_in[:, 416:]], axis=1).astype(BF16)
    w_uq_p = _head_slots(w_uq, H, MLA_QK).astype(BF16)
    w_ukv3 = w_ukv.reshape(MLA_KV_LORA, H, MLA_NOPE + MLA_V)
    w_uk_p = _head_slots(w_ukv3[:, :, :MLA_NOPE].reshape(MLA_KV_LORA, H * MLA_NOPE), H, MLA_NOPE).astype(BF16)
    w_uv = _head_slots(w_ukv3[:, :, MLA_NOPE:].reshape(MLA_KV_LORA, H * MLA_V), H, MLA_V).T.astype(BF16)
    pad = LANES - MLA_QK
    qg = jnp.pad(q_norm, (0, pad)).reshape(1, LANES)
    kg = jnp.pad(k_norm, (0, pad)).reshape(1, LANES)
    half = MLA_ROPE // 2
    inv_freq = (ROPE_THETA ** (-jnp.arange(half, dtype=F32) / half)).reshape(half, 1)
    row = lambda n: pl.BlockSpec((tm, n), lambda i: (i, 0))
    n_in = w_in_p.shape[1]
    return pl.pallas_call(
        _inproj0_kernel,
        grid=(T // tm,),
        in_specs=[row(D), pl.BlockSpec((1, tm), lambda i: (0, i)), _full((1, D)), _full((D, n_in)),
                  _full((1, MLA_Q_LORA)), _full((MLA_Q_LORA, H * LANES)), _full((1, MLA_KV_LORA)),
                  _full((MLA_KV_LORA, H * LANES)), _full((H * LANES, MLA_KV_LORA)), _full((1, LANES)),
                  _full((1, LANES)), _full((half, 1))],
        out_specs=[row(H * LANES), row(H * LANES), pl.BlockSpec((H * LANES, tm), lambda i: (0, i)), row(CONV_CH)],
        out_shape=[jax.ShapeDtypeStruct((T, H * LANES), BF16), jax.ShapeDtypeStruct((T, H * LANES), BF16),
                   jax.ShapeDtypeStruct((H * LANES, T), BF16), jax.ShapeDtypeStruct((T, CONV_CH), F32)],
        compiler_params=_params("parallel"),
        name="inproj0",
    )(x2d, pos_row, gmix.reshape(1, D), w_in_p, q_a_norm.reshape(1, -1), w_uq_p, kv_a_norm.reshape(1, -1),
      w_uk_p, w_uv, qg, kg, inv_freq)


MLA_TQ = 512
MLA_TK = 512
MLA_ONE = MLA_V


def _mla_attn_kernel(q_ref, k_ref, vt_ref, o_ref):
    S = q_ref.shape[1]
    tq, tk = MLA_TQ, MLA_TK
    krow = lax.broadcasted_iota(I32, (tk, 1), 0)
    qcol = lax.broadcasted_iota(I32, (1, tq), 1)

    def q_block(qi, carry):
        q0 = pl.multiple_of(qi * tq, tq)
        qs = [q_ref[0, pl.ds(q0, tq), hh * LANES:(hh + 1) * LANES] for hh in range(2)]

        def kv_step(j, state, masked):
            k0 = pl.multiple_of(j * tk, tk)
            hs = [slice(hh * LANES, (hh + 1) * LANES) for hh in range(2)]
            ss = [_dot_nt(k_ref[0, pl.ds(k0, tk), hs[hh]], qs[hh]) for hh in range(2)]
            out = []
            for hh in range(2):
                m, acc = state[hh]
                s = jnp.where(k0 + krow <= q0 + qcol, ss[hh], NEG_INF) if masked else ss[hh]
                m_new = jnp.maximum(m, jnp.max(s, axis=0, keepdims=True))
                p = jnp.exp2(s - m_new).astype(BF16)
                out.append((m_new, jnp.exp2(m - m_new) * acc + _dot(vt_ref[hs[hh], pl.ds(k0, tk)], p)))
            return tuple(out)

        init = (jnp.full((1, tq), NEG_INF, F32), jnp.zeros((LANES, tq), F32))
        n_full = (qi * tq) // tk
        state = lax.fori_loop(0, n_full, functools.partial(kv_step, masked=False), (init, init))
        state = kv_step(n_full, state, True)
        outs = []
        for hh in range(2):
            acc = state[hh][1]
            outs.append((acc * (1.0 / acc[MLA_ONE:MLA_ONE + 1, :]))[:MLA_V, :])
        o_ref[0, pl.ds(q0, tq), :] = jnp.concatenate(outs, axis=0).T.astype(BF16)
        return carry

    lax.fori_loop(0, S // tq, q_block, 0)


def _mla_attn(q, k, vt):
    B, S, _ = q.shape
    spec = pl.BlockSpec((1, S, 2 * LANES), lambda b, h: (b, 0, h))
    return pl.pallas_call(
        _mla_attn_kernel,
        grid=(B, MLA_HEADS // 2),
        in_specs=[spec, spec, pl.BlockSpec((2 * LANES, S), lambda b, h: (h, b))],
        out_specs=pl.BlockSpec((1, S, 2 * MLA_V), lambda b, h: (b, 0, h)),
        out_shape=jax.ShapeDtypeStruct((B, S, MLA_HEADS * MLA_V), BF16),
        compiler_params=_params("parallel", "parallel"),
        name="mla_attn",
    )(q, k, vt)


CONV_TILE = 64
CONV_PAD = 32


CONV_ROWS = 512


def _conv_out0_kernel(u_ref, o_ref, x_ref, dww_ref, dwb_ref, lng_ref, lnb_ref, wo_ref, out_ref, upad, act, shifted):
    step = pl.program_id(1)

    @pl.when(step == 0)
    def _():
        upad[0:CONV_PAD, :] = jnp.zeros((CONV_PAD, CONV_CH), F32)
        upad[CONV_PAD:, :] = u_ref[0]

    lead = CONV_PAD - (CONV_W - 1)
    base = step * CONV_ROWS

    def tile(i, carry):
        t0 = pl.multiple_of(i * CONV_TILE, CONV_TILE)
        win = upad[pl.ds(pl.multiple_of(base + t0, CONV_TILE), CONV_TILE + CONV_PAD), :]
        acc = jnp.zeros((CONV_TILE, CONV_CH), F32) + dwb_ref[...]
        span = CONV_TILE + CONV_PAD - 8
        for r in range(1, 8):
            shifted[r - 1] = win[r:r + span, :]
        for j in range(CONV_W):
            r, k = (lead + j) % 8, (lead + j) // 8 * 8
            tap = win[k:k + CONV_TILE, :] if r == 0 else shifted[r - 1, k:k + CONV_TILE, :]
            acc = acc + tap * dww_ref[j:j + 1, :]
        xc = acc - _row_mean(acc)
        var = _row_mean(xc * xc)
        y = xc * lax.rsqrt(var + 1e-5) * lng_ref[...] + lnb_ref[...]
        act[pl.ds(t0, CONV_TILE), :] = (y * jax.nn.sigmoid(y)).astype(BF16)
        return carry

    lax.fori_loop(0, CONV_ROWS // CONV_TILE, tile, 0)
    n_o = o_ref.shape[2]
    y = _dot(o_ref[0], wo_ref[:n_o, :]) + _dot(act[...], wo_ref[n_o:, :])
    out_ref[0] = x_ref[0] + y


def _conv_out0(u, o_mla, x, dw_w, dw_b, ln_g, ln_b, w_out):
    B, S, D = x.shape
    n_o = o_mla.shape[2]
    w_out_p = w_out.astype(BF16)
    tspec = lambda n: pl.BlockSpec((1, CONV_ROWS, n), lambda b, t: (b, t, 0))
    return pl.pallas_call(
        _conv_out0_kernel,
        grid=(B, S // CONV_ROWS),
        in_specs=[pl.BlockSpec((1, S, CONV_CH), lambda b, t: (b, 0, 0)), tspec(n_o), tspec(D),
                  _full((CONV_W, CONV_CH)), _full((1, CONV_CH)), _full((1, CONV_CH)), _full((1, CONV_CH)),
                  _full((n_o + CONV_CH, D))],
        out_specs=tspec(D),
        out_shape=jax.ShapeDtypeStruct((B, S, D), F32),
        scratch_shapes=[pltpu.VMEM((S + CONV_PAD, CONV_CH), F32), pltpu.VMEM((CONV_ROWS, CONV_CH), BF16),
                        pltpu.VMEM((7, CONV_TILE + CONV_PAD - 8, CONV_CH), F32)],
        compiler_params=_params("parallel", "arbitrary"),
        name="conv_out0",
    )(u, o_mla, x, dw_w, dw_b.reshape(1, -1), ln_g.reshape(1, -1), ln_b.reshape(1, -1), w_out_p)


def _mixer0(x, positions, gmix, w_in, q_a_norm, w_uq, kv_a_norm, w_ukv, q_norm, k_norm,
            dw_w, dw_b, ln_g, ln_b, w_out):
    B, S, D = x.shape
    T = B * S
    q, k, v, u = _inproj0(x.reshape(T, D), positions.reshape(1, T), gmix, w_in, q_a_norm, w_uq, kv_a_norm, w_ukv,
                          q_norm, k_norm)
    o = _mla_attn(q.reshape(B, S, -1), k.reshape(B, S, -1), v)
    return _conv_out0(u.reshape(B, S, -1), o, x, dw_w, dw_b, ln_g, ln_b, w_out)


ROUTE_TILE = 1024
ROUTE_ROWS = 40


def _route_kernel(x_ref, g_ref, wr_ref, br_ref, tri_ref, hn_ref, oi_ref, of_ref, cnt_ref, carry):
    @pl.when(pl.program_id(0) == 0)
    def _():
        carry[...] = jnp.zeros_like(carry)

    hn = _rms(x_ref[...]) * g_ref[...]
    _slab_store(hn_ref, hn)
    tm = hn.shape[0]
    logits = _dot_nt(wr_ref[...], hn, precision=lax.Precision.HIGHEST) + br_ref[...]
    gl = logits[MOE_EXPERTS:MOE_EXPERTS + MOE_GROUPS]
    rid_g = lax.broadcasted_iota(I32, (MOE_GROUPS, tm), 0)
    gmax = jnp.max(gl, axis=0, keepdims=True)
    grp = jnp.min(jnp.where(gl == gmax, rid_g, MOE_GROUPS), axis=0, keepdims=True)
    g_w = 1.0 / jnp.sum(jnp.exp(gl - gmax), axis=0, keepdims=True)
    e_in = jnp.zeros((MOE_EPG, tm), F32)
    for gi in range(MOE_GROUPS):
        e_in = jnp.where(grp == gi, logits[gi * MOE_EPG:(gi + 1) * MOE_EPG], e_in)
    rid_e = lax.broadcasted_iota(I32, (MOE_EPG, tm), 0)
    v1 = jnp.max(e_in, axis=0, keepdims=True)
    i1 = jnp.min(jnp.where(e_in == v1, rid_e, MOE_EPG), axis=0, keepdims=True)
    rest = jnp.where(rid_e == i1, -jnp.inf, e_in)
    v2 = jnp.max(rest, axis=0, keepdims=True)
    i2 = jnp.min(jnp.where(rest == v2, rid_e, MOE_EPG), axis=0, keepdims=True)
    ex = jnp.exp(v2 - v1)
    den = 1.0 + ex
    e1 = grp * MOE_EPG + i1
    e2 = grp * MOE_EPG + i2
    rid = lax.broadcasted_iota(I32, (MOE_EXPERTS, tm), 0)
    hit1 = rid == e1
    hit2 = rid == e2
    member = jnp.where(hit1 | hit2, 1.0, 0.0)
    before = _dot(member.astype(BF16), tri_ref[...]) + carry[...]
    r1 = jnp.sum(jnp.where(hit1, before, 0.0), axis=0, keepdims=True)
    r2 = jnp.sum(jnp.where(hit2, before, 0.0), axis=0, keepdims=True)
    carry[...] = carry[...] + jnp.sum(member, axis=1, keepdims=True)
    oi_ref[...] = jnp.zeros_like(oi_ref)
    oi_ref[0:1, :] = e1
    oi_ref[1:2, :] = e2
    oi_ref[2:3, :] = r1.astype(I32)
    oi_ref[3:4, :] = r2.astype(I32)
    of_ref[...] = jnp.zeros_like(of_ref)
    of_ref[0:1, :] = g_w / den
    of_ref[1:2, :] = g_w * ex / den
    cnt_ref[...] = jnp.broadcast_to(carry[...], cnt_ref.shape).astype(I32)


def _route(x2d, gain, router_group, router_group_b, router_expert, router_expert_b):
    T, D = x2d.shape
    tm = ROUTE_TILE
    pad = ROUTE_ROWS - MOE_EXPERTS - MOE_GROUPS
    wr = jnp.concatenate([router_expert.T, router_group.T, jnp.zeros((pad, D), F32)], axis=0)
    br = jnp.concatenate([router_expert_b, router_group_b, jnp.zeros((pad,), F32)]).reshape(ROUTE_ROWS, 1)
    tri = (jnp.arange(tm)[:, None] < jnp.arange(tm)[None, :]).astype(BF16)
    return pl.pallas_call(
        _route_kernel,
        grid=(T // tm,),
        in_specs=[pl.BlockSpec((tm, D), lambda i: (i, 0)), _full((1, D)), _full((ROUTE_ROWS, D)),
                  _full((ROUTE_ROWS, 1)), _full((tm, tm))],
        out_specs=[pl.BlockSpec((tm * SLAB, LANES), lambda i: (i, 0)), pl.BlockSpec((8, tm), lambda i: (0, i)),
                   pl.BlockSpec((8, tm), lambda i: (0, i)), _full((MOE_EXPERTS, LANES))],
        out_shape=[jax.ShapeDtypeStruct((T * SLAB, LANES), F32), jax.ShapeDtypeStruct((8, T), I32),
                   jax.ShapeDtypeStruct((8, T), F32), jax.ShapeDtypeStruct((MOE_EXPERTS, LANES), I32)],
        scratch_shapes=[pltpu.VMEM((MOE_EXPERTS, 1), F32)],
        compiler_params=_params("arbitrary"),
        name="moe_route",
    )(x2d, gain.reshape(1, D), wr, br, tri)


MOVE_CHUNK = 512
MOVE_UNROLL = 8


SLAB = 8


def _slab_load(ref, n, first=0, stride=SLAB):
    return jnp.concatenate([ref[pl.ds(first + j, n, stride=stride), :] for j in range(SLAB)], axis=1)


def _slab_store(ref, value):
    n = value.shape[0]
    for j in range(SLAB):
        ref[pl.ds(j, n, stride=SLAB), :] = value[:, j * LANES:(j + 1) * LANES]


def _slab(row):
    return pl.ds(pl.multiple_of(row, SLAB), SLAB)


def _for_tokens(n, fn):
    def body(t, carry):
        fn(t, 0)
        fn(t, 1)
        return carry
    lax.fori_loop(0, n, body, 0, unroll=MOVE_UNROLL)


def _dispatch_kernel(dest_ref, hn_ref, xs_init_ref, xs_ref, sem):
    del xs_init_ref
    base = pl.program_id(0) * (2 * MOVE_CHUNK)

    def copy(t, k):
        return pltpu.make_async_copy(hn_ref.at[_slab(t * SLAB)], xs_ref.at[_slab(dest_ref[base + 2 * t + k])], sem)

    _for_tokens(MOVE_CHUNK, lambda t, k: copy(t, k).start())
    _for_tokens(MOVE_CHUNK, lambda t, k: copy(t, k).wait())


def _dispatch(dest, hn, n_rows):
    T = hn.shape[0] // SLAB
    tm = MOVE_CHUNK
    return pl.pallas_call(
        _dispatch_kernel,
        grid_spec=pltpu.PrefetchScalarGridSpec(
            num_scalar_prefetch=1,
            grid=(T // tm,),
            in_specs=[pl.BlockSpec((tm * SLAB, LANES), lambda i, d: (i, 0)), pl.BlockSpec(memory_space=pl.ANY)],
            out_specs=pl.BlockSpec(memory_space=pl.ANY),
            scratch_shapes=[pltpu.SemaphoreType.DMA(())],
        ),
        out_shape=jax.ShapeDtypeStruct((n_rows * SLAB, LANES), F32),
        input_output_aliases={2: 0},
        compiler_params=_params("arbitrary"),
        name="moe_dispatch",
    )(dest, hn, jnp.zeros((n_rows * SLAB, LANES), F32))


def _expert_kernel(blk_e_ref, n_used_ref, xs_ref, wg_ref, wu_ref, wd_ref, ys_ref, wg_s, wu_s, wd_s):
    b = pl.program_id(0)

    @pl.when((b == 0) | (blk_e_ref[b] != blk_e_ref[jnp.maximum(b - 1, 0)]))
    def _():
        wg_s[...] = wg_ref[0].astype(BF16)
        wu_s[...] = wu_ref[0].astype(BF16)
        wd_s[...] = wd_ref[0].astype(BF16)

    @pl.when(b < n_used_ref[0])
    def _():
        x = _slab_load(xs_ref, MOE_ROW_BLOCK).astype(BF16)
        gate = _dot(x, wg_s[...])
        hid = (gate * jax.nn.sigmoid(gate) * _dot(x, wu_s[...])).astype(BF16)
        _slab_store(ys_ref, _dot(hid, wd_s[...]))

    @pl.when(b >= n_used_ref[0])
    def _():
        ys_ref[...] = jnp.zeros_like(ys_ref)


def _experts(blk_expert, n_used, xs, w_gate, w_up, w_down, layer):
    R = xs.shape[0] // SLAB
    rb = MOE_ROW_BLOCK
    D = w_gate.shape[2]
    up_spec = pl.BlockSpec((None, 1, D, MOE_HIDDEN), lambda b, be, nu: (layer, be[b], 0, 0))
    return pl.pallas_call(
        _expert_kernel,
        grid_spec=pltpu.PrefetchScalarGridSpec(
            num_scalar_prefetch=2,
            grid=(R // rb,),
            in_specs=[pl.BlockSpec((rb * SLAB, LANES), lambda b, be, nu: (jnp.minimum(b, nu[0] - 1), 0)),
                      up_spec, up_spec,
                      pl.BlockSpec((None, 1, MOE_HIDDEN, D), lambda b, be, nu: (layer, be[b], 0, 0))],
            out_specs=pl.BlockSpec((rb * SLAB, LANES), lambda b, be, nu: (b, 0)),
            scratch_shapes=[pltpu.VMEM((D, MOE_HIDDEN), BF16), pltpu.VMEM((D, MOE_HIDDEN), BF16),
                            pltpu.VMEM((MOE_HIDDEN, D), BF16)],
        ),
        out_shape=jax.ShapeDtypeStruct((R * SLAB, LANES), F32),
        compiler_params=_params("arbitrary"),
        name="moe_experts",
    )(blk_expert, n_used, xs, w_gate, w_up, w_down)


def _combine_kernel(dest_ref, x_ref, gate_ref, ys_ref, out_ref, buf, sem):
    tm = x_ref.shape[0]
    step = pl.program_id(0)

    def copy(s, t, k):
        a = 2 * t + k
        slot = s % 2
        return pltpu.make_async_copy(ys_ref.at[_slab(dest_ref[s * (2 * tm) + a])], buf.at[slot, _slab(a * SLAB)],
                                     sem.at[slot])

    @pl.when(step == 0)
    def _():
        _for_tokens(tm, lambda t, k: copy(step, t, k).start())

    @pl.when(step + 1 < pl.num_programs(0))
    def _():
        _for_tokens(tm, lambda t, k: copy(step + 1, t, k).start())

    _for_tokens(tm, lambda t, k: copy(step, t, k).wait())
    mine = buf.at[step % 2]
    y0 = _slab_load(mine, tm, 0, 2 * SLAB)
    y1 = _slab_load(mine, tm, SLAB, 2 * SLAB)
    out_ref[...] = x_ref[...] + gate_ref[:, 0:1] * y0 + gate_ref[:, 1:2] * y1


def _combine(dest, x2d, gates_col, ys, tm=MOVE_CHUNK):
    T, D = x2d.shape
    return pl.pallas_call(
        _combine_kernel,
        grid_spec=pltpu.PrefetchScalarGridSpec(
            num_scalar_prefetch=1,
            grid=(T // tm,),
            in_specs=[pl.BlockSpec((tm, D), lambda i, d: (i, 0)), pl.BlockSpec((tm, 2), lambda i, d: (i, 0)),
                      pl.BlockSpec(memory_space=pl.ANY)],
            out_specs=pl.BlockSpec((tm, D), lambda i, d: (i, 0)),
            scratch_shapes=[pltpu.VMEM((2, tm * 2 * SLAB, LANES), F32), pltpu.SemaphoreType.DMA((2,))],
        ),
        out_shape=jax.ShapeDtypeStruct((T, D), F32),
        compiler_params=_params("arbitrary"),
        name="moe_combine",
    )(dest, x2d, gates_col, ys)


def _moe(x, gain, router_group, router_group_b, router_expert, router_expert_b, w_gate, w_up, w_down, layer):
    B, S, D = x.shape
    T = B * S
    x2d = x.reshape(T, D)
    hn, oi, of, cnt = _route(x2d, gain, router_group, router_group_b, router_expert, router_expert_b)
    rb = MOE_ROW_BLOCK
    counts = cnt[:, 0]
    cap = (counts + rb - 1) // rb * rb
    cap_end = jnp.cumsum(cap)
    start = cap_end - cap
    eid = oi[0:2].T
    first = jnp.sum(jnp.where(eid[:, :, None] == jnp.arange(MOE_EXPERTS), start, 0), axis=-1)
    dest = ((first + oi[2:4].T) * SLAB).reshape(2 * T).astype(I32)
    n_rows = 2 * T + MOE_EXPERTS * rb
    n_blk = n_rows // rb
    blk_first_row = jnp.arange(n_blk, dtype=I32) * rb
    blk_expert = jnp.minimum(jnp.sum(cap_end[None, :] <= blk_first_row[:, None], axis=1), MOE_EXPERTS - 1)
    n_used = (cap_end[-1:] // rb).astype(I32)
    xs = _dispatch(dest, hn, n_rows)
    ys = _experts(blk_expert.astype(I32), n_used, xs, w_gate, w_up, w_down, layer)
    out = _combine(dest, x2d, of[0:2].T, ys)
    return out.reshape(B, S, D)


NSA_TQ = 256
SLC_TQ = 256
NSA_PAIRS = NSA_HPG // 2
N_SEL = 32
NSA_Q_SCALE = NSA_DK ** -0.5 * LOG2E
LO_ONE = LANES - 1
HI_ONE = 0
SLC_TK = 512
WIN_SPAN = WINDOW + SLC_TQ
AUG_POS = N_SEL
POS_BYTES = 4
MASK_BIG = 1e30


def _pair_norm(t, gain2, lane_lo):
    t2 = t * t
    s_lo = jnp.sum(jnp.where(lane_lo, t2, 0.0), axis=-1, keepdims=True)
    s_hi = jnp.sum(jnp.where(lane_lo, 0.0, t2), axis=-1, keepdims=True)
    inv = jnp.where(lane_lo, lax.rsqrt(s_lo * (1.0 / NSA_DK) + 1e-6), lax.rsqrt(s_hi * (1.0 / NSA_DK) + 1e-6))
    return t * inv * gain2


def _inproj1_kernel(x_ref, prel_ref, blocks_ref, g_ref, win_ref, wvt_ref, qg_ref, ksg_ref, kwg_ref,
                    q_out, kc_out, vc_out, ks_out, vs_out, kw_out, vw_out, gate_out):
    h = _rms(x_ref[...]) * g_ref[...]
    hb = h.astype(BF16)
    proj = _dot(hb, win_ref[...])
    lane = lax.broadcasted_iota(I32, (1, LANES), 1)
    lane_lo = lane < NSA_DK
    nq = NSA_HEADS * NSA_DK
    for p in range(nq // LANES):
        sl = slice(p * LANES, (p + 1) * LANES)
        q_out[:, sl] = (_pair_norm(proj[:, sl], qg_ref[...], lane_lo) * NSA_Q_SCALE).astype(BF16)
    part = lambda i: proj[:, nq + i * LANES:nq + (i + 1) * LANES]
    kc_out[...] = part(0)
    vc_out[...] = part(1)

    def lo_hi(t, lo_pad, hi_pad):
        r = pltpu.roll(t, NSA_DK, 1)
        return jnp.concatenate([jnp.where(lane_lo, t, lo_pad), jnp.where(lane_lo, hi_pad, r),
                                jnp.where(lane_lo, r, lo_pad), jnp.where(lane_lo, hi_pad, t)], axis=1).astype(BF16)

    prel = prel_ref[...]
    half_lane = (lane & (NSA_DK - 1)) - AUG_POS
    kaug = blocks_ref[...]
    for k in range(POS_BYTES):
        byte = ((prel >> (8 * (POS_BYTES - 1 - k))) & 255).astype(F32)
        kaug = kaug + jnp.where((half_lane == k) | (half_lane == k + POS_BYTES), byte, 0.0)
    ks_out[...] = lo_hi(_pair_norm(part(2), ksg_ref[...], lane_lo), kaug, kaug)
    kw_out[...] = lo_hi(_pair_norm(part(3), kwg_ref[...], lane_lo), kaug, kaug)
    gate_out[...] = jax.nn.sigmoid(part(4))

    vt = _dot_nt(wvt_ref[...], hb)
    tm = vt.shape[1]
    sub = lax.broadcasted_iota(I32, (NSA_DK, 1), 0)
    pad_lo = jnp.broadcast_to(jnp.where(sub == LO_ONE - NSA_DK, 1.0, 0.0), (NSA_DK, tm))
    pad_hi = jnp.broadcast_to(jnp.where(sub == HI_ONE, 1.0, 0.0), (NSA_DK, tm))
    for branch, out in enumerate((vs_out, vw_out)):
        pieces = []
        for grp in range(NSA_G):
            v = vt[(2 * branch + grp) * NSA_DK:(2 * branch + grp + 1) * NSA_DK, :]
            pieces += [v, pad_lo, pad_hi, v]
        out[...] = jnp.concatenate(pieces, axis=0).astype(BF16)


def _inproj1(x2d, positions, gain, w_in, q_norm, k_norm_slc, k_norm_win, tm=256):
    T, D = x2d.shape
    S = positions.shape[1]
    prel = (positions - positions[:, :1]).reshape(T, 1)
    own_block = np.arange(S)[:, None] // SLC_BLOCK == (np.arange(LANES)[None, :] % NSA_DK)
    blocks = jnp.asarray(own_block * MASK_BIG, F32)
    nq_, kvw = NSA_HEADS * NSA_DK, NSA_KVW
    keep = [w_in[:, :nq_ + 3 * kvw], w_in[:, nq_ + 4 * kvw:nq_ + 5 * kvw], w_in[:, nq_ + 6 * kvw:]]
    w_in_p = jnp.concatenate(keep, axis=1)
    n_in = w_in_p.shape[1]
    n_pad = -n_in % LANES
    w_in_p = jnp.pad(w_in_p, ((0, 0), (0, n_pad))).astype(BF16)
    two = lambda g: jnp.tile(g, 2).reshape(1, LANES)
    row = lambda n: pl.BlockSpec((tm, n), lambda i: (i, 0))
    nq = NSA_HEADS * NSA_DK
    vcols = lambda i: w_in[:, nq + i * NSA_KVW:nq + (i + 1) * NSA_KVW]
    w_vt = jnp.concatenate([vcols(3), vcols(5)], axis=1).T.astype(BF16)
    bf = lambda n: jax.ShapeDtypeStruct((T, n), BF16)
    f32 = lambda n: jax.ShapeDtypeStruct((T, n), F32)
    col = pl.BlockSpec((4 * LANES, tm), lambda i: (0, i))
    vt_shape = jax.ShapeDtypeStruct((4 * LANES, T), BF16)
    return pl.pallas_call(
        _inproj1_kernel,
        grid=(T // tm,),
        in_specs=[row(D), row(1), pl.BlockSpec((tm, LANES), lambda i: (i % (S // tm), 0)), _full((1, D)),
                  _full((D, n_in + n_pad)), _full((2 * NSA_KVW, D)), _full((1, LANES)), _full((1, LANES)),
                  _full((1, LANES))],
        out_specs=[row(nq), row(LANES), row(LANES), row(4 * LANES), col, row(4 * LANES), col, row(LANES)],
        out_shape=[bf(nq), f32(LANES), f32(LANES), bf(4 * LANES), vt_shape, bf(4 * LANES), vt_shape, f32(LANES)],
        compiler_params=_params("parallel"),
        name="inproj1",
    )(x2d, prel, blocks, gain.reshape(1, D), w_in_p, w_vt, two(q_norm), two(k_norm_slc), two(k_norm_win))


def _stack_pairs(q_ref, rows):
    return jnp.concatenate([q_ref[0, rows, p * LANES:(p + 1) * LANES] for p in range(NSA_PAIRS)], axis=0)


def _cmp_kernel(q_ref, kch_ref, vch_ref, pos3_ref, pend_ref, gate_ref, slope_ref,
                pek_ref, w1k_ref, w2k_ref, kcg_ref, pev_ref, w1v_ref, w2v_ref, ovl_ref, eye_ref,
                o_ref, sel_ref, kc_s, vc_s):
    tq = NSA_TQ
    S = q_ref.shape[1]

    def compress(ch_ref, pe_ref, w1_ref, w2_ref):
        a = ch_ref[0, 0]
        h_lo = _dot((a + pe_ref[0:1, :]).astype(BF16), w1_ref[0])
        h_hi = _dot((a + pe_ref[1:2, :]).astype(BF16), w1_ref[1])
        n = h_hi.shape[0]
        hid = jax.nn.gelu(h_lo + pltpu.roll(h_hi, n - 1, 0)).astype(BF16)
        return _dot(hid, w2_ref[0]), _dot(hid, w2_ref[1])

    k_lo, k_hi = compress(kch_ref, pek_ref, w1k_ref, w2k_ref)
    for e, kk in enumerate((k_lo, k_hi)):
        kk = kk * lax.rsqrt(jnp.sum(kk * kk, axis=-1, keepdims=True) * (1.0 / NSA_DK) + 1e-6) * kcg_ref[e:e + 1, :]
        kc_s[e] = kk.astype(BF16)
    v_lo, v_hi = compress(vch_ref, pev_ref, w1v_ref, w2v_ref)
    vc_s[0] = v_lo.T.astype(BF16)
    vc_s[1] = v_hi.T.astype(BF16)

    n_cmp = (S - CMP_BLOCK) // CMP_STRIDE + 1
    n_lanes = NSA_PAIRS * tq
    lane = lax.broadcasted_iota(I32, (1, LANES), 1)
    sub_lo = lax.broadcasted_iota(I32, (LANES, 1), 0) < NSA_DK
    blk_row = lax.broadcasted_iota(I32, (N_SEL, 1), 0)
    cmp_row = lax.broadcasted_iota(I32, (LANES, 1), 0)
    in_range = cmp_row < n_cmp

    def q_block(qi, carry):
        t0 = pl.multiple_of(qi * tq, tq)
        rows = pl.ds(t0, tq)
        qs = _stack_pairs(q_ref, rows)
        pos_q0 = pos3_ref[0, pl.ds(qi * (tq // LANES), 1), :][:, 0:1]
        posrel = (pend_ref[0] - pos_q0).astype(F32)
        t_lane = t0 + (lax.broadcasted_iota(I32, (1, n_lanes), 1) & (tq - 1))
        valid = (t_lane >= CMP_STRIDE * cmp_row + (CMP_BLOCK - 1)) & in_range
        gt = gate_ref[0, 0, :, rows]
        psum = jnp.zeros((LANES, tq), F32)
        outs = []
        for e in range(2):
            slopes = jnp.concatenate([jnp.broadcast_to(slope_ref[0, NSA_PAIRS * e + p:NSA_PAIRS * e + p + 1, 0:1], (1, tq))
                                      for p in range(NSA_PAIRS)], axis=1)
            s = _dot_nt(kc_s[e], qs) + posrel * slopes
            s = jnp.where(valid, s, NEG_INF)
            m = jnp.max(s, axis=0, keepdims=True)
            p = jnp.where(valid, jnp.exp2(s - m), 0.0)
            p = p / jnp.maximum(jnp.sum(p, axis=0, keepdims=True), 1e-20)
            psum = psum + functools.reduce(jnp.add, [p[:, i * tq:(i + 1) * tq] for i in range(NSA_PAIRS)])
            gate = jnp.concatenate([gt[2 * i + e:2 * i + e + 1, :] for i in range(NSA_PAIRS)], axis=1)
            outs.append(_dot(vc_s[e], p.astype(BF16)) * gate)
        z = jnp.where(sub_lo, outs[0], outs[1])
        imp = jnp.dot(ovl_ref[...], psum, preferred_element_type=F32, precision=lax.Precision.HIGHEST)
        cur = (t0 + lax.broadcasted_iota(I32, (1, tq), 1)) // SLC_BLOCK
        forced = (blk_row == 0) | (blk_row == cur) | (blk_row == cur - 1)
        imp = jnp.where(forced, FORCE_SCORE, jnp.where(blk_row <= cur, imp, -1.0))
        rank = jnp.zeros((N_SEL, tq), I32)
        for i in range(N_SEL):
            ri = imp[i:i + 1, :]
            rank = rank + jnp.where((ri > imp) | ((ri == imp) & (blk_row > i)), 1, 0)
        sel_t = jnp.where(rank < SLC_TOP_N, 1.0, 0.0).astype(BF16)
        gap = jnp.zeros((NSA_DK - N_SEL, tq), BF16)
        sel_t = jnp.concatenate([sel_t, gap, sel_t, gap], axis=0)
        sel = _dot_nt(eye_ref[...], sel_t)
        sel_ref[0, 0, rows, :] = jnp.where((lane & (NSA_DK - 1)) < N_SEL, sel - 1.0, 0.0).astype(BF16)
        for p in range(NSA_PAIRS):
            o_ref[0, rows, p * LANES:(p + 1) * LANES] = z[:, p * tq:(p + 1) * tq].T.astype(BF16)
        return carry

    lax.fori_loop(0, S // tq, q_block, 0)


def _lo_hi_cols(w):
    z = jnp.zeros_like(w)
    return jnp.stack([jnp.concatenate([w, z], axis=1), jnp.concatenate([z, w], axis=1)])


def _nsa_cmp(q, kch, vch, pos3, pend, gates_g, slopes_t, pe_k, w1_k, w2_k, k_norm_cmp, pe_v, w1_v, w2_v):
    B, S, _ = q.shape
    G = NSA_G
    nch = S // CMP_STRIDE
    half = CMP_STRIDE * NSA_DK
    pe2 = lambda pe: pe.reshape(2, half)
    w1_2 = lambda w: w.reshape(2, half, CMP_HIDDEN).astype(BF16)
    n_sel = S // SLC_BLOCK
    n_cmp = (S - CMP_BLOCK) // CMP_STRIDE + 1
    cmp_start = np.arange(LANES) * CMP_STRIDE
    slc_start = np.arange(n_sel) * SLC_BLOCK
    overlap = ((cmp_start[None, :] < slc_start[:, None] + SLC_BLOCK) & (cmp_start[None, :] + CMP_BLOCK > slc_start[:, None])
               & (np.arange(LANES)[None, :] < n_cmp))
    ovl = jnp.asarray(overlap, F32)
    eye = jnp.eye(NSA_TQ, dtype=BF16)
    qspec = pl.BlockSpec((1, S, NSA_PAIRS * LANES), lambda b, g: (b, 0, g))
    chspec = pl.BlockSpec((1, 1, nch, half), lambda b, g: (b, g, 0, 0))
    return pl.pallas_call(
        _cmp_kernel,
        grid=(B, G),
        in_specs=[qspec, chspec, chspec,
                  pl.BlockSpec((1, S // LANES, LANES), lambda b, g: (b, 0, 0)),
                  pl.BlockSpec((1, LANES, 1), lambda b, g: (b, 0, 0)),
                  pl.BlockSpec((1, 1, 3 * NSA_HPG, S), lambda b, g: (b, g, 0, 0)),
                  pl.BlockSpec((1, NSA_HPG, LANES), lambda b, g: (g, 0, 0)),
                  _full((2, half)), _full((2, half, CMP_HIDDEN)), _full((2, CMP_HIDDEN, LANES)), _full((2, LANES)),
                  _full((2, half)), _full((2, half, CMP_HIDDEN)), _full((2, CMP_HIDDEN, LANES)),
                  _full((n_sel, LANES)), _full((NSA_TQ, NSA_TQ))],
        out_specs=[qspec, pl.BlockSpec((1, 1, S, LANES), lambda b, g: (b, g, 0, 0))],
        out_shape=[jax.ShapeDtypeStruct((B, S, NSA_HEADS * NSA_DK), BF16),
                   jax.ShapeDtypeStruct((B, G, S, LANES), BF16)],
        scratch_shapes=[pltpu.VMEM((2, nch, LANES), BF16), pltpu.VMEM((2, nch, LANES), BF16)],
        compiler_params=_params("parallel", "parallel"),
        name="nsa_cmp",
    )(q, kch, vch, pos3, pend, gates_g, slopes_t, pe2(pe_k), w1_2(w1_k), _lo_hi_cols(w2_k).astype(BF16),
      _lo_hi_cols(k_norm_cmp.reshape(1, -1)).reshape(2, LANES), pe2(pe_v), w1_2(w1_v),
      _lo_hi_cols(w2_v).astype(BF16), ovl, eye)


def _slc_win_kernel(q_ref, ks_ref, vs_ref, kw_ref, vw_ref, sel_ref, gate_ref, slopeq_ref, ocmp_ref, o_ref):
    tq = SLC_TQ
    S = q_ref.shape[1]
    lane = lax.broadcasted_iota(I32, (1, LANES), 1)
    lane_lo = lane < NSA_DK
    sub_lo = lax.broadcasted_iota(I32, (LANES, 1), 0) < NSA_DK
    ones_row = (LO_ONE, HI_ONE)

    def q_block(qi, carry):
        t0 = pl.multiple_of(qi * tq, tq)
        rows = pl.ds(t0, tq)
        qs = _stack_pairs(q_ref, rows)
        selm1 = sel_ref[0, 0, rows, :]

        def q_aug(e, with_sel):
            feats = [slopeq_ref[0, NSA_PAIRS * e + p:NSA_PAIRS * e + p + 1, :] for p in range(NSA_PAIRS)]
            if with_sel:
                extra = jnp.concatenate([selm1 + f for f in feats], axis=0)
            else:
                extra = jnp.concatenate([jnp.broadcast_to(f, (tq, LANES)) for f in feats], axis=0)
            return jnp.where(lane_lo, qs, extra) if e == 0 else jnp.where(lane_lo, extra, qs)

        def k_aug(k_ref, e, r0, n):
            return k_ref[0, pl.ds(r0, n), e * LANES:(e + 1) * LANES]

        gt = gate_ref[0, 0, :, rows]
        tq_lane = t0 + (lax.broadcasted_iota(I32, (1, NSA_PAIRS * tq), 1) & (tq - 1))

        def gated(accs, branch):
            outs = []
            for e, acc in enumerate(accs):
                gate = jnp.concatenate([gt[branch * NSA_HPG + 2 * p + e:branch * NSA_HPG + 2 * p + e + 1, :]
                                        for p in range(NSA_PAIRS)], axis=1)
                outs.append(acc * (gate / acc[ones_row[e]:ones_row[e] + 1, :]))
            return jnp.where(sub_lo, outs[0], outs[1])

        qa = [q_aug(0, True), q_aug(1, True)]
        krow = lax.broadcasted_iota(I32, (SLC_TK, 1), 0)

        def slc_scores(j):
            r0 = pl.multiple_of(j * SLC_TK, SLC_TK)
            return [_dot_nt(k_aug(ks_ref, e, r0, SLC_TK), qa[e]) for e in range(2)]

        def slc_softmax(j, ss, state, last):
            r0 = pl.multiple_of(j * SLC_TK, SLC_TK)
            out = []
            for e in range(2):
                m, acc = state[e]
                s = jnp.where(r0 + krow <= tq_lane, ss[e], NEG_INF) if last else ss[e]
                m_new = jnp.maximum(m, jnp.max(s, axis=0, keepdims=True))
                p = jnp.exp2(s - m_new).astype(BF16)
                pv = _dot(vs_ref[e * LANES:(e + 1) * LANES, pl.ds(r0, SLC_TK)], p)
                out.append((m_new, jnp.exp2(m - m_new) * acc + pv))
            return tuple(out)

        init = (jnp.full((1, NSA_PAIRS * tq), NEG_INF, F32), jnp.zeros((LANES, NSA_PAIRS * tq), F32))
        n_full = qi // (SLC_TK // tq)
        state = lax.fori_loop(0, n_full, lambda j, st: slc_softmax(j, slc_scores(j), st, False), (init, init))

        w0 = pl.multiple_of(jnp.maximum(t0 - WINDOW, 0), tq)
        ss_last = slc_scores(n_full)
        ss_win = [_dot_nt(k_aug(kw_ref, e, w0, WIN_SPAN), q_aug(e, False)) for e in range(2)]
        state = slc_softmax(n_full, ss_last, state, True)
        z = gated([state[0][1], state[1][1]], 1)
        rel = tq_lane - (w0 + lax.broadcasted_iota(I32, (WIN_SPAN, 1), 0))
        wmask = (rel >= 0) & (rel < WINDOW)
        accs = []
        for e in range(2):
            s = jnp.where(wmask, ss_win[e], NEG_INF)
            p = jnp.exp2(s - jnp.max(s, axis=0, keepdims=True)).astype(BF16)
            accs.append(_dot(vw_ref[e * LANES:(e + 1) * LANES, pl.ds(w0, WIN_SPAN)], p))
        z = z + gated(accs, 2)

        for p in range(NSA_PAIRS):
            sl = slice(p * LANES, (p + 1) * LANES)
            o_ref[0, rows, sl] = (ocmp_ref[0, rows, sl].astype(F32) + z[:, p * tq:(p + 1) * tq].T).astype(BF16)
        return carry

    lax.fori_loop(0, S // tq, q_block, 0)


def _slope_aug(slopes_eo):
    hi = slopes_eo.astype(BF16).astype(F32)
    lo = (slopes_eo - hi).astype(BF16).astype(F32)
    w = jnp.asarray([256.0 ** (POS_BYTES - 1 - k) for k in range(POS_BYTES)], F32)
    feats = jnp.concatenate([hi[..., None] * w, lo[..., None] * w], axis=-1)
    half = jnp.pad(feats, ((0, 0), (0, 0), (AUG_POS, NSA_DK - AUG_POS - 2 * POS_BYTES)))
    return jnp.concatenate([half, half], axis=-1).astype(BF16)


def _nsa_slc_win(q, ks4, vst4, kw4, vwt4, sel, gates_t, slopeq, o_cmp):
    B, S, _ = q.shape
    qspec = pl.BlockSpec((1, S, NSA_PAIRS * LANES), lambda b, g: (b, 0, g))
    kvspec = pl.BlockSpec((1, S, 2 * LANES), lambda b, g: (b, 0, g))
    vtspec = pl.BlockSpec((2 * LANES, S), lambda b, g: (g, b))
    return pl.pallas_call(
        _slc_win_kernel,
        grid=(B, NSA_G),
        in_specs=[qspec, kvspec, vtspec, kvspec, vtspec,
                  pl.BlockSpec((1, 1, S, LANES), lambda b, g: (b, g, 0, 0)),
                  pl.BlockSpec((1, 1, 3 * NSA_HPG, S), lambda b, g: (b, g, 0, 0)),
                  pl.BlockSpec((1, NSA_HPG, LANES), lambda b, g: (g, 0, 0)),
                  qspec],
        out_specs=qspec,
        out_shape=jax.ShapeDtypeStruct((B, S, NSA_HEADS * NSA_DK), BF16),
        compiler_params=_params("parallel", "parallel"),
        name="nsa_slc_win",
    )(q, ks4, vst4, kw4, vwt4, sel, gates_t, slopeq, o_cmp)


def _outproj_kernel(o_ref, x_ref, w_ref, out_ref):
    out_ref[...] = x_ref[...] + _dot(o_ref[...], w_ref[...])


def _outproj(o2d, x2d, w_out, tm=512):
    T, D = x2d.shape
    K = o2d.shape[1]
    return pl.pallas_call(
        _outproj_kernel,
        grid=(T // tm,),
        in_specs=[pl.BlockSpec((tm, K), lambda i: (i, 0)), pl.BlockSpec((tm, D), lambda i: (i, 0)), _full((K, D))],
        out_specs=pl.BlockSpec((tm, D), lambda i: (i, 0)),
        out_shape=jax.ShapeDtypeStruct((T, D), F32),
        compiler_params=_params("parallel"),
        name="outproj1",
    )(o2d, x2d, w_out.astype(BF16))


def _mixer1(x, positions, gain, w_in, q_norm, k_norm_cmp, k_norm_slc, k_norm_win,
            pe_k, w1_k, w2_k, pe_v, w1_v, w2_v, w_out):
    B, S, D = x.shape
    T = B * S
    G, HPG, DK = NSA_G, NSA_HPG, NSA_DK
    assert S // SLC_BLOCK == N_SEL and S % NSA_TQ == 0 and NSA_TQ % LANES == 0
    x2d = x.reshape(T, D)
    q, kc, vc, ks4, vs4, kw4, vw4, gates = _inproj1(x2d, positions, gain, w_in, q_norm, k_norm_slc,
                                                    k_norm_win)
    b3 = lambda t: t.reshape(B, S, -1)
    chunks = lambda t: t.reshape(B, S, G, DK).transpose(0, 2, 1, 3).reshape(B, G, S // CMP_STRIDE, CMP_STRIDE * DK)
    gates_t = gates[:, :3 * NSA_HEADS].reshape(B, S, 3, G, HPG).transpose(0, 3, 2, 4, 1).reshape(B, G, 3 * HPG, S)
    n_cmp = (S - CMP_BLOCK) // CMP_STRIDE + 1
    pend = jnp.pad(positions[:, CMP_BLOCK - 1::CMP_STRIDE][:, :n_cmp], ((0, 0), (0, LANES - n_cmp))).reshape(B, LANES, 1)
    pos3 = positions.reshape(B, S // LANES, LANES)
    slopes = jnp.asarray(2.0 ** (-8.0 * np.arange(1, NSA_HEADS + 1) / NSA_HEADS), F32) * LOG2E
    slopes_eo = slopes.reshape(G, NSA_PAIRS, 2).transpose(0, 2, 1).reshape(G, HPG)
    slopes_t = jnp.broadcast_to(slopes_eo[:, :, None], (G, HPG, LANES))
    o_cmp, sel = _nsa_cmp(b3(q), chunks(kc), chunks(vc), pos3, pend, gates_t, slopes_t,
                          pe_k, w1_k, w2_k, k_norm_cmp, pe_v, w1_v, w2_v)
    o = _nsa_slc_win(b3(q), b3(ks4), vs4, b3(kw4), vw4, sel, gates_t, _slope_aug(slopes_eo), o_cmp)
    return _outproj(o.reshape(T, -1), x2d, w_out).reshape(B, S, D)


def kernel(x, positions, norm_mix, norm_ffn, mix0_w_in, mla_q_a_norm, mla_w_uq, mla_kv_a_norm, mla_w_ukv, mla_q_norm, mla_k_norm, conv_dw_w, conv_dw_b, conv_ln_g, conv_ln_b, mix0_w_out, nsa_w_in, nsa_q_norm, nsa_k_norm_cmp, nsa_k_norm_slc, nsa_k_norm_win, nsa_cmp_pe_k, nsa_cmp_w1_k, nsa_cmp_w2_k, nsa_cmp_pe_v, nsa_cmp_w1_v, nsa_cmp_w2_v, nsa_w_out, moe_router_group, moe_router_group_b, moe_router_expert, moe_router_expert_b, moe_w_gate, moe_w_up, moe_w_down):
    def moe(x, layer):
        return _moe(x, norm_ffn[layer], moe_router_group[layer], moe_router_group_b[layer], moe_router_expert[layer],
                    moe_router_expert_b[layer], moe_w_gate, moe_w_up, moe_w_down, layer)

    x = _mixer0(x, positions, norm_mix[0], mix0_w_in[0], mla_q_a_norm[0], mla_w_uq[0], mla_kv_a_norm[0], mla_w_ukv[0],
                mla_q_norm[0], mla_k_norm[0], conv_dw_w[0], conv_dw_b[0], conv_ln_g[0], conv_ln_b[0], mix0_w_out[0])
    x = moe(x, 0)
    x = _mixer1(x, positions, norm_mix[1], nsa_w_in[0], nsa_q_norm[0], nsa_k_norm_cmp[0], nsa_k_norm_slc[0],
                nsa_k_norm_win[0], nsa_cmp_pe_k[0], nsa_cmp_w1_k[0], nsa_cmp_w2_k[0], nsa_cmp_pe_v[0],
                nsa_cmp_w1_v[0], nsa_cmp_w2_v[0], nsa_w_out[0])
    return moe(x, 1)
```
